```python
import math
import jax, jax.numpy as jnp
from jax import lax
import numpy as np

D_MODEL = 1024
BATCH = 8
SEQ = 2048
DEPTH = 1

CHUNK = 64
Q_BLOCK = 128
EPS = 1e-6
N_HEADS_A = 8
HEAD_DIM_A = 64
KV_DIM_A = 64
ATTN_OUT = N_HEADS_A * KV_DIM_A
N_IDX_HEADS = 4
IDX_DIM = 64
TOPK_MAX = 256
LRU_WIDTH = 512
LRU_BLOCKS = 8
LRU_BLOCK_DIM = LRU_WIDTH // LRU_BLOCKS
CONV_WIDTH = 4
LRU_C = 8.0
N_GROUPS = 4
EXPERTS_PER_GROUP = 8
N_EXPERTS = N_GROUPS * EXPERTS_PER_GROUP
TOPK_IN_GROUP = 2
D_FF_EXPERT = 256
SPLITS = (N_HEADS_A * HEAD_DIM_A, KV_DIM_A, KV_DIM_A, N_IDX_HEADS * IDX_DIM, IDX_DIM, N_IDX_HEADS,
          LRU_WIDTH, LRU_WIDTH, D_MODEL, D_MODEL)
D_IN = sum(SPLITS)

kernel_name = "hybrid_dsa_rglru_hmoe_block"


def _split_offsets():
    return [int(o) for o in np.cumsum(SPLITS)[:-1]]


def rmsnorm(x, g):
    xf = x.astype(jnp.float32)
    y = xf * lax.rsqrt(jnp.mean(xf * xf, axis=-1, keepdims=True) + EPS)
    return (y * g.astype(jnp.float32)).astype(x.dtype)


def modulate(h, shift, scale):
    return h * (1.0 + scale[:, None, :]) + shift[:, None, :]


def alibi_slopes(n):
    return jnp.exp2(-8.0 * jnp.arange(1, n + 1, dtype=jnp.float32) / n)


def dsa_attention(q, k, v, q_idx, k_idx, w_idx, q_norm_w, k_norm_w):
    B, S = q.shape[0], q.shape[1]
    topk = min(TOPK_MAX, S // 4)
    n_blocks = S // Q_BLOCK
    q = rmsnorm(q, q_norm_w) * (HEAD_DIM_A ** -0.5)
    k = rmsnorm(k, k_norm_w)
    w_idx = w_idx * (N_IDX_HEADS ** -0.5 * IDX_DIM ** -0.5)
    slopes = alibi_slopes(N_HEADS_A)
    key_chunk = jnp.arange(S) // CHUNK
    gather = jax.vmap(lambda table, idx: table[idx])

    def block(i):
        start = i * Q_BLOCK
        qb = lax.dynamic_slice_in_dim(q, start, Q_BLOCK, axis=1)
        qib = lax.dynamic_slice_in_dim(q_idx, start, Q_BLOCK, axis=1)
        wb = lax.dynamic_slice_in_dim(w_idx, start, Q_BLOCK, axis=1)
        q_pos = start + jnp.arange(Q_BLOCK)
        q_chunk = q_pos // CHUNK
        admissible = key_chunk[None, :] <= q_chunk[:, None]
        rel = jax.nn.relu(jnp.einsum('bqhd,bsd->bqhs', qib, k_idx).astype(jnp.float32))
        score = jnp.einsum('bqhs,bqh->bqs', rel, wb.astype(jnp.float32))
        score = jnp.where(admissible[None], score, -jnp.inf)
        _, sel = lax.top_k(score, topk)
        k_sel = gather(k, sel)
        v_sel = gather(v, sel)
        logits = jnp.einsum('bqhd,bqkd->bqhk', qb, k_sel).astype(jnp.float32)
        dist = jnp.abs(q_pos[None, :, None] - sel).astype(jnp.float32)
        logits = logits - slopes[None, None, :, None] * dist[:, :, None, :]
        valid = (sel // CHUNK) <= q_chunk[None, :, None]
        logits = jnp.where(valid[:, :, None, :], logits, -jnp.inf)
        p = jax.nn.softmax(logits, axis=-1).astype(v.dtype)
        return jnp.einsum('bqhk,bqkd->bqhd', p, v_sel)

    out = lax.map(block, jnp.arange(n_blocks))
    return jnp.moveaxis(out, 0, 1).reshape(B, S, ATTN_OUT)


def rg_lru_branch(xb, gb, conv_w, conv_b, w_rec_gate, b_rec_gate, w_in_gate, b_in_gate, lam):
    B, S, W = xb.shape
    xpad = jnp.pad(xb, ((0, 0), (CONV_WIDTH - 1, 0), (0, 0)))
    xc = conv_b + xpad[:, 0:S] * conv_w[0]
    for j in range(1, CONV_WIDTH):
        xc = xc + xpad[:, j:j + S] * conv_w[j]
    xblk = xc.reshape(B, S, LRU_BLOCKS, LRU_BLOCK_DIM)
    r = jax.nn.sigmoid(jnp.einsum('bsnd,nde->bsne', xblk, w_rec_gate).reshape(B, S, W) + b_rec_gate)
    i = jax.nn.sigmoid(jnp.einsum('bsnd,nde->bsne', xblk, w_in_gate).reshape(B, S, W) + b_in_gate)
    log_a = -LRU_C * r.astype(jnp.float32) * jax.nn.softplus(-lam.astype(jnp.float32))
    a = jnp.exp(log_a)
    b = jnp.sqrt(-jnp.expm1(2.0 * log_a)) * (i * xc).astype(jnp.float32)

    def combine(left, right):
        a1, b1 = left
        a2, b2 = right
        return a1 * a2, a2 * b1 + b2

    _, h = lax.associative_scan(combine, (a, b), axis=1)
    return h.astype(xb.dtype) * jax.nn.gelu(gb)


def hierarchical_moe(h, w_group, b_group, w_expert_router, b_expert_router, w1, w3, w2):
    B, S, D = h.shape
    hf = h.reshape(B * S, D)
    g_logits = (hf @ w_group + b_group).astype(jnp.float32)
    g_prob = jax.nn.softmax(g_logits, axis=-1)
    g_sel = jnp.argmax(g_logits, axis=-1)
    g_weight = jnp.take_along_axis(g_prob, g_sel[:, None], axis=1)
    e_logits = (hf @ w_expert_router + b_expert_router).astype(jnp.float32)
    e_logits = e_logits.reshape(-1, N_GROUPS, EXPERTS_PER_GROUP)
    e_logits = jnp.take_along_axis(e_logits, g_sel[:, None, None], axis=1)[:, 0]
    top_val, top_idx = lax.top_k(e_logits, TOPK_IN_GROUP)
    top_w = jax.nn.softmax(top_val, axis=-1) * g_weight
    expert_id = g_sel[:, None] * EXPERTS_PER_GROUP + top_idx
    comb = jnp.sum(jax.nn.one_hot(expert_id, N_EXPERTS, dtype=jnp.float32) * top_w[..., None], axis=1)
    comb = comb.astype(hf.dtype)
    y = jnp.zeros_like(hf)
    for e in range(N_EXPERTS):
        act = jax.nn.silu(hf @ w1[e]) * (hf @ w3[e])
        y = y + comb[:, e:e + 1] * (act @ w2[e])
    return y.reshape(B, S, D)


def setup_inputs(seed: int = 0) -> dict:
    key = jax.random.key(seed)
    ks = jax.random.split(key, 32)
    f32 = jnp.float32
    nrm = lambda k, shape, scale: jax.random.normal(k, shape, f32) * scale
    L, D = DEPTH, D_MODEL
    u = jax.random.uniform(ks[15], (L, LRU_WIDTH), f32, minval=0.9, maxval=0.999)
    sp = -jnp.log(u) / LRU_C
    lru_lambda = -jnp.log(jnp.expm1(sp))
    return {
        "x": nrm(ks[0], (BATCH, SEQ, D), 1.0),
        "c": nrm(ks[1], (BATCH, D), 1.0),
        "ada_w": nrm(ks[2], (L, D, 6 * D), 0.5 * D ** -0.5),
        "ada_b": nrm(ks[3], (L, 6 * D), 0.01),
        "norm_mix_w": 1.0 + nrm(ks[4], (L, D), 0.02),
        "w_in": nrm(ks[5], (L, D, D_IN), D ** -0.5),
        "q_norm_w": 1.0 + nrm(ks[6], (L, HEAD_DIM_A), 0.02),
        "k_norm_w": 1.0 + nrm(ks[7], (L, KV_DIM_A), 0.02),
        "conv_w": nrm(ks[8], (L, CONV_WIDTH, LRU_WIDTH), CONV_WIDTH ** -0.5),
        "conv_b": nrm(ks[9], (L, LRU_WIDTH), 0.01),
        "w_rec_gate": nrm(ks[10], (L, LRU_BLOCKS, LRU_BLOCK_DIM, LRU_BLOCK_DIM), LRU_BLOCK_DIM ** -0.5),
        "b_rec_gate": nrm(ks[11], (L, LRU_WIDTH), 0.01),
        "w_in_gate": nrm(ks[12], (L, LRU_BLOCKS, LRU_BLOCK_DIM, LRU_BLOCK_DIM), LRU_BLOCK_DIM ** -0.5),
        "b_in_gate": nrm(ks[13], (L, LRU_WIDTH), 0.01),
        "lru_lambda": lru_lambda,
        "w_proj_a": nrm(ks[16], (L, ATTN_OUT, D), ATTN_OUT ** -0.5),
        "w_proj_b": nrm(ks[17], (L, LRU_WIDTH, D), LRU_WIDTH ** -0.5),
        "w_out": nrm(ks[18], (L, D, D), D ** -0.5),
        "norm_ffn_w": 1.0 + nrm(ks[19], (L, D), 0.02),
        "w_group": nrm(ks[20], (L, D, N_GROUPS), D ** -0.5),
        "b_group": nrm(ks[21], (L, N_GROUPS), 0.01),
        "w_expert_router": nrm(ks[22], (L, D, N_EXPERTS), D ** -0.5),
        "b_expert_router": nrm(ks[23], (L, N_EXPERTS), 0.01),
        "w1": nrm(ks[24], (L, N_EXPERTS, D, D_FF_EXPERT), D ** -0.5),
        "w3": nrm(ks[25], (L, N_EXPERTS, D, D_FF_EXPERT), D ** -0.5),
        "w2": nrm(ks[26], (L, N_EXPERTS, D_FF_EXPERT, D), D_FF_EXPERT ** -0.5),
    }


def reference(x, c, ada_w, ada_b, norm_mix_w, w_in, q_norm_w, k_norm_w, conv_w, conv_b,
              w_rec_gate, b_rec_gate, w_in_gate, b_in_gate, lru_lambda, w_proj_a, w_proj_b,
              w_out, norm_ffn_w, w_group, b_group, w_expert_router, b_expert_router, w1, w3, w2):
    B, S, _ = x.shape
    cond = jax.nn.silu(c)
    offsets = _split_offsets()
    for l in range(DEPTH):
        mod = cond @ ada_w[l] + ada_b[l]
        shift_m, scale_m, gate_m, shift_f, scale_f, gate_f = jnp.split(mod, 6, axis=-1)
        h = modulate(rmsnorm(x, norm_mix_w[l]), shift_m, scale_m)
        proj = h @ w_in[l]
        q, k, v, q_idx, k_idx, w_idx, lru_x, lru_g, gate_a, gate_b = jnp.split(proj, offsets, axis=-1)
        q = q.reshape(B, S, N_HEADS_A, HEAD_DIM_A)
        q_idx = q_idx.reshape(B, S, N_IDX_HEADS, IDX_DIM)
        y_a = dsa_attention(q, k, v, q_idx, k_idx, w_idx, q_norm_w[l], k_norm_w[l]) @ w_proj_a[l]
        y_b = rg_lru_branch(lru_x, lru_g, conv_w[l], conv_b[l], w_rec_gate[l], b_rec_gate[l],
                            w_in_gate[l], b_in_gate[l], lru_lambda[l]) @ w_proj_b[l]
        merged = jax.nn.sigmoid(gate_a) * y_a + jax.nn.sigmoid(gate_b) * y_b
        x = x + gate_m[:, None, :] * (merged @ w_out[l])
        h2 = modulate(rmsnorm(x, norm_ffn_w[l]), shift_f, scale_f)
        x = x + gate_f[:, None, :] * hierarchical_moe(h2, w_group[l], b_group[l], w_expert_router[l],
                                                      b_expert_router[l], w1[l], w3[l], w2[l])
    return x
```

```python
import functools

import jax
import jax.numpy as jnp
import numpy as np
from jax import lax
from jax.experimental import pallas as pl
from jax.experimental.pallas import tpu as pltpu

F32 = jnp.float32
BF16 = jnp.bfloat16
I32 = jnp.int32

EPS = 1e-6
CHUNK = 64
Q_BLOCK = 128
N_HEADS_A = 8
HEAD_DIM_A = 64
KV_DIM_A = 64
ATTN_OUT = N_HEADS_A * KV_DIM_A
N_IDX_HEADS = 4
IDX_DIM = 64
TOPK_MAX = 256
LRU_WIDTH = 512
LRU_BLOCKS = 8
LRU_BLOCK_DIM = LRU_WIDTH // LRU_BLOCKS
CONV_WIDTH = 4
LRU_C = 8.0
N_GROUPS = 4
EXPERTS_PER_GROUP = 8
N_EXPERTS = N_GROUPS * EXPERTS_PER_GROUP
D_FF_EXPERT = 256

LANES = 128
INT_MIN = -2 ** 31
VMEM_LIMIT = 56 * 1024 * 1024

QA_W = N_HEADS_A * HEAD_DIM_A + N_IDX_HEADS * IDX_DIM
KV_W = 256
K_OFF, V_OFF, KI_OFF, WI_OFF = 0, 64, 128, 192

TM_PROJ = 256
TM_MERGE = 256
TM_MOE = 256
TM_ROWS = 256

R_E1, R_E2, R_W1, R_W2, R_RANK1, R_RANK2 = 0, 1, 2, 3, 4, 5


def _dot(a, b, **kw):
    return jnp.dot(a, b, preferred_element_type=F32, **kw)


def _dot_nt(a, b):
    return lax.dot_general(a, b, (((1,), (1,)), ((), ())), preferred_element_type=F32)


def _params(*sem):
    return pltpu.CompilerParams(dimension_semantics=sem, vmem_limit_bytes=VMEM_LIMIT)


def _ada_kernel(c_ref, w_ref, b_ref, o_ref):
    c = c_ref[...]
    cond = c * jax.nn.sigmoid(c)
    o_ref[...] = _dot(cond, w_ref[...], precision=lax.Precision.HIGHEST) + b_ref[...]


def _ada(c, ada_w, ada_b):
    B, D = c.shape
    n_out = ada_w.shape[1]
    tn = 1024
    return pl.pallas_call(
        _ada_kernel,
        grid=(n_out // tn,),
        in_specs=[pl.BlockSpec((B, D), lambda j: (0, 0)),
                  pl.BlockSpec((D, tn), lambda j: (0, j)),
                  pl.BlockSpec((1, tn), lambda j: (0, j))],
        out_specs=pl.BlockSpec((B, tn), lambda j: (0, j)),
        out_shape=jax.ShapeDtypeStruct((B, n_out), F32),
        compiler_params=_params("arbitrary"),
        name="ada",
    )(c, ada_w, ada_b.reshape(1, n_out))


def _rms_mod(x, w, shift, scale):
    ms = jnp.mean(x * x, axis=-1, keepdims=True)
    y = x * lax.rsqrt(ms + EPS) * w
    return y * (1.0 + scale) + shift


def _proj_kernel(x_ref, mod_ref, nw_ref, wa_ref, wb_ref, wg_ref, oqa_ref, okv_ref, olru_ref, og_ref):
    h = _rms_mod(x_ref[...], nw_ref[...], mod_ref[0, 0:1, :], mod_ref[0, 1:2, :])
    hb = h.astype(BF16)
    pa = _dot(hb, wa_ref[...])
    oqa_ref[...] = pa[:, :QA_W]
    okv_ref[...] = pa[:, QA_W:]
    olru_ref[...] = _dot(hb, wb_ref[...])
    og_ref[...] = _dot(hb, wg_ref[...])


def _proj(x2, mod3, norm_w, wa, wb, wg, S):
    N, D = x2.shape
    tm = TM_PROJ
    steps_per_batch = S // tm
    const = lambda t: (0, 0)
    return pl.pallas_call(
        _proj_kernel,
        grid=(N // tm,),
        in_specs=[pl.BlockSpec((tm, D), lambda t: (t, 0)),
                  pl.BlockSpec((1, 6, D), lambda t: (t // steps_per_batch, 0, 0)),
                  pl.BlockSpec((1, D), const),
                  pl.BlockSpec(wa.shape, const),
                  pl.BlockSpec(wb.shape, const),
                  pl.BlockSpec(wg.shape, const)],
        out_specs=[pl.BlockSpec((tm, QA_W), lambda t: (t, 0)),
                   pl.BlockSpec((tm, KV_W), lambda t: (t, 0)),
                   pl.BlockSpec((tm, 2 * LRU_WIDTH), lambda t: (t, 0)),
                   pl.BlockSpec((tm, 2 * D), lambda t: (t, 0))],
        out_shape=[jax.ShapeDtypeStruct((N, QA_W), F32),
                   jax.ShapeDtypeStruct((N, KV_W), F32),
                   jax.ShapeDtypeStruct((N, 2 * LRU_WIDTH), F32),
                   jax.ShapeDtypeStruct((N, 2 * D), F32)],
        compiler_params=_params("arbitrary"),
        name="proj",
    )(x2, mod3, norm_w.reshape(1, D), wa, wb, wg)


def _attn_kernel(qa_ref, kvq_ref, kv_ref, qnw_ref, knw_ref, o_ref, kn_s, v_s, ki_s, key_s, bias_s):
    i = pl.program_id(1)
    S = kv_ref.shape[0]
    topk = float(min(TOPK_MAX, S // 4))

    @pl.when(i == 0)
    def _():
        k = kv_ref[:, K_OFF:K_OFF + KV_DIM_A]
        ms = jnp.mean(k * k, axis=-1, keepdims=True)
        kn_s[...] = (k * lax.rsqrt(ms + EPS) * knw_ref[...]).astype(BF16)
        v_s[...] = kv_ref[:, V_OFF:V_OFF + KV_DIM_A].astype(BF16)
        ki_s[...] = kv_ref[:, KI_OFF:KI_OFF + IDX_DIM].astype(BF16)

    wq = kvq_ref[:, WI_OFF:WI_OFF + N_IDX_HEADS] * (N_IDX_HEADS ** -0.5 * IDX_DIM ** -0.5)
    ki = ki_s[...]
    score = jnp.zeros((Q_BLOCK, S), F32)
    for h in range(N_IDX_HEADS):
        lo = N_HEADS_A * HEAD_DIM_A + h * IDX_DIM
        qi = qa_ref[:, lo:lo + IDX_DIM].astype(BF16)
        score = score + jnp.maximum(_dot_nt(qi, ki), 0.0) * wq[:, h:h + 1]

    q_pos = i * Q_BLOCK + lax.broadcasted_iota(I32, (Q_BLOCK, 1), 0)
    k_pos = lax.broadcasted_iota(I32, (Q_BLOCK, S), 1)
    limit = (q_pos // CHUNK + 1) * CHUNK
    adm = k_pos < limit

    bits = pltpu.bitcast(score, I32)
    skey = jnp.where(bits < 0, -(bits & 0x7FFFFFFF), bits)
    key_s[...] = jnp.where(adm, skey, INT_MIN)

    def bisect(it, r):
        c = r | jnp.left_shift(jnp.int32(1), 31 - it)
        cnt = jnp.sum(jnp.where(key_s[...] >= (c ^ INT_MIN), 1.0, 0.0), axis=-1, keepdims=True)
        return jnp.where(cnt >= topk, c, r)

    thr = lax.fori_loop(0, 32, bisect, jnp.zeros((Q_BLOCK, 1), I32)) ^ INT_MIN

    key = key_s[...]
    n_gt = jnp.sum(jnp.where(key > thr, 1.0, 0.0), axis=-1, keepdims=True)
    need = topk - n_gt

    rr = lax.broadcasted_iota(I32, (LANES, 2 * LANES), 0)
    cc = lax.broadcasted_iota(I32, (LANES, 2 * LANES), 1)
    tri = jnp.where((rr < cc) | (cc >= LANES), 1.0, 0.0).astype(BF16)
    seen = jnp.zeros((Q_BLOCK, LANES), F32)
    for c in range(S // LANES):
        kc = key[:, c * LANES:(c + 1) * LANES]
        eq = kc == thr
        pt = _dot(jnp.where(eq, 1.0, 0.0).astype(BF16), tri)
        rank = pt[:, :LANES] + seen
        seen = seen + pt[:, LANES:]
        tie_bias = jnp.where(jnp.where(eq, rank, topk) < need, 0.0, -jnp.inf)
        bias = jnp.where(kc > thr, 0.0, tie_bias)
        bias_s[:, c * LANES:(c + 1) * LANES] = jnp.where(kc == INT_MIN, -jnp.inf, bias)

    dist = jnp.abs(q_pos - k_pos).astype(F32)
    kn = kn_s[...]
    v = v_s[...]
    qnw = qnw_ref[...]
    for h in range(N_HEADS_A):
        qh = qa_ref[:, h * HEAD_DIM_A:(h + 1) * HEAD_DIM_A]
        ms = jnp.mean(qh * qh, axis=-1, keepdims=True)
        qn = ((qh * lax.rsqrt(ms + EPS) * qnw) * (HEAD_DIM_A ** -0.5)).astype(BF16)
        logits = _dot_nt(qn, kn) - (2.0 ** -(h + 1)) * dist + bias_s[...]
        m = jnp.max(logits, axis=-1, keepdims=True)
        p = jnp.exp(logits - m)
        l = jnp.sum(p, axis=-1, keepdims=True)
        o = _dot(p.astype(BF16), v)
        o_ref[:, h * KV_DIM_A:(h + 1) * KV_DIM_A] = o / l


def _attn(qa, kv, q_norm_w, k_norm_w, B, S):
    N = qa.shape[0]
    nb = S // Q_BLOCK
    return pl.pallas_call(
        _attn_kernel,
        grid=(B, nb),
        in_specs=[pl.BlockSpec((Q_BLOCK, QA_W), lambda b, i: (b * nb + i, 0)),
                  pl.BlockSpec((Q_BLOCK, KV_W), lambda b, i: (b * nb + i, 0)),
                  pl.BlockSpec((S, KV_W), lambda b, i: (b, 0)),
                  pl.BlockSpec((1, HEAD_DIM_A), lambda b, i: (0, 0)),
                  pl.BlockSpec((1, KV_DIM_A), lambda b, i: (0, 0))],
        out_specs=pl.BlockSpec((Q_BLOCK, ATTN_OUT), lambda b, i: (b * nb + i, 0)),
        out_shape=jax.ShapeDtypeStruct((N, ATTN_OUT), F32),
        scratch_shapes=[pltpu.VMEM((S, KV_DIM_A), BF16),
                        pltpu.VMEM((S, KV_DIM_A), BF16),
                        pltpu.VMEM((S, IDX_DIM), BF16),
                        pltpu.VMEM((Q_BLOCK, S), I32),
                        pltpu.VMEM((Q_BLOCK, S), F32)],
        compiler_params=_params("arbitrary", "arbitrary"),
        name="attn",
    )(qa, kv, kv, q_norm_w.reshape(1, HEAD_DIM_A), k_norm_w.reshape(1, KV_DIM_A))


def _shift_rows(x, d, fill, row):
    return jnp.where(row >= d, pltpu.roll(x, d, 0), fill)


def _lru_kernel(xb_ref, gb_ref, cw_ref, cb_ref, wr_ref, br_ref, wi_ref, bi_ref, lam_ref, o_ref):
    x = xb_ref[...]
    S = x.shape[0]
    row = lax.broadcasted_iota(I32, x.shape, 0)
    xc = cb_ref[...] + _shift_rows(x, CONV_WIDTH - 1, 0.0, row) * cw_ref[0:1, :]
    for j in range(1, CONV_WIDTH):
        d = CONV_WIDTH - 1 - j
        xs = x if d == 0 else _shift_rows(x, d, 0.0, row)
        xc = xc + xs * cw_ref[j:j + 1, :]
    xcb = xc.astype(BF16)
    r = jax.nn.sigmoid(_dot(xcb, wr_ref[0]) + br_ref[...])
    ig = jax.nn.sigmoid(_dot(xcb, wi_ref[0]) + bi_ref[...])
    z = -lam_ref[...]
    softplus = jnp.maximum(z, 0.0) + jnp.log1p(jnp.exp(-jnp.abs(z)))
    log_a = -LRU_C * r * softplus
    a = jnp.exp(log_a)
    b = jnp.sqrt(-jnp.tanh(log_a) * (1.0 + a * a)) * (ig * xc)
    d = 1
    while d < S:
        b = a * _shift_rows(b, d, 0.0, row) + b
        if 2 * d < S:
            a = a * _shift_rows(a, d, 1.0, row)
        d *= 2
    g = gb_ref[...]
    gelu = 0.5 * g * (1.0 + jnp.tanh(float(np.sqrt(2.0 / np.pi)) * (g + 0.044715 * (g * g * g))))
    o_ref[...] = b * gelu


def _lru(lru, conv_w, conv_b, wr_bd, b_rec, wi_bd, b_in, lam, B, S):
    N = lru.shape[0]
    nt = LRU_WIDTH // LANES
    vec = lambda b, j: (0, j)
    return pl.pallas_call(
        _lru_kernel,
        grid=(B, nt),
        in_specs=[pl.BlockSpec((S, LANES), lambda b, j: (b, j)),
                  pl.BlockSpec((S, LANES), lambda b, j: (b, nt + j)),
                  pl.BlockSpec((CONV_WIDTH, LANES), vec),
                  pl.BlockSpec((1, LANES), vec),
                  pl.BlockSpec((1, LANES, LANES), lambda b, j: (j, 0, 0)),
                  pl.BlockSpec((1, LANES), vec),
                  pl.BlockSpec((1, LANES, LANES), lambda b, j: (j, 0, 0)),
                  pl.BlockSpec((1, LANES), vec),
                  pl.BlockSpec((1, LANES), vec)],
        out_specs=pl.BlockSpec((S, LANES), lambda b, j: (b, j)),
        out_shape=jax.ShapeDtypeStruct((N, LRU_WIDTH), F32),
        compiler_params=_params("arbitrary", "arbitrary"),
        name="lru",
    )(lru, lru, conv_w, conv_b.reshape(1, -1), wr_bd, b_rec.reshape(1, -1), wi_bd,
      b_in.reshape(1, -1), lam.reshape(1, -1))


def _block_diag_tiles(w):
    per = LANES // LRU_BLOCK_DIM
    nt = LRU_WIDTH // LANES
    out = jnp.zeros((nt, LANES, LANES), w.dtype)
    for t in range(nt):
        for p in range(per):
            lo = p * LRU_BLOCK_DIM
            out = out.at[t, lo:lo + LRU_BLOCK_DIM, lo:lo + LRU_BLOCK_DIM].set(w[t * per + p])
    return out


def _first_index_of_max(vals, lane):
    m = jnp.max(vals, axis=-1, keepdims=True)
    idx = jnp.min(jnp.where(vals == m, lane, 4 * LANES), axis=-1, keepdims=True)
    return m, idx


def _merge_kernel(attn_ref, lru_ref, g_ref, x_ref, mod_ref, wpa_ref, wpb_ref, wo_ref, nw_ref, wr_ref, br_ref,
                  x1_ref, h2_ref, route_ref, cnt_ref, run_s):
    t = pl.program_id(0)
    D = x_ref.shape[1]
    tm = x_ref.shape[0]

    @pl.when(t == 0)
    def _():
        run_s[...] = jnp.zeros_like(run_s)

    y_a = _dot(attn_ref[...].astype(BF16), wpa_ref[...])
    y_b = _dot(lru_ref[...].astype(BF16), wpb_ref[...])
    merged = jax.nn.sigmoid(g_ref[:, :D]) * y_a + jax.nn.sigmoid(g_ref[:, D:]) * y_b
    x1 = x_ref[...] + mod_ref[0, 2:3, :] * _dot(merged.astype(BF16), wo_ref[...])
    x1_ref[...] = x1
    h2 = _rms_mod(x1, nw_ref[...], mod_ref[0, 3:4, :], mod_ref[0, 4:5, :])
    h2_ref[...] = h2

    logits = _dot(h2, wr_ref[...], precision=lax.Precision.HIGHEST) + br_ref[...]
    lane = lax.broadcasted_iota(I32, logits.shape, 1)
    gl = jnp.where(lane < N_GROUPS, logits, -jnp.inf)
    gmax, g_sel = _first_index_of_max(gl, lane)
    g_weight = 1.0 / jnp.sum(jnp.exp(gl - gmax), axis=-1, keepdims=True)
    e_lo = N_GROUPS + g_sel * EXPERTS_PER_GROUP
    el = jnp.where((lane >= e_lo) & (lane < e_lo + EXPERTS_PER_GROUP), logits, -jnp.inf)
    v1, i1 = _first_index_of_max(el, lane)
    el2 = jnp.where(lane == i1, -jnp.inf, el)
    v2, i2 = _first_index_of_max(el2, lane)
    e2x = jnp.exp(v2 - v1)
    w1 = g_weight / (1.0 + e2x)
    w2 = g_weight * e2x / (1.0 + e2x)

    hot1 = lane == i1
    hot2 = lane == i2
    hot = jnp.where(hot1 | hot2, 1.0, 0.0)
    rr = lax.broadcasted_iota(I32, (tm, tm), 0)
    cc = lax.broadcasted_iota(I32, (tm, tm), 1)
    lower = jnp.where(cc < rr, 1.0, 0.0).astype(BF16)
    before = _dot(lower, hot.astype(BF16)) + run_s[...]
    rank1 = jnp.sum(jnp.where(hot1, before, 0.0), axis=-1, keepdims=True)
    rank2 = jnp.sum(jnp.where(hot2, before, 0.0), axis=-1, keepdims=True)
    run_s[...] = run_s[...] + jnp.sum(hot, axis=0, keepdims=True)
    cnt_ref[...] = run_s[...]

    rec = jnp.zeros(logits.shape, F32)
    for col, val in ((R_E1, (i1 - N_GROUPS).astype(F32)), (R_E2, (i2 - N_GROUPS).astype(F32)),
                     (R_W1, w1), (R_W2, w2), (R_RANK1, rank1), (R_RANK2, rank2)):
        rec = jnp.where(lane == col, val, rec)
    route_ref[...] = rec


def _merge(attn, lru_out, gates, x2, mod3, wpa, wpb, wo, norm_w, w_router, b_router, S):
    N, D = x2.shape
    tm = TM_MERGE
    steps_per_batch = S // tm
    const = lambda t: (0, 0)
    rows = lambda t: (t, 0)
    return pl.pallas_call(
        _merge_kernel,
        grid=(N // tm,),
        in_specs=[pl.BlockSpec((tm, ATTN_OUT), rows),
                  pl.BlockSpec((tm, LRU_WIDTH), rows),
                  pl.BlockSpec((tm, 2 * D), rows),
                  pl.BlockSpec((tm, D), rows),
                  pl.BlockSpec((1, 6, D), lambda t: (t // steps_per_batch, 0, 0)),
                  pl.BlockSpec(wpa.shape, const),
                  pl.BlockSpec(wpb.shape, const),
                  pl.BlockSpec(wo.shape, const),
                  pl.BlockSpec((1, D), const),
                  pl.BlockSpec(w_router.shape, const),
                  pl.BlockSpec((1, LANES), const)],
        out_specs=[pl.BlockSpec((tm, D), rows),
                   pl.BlockSpec((tm, D), rows),
                   pl.BlockSpec((tm, LANES), rows),
                   pl.BlockSpec((1, LANES), const)],
        out_shape=[jax.ShapeDtypeStruct((N, D), F32),
                   jax.ShapeDtypeStruct((N, D), F32),
                   jax.ShapeDtypeStruct((N, LANES), F32),
                   jax.ShapeDtypeStruct((1, LANES), F32)],
        scratch_shapes=[pltpu.VMEM((1, LANES), F32)],
        compiler_params=_params("arbitrary"),
        name="merge",
    )(attn, lru_out, gates, x2, mod3, wpa, wpb, wo, norm_w.reshape(1, D), w_router, b_router)


def _scatter_kernel(pos1_ref, pos2_ref, h2_ref, hs_in_ref, hs_ref, sem):
    del hs_in_ref
    base = pl.program_id(0) * TM_ROWS

    def row_copy(t, pos_ref):
        return pltpu.make_async_copy(h2_ref.at[pl.ds(t, 1)], hs_ref.at[pl.ds(pos_ref[t], 1)], sem)

    def start(k, carry):
        row_copy(base + k, pos1_ref).start()
        row_copy(base + k, pos2_ref).start()
        return carry

    def wait(k, carry):
        row_copy(base + k, pos1_ref).wait()
        row_copy(base + k, pos2_ref).wait()
        return carry

    lax.fori_loop(0, TM_ROWS, start, 0)
    lax.fori_loop(0, TM_ROWS, wait, 0)


def _scatter(pos1, pos2, h2, n_sorted):
    N, D = h2.shape
    grid_spec = pltpu.PrefetchScalarGridSpec(
        num_scalar_prefetch=2,
        grid=(N // TM_ROWS,),
        in_specs=[pl.BlockSpec(memory_space=pl.ANY), pl.BlockSpec(memory_space=pl.ANY)],
        out_specs=pl.BlockSpec(memory_space=pl.ANY),
        scratch_shapes=[pltpu.SemaphoreType.DMA(())],
    )
    return pl.pallas_call(
        _scatter_kernel,
        grid_spec=grid_spec,
        out_shape=jax.ShapeDtypeStruct((n_sorted, D), h2.dtype),
        input_output_aliases={3: 0},
        compiler_params=_params("arbitrary"),
        name="scatter",
    )(pos1, pos2, h2, jnp.zeros((n_sorted, D), h2.dtype))


def _moe_kernel(te_ref, nt_ref, hs_ref, w1_ref, w3_ref, w2_ref, ys_ref):
    i = pl.program_id(0)

    @pl.when(i < nt_ref[0])
    def _():
        hb = hs_ref[...].astype(BF16)
        a = _dot(hb, w1_ref[0])
        act = (a * jax.nn.sigmoid(a)) * _dot(hb, w3_ref[0])
        ys_ref[...] = _dot(act.astype(BF16), w2_ref[0])

    @pl.when(i >= nt_ref[0])
    def _():
        ys_ref[...] = jnp.zeros_like(ys_ref)


def _moe(tile_expert, n_tiles_used, hs, w1, w3, w2):
    P, D = hs.shape
    tm = TM_MOE
    F = w1.shape[2]
    grid_spec = pltpu.PrefetchScalarGridSpec(
        num_scalar_prefetch=2,
        grid=(P // tm,),
        in_specs=[pl.BlockSpec((tm, D), lambda i, te, nt: (i, 0)),
                  pl.BlockSpec((1, D, F), lambda i, te, nt: (te[i], 0, 0)),
                  pl.BlockSpec((1, D, F), lambda i, te, nt: (te[i], 0, 0)),
                  pl.BlockSpec((1, F, D), lambda i, te, nt: (te[i], 0, 0))],
        out_specs=pl.BlockSpec((tm, D), lambda i, te, nt: (i, 0)),
    )
    return pl.pallas_call(
        _moe_kernel,
        grid_spec=grid_spec,
        out_shape=jax.ShapeDtypeStruct((P, D), F32),
        compiler_params=_params("arbitrary"),
        name="moe",
    )(tile_expert, n_tiles_used, hs, w1, w3, w2)


def _combine_kernel(pos1_ref, pos2_ref, ys_ref, x1_ref, route_ref, mod_ref, o_ref, y1_s, y2_s, sem):
    base = pl.program_id(0) * TM_ROWS

    def row_copy(k, pos_ref, dst):
        return pltpu.make_async_copy(ys_ref.at[pl.ds(pos_ref[base + k], 1)], dst.at[pl.ds(k, 1)], sem)

    def start(k, carry):
        row_copy(k, pos1_ref, y1_s).start()
        row_copy(k, pos2_ref, y2_s).start()
        return carry

    def wait(k, carry):
        row_copy(k, pos1_ref, y1_s).wait()
        row_copy(k, pos2_ref, y2_s).wait()
        return carry

    lax.fori_loop(0, TM_ROWS, start, 0)
    lax.fori_loop(0, TM_ROWS, wait, 0)
    w1 = route_ref[:, R_W1:R_W1 + 1]
    w2 = route_ref[:, R_W2:R_W2 + 1]
    y = w1 * y1_s[...] + w2 * y2_s[...]
    o_ref[...] = x1_ref[...] + mod_ref[0, 5:6, :] * y


def _combine(pos1, pos2, ys, x1, route, mod3, S):
    N, D = x1.shape
    tm = TM_ROWS
    steps_per_batch = S // tm
    grid_spec = pltpu.PrefetchScalarGridSpec(
        num_scalar_prefetch=2,
        grid=(N // tm,),
        in_specs=[pl.BlockSpec(memory_space=pl.ANY),
                  pl.BlockSpec((tm, D), lambda t, p1, p2: (t, 0)),
                  pl.BlockSpec((tm, LANES), lambda t, p1, p2: (t, 0)),
                  pl.BlockSpec((1, 6, D), lambda t, p1, p2: (t // steps_per_batch, 0, 0))],
        out_specs=pl.BlockSpec((tm, D), lambda t, p1, p2: (t, 0)),
        scratch_shapes=[pltpu.VMEM((tm, D), F32), pltpu.VMEM((tm, D), F32), pltpu.SemaphoreType.DMA(())],
    )
    return pl.pallas_call(
        _combine_kernel,
        grid_spec=grid_spec,
        out_shape=jax.ShapeDtypeStruct((N, D), F32),
        compiler_params=_params("arbitrary"),
        name="combine",
    )(pos1, pos2, ys, x1, route, mod3)


def _layer(x2, mod3, B, S, norm_mix_w, w_in, q_norm_w, k_norm_w, conv_w, conv_b, w_rec_gate, b_rec_gate,
           w_in_gate, b_in_gate, lru_lambda, w_proj_a, w_proj_b, w_out, norm_ffn_w, w_group, b_group,
           w_expert_router, b_expert_router, w1, w3, w2):
    N, D = x2.shape
    nq = N_HEADS_A * HEAD_DIM_A
    o_k = nq
    o_v = o_k + KV_DIM_A
    o_qi = o_v + KV_DIM_A
    o_ki = o_qi + N_IDX_HEADS * IDX_DIM
    o_wi = o_ki + IDX_DIM
    o_lx = o_wi + N_IDX_HEADS
    o_g = o_lx + 2 * LRU_WIDTH
    pad = jnp.zeros((D, KV_W - (WI_OFF + N_IDX_HEADS)), w_in.dtype)
    wa = jnp.concatenate([w_in[:, :nq], w_in[:, o_qi:o_ki], w_in[:, o_k:o_qi], w_in[:, o_ki:o_lx], pad],
                         axis=1).astype(BF16)
    wb = w_in[:, o_lx:o_g].astype(BF16)
    wg = w_in[:, o_g:].astype(BF16)

    qa, kv, lru, gates = _proj(x2, mod3, norm_mix_w, wa, wb, wg, S)
    attn = _attn(qa, kv, q_norm_w, k_norm_w, B, S)
    lru_out = _lru(lru, conv_w, conv_b, _block_diag_tiles(w_rec_gate).astype(BF16), b_rec_gate,
                   _block_diag_tiles(w_in_gate).astype(BF16), b_in_gate, lru_lambda, B, S)

    n_r = N_GROUPS + N_EXPERTS
    w_router = jnp.concatenate([w_group, w_expert_router, jnp.zeros((D, LANES - n_r), F32)], axis=1)
    b_router = jnp.concatenate([b_group, b_expert_router, jnp.zeros((LANES - n_r,), F32)]).reshape(1, LANES)
    x1, h2, route, counts = _merge(attn, lru_out, gates, x2, mod3, w_proj_a.astype(BF16), w_proj_b.astype(BF16),
                                   w_out.astype(BF16), norm_ffn_w, w_router, b_router, S)

    cnt = counts[0, N_GROUPS:n_r].astype(I32)
    seg_tiles = (cnt + TM_MOE - 1) // TM_MOE
    tile_end = jnp.cumsum(seg_tiles)
    seg_off = (tile_end - seg_tiles) * TM_MOE
    n_tiles = (2 * N) // TM_MOE + N_EXPERTS
    tile_expert = jnp.minimum(jnp.searchsorted(tile_end, jnp.arange(n_tiles, dtype=I32), side="right"),
                              N_EXPERTS - 1).astype(I32)
    e1 = route[:, R_E1].astype(I32)
    e2 = route[:, R_E2].astype(I32)
    pos1 = seg_off[e1] + route[:, R_RANK1].astype(I32)
    pos2 = seg_off[e2] + route[:, R_RANK2].astype(I32)

    hs = _scatter(pos1, pos2, h2, n_tiles * TM_MOE)
    ys = _moe(tile_expert, tile_end[-1:].astype(I32), hs, w1.astype(BF16), w3.astype(BF16), w2.astype(BF16))
    return _combine(pos1, pos2, ys, x1, route, mod3, S)


def kernel(x, c, ada_w, ada_b, norm_mix_w, w_in, q_norm_w, k_norm_w, conv_w, conv_b, w_rec_gate, b_rec_gate,
           w_in_gate, b_in_gate, lru_lambda, w_proj_a, w_proj_b, w_out, norm_ffn_w, w_group, b_group,
           w_expert_router, b_expert_router, w1, w3, w2):
    B, S, D = x.shape
    x2 = x.reshape(B * S, D)
    for l in range(ada_w.shape[0]):
        mod3 = _ada(c, ada_w[l], ada_b[l]).reshape(B, 6, D)
        x2 = _layer(x2, mod3, B, S, norm_mix_w[l], w_in[l], q_norm_w[l], k_norm_w[l], conv_w[l], conv_b[l],
                    w_rec_gate[l], b_rec_gate[l], w_in_gate[l], b_in_gate[l], lru_lambda[l], w_proj_a[l],
                    w_proj_b[l], w_out[l], norm_ffn_w[l], w_group[l], b_group[l], w_expert_router[l],
                    b_expert_router[l], w1[l], w3[l], w2[l])
    return x2.reshape(B, S, D)
```

```python
import functools

import jax
import jax.numpy as jnp
import numpy as np
from jax import lax
from jax.experimental import pallas as pl
from jax.experimental.pallas import tpu as pltpu

F32 = jnp.float32
BF16 = jnp.bfloat16
I32 = jnp.int32

EPS = 1e-6
CHUNK = 64
Q_BLOCK = 128
N_HEADS_A = 8
HEAD_DIM_A = 64
KV_DIM_A = 64
ATTN_OUT = N_HEADS_A * KV_DIM_A
N_IDX_HEADS = 4
IDX_DIM = 64
TOPK_MAX = 256
LRU_WIDTH = 512
LRU_BLOCKS = 8
LRU_BLOCK_DIM = LRU_WIDTH // LRU_BLOCKS
CONV_WIDTH = 4
LRU_C = 8.0
N_GROUPS = 4
EXPERTS_PER_GROUP = 8
N_EXPERTS = N_GROUPS * EXPERTS_PER_GROUP
D_FF_EXPERT = 256

LANES = 128
INT_MIN = -2 ** 31
VMEM_LIMIT = 56 * 1024 * 1024

QA_W = N_HEADS_A * HEAD_DIM_A + N_IDX_HEADS * IDX_DIM
KV_W = 256
K_OFF, V_OFF, KI_OFF, WI_OFF = 0, 64, 128, 192

TM_PROJ = 256
TM_MERGE = 256
TM_MOE = 256
TM_ROWS = 256

R_E1, R_E2, R_W1, R_W2, R_RANK1, R_RANK2 = 0, 1, 2, 3, 4, 5


def _dot(a, b, **kw):
    return jnp.dot(a, b, preferred_element_type=F32, **kw)


def _dot_nt(a, b):
    return lax.dot_general(a, b, (((1,), (1,)), ((), ())), preferred_element_type=F32)


def _params(*sem):
    return pltpu.CompilerParams(dimension_semantics=sem, vmem_limit_bytes=VMEM_LIMIT)


def _ada_kernel(c_ref, w_ref, b_ref, o_ref):
    c = c_ref[...]
    cond = c * jax.nn.sigmoid(c)
    o_ref[...] = _dot(cond, w_ref[...], precision=lax.Precision.HIGHEST) + b_ref[...]


def _ada(c, ada_w, ada_b):
    B, D = c.shape
    n_out = ada_w.shape[1]
    tn = 1024
    return pl.pallas_call(
        _ada_kernel,
        grid=(n_out // tn,),
        in_specs=[pl.BlockSpec((B, D), lambda j: (0, 0)),
                  pl.BlockSpec((D, tn), lambda j: (0, j)),
                  pl.BlockSpec((1, tn), lambda j: (0, j))],
        out_specs=pl.BlockSpec((B, tn), lambda j: (0, j)),
        out_shape=jax.ShapeDtypeStruct((B, n_out), F32),
        compiler_params=_params("arbitrary"),
        name="ada",
    )(c, ada_w, ada_b.reshape(1, n_out))


def _rms_mod(x, w, shift, scale):
    ms = jnp.mean(x * x, axis=-1, keepdims=True)
    y = x * lax.rsqrt(ms + EPS) * w
    return y * (1.0 + scale) + shift


def _proj_kernel(x_ref, mod_ref, nw_ref, wa_ref, wb_ref, wg_ref, oqa_ref, okv_ref, olru_ref, og_ref):
    h = _rms_mod(x_ref[...], nw_ref[...], mod_ref[0, 0:1, :], mod_ref[0, 1:2, :])
    hb = h.astype(BF16)
    pa = _dot(hb, wa_ref[...])
    oqa_ref[...] = pa[:, :QA_W]
    okv_ref[...] = pa[:, QA_W:]
    olru_ref[...] = _dot(hb, wb_ref[...])
    og_ref[...] = _dot(hb, wg_ref[...])


def _proj(x2, mod3, norm_w, wa, wb, wg, S):
    N, D = x2.shape
    tm = TM_PROJ
    steps_per_batch = S // tm
    const = lambda t: (0, 0)
    return pl.pallas_call(
        _proj_kernel,
        grid=(N // tm,),
        in_specs=[pl.BlockSpec((tm, D), lambda t: (t, 0)),
                  pl.BlockSpec((1, 6, D), lambda t: (t // steps_per_batch, 0, 0)),
                  pl.BlockSpec((1, D), const),
                  pl.BlockSpec(wa.shape, const),
                  pl.BlockSpec(wb.shape, const),
                  pl.BlockSpec(wg.shape, const)],
        out_specs=[pl.BlockSpec((tm, QA_W), lambda t: (t, 0)),
                   pl.BlockSpec((tm, KV_W), lambda t: (t, 0)),
                   pl.BlockSpec((tm, 2 * LRU_WIDTH), lambda t: (t, 0)),
                   pl.BlockSpec((tm, 2 * D), lambda t: (t, 0))],
        out_shape=[jax.ShapeDtypeStruct((N, QA_W), F32),
                   jax.ShapeDtypeStruct((N, KV_W), F32),
                   jax.ShapeDtypeStruct((N, 2 * LRU_WIDTH), F32),
                   jax.ShapeDtypeStruct((N, 2 * D), F32)],
        compiler_params=_params("arbitrary"),
        name="proj",
    )(x2, mod3, norm_w.reshape(1, D), wa, wb, wg)


ATTN_BUCKETS = 4
ROW_FOLD = 64
KEY_ROWS = 256


def _reduce_rows(op, x):
    r = x.shape[0]
    if r > ROW_FOLD and r % ROW_FOLD == 0:
        x = op(x.reshape(r // ROW_FOLD, ROW_FOLD, x.shape[1]), axis=0)
    return op(x, axis=0, keepdims=True)


def _attn_block(s_eff, topk, i, qa_ref, kvq_ref, qnw_ref, o_ref, kn_s, vt_s, ki_s, key_s, bias_s, dist_s, lg_s, ot_s):
    nq = N_HEADS_A * HEAD_DIM_A
    q_t = qa_ref[:, :nq].T
    qi_t = qa_ref[:, nq:].T
    wq_t = kvq_ref[...].T[WI_OFF:WI_OFF + N_IDX_HEADS, :] * (N_IDX_HEADS ** -0.5 * IDX_DIM ** -0.5)

    q_pos = i * Q_BLOCK + lax.broadcasted_iota(I32, (1, Q_BLOCK), 1)
    limit = (q_pos // CHUNK + 1) * CHUNK
    rows_per = KEY_ROWS if s_eff % KEY_ROWS == 0 else Q_BLOCK
    chunks = [slice(c * rows_per, (c + 1) * rows_per) for c in range(s_eff // rows_per)]

    qi = [qi_t[h * IDX_DIM:(h + 1) * IDX_DIM, :].astype(BF16) for h in range(N_IDX_HEADS)]
    for rows in chunks:
        ki = ki_s[rows, :]
        score = jnp.zeros((rows_per, Q_BLOCK), F32)
        for h in range(N_IDX_HEADS):
            score = score + jnp.maximum(_dot(ki, qi[h]), 0.0) * wq_t[h:h + 1, :]
        k_pos = rows.start + lax.broadcasted_iota(I32, (rows_per, Q_BLOCK), 0)
        key_s[rows, :] = jnp.where(k_pos < limit, score, -jnp.inf)
        dist_s[rows, :] = jnp.abs(k_pos - q_pos).astype(F32)

    def count(mask):
        return _reduce_rows(jnp.sum, jnp.where(mask, 1.0, 0.0))

    def ordered_to_float(c):
        s = c ^ INT_MIN
        return pltpu.bitcast(jnp.where(s < 0, INT_MIN | (-s), s), F32)

    def bisect(it, r):
        c = r | jnp.left_shift(jnp.int32(1), 31 - it)
        return jnp.where(count(key_s[:s_eff, :] >= ordered_to_float(c)) >= topk, c, r)

    thr = ordered_to_float(lax.fori_loop(0, 32, bisect, jnp.zeros((1, Q_BLOCK), I32)))

    key = key_s[:s_eff, :]
    need = topk - count(key > thr)

    rr = lax.broadcasted_iota(I32, (LANES, LANES), 0)
    cc = lax.broadcasted_iota(I32, (LANES, LANES), 1)
    tri = jnp.where(cc < rr, 1.0, 0.0).astype(BF16)
    seen = jnp.zeros((1, Q_BLOCK), F32)
    for c in range(s_eff // LANES):
        kc = key[c * LANES:(c + 1) * LANES, :]
        eq = kc == thr
        eqf = jnp.where(eq, 1.0, 0.0)
        rank = _dot(tri, eqf.astype(BF16)) + seen
        seen = seen + jnp.sum(eqf, axis=0, keepdims=True)
        tie_bias = jnp.where(jnp.where(eq, rank, topk) < need, 0.0, -jnp.inf)
        bias = jnp.where(kc > thr, 0.0, tie_bias)
        k_pos = c * LANES + lax.broadcasted_iota(I32, (LANES, Q_BLOCK), 0)
        bias_s[c * LANES:(c + 1) * LANES, :] = jnp.where(k_pos < limit, bias, -jnp.inf)

    qnw = qnw_ref[...]
    for h in range(N_HEADS_A):
        qh = q_t[h * HEAD_DIM_A:(h + 1) * HEAD_DIM_A, :]
        ms = jnp.mean(qh * qh, axis=0, keepdims=True)
        qn = ((qh * lax.rsqrt(ms + EPS) * qnw) * (HEAD_DIM_A ** -0.5)).astype(BF16)
        maxes = []
        for rows in chunks:
            logits = _dot(kn_s[rows, :], qn) - (2.0 ** -(h + 1)) * dist_s[rows, :] + bias_s[rows, :]
            lg_s[h, rows, :] = logits
            maxes.append(_reduce_rows(jnp.max, logits))
        m = functools.reduce(jnp.maximum, maxes)
        l = jnp.zeros((1, Q_BLOCK), F32)
        acc = jnp.zeros((KV_DIM_A, Q_BLOCK), F32)
        for rows in chunks:
            p = jnp.exp(lg_s[h, rows, :] - m)
            l = l + _reduce_rows(jnp.sum, p)
            acc = acc + _dot(vt_s[:, rows], p.astype(BF16))
        ot_s[h * KV_DIM_A:(h + 1) * KV_DIM_A, :] = acc / l
    o_ref[...] = ot_s[...].T


def _attn_kernel(qa_ref, kvq_ref, kv_ref, qnw_ref, knw_ref, o_ref, kn_s, vt_s, ki_s, key_s, bias_s, dist_s, lg_s, ot_s):
    i = pl.program_id(1)
    S = kv_ref.shape[0]
    nb = S // Q_BLOCK
    topk = float(min(TOPK_MAX, S // 4))

    @pl.when(i == 0)
    def _():
        k = kv_ref[:, K_OFF:K_OFF + KV_DIM_A]
        ms = jnp.mean(k * k, axis=-1, keepdims=True)
        kn_s[...] = (k * lax.rsqrt(ms + EPS) * knw_ref[...]).astype(BF16)
        vt_s[...] = kv_ref[:, :LANES].T[V_OFF:V_OFF + KV_DIM_A, :].astype(BF16)
        ki_s[...] = kv_ref[:, KI_OFF:KI_OFF + IDX_DIM].astype(BF16)

    n_buckets = min(ATTN_BUCKETS, nb)
    per = nb // n_buckets
    for j in range(n_buckets):
        pl.when(i // per == j)(functools.partial(
            _attn_block, (j + 1) * per * Q_BLOCK, topk, i, qa_ref, kvq_ref, qnw_ref, o_ref,
            kn_s, vt_s, ki_s, key_s, bias_s, dist_s, lg_s, ot_s))


def _attn(qa, kv, q_norm_w, k_norm_w, B, S):
    N = qa.shape[0]
    nb = S // Q_BLOCK
    return pl.pallas_call(
        _attn_kernel,
        grid=(B, nb),
        in_specs=[pl.BlockSpec((Q_BLOCK, QA_W), lambda b, i: (b * nb + i, 0)),
                  pl.BlockSpec((Q_BLOCK, KV_W), lambda b, i: (b * nb + i, 0)),
                  pl.BlockSpec((S, KV_W), lambda b, i: (b, 0)),
                  pl.BlockSpec((HEAD_DIM_A, 1), lambda b, i: (0, 0)),
                  pl.BlockSpec((1, KV_DIM_A), lambda b, i: (0, 0))],
        out_specs=pl.BlockSpec((Q_BLOCK, ATTN_OUT), lambda b, i: (b * nb + i, 0)),
        out_shape=jax.ShapeDtypeStruct((N, ATTN_OUT), F32),
        scratch_shapes=[pltpu.VMEM((S, KV_DIM_A), BF16),
                        pltpu.VMEM((KV_DIM_A, S), BF16),
                        pltpu.VMEM((S, IDX_DIM), BF16),
                        pltpu.VMEM((S, Q_BLOCK), F32),
                        pltpu.VMEM((S, Q_BLOCK), F32),
                        pltpu.VMEM((S, Q_BLOCK), F32),
                        pltpu.VMEM((N_HEADS_A, S, Q_BLOCK), F32),
                        pltpu.VMEM((ATTN_OUT, Q_BLOCK), F32)],
        compiler_params=_params("arbitrary", "arbitrary"),
        name="attn",
    )(qa, kv, kv, q_norm_w.reshape(HEAD_DIM_A, 1), k_norm_w.reshape(1, KV_DIM_A))


def _shift_rows(x, d, fill, row):
    return jnp.where(row >= d, pltpu.roll(x, d, 0), fill)


def _lru_kernel(xb_ref, gb_ref, cw_ref, cb_ref, wr_ref, br_ref, wi_ref, bi_ref, lam_ref, o_ref):
    x = xb_ref[...]
    S = x.shape[0]
    row = lax.broadcasted_iota(I32, x.shape, 0)
    xc = cb_ref[...] + _shift_rows(x, CONV_WIDTH - 1, 0.0, row) * cw_ref[0:1, :]
    for j in range(1, CONV_WIDTH):
        d = CONV_WIDTH - 1 - j
        xs = x if d == 0 else _shift_rows(x, d, 0.0, row)
        xc = xc + xs * cw_ref[j:j + 1, :]
    xcb = xc.astype(BF16)
    r = jax.nn.sigmoid(_dot(xcb, wr_ref[0]) + br_ref[...])
    ig = jax.nn.sigmoid(_dot(xcb, wi_ref[0]) + bi_ref[...])
    z = -lam_ref[...]
    softplus = jnp.maximum(z, 0.0) + jnp.log1p(jnp.exp(-jnp.abs(z)))
    log_a = -LRU_C * r * softplus
    a = jnp.exp(log_a)
    b = jnp.sqrt(-jnp.tanh(log_a) * (1.0 + a * a)) * (ig * xc)
    d = 1
    while d < S:
        b = a * _shift_rows(b, d, 0.0, row) + b
        if 2 * d < S:
            a = a * _shift_rows(a, d, 1.0, row)
        d *= 2
    g = gb_ref[...]
    gelu = 0.5 * g * (1.0 + jnp.tanh(float(np.sqrt(2.0 / np.pi)) * (g + 0.044715 * (g * g * g))))
    o_ref[...] = b * gelu


def _lru(lru, conv_w, conv_b, wr_bd, b_rec, wi_bd, b_in, lam, B, S):
    N = lru.shape[0]
    nt = LRU_WIDTH // LANES
    vec = lambda b, j: (0, j)
    return pl.pallas_call(
        _lru_kernel,
        grid=(B, nt),
        in_specs=[pl.BlockSpec((S, LANES), lambda b, j: (b, j)),
                  pl.BlockSpec((S, LANES), lambda b, j: (b, nt + j)),
                  pl.BlockSpec((CONV_WIDTH, LANES), vec),
                  pl.BlockSpec((1, LANES), vec),
                  pl.BlockSpec((1, LANES, LANES), lambda b, j: (j, 0, 0)),
                  pl.BlockSpec((1, LANES), vec),
                  pl.BlockSpec((1, LANES, LANES), lambda b, j: (j, 0, 0)),
                  pl.BlockSpec((1, LANES), vec),
                  pl.BlockSpec((1, LANES), vec)],
        out_specs=pl.BlockSpec((S, LANES), lambda b, j: (b, j)),
        out_shape=jax.ShapeDtypeStruct((N, LRU_WIDTH), F32),
        compiler_params=_params("arbitrary", "arbitrary"),
        name="lru",
    )(lru, lru, conv_w, conv_b.reshape(1, -1), wr_bd, b_rec.reshape(1, -1), wi_bd,
      b_in.reshape(1, -1), lam.reshape(1, -1))


def _block_diag_tiles(w):
    per = LANES // LRU_BLOCK_DIM
    nt = LRU_WIDTH // LANES
    w5 = w.reshape(nt, per, LRU_BLOCK_DIM, 1, LRU_BLOCK_DIM)
    eye = jnp.eye(per, dtype=w.dtype).reshape(1, per, 1, per, 1)
    return (w5 * eye).reshape(nt, LANES, LANES)


def _first_index_of_max(vals, lane):
    m = jnp.max(vals, axis=-1, keepdims=True)
    idx = jnp.min(jnp.where(vals == m, lane, 4 * LANES), axis=-1, keepdims=True)
    return m, idx


def _merge_kernel(attn_ref, lru_ref, g_ref, x_ref, mod_ref, wpa_ref, wpb_ref, wo_ref, nw_ref, wr_ref, br_ref,
                  x1_ref, h2_ref, route_ref, cnt_ref, run_s):
    t = pl.program_id(0)
    D = x_ref.shape[1]
    tm = x_ref.shape[0]

    @pl.when(t == 0)
    def _():
        run_s[...] = jnp.zeros_like(run_s)

    y_a = _dot(attn_ref[...].astype(BF16), wpa_ref[...])
    y_b = _dot(lru_ref[...].astype(BF16), wpb_ref[...])
    merged = jax.nn.sigmoid(g_ref[:, :D]) * y_a + jax.nn.sigmoid(g_ref[:, D:]) * y_b
    x1 = x_ref[...] + mod_ref[0, 2:3, :] * _dot(merged.astype(BF16), wo_ref[...])
    x1_ref[...] = x1
    h2 = _rms_mod(x1, nw_ref[...], mod_ref[0, 3:4, :], mod_ref[0, 4:5, :])
    h2_ref[...] = h2

    logits = _dot(h2, wr_ref[...], precision=lax.Precision.HIGHEST) + br_ref[...]
    lane = lax.broadcasted_iota(I32, logits.shape, 1)
    gl = jnp.where(lane < N_GROUPS, logits, -jnp.inf)
    gmax, g_sel = _first_index_of_max(gl, lane)
    g_weight = 1.0 / jnp.sum(jnp.exp(gl - gmax), axis=-1, keepdims=True)
    e_lo = N_GROUPS + g_sel * EXPERTS_PER_GROUP
    el = jnp.where((lane >= e_lo) & (lane < e_lo + EXPERTS_PER_GROUP), logits, -jnp.inf)
    v1, i1 = _first_index_of_max(el, lane)
    el2 = jnp.where(lane == i1, -jnp.inf, el)
    v2, i2 = _first_index_of_max(el2, lane)
    e2x = jnp.exp(v2 - v1)
    w1 = g_weight / (1.0 + e2x)
    w2 = g_weight * e2x / (1.0 + e2x)

    hot1 = lane == i1
    hot2 = lane == i2
    hot = jnp.where(hot1 | hot2, 1.0, 0.0)
    rr = lax.broadcasted_iota(I32, (tm, tm), 0)
    cc = lax.broadcasted_iota(I32, (tm, tm), 1)
    lower = jnp.where(cc < rr, 1.0, 0.0).astype(BF16)
    before = _dot(lower, hot.astype(BF16)) + run_s[...]
    rank1 = jnp.sum(jnp.where(hot1, before, 0.0), axis=-1, keepdims=True)
    rank2 = jnp.sum(jnp.where(hot2, before, 0.0), axis=-1, keepdims=True)
    run_s[...] = run_s[...] + jnp.sum(hot, axis=0, keepdims=True)
    cnt_ref[...] = run_s[...]

    rec = jnp.zeros(logits.shape, F32)
    for col, val in ((R_E1, (i1 - N_GROUPS).astype(F32)), (R_E2, (i2 - N_GROUPS).astype(F32)),
                     (R_W1, w1), (R_W2, w2), (R_RANK1, rank1), (R_RANK2, rank2)):
        rec = jnp.where(lane == col, val, rec)
    route_ref[...] = rec


def _merge(attn, lru_out, gates, x2, mod3, wpa, wpb, wo, norm_w, w_router, b_router, S):
    N, D = x2.shape
    tm = TM_MERGE
    steps_per_batch = S // tm
    const = lambda t: (0, 0)
    rows = lambda t: (t, 0)
    return pl.pallas_call(
        _merge_kernel,
        grid=(N // tm,),
        in_specs=[pl.BlockSpec((tm, ATTN_OUT), rows),
                  pl.BlockSpec((tm, LRU_WIDTH), rows),
                  pl.BlockSpec((tm, 2 * D), rows),
                  pl.BlockSpec((tm, D), rows),
                  pl.BlockSpec((1, 6, D), lambda t: (t // steps_per_batch, 0, 0)),
                  pl.BlockSpec(wpa.shape, const),
                  pl.BlockSpec(wpb.shape, const),
                  pl.BlockSpec(wo.shape, const),
                  pl.BlockSpec((1, D), const),
                  pl.BlockSpec(w_router.shape, const),
                  pl.BlockSpec((1, LANES), const)],
        out_specs=[pl.BlockSpec((tm, D), rows),
                   pl.BlockSpec((tm, D), rows),
                   pl.BlockSpec((tm, LANES), rows),
                   pl.BlockSpec((1, LANES), const)],
        out_shape=[jax.ShapeDtypeStruct((N, D), F32),
                   jax.ShapeDtypeStruct((N, D), F32),
                   jax.ShapeDtypeStruct((N, LANES), F32),
                   jax.ShapeDtypeStruct((1, LANES), F32)],
        scratch_shapes=[pltpu.VMEM((1, LANES), F32)],
        compiler_params=_params("arbitrary"),
        name="merge",
    )(attn, lru_out, gates, x2, mod3, wpa, wpb, wo, norm_w.reshape(1, D), w_router, b_router)


def _scatter_kernel(pos1_ref, pos2_ref, h2_ref, hs_in_ref, hs_ref, sem):
    del hs_in_ref
    base = pl.program_id(0) * TM_ROWS

    def row_copy(k, pos_ref):
        return pltpu.make_async_copy(h2_ref.at[pl.ds(k, 1)], hs_ref.at[pl.ds(pos_ref[base + k], 1)], sem)

    def start(k, carry):
        row_copy(k, pos1_ref).start()
        row_copy(k, pos2_ref).start()
        return carry

    def wait(k, carry):
        row_copy(k, pos1_ref).wait()
        row_copy(k, pos2_ref).wait()
        return carry

    lax.fori_loop(0, TM_ROWS, start, 0)
    lax.fori_loop(0, TM_ROWS, wait, 0)


def _scatter(pos1, pos2, h2, n_sorted):
    N, D = h2.shape
    grid_spec = pltpu.PrefetchScalarGridSpec(
        num_scalar_prefetch=2,
        grid=(N // TM_ROWS,),
        in_specs=[pl.BlockSpec((TM_ROWS, D), lambda t, p1, p2: (t, 0)), pl.BlockSpec(memory_space=pl.ANY)],
        out_specs=pl.BlockSpec(memory_space=pl.ANY),
        scratch_shapes=[pltpu.SemaphoreType.DMA(())],
    )
    return pl.pallas_call(
        _scatter_kernel,
        grid_spec=grid_spec,
        out_shape=jax.ShapeDtypeStruct((n_sorted, D), h2.dtype),
        input_output_aliases={3: 0},
        compiler_params=_params("arbitrary"),
        name="scatter",
    )(pos1, pos2, h2, jnp.zeros((n_sorted, D), h2.dtype))


def _moe_kernel(te_ref, nt_ref, hs_ref, w1_ref, w3_ref, w2_ref, ys_ref):
    i = pl.program_id(0)

    @pl.when(i < nt_ref[0])
    def _():
        hb = hs_ref[...].astype(BF16)
        a = _dot(hb, w1_ref[0])
        act = (a * jax.nn.sigmoid(a)) * _dot(hb, w3_ref[0])
        ys_ref[...] = _dot(act.astype(BF16), w2_ref[0])

    @pl.when(i >= nt_ref[0])
    def _():
        ys_ref[...] = jnp.zeros_like(ys_ref)


def _moe(tile_expert, n_tiles_used, hs, w1, w3, w2):
    P, D = hs.shape
    tm = TM_MOE
    F = w1.shape[2]
    grid_spec = pltpu.PrefetchScalarGridSpec(
        num_scalar_prefetch=2,
        grid=(P // tm,),
        in_specs=[pl.BlockSpec((tm, D), lambda i, te, nt: (i, 0)),
                  pl.BlockSpec((1, D, F), lambda i, te, nt: (te[i], 0, 0)),
                  pl.BlockSpec((1, D, F), lambda i, te, nt: (te[i], 0, 0)),
                  pl.BlockSpec((1, F, D), lambda i, te, nt: (te[i], 0, 0))],
        out_specs=pl.BlockSpec((tm, D), lambda i, te, nt: (i, 0)),
    )
    return pl.pallas_call(
        _moe_kernel,
        grid_spec=grid_spec,
        out_shape=jax.ShapeDtypeStruct((P, D), F32),
        compiler_params=_params("arbitrary"),
        name="moe",
    )(tile_expert, n_tiles_used, hs, w1, w3, w2)


def _combine_kernel(pos1_ref, pos2_ref, ys_ref, x1_ref, route_ref, mod_ref, o_ref, y1_s, y2_s, sem):
    base = pl.program_id(0) * TM_ROWS

    def row_copy(k, pos_ref, dst):
        return pltpu.make_async_copy(ys_ref.at[pl.ds(pos_ref[base + k], 1)], dst.at[pl.ds(k, 1)], sem)

    def start(k, carry):
        row_copy(k, pos1_ref, y1_s).start()
        row_copy(k, pos2_ref, y2_s).start()
        return carry

    def wait(k, carry):
        row_copy(k, pos1_ref, y1_s).wait()
        row_copy(k, pos2_ref, y2_s).wait()
        return carry

    lax.fori_loop(0, TM_ROWS, start, 0)
    lax.fori_loop(0, TM_ROWS, wait, 0)
    w1 = route_ref[:, R_W1:R_W1 + 1]
    w2 = route_ref[:, R_W2:R_W2 + 1]
    y = w1 * y1_s[...] + w2 * y2_s[...]
    o_ref[...] = x1_ref[...] + mod_ref[0, 5:6, :] * y


def _combine(pos1, pos2, ys, x1, route, mod3, S):
    N, D = x1.shape
    tm = TM_ROWS
    steps_per_batch = S // tm
    grid_spec = pltpu.PrefetchScalarGridSpec(
        num_scalar_prefetch=2,
        grid=(N // tm,),
        in_specs=[pl.BlockSpec(memory_space=pl.ANY),
                  pl.BlockSpec((tm, D), lambda t, p1, p2: (t, 0)),
                  pl.BlockSpec((tm, LANES), lambda t, p1, p2: (t, 0)),
                  pl.BlockSpec((1, 6, D), lambda t, p1, p2: (t // steps_per_batch, 0, 0))],
        out_specs=pl.BlockSpec((tm, D), lambda t, p1, p2: (t, 0)),
        scratch_shapes=[pltpu.VMEM((tm, D), F32), pltpu.VMEM((tm, D), F32), pltpu.SemaphoreType.DMA(())],
    )
    return pl.pallas_call(
        _combine_kernel,
        grid_spec=grid_spec,
        out_shape=jax.ShapeDtypeStruct((N, D), F32),
        compiler_params=_params("arbitrary"),
        name="combine",
    )(pos1, pos2, ys, x1, route, mod3)


def _layer(x2, mod3, B, S, norm_mix_w, w_in, q_norm_w, k_norm_w, conv_w, conv_b, w_rec_gate, b_rec_gate,
           w_in_gate, b_in_gate, lru_lambda, w_proj_a, w_proj_b, w_out, norm_ffn_w, w_group, b_group,
           w_expert_router, b_expert_router, w1, w3, w2):
    N, D = x2.shape
    nq = N_HEADS_A * HEAD_DIM_A
    o_k = nq
    o_v = o_k + KV_DIM_A
    o_qi = o_v + KV_DIM_A
    o_ki = o_qi + N_IDX_HEADS * IDX_DIM
    o_wi = o_ki + IDX_DIM
    o_lx = o_wi + N_IDX_HEADS
    o_g = o_lx + 2 * LRU_WIDTH
    pad = jnp.zeros((D, KV_W - (WI_OFF + N_IDX_HEADS)), w_in.dtype)
    wa = jnp.concatenate([w_in[:, :nq], w_in[:, o_qi:o_ki], w_in[:, o_k:o_qi], w_in[:, o_ki:o_lx], pad],
                         axis=1).astype(BF16)
    wb = w_in[:, o_lx:o_g].astype(BF16)
    wg = w_in[:, o_g:].astype(BF16)

    qa, kv, lru, gates = _proj(x2, mod3, norm_mix_w, wa, wb, wg, S)
    attn = _attn(qa, kv, q_norm_w, k_norm_w, B, S)
    lru_out = _lru(lru, conv_w, conv_b, _block_diag_tiles(w_rec_gate).astype(BF16), b_rec_gate,
                   _block_diag_tiles(w_in_gate).astype(BF16), b_in_gate, lru_lambda, B, S)

    n_r = N_GROUPS + N_EXPERTS
    w_router = jnp.concatenate([w_group, w_expert_router, jnp.zeros((D, LANES - n_r), F32)], axis=1)
    b_router = jnp.concatenate([b_group, b_expert_router, jnp.zeros((LANES - n_r,), F32)]).reshape(1, LANES)
    x1, h2, route, counts = _merge(attn, lru_out, gates, x2, mod3, w_proj_a.astype(BF16), w_proj_b.astype(BF16),
                                   w_out.astype(BF16), norm_ffn_w, w_router, b_router, S)

    cnt = counts[0, N_GROUPS:n_r].astype(I32)
    seg_tiles = (cnt + TM_MOE - 1) // TM_MOE
    tile_end = jnp.cumsum(seg_tiles)
    seg_off = (tile_end - seg_tiles) * TM_MOE
    n_tiles = (2 * N) // TM_MOE + N_EXPERTS
    tile_ids = jnp.arange(n_tiles, dtype=I32)
    tile_expert = jnp.minimum(jnp.sum((tile_end[None, :] <= tile_ids[:, None]).astype(I32), axis=1), N_EXPERTS - 1)
    e1 = route[:, R_E1].astype(I32)
    e2 = route[:, R_E2].astype(I32)
    pos1 = seg_off[e1] + route[:, R_RANK1].astype(I32)
    pos2 = seg_off[e2] + route[:, R_RANK2].astype(I32)

    hs = _scatter(pos1, pos2, h2, n_tiles * TM_MOE)
    ys = _moe(tile_expert, tile_end[-1:].astype(I32), hs, w1.astype(BF16), w3.astype(BF16), w2.astype(BF16))
    return _combine(pos1, pos2, ys, x1, route, mod3, S)


def kernel(x, c, ada_w, ada_b, norm_mix_w, w_in, q_norm_w, k_norm_w, conv_w, conv_b, w_rec_gate, b_rec_gate,
           w_in_gate, b_in_gate, lru_lambda, w_proj_a, w_proj_b, w_out, norm_ffn_w, w_group, b_group,
           w_expert_router, b_expert_router, w1, w3, w2):
    B, S, D = x.shape
    x2 = x.reshape(B * S, D)
    for l in range(ada_w.shape[0]):
        mod3 = _ada(c, ada_w[l], ada_b[l]).reshape(B, 6, D)
        x2 = _layer(x2, mod3, B, S, norm_mix_w[l], w_in[l], q_norm_w[l], k_norm_w[l], conv_w[l], conv_b[l],
                    w_rec_gate[l], b_rec_gate[l], w_in_gate[l], b_in_gate[l], lru_lambda[l], w_proj_a[l],
                    w_proj_b[l], w_out[l], norm_ffn_w[l], w_group[l], b_group[l], w_expert_router[l],
                    b_expert_router[l], w1[l], w3[l], w2[l])
    return x2.reshape(B, S, D)
```

```python
import functools

import jax
import jax.numpy as jnp
import numpy as np
from jax import lax
from jax.experimental import pallas as pl
from jax.experimental.pallas import tpu as pltpu

F32 = jnp.float32
BF16 = jnp.bfloat16
I32 = jnp.int32

EPS = 1e-6
CHUNK = 64
Q_BLOCK = 128
N_HEADS_A = 8
HEAD_DIM_A = 64
KV_DIM_A = 64
ATTN_OUT = N_HEADS_A * KV_DIM_A
N_IDX_HEADS = 4
IDX_DIM = 64
TOPK_MAX = 256
LRU_WIDTH = 512
LRU_BLOCKS = 8
LRU_BLOCK_DIM = LRU_WIDTH // LRU_BLOCKS
CONV_WIDTH = 4
LRU_C = 8.0
N_GROUPS = 4
EXPERTS_PER_GROUP = 8
N_EXPERTS = N_GROUPS * EXPERTS_PER_GROUP
D_FF_EXPERT = 256

LANES = 128
INT_MIN = -2 ** 31
VMEM_LIMIT = 56 * 1024 * 1024

QA_W = N_HEADS_A * HEAD_DIM_A + N_IDX_HEADS * IDX_DIM
KV_W = 256
K_OFF, V_OFF, KI_OFF, WI_OFF = 0, 64, 128, 192

TM_PROJ = 256

TM_SORT = 512
GRAN = 16
TM_MOE = 256
GRAN_PER_TILE = TM_MOE // GRAN
LOCAL_ROWS = -(-(2 * TM_SORT + N_EXPERTS * (GRAN - 1)) // LANES) * LANES
LOCAL_GRAN = LOCAL_ROWS // GRAN
AUX_W = LANES
W_PIECES = 3

R_LP1, R_LP2 = 0, 1


def _dot(a, b, **kw):
    return jnp.dot(a, b, preferred_element_type=F32, **kw)


def _params(*sem):
    return pltpu.CompilerParams(dimension_semantics=sem, vmem_limit_bytes=VMEM_LIMIT)


def _ada_kernel(c_ref, w_ref, b_ref, o_ref):
    c = c_ref[...]
    cond = c * jax.nn.sigmoid(c)
    o_ref[...] = _dot(cond, w_ref[...], precision=lax.Precision.HIGHEST) + b_ref[...]


def _ada(c, ada_w, ada_b):
    B, D = c.shape
    n_out = ada_w.shape[1]
    tn = 1024
    return pl.pallas_call(
        _ada_kernel,
        grid=(n_out // tn,),
        in_specs=[pl.BlockSpec((B, D), lambda j: (0, 0)),
                  pl.BlockSpec((D, tn), lambda j: (0, j)),
                  pl.BlockSpec((1, tn), lambda j: (0, j))],
        out_specs=pl.BlockSpec((B, tn), lambda j: (0, j)),
        out_shape=jax.ShapeDtypeStruct((B, n_out), F32),
        compiler_params=_params("arbitrary"),
        name="ada",
    )(c, ada_w, ada_b.reshape(1, n_out))


def _rms_mod(x, w, shift, scale):
    ms = jnp.mean(x * x, axis=-1, keepdims=True)
    y = x * lax.rsqrt(ms + EPS) * w
    return y * (1.0 + scale) + shift


def _proj_kernel(x_ref, mod_ref, nw_ref, wa_ref, wb_ref, wg_ref, oqa_ref, okv_ref, olru_ref, og_ref):
    h = _rms_mod(x_ref[...], nw_ref[...], mod_ref[0, 0:1, :], mod_ref[0, 1:2, :])
    hb = h.astype(BF16)
    pa = _dot(hb, wa_ref[...])
    oqa_ref[...] = pa[:, :QA_W]
    okv_ref[...] = pa[:, QA_W:]
    olru_ref[...] = _dot(hb, wb_ref[...])
    og_ref[...] = _dot(hb, wg_ref[...])


def _proj(x2, mod3, norm_w, wa, wb, wg, S):
    N, D = x2.shape
    tm = TM_PROJ
    steps_per_batch = S // tm
    const = lambda t: (0, 0)
    return pl.pallas_call(
        _proj_kernel,
        grid=(N // tm,),
        in_specs=[pl.BlockSpec((tm, D), lambda t: (t, 0)),
                  pl.BlockSpec((1, 6, D), lambda t: (t // steps_per_batch, 0, 0)),
                  pl.BlockSpec((1, D), const),
                  pl.BlockSpec(wa.shape, const),
                  pl.BlockSpec(wb.shape, const),
                  pl.BlockSpec(wg.shape, const)],
        out_specs=[pl.BlockSpec((tm, QA_W), lambda t: (t, 0)),
                   pl.BlockSpec((tm, KV_W), lambda t: (t, 0)),
                   pl.BlockSpec((tm, 2 * LRU_WIDTH), lambda t: (t, 0)),
                   pl.BlockSpec((tm, 2 * D), lambda t: (t, 0))],
        out_shape=[jax.ShapeDtypeStruct((N, QA_W), F32),
                   jax.ShapeDtypeStruct((N, KV_W), F32),
                   jax.ShapeDtypeStruct((N, 2 * LRU_WIDTH), F32),
                   jax.ShapeDtypeStruct((N, 2 * D), F32)],
        compiler_params=_params("arbitrary"),
        name="proj",
    )(x2, mod3, norm_w.reshape(1, D), wa, wb, wg)


ATTN_BUCKETS = 4
ROW_FOLD = 64
KEY_ROWS = 256


def _reduce_rows(op, x):
    r = x.shape[0]
    if r > ROW_FOLD and r % ROW_FOLD == 0:
        x = op(x.reshape(r // ROW_FOLD, ROW_FOLD, x.shape[1]), axis=0)
    return op(x, axis=0, keepdims=True)


def _attn_block(s_eff, topk, i, qa_ref, kvq_ref, qnw_ref, o_ref, kn_s, vt_s, ki_s, key_s, bias_s, dist_s, lg_s, ot_s):
    nq = N_HEADS_A * HEAD_DIM_A
    q_t = qa_ref[:, :nq].T
    qi_t = qa_ref[:, nq:].T
    wq_t = kvq_ref[...].T[WI_OFF:WI_OFF + N_IDX_HEADS, :] * (N_IDX_HEADS ** -0.5 * IDX_DIM ** -0.5)

    q_pos = i * Q_BLOCK + lax.broadcasted_iota(I32, (1, Q_BLOCK), 1)
    limit = (q_pos // CHUNK + 1) * CHUNK
    rows_per = KEY_ROWS if s_eff % KEY_ROWS == 0 else Q_BLOCK
    chunks = [slice(c * rows_per, (c + 1) * rows_per) for c in range(s_eff // rows_per)]

    qi = [qi_t[h * IDX_DIM:(h + 1) * IDX_DIM, :].astype(BF16) for h in range(N_IDX_HEADS)]
    for rows in chunks:
        ki = ki_s[rows, :]
        score = jnp.zeros((rows_per, Q_BLOCK), F32)
        for h in range(N_IDX_HEADS):
            score = score + jnp.maximum(_dot(ki, qi[h]), 0.0) * wq_t[h:h + 1, :]
        k_pos = rows.start + lax.broadcasted_iota(I32, (rows_per, Q_BLOCK), 0)
        key_s[rows, :] = jnp.where(k_pos < limit, score, -jnp.inf)
        dist_s[rows, :] = jnp.abs(k_pos - q_pos).astype(F32)

    def count(mask):
        return _reduce_rows(jnp.sum, jnp.where(mask, 1.0, 0.0))

    def ordered_to_float(c):
        s = c ^ INT_MIN
        return pltpu.bitcast(jnp.where(s < 0, INT_MIN | (-s), s), F32)

    def bisect(it, r):
        c = r | jnp.left_shift(jnp.int32(1), 31 - it)
        return jnp.where(count(key_s[:s_eff, :] >= ordered_to_float(c)) >= topk, c, r)

    thr = ordered_to_float(lax.fori_loop(0, 32, bisect, jnp.zeros((1, Q_BLOCK), I32)))

    key = key_s[:s_eff, :]
    need = topk - count(key > thr)

    rr = lax.broadcasted_iota(I32, (LANES, LANES), 0)
    cc = lax.broadcasted_iota(I32, (LANES, LANES), 1)
    tri = jnp.where(cc < rr, 1.0, 0.0).astype(BF16)
    seen = jnp.zeros((1, Q_BLOCK), F32)
    for c in range(s_eff // LANES):
        kc = key[c * LANES:(c + 1) * LANES, :]
        eq = kc == thr
        eqf = jnp.where(eq, 1.0, 0.0)
        rank = _dot(tri, eqf.astype(BF16)) + seen
        seen = seen + jnp.sum(eqf, axis=0, keepdims=True)
        tie_bias = jnp.where(jnp.where(eq, rank, topk) < need, 0.0, -jnp.inf)
        bias = jnp.where(kc > thr, 0.0, tie_bias)
        k_pos = c * LANES + lax.broadcasted_iota(I32, (LANES, Q_BLOCK), 0)
        bias_s[c * LANES:(c + 1) * LANES, :] = jnp.where(k_pos < limit, bias, -jnp.inf)

    qnw = qnw_ref[...]
    for h in range(N_HEADS_A):
        qh = q_t[h * HEAD_DIM_A:(h + 1) * HEAD_DIM_A, :]
        ms = jnp.mean(qh * qh, axis=0, keepdims=True)
        qn = ((qh * lax.rsqrt(ms + EPS) * qnw) * (HEAD_DIM_A ** -0.5)).astype(BF16)
        maxes = []
        for rows in chunks:
            logits = _dot(kn_s[rows, :], qn) - (2.0 ** -(h + 1)) * dist_s[rows, :] + bias_s[rows, :]
            lg_s[h, rows, :] = logits
            maxes.append(_reduce_rows(jnp.max, logits))
        m = functools.reduce(jnp.maximum, maxes)
        l = jnp.zeros((1, Q_BLOCK), F32)
        acc = jnp.zeros((KV_DIM_A, Q_BLOCK), F32)
        for rows in chunks:
            p = jnp.exp(lg_s[h, rows, :] - m)
            l = l + _reduce_rows(jnp.sum, p)
            acc = acc + _dot(vt_s[:, rows], p.astype(BF16))
        ot_s[h * KV_DIM_A:(h + 1) * KV_DIM_A, :] = acc / l
    o_ref[...] = ot_s[...].T


def _attn_kernel(qa_ref, kvq_ref, kv_ref, qnw_ref, knw_ref, o_ref, kn_s, vt_s, ki_s, key_s, bias_s, dist_s, lg_s, ot_s):
    i = pl.program_id(1)
    S = kv_ref.shape[0]
    nb = S // Q_BLOCK
    topk = float(min(TOPK_MAX, S // 4))

    @pl.when(i == 0)
    def _():
        k = kv_ref[:, K_OFF:K_OFF + KV_DIM_A]
        ms = jnp.mean(k * k, axis=-1, keepdims=True)
        kn_s[...] = (k * lax.rsqrt(ms + EPS) * knw_ref[...]).astype(BF16)
        vt_s[...] = kv_ref[:, :LANES].T[V_OFF:V_OFF + KV_DIM_A, :].astype(BF16)
        ki_s[...] = kv_ref[:, KI_OFF:KI_OFF + IDX_DIM].astype(BF16)

    n_buckets = min(ATTN_BUCKETS, nb)
    per = nb // n_buckets
    for j in range(n_buckets):
        pl.when(i // per == j)(functools.partial(
            _attn_block, (j + 1) * per * Q_BLOCK, topk, i, qa_ref, kvq_ref, qnw_ref, o_ref,
            kn_s, vt_s, ki_s, key_s, bias_s, dist_s, lg_s, ot_s))


def _attn(qa, kv, q_norm_w, k_norm_w, B, S):
    N = qa.shape[0]
    nb = S // Q_BLOCK
    return pl.pallas_call(
        _attn_kernel,
        grid=(B, nb),
        in_specs=[pl.BlockSpec((Q_BLOCK, QA_W), lambda b, i: (b * nb + i, 0)),
                  pl.BlockSpec((Q_BLOCK, KV_W), lambda b, i: (b * nb + i, 0)),
                  pl.BlockSpec((S, KV_W), lambda b, i: (b, 0)),
                  pl.BlockSpec((HEAD_DIM_A, 1), lambda b, i: (0, 0)),
                  pl.BlockSpec((1, KV_DIM_A), lambda b, i: (0, 0))],
        out_specs=pl.BlockSpec((Q_BLOCK, ATTN_OUT), lambda b, i: (b * nb + i, 0)),
        out_shape=jax.ShapeDtypeStruct((N, ATTN_OUT), F32),
        scratch_shapes=[pltpu.VMEM((S, KV_DIM_A), BF16),
                        pltpu.VMEM((KV_DIM_A, S), BF16),
                        pltpu.VMEM((S, IDX_DIM), BF16),
                        pltpu.VMEM((S, Q_BLOCK), F32),
                        pltpu.VMEM((S, Q_BLOCK), F32),
                        pltpu.VMEM((S, Q_BLOCK), F32),
                        pltpu.VMEM((N_HEADS_A, S, Q_BLOCK), F32),
                        pltpu.VMEM((ATTN_OUT, Q_BLOCK), F32)],
        compiler_params=_params("arbitrary", "arbitrary"),
        name="attn",
    )(qa, kv, kv, q_norm_w.reshape(HEAD_DIM_A, 1), k_norm_w.reshape(1, KV_DIM_A))


def _shift_rows(x, d, fill, row):
    return jnp.where(row >= d, pltpu.roll(x, d, 0), fill)


def _lru_kernel(xb_ref, gb_ref, cw_ref, cb_ref, wr_ref, br_ref, wi_ref, bi_ref, lam_ref, o_ref):
    x = xb_ref[...]
    S = x.shape[0]
    row = lax.broadcasted_iota(I32, x.shape, 0)
    xc = cb_ref[...] + _shift_rows(x, CONV_WIDTH - 1, 0.0, row) * cw_ref[0:1, :]
    for j in range(1, CONV_WIDTH):
        d = CONV_WIDTH - 1 - j
        xs = x if d == 0 else _shift_rows(x, d, 0.0, row)
        xc = xc + xs * cw_ref[j:j + 1, :]
    xcb = xc.astype(BF16)
    r = jax.nn.sigmoid(_dot(xcb, wr_ref[0]) + br_ref[...])
    ig = jax.nn.sigmoid(_dot(xcb, wi_ref[0]) + bi_ref[...])
    z = -lam_ref[...]
    softplus = jnp.maximum(z, 0.0) + jnp.log1p(jnp.exp(-jnp.abs(z)))
    log_a = -LRU_C * r * softplus
    a = jnp.exp(log_a)
    b = jnp.sqrt(-jnp.tanh(log_a) * (1.0 + a * a)) * (ig * xc)
    d = 1
    while d < S:
        b = a * _shift_rows(b, d, 0.0, row) + b
        if 2 * d < S:
            a = a * _shift_rows(a, d, 1.0, row)
        d *= 2
    g = gb_ref[...]
    gelu = 0.5 * g * (1.0 + jnp.tanh(float(np.sqrt(2.0 / np.pi)) * (g + 0.044715 * (g * g * g))))
    o_ref[...] = b * gelu


def _lru(lru, conv_w, conv_b, wr_bd, b_rec, wi_bd, b_in, lam, B, S):
    N = lru.shape[0]
    nt = LRU_WIDTH // LANES
    vec = lambda b, j: (0, j)
    return pl.pallas_call(
        _lru_kernel,
        grid=(B, nt),
        in_specs=[pl.BlockSpec((S, LANES), lambda b, j: (b, j)),
                  pl.BlockSpec((S, LANES), lambda b, j: (b, nt + j)),
                  pl.BlockSpec((CONV_WIDTH, LANES), vec),
                  pl.BlockSpec((1, LANES), vec),
                  pl.BlockSpec((1, LANES, LANES), lambda b, j: (j, 0, 0)),
                  pl.BlockSpec((1, LANES), vec),
                  pl.BlockSpec((1, LANES, LANES), lambda b, j: (j, 0, 0)),
                  pl.BlockSpec((1, LANES), vec),
                  pl.BlockSpec((1, LANES), vec)],
        out_specs=pl.BlockSpec((S, LANES), lambda b, j: (b, j)),
        out_shape=jax.ShapeDtypeStruct((N, LRU_WIDTH), F32),
        compiler_params=_params("arbitrary", "arbitrary"),
        name="lru",
    )(lru, lru, conv_w, conv_b.reshape(1, -1), wr_bd, b_rec.reshape(1, -1), wi_bd,
      b_in.reshape(1, -1), lam.reshape(1, -1))


def _block_diag_tiles(w):
    per = LANES // LRU_BLOCK_DIM
    nt = LRU_WIDTH // LANES
    w5 = w.reshape(nt, per, LRU_BLOCK_DIM, 1, LRU_BLOCK_DIM)
    eye = jnp.eye(per, dtype=w.dtype).reshape(1, per, 1, per, 1)
    return (w5 * eye).reshape(nt, LANES, LANES)


def _first_index_of_max(vals, lane):
    m = jnp.max(vals, axis=-1, keepdims=True)
    idx = jnp.min(jnp.where(vals == m, lane, 4 * LANES), axis=-1, keepdims=True)
    return m, idx


def _weight_pieces(w, lane):
    out = jnp.zeros(lane.shape, F32)
    rest = w
    for k in range(W_PIECES):
        piece = rest.astype(BF16).astype(F32)
        out = jnp.where(lane == k, piece, out)
        rest = rest - piece
    return out.astype(BF16)


def _one_hot_rows(n_rows, pos_row):
    rows = lax.broadcasted_iota(I32, (n_rows, pos_row.shape[1]), 0)
    return jnp.where(rows == pos_row, 1.0, 0.0).astype(BF16)


def _merge_kernel(attn_ref, lru_ref, g_ref, x_ref, mod_ref, wpa_ref, wpb_ref, wo_ref, nw_ref, wrh_ref, wrl_ref,
                  br_ref, x1_ref, hsl_ref, route_ref, gc_ref):
    D = x_ref.shape[1]
    tm = x_ref.shape[0]

    y_a = _dot(attn_ref[...].astype(BF16), wpa_ref[...])
    y_b = _dot(lru_ref[...].astype(BF16), wpb_ref[...])
    merged = jax.nn.sigmoid(g_ref[:, :D]) * y_a + jax.nn.sigmoid(g_ref[:, D:]) * y_b
    x1 = x_ref[...] + mod_ref[0, 2:3, :] * _dot(merged.astype(BF16), wo_ref[...])
    x1_ref[...] = x1
    h2 = _rms_mod(x1, nw_ref[...], mod_ref[0, 3:4, :], mod_ref[0, 4:5, :])
    h2b = h2.astype(BF16)

    h2l = (h2 - h2b.astype(F32)).astype(BF16)
    wrh = wrh_ref[...]
    logits = _dot(h2b, wrh) + (_dot(h2l, wrh) + _dot(h2b, wrl_ref[...])) + br_ref[...]
    lane = lax.broadcasted_iota(I32, logits.shape, 1)
    gl = jnp.where(lane < N_GROUPS, logits, -jnp.inf)
    gmax, g_sel = _first_index_of_max(gl, lane)
    g_weight = 1.0 / jnp.sum(jnp.exp(gl - gmax), axis=-1, keepdims=True)
    e_lo = N_GROUPS + g_sel * EXPERTS_PER_GROUP
    el = jnp.where((lane >= e_lo) & (lane < e_lo + EXPERTS_PER_GROUP), logits, -jnp.inf)
    v1, i1 = _first_index_of_max(el, lane)
    el2 = jnp.where(lane == i1, -jnp.inf, el)
    v2, i2 = _first_index_of_max(el2, lane)
    e2x = jnp.exp(v2 - v1)
    w1 = g_weight / (1.0 + e2x)
    w2 = g_weight * e2x / (1.0 + e2x)

    hot1 = lane == i1
    hot2 = lane == i2
    hot = jnp.where(hot1, 1.0, jnp.where(hot2, 1.0, 0.0))
    gcnt = jnp.floor((jnp.sum(hot, axis=0, keepdims=True) + (GRAN - 1.0)) * (1.0 / GRAN))
    gc_ref[0] = gcnt
    rr = lax.broadcasted_iota(I32, (LANES, LANES), 0)
    cc = lax.broadcasted_iota(I32, (LANES, LANES), 1)
    upper = jnp.where(rr < cc, 1.0, 0.0).astype(BF16)
    run_start = GRAN * _dot(jnp.broadcast_to(gcnt, (8, LANES)).astype(BF16), upper)[0:1, :]
    rr = lax.broadcasted_iota(I32, (tm, tm), 0)
    cc = lax.broadcasted_iota(I32, (tm, tm), 1)
    lower = jnp.where(cc < rr, 1.0, 0.0).astype(BF16)
    pos = _dot(lower, hot.astype(BF16)) + run_start
    pos1 = jnp.where(hot1, pos, 0.0)
    pos2 = jnp.where(hot2, pos, 0.0)
    lp1 = jnp.sum(pos1, axis=-1, keepdims=True)
    lp2 = jnp.sum(pos2, axis=-1, keepdims=True)
    route_ref[...] = jnp.where(lane == R_LP1, lp1, jnp.where(lane == R_LP2, lp2, 0.0))

    p1 = _one_hot_rows(LOCAL_ROWS, jnp.sum(pos1.T, axis=0, keepdims=True).astype(I32))
    p2 = _one_hot_rows(LOCAL_ROWS, jnp.sum(pos2.T, axis=0, keepdims=True).astype(I32))
    hsl_ref[:, :D] = _dot(p1 + p2, h2b).astype(BF16)
    aux = _dot(p1, _weight_pieces(w1, lane)) + _dot(p2, _weight_pieces(w2, lane))
    hsl_ref[:, D:] = aux.astype(BF16)


def _merge(attn, lru_out, gates, x2, mod3, wpa, wpb, wo, norm_w, wr_hi, wr_lo, b_router, S):
    N, D = x2.shape
    tm = TM_SORT
    n_tiles = N // tm
    steps_per_batch = S // tm
    const = lambda t: (0, 0)
    rows = lambda t: (t, 0)
    return pl.pallas_call(
        _merge_kernel,
        grid=(n_tiles,),
        in_specs=[pl.BlockSpec((tm, ATTN_OUT), rows),
                  pl.BlockSpec((tm, LRU_WIDTH), rows),
                  pl.BlockSpec((tm, 2 * D), rows),
                  pl.BlockSpec((tm, D), rows),
                  pl.BlockSpec((1, 6, D), lambda t: (t // steps_per_batch, 0, 0)),
                  pl.BlockSpec(wpa.shape, const),
                  pl.BlockSpec(wpb.shape, const),
                  pl.BlockSpec(wo.shape, const),
                  pl.BlockSpec((1, D), const),
                  pl.BlockSpec(wr_hi.shape, const),
                  pl.BlockSpec(wr_lo.shape, const),
                  pl.BlockSpec((1, LANES), const)],
        out_specs=[pl.BlockSpec((tm, D), rows),
                   pl.BlockSpec((LOCAL_ROWS, D + AUX_W), rows),
                   pl.BlockSpec((tm, LANES), rows),
                   pl.BlockSpec((1, 1, LANES), lambda t: (t, 0, 0))],
        out_shape=[jax.ShapeDtypeStruct((N, D), F32),
                   jax.ShapeDtypeStruct((n_tiles * LOCAL_ROWS, D + AUX_W), BF16),
                   jax.ShapeDtypeStruct((N, LANES), F32),
                   jax.ShapeDtypeStruct((n_tiles, 1, LANES), F32)],
        compiler_params=_params("arbitrary"),
        name="merge",
    )(attn, lru_out, gates, x2, mod3, wpa, wpb, wo, norm_w.reshape(1, D), wr_hi, wr_lo, b_router)


def _plan_sizes(n_tiles):
    max_gran = n_tiles * LOCAL_GRAN + N_EXPERTS * (GRAN_PER_TILE - 1)
    max_row_tiles = -(-max_gran // GRAN_PER_TILE)
    return max_row_tiles * GRAN_PER_TILE, max_row_tiles


def _plan_kernel(gc_ref, src_ref, texp_ref, nt_ref, dst_ref, local_next):
    n_tiles = gc_ref.shape[0]
    max_gran, max_row_tiles = _plan_sizes(n_tiles)

    def zero(ref, n):
        def body(k, c):
            ref[k] = 0
            return c
        lax.fori_loop(0, n, body, 0)

    zero(src_ref, max_gran)
    zero(dst_ref, n_tiles * LOCAL_GRAN)
    zero(local_next, n_tiles)

    def per_expert(e, g):
        first_tile = g // GRAN_PER_TILE

        def per_tile(t, g):
            n = gc_ref[t, N_GROUPS + e]
            base = t * LOCAL_GRAN + local_next[t]

            def per_granule(k, g):
                src_ref[g] = base + k
                dst_ref[base + k] = g
                return g + 1

            g = lax.fori_loop(0, n, per_granule, g)
            local_next[t] = local_next[t] + n
            return g

        g = lax.fori_loop(0, n_tiles, per_tile, g)
        g = (g + GRAN_PER_TILE - 1) // GRAN_PER_TILE * GRAN_PER_TILE

        def mark(j, c):
            texp_ref[j] = e
            return c

        lax.fori_loop(first_tile, g // GRAN_PER_TILE, mark, 0)
        return g

    g = lax.fori_loop(0, N_EXPERTS, per_expert, 0)
    used = g // GRAN_PER_TILE
    nt_ref[0] = used

    def mark_rest(j, c):
        texp_ref[j] = N_EXPERTS - 1
        return c

    lax.fori_loop(used, max_row_tiles, mark_rest, 0)


def _plan(gc):
    n_tiles = gc.shape[0]
    max_gran, max_row_tiles = _plan_sizes(n_tiles)
    smem = pl.BlockSpec(memory_space=pltpu.SMEM)
    return pl.pallas_call(
        _plan_kernel,
        in_specs=[smem],
        out_specs=[smem, smem, smem, smem],
        out_shape=[jax.ShapeDtypeStruct((max_gran,), I32),
                   jax.ShapeDtypeStruct((max_row_tiles,), I32),
                   jax.ShapeDtypeStruct((1,), I32),
                   jax.ShapeDtypeStruct((n_tiles * LOCAL_GRAN,), I32)],
        scratch_shapes=[pltpu.SMEM((n_tiles,), I32)],
        name="plan",
    )(gc)


def _granule_copies(idx_ref, first, count, src_hbm, dst_vmem, sem, wait):
    def body(k, c):
        g = idx_ref[first + k]
        cp = pltpu.make_async_copy(src_hbm.at[pl.ds(pl.multiple_of(g * GRAN, GRAN), GRAN)],
                                   dst_vmem.at[pl.ds(pl.multiple_of(k * GRAN, GRAN), GRAN)], sem)
        if wait:
            cp.wait()
        else:
            cp.start()
        return c
    lax.fori_loop(0, count, body, 0)


def _moe_kernel(src_ref, texp_ref, nt_ref, hsl_ref, w1_ref, w3_ref, w2_ref, ys_ref, xbuf, w1b, w3b, w2b, sem):
    i = pl.program_id(0)
    used = nt_ref[0]
    slot = i % 2
    D = ys_ref.shape[1]

    def gather(tile, slot_, wait):
        _granule_copies(src_ref, tile * GRAN_PER_TILE, GRAN_PER_TILE, hsl_ref, xbuf.at[slot_], sem.at[slot_], wait)

    @pl.when(i == 0)
    def _():
        gather(0, 0, False)

    @pl.when(i + 1 < used)
    def _():
        gather(i + 1, 1 - slot, False)

    @pl.when(i < used)
    def _():
        gather(i, slot, True)

        @pl.when((i == 0) | (texp_ref[i] != texp_ref[jnp.maximum(i - 1, 0)]))
        def _():
            w1b[...] = w1_ref[0].astype(BF16)
            w3b[...] = w3_ref[0].astype(BF16)
            w2b[...] = w2_ref[0].astype(BF16)

        xb = xbuf[slot]
        hb = xb[:, :D]
        aux = xb[:, D:].astype(F32)
        w_row = aux[:, 0:1]
        for k in range(1, W_PIECES):
            w_row = w_row + aux[:, k:k + 1]
        a = _dot(hb, w1b[...])
        act = (a * jax.nn.sigmoid(a)) * _dot(hb, w3b[...])
        ys_ref[...] = (w_row * _dot(act.astype(BF16), w2b[...])).astype(BF16)

    @pl.when(i >= used)
    def _():
        ys_ref[...] = jnp.zeros_like(ys_ref)


def _moe(src, texp, used, hsl, w1, w3, w2):
    max_row_tiles = texp.shape[0]
    D = w1.shape[1]
    F = w1.shape[2]
    grid_spec = pltpu.PrefetchScalarGridSpec(
        num_scalar_prefetch=3,
        grid=(max_row_tiles,),
        in_specs=[pl.BlockSpec(memory_space=pl.ANY),
                  pl.BlockSpec((1, D, F), lambda i, s, te, nt: (te[i], 0, 0)),
                  pl.BlockSpec((1, D, F), lambda i, s, te, nt: (te[i], 0, 0)),
                  pl.BlockSpec((1, F, D), lambda i, s, te, nt: (te[i], 0, 0))],
        out_specs=pl.BlockSpec((TM_MOE, D), lambda i, s, te, nt: (i, 0)),
        scratch_shapes=[pltpu.VMEM((2, TM_MOE, D + AUX_W), BF16),
                        pltpu.VMEM((D, F), BF16), pltpu.VMEM((D, F), BF16), pltpu.VMEM((F, D), BF16),
                        pltpu.SemaphoreType.DMA((2,))],
    )
    return pl.pallas_call(
        _moe_kernel,
        grid_spec=grid_spec,
        out_shape=jax.ShapeDtypeStruct((max_row_tiles * TM_MOE, D), BF16),
        compiler_params=_params("arbitrary"),
        name="moe",
    )(src, texp, used, hsl, w1, w3, w2)


def _combine_kernel(dst_ref, ys_ref, x1_ref, route_ref, mod_ref, o_ref, ybuf, sem):
    t = pl.program_id(0)
    slot = t % 2
    tm = x1_ref.shape[0]

    def gather(tile, slot_, wait):
        _granule_copies(dst_ref, tile * LOCAL_GRAN, LOCAL_GRAN, ys_ref, ybuf.at[slot_], sem.at[slot_], wait)

    @pl.when(t == 0)
    def _():
        gather(0, 0, False)

    @pl.when(t + 1 < pl.num_programs(0))
    def _():
        gather(t + 1, 1 - slot, False)

    gather(t, slot, True)
    lp1 = route_ref[:, R_LP1:R_LP1 + 1].astype(I32)
    lp2 = route_ref[:, R_LP2:R_LP2 + 1].astype(I32)
    col = lax.broadcasted_iota(I32, (tm, LOCAL_ROWS), 1)
    unsort = jnp.where(col == lp1, 1.0, jnp.where(col == lp2, 1.0, 0.0)).astype(BF16)
    o_ref[...] = x1_ref[...] + mod_ref[0, 5:6, :] * _dot(unsort, ybuf[slot])


def _combine(dst, ys, x1, route, mod3, S):
    N, D = x1.shape
    tm = TM_SORT
    steps_per_batch = S // tm
    grid_spec = pltpu.PrefetchScalarGridSpec(
        num_scalar_prefetch=1,
        grid=(N // tm,),
        in_specs=[pl.BlockSpec(memory_space=pl.ANY),
                  pl.BlockSpec((tm, D), lambda t, d: (t, 0)),
                  pl.BlockSpec((tm, LANES), lambda t, d: (t, 0)),
                  pl.BlockSpec((1, 6, D), lambda t, d: (t // steps_per_batch, 0, 0))],
        out_specs=pl.BlockSpec((tm, D), lambda t, d: (t, 0)),
        scratch_shapes=[pltpu.VMEM((2, LOCAL_ROWS, D), BF16), pltpu.SemaphoreType.DMA((2,))],
    )
    return pl.pallas_call(
        _combine_kernel,
        grid_spec=grid_spec,
        out_shape=jax.ShapeDtypeStruct((N, D), F32),
        compiler_params=_params("arbitrary"),
        name="combine",
    )(dst, ys, x1, route, mod3)


def _layer(x2, mod3, B, S, norm_mix_w, w_in, q_norm_w, k_norm_w, conv_w, conv_b, w_rec_gate, b_rec_gate,
           w_in_gate, b_in_gate, lru_lambda, w_proj_a, w_proj_b, w_out, norm_ffn_w, w_group, b_group,
           w_expert_router, b_expert_router, w1, w3, w2):
    N, D = x2.shape
    nq = N_HEADS_A * HEAD_DIM_A
    o_k = nq
    o_v = o_k + KV_DIM_A
    o_qi = o_v + KV_DIM_A
    o_ki = o_qi + N_IDX_HEADS * IDX_DIM
    o_wi = o_ki + IDX_DIM
    o_lx = o_wi + N_IDX_HEADS
    o_g = o_lx + 2 * LRU_WIDTH
    pad = jnp.zeros((D, KV_W - (WI_OFF + N_IDX_HEADS)), w_in.dtype)
    wa = jnp.concatenate([w_in[:, :nq], w_in[:, o_qi:o_ki], w_in[:, o_k:o_qi], w_in[:, o_ki:o_lx], pad],
                         axis=1).astype(BF16)
    wb = w_in[:, o_lx:o_g].astype(BF16)
    wg = w_in[:, o_g:].astype(BF16)

    qa, kv, lru, gates = _proj(x2, mod3, norm_mix_w, wa, wb, wg, S)
    attn = _attn(qa, kv, q_norm_w, k_norm_w, B, S)
    lru_out = _lru(lru, conv_w, conv_b, _block_diag_tiles(w_rec_gate).astype(BF16), b_rec_gate,
                   _block_diag_tiles(w_in_gate).astype(BF16), b_in_gate, lru_lambda, B, S)

    n_r = N_GROUPS + N_EXPERTS
    w_router = jnp.concatenate([w_group, w_expert_router, jnp.zeros((D, LANES - n_r), F32)], axis=1)
    wr_hi = w_router.astype(BF16)
    wr_lo = (w_router - wr_hi.astype(F32)).astype(BF16)
    b_router = jnp.concatenate([b_group, b_expert_router, jnp.zeros((LANES - n_r,), F32)]).reshape(1, LANES)
    x1, hsl, route, gc = _merge(attn, lru_out, gates, x2, mod3, w_proj_a.astype(BF16), w_proj_b.astype(BF16),
                                w_out.astype(BF16), norm_ffn_w, wr_hi, wr_lo, b_router, S)

    src, texp, used, dst = _plan(gc.reshape(gc.shape[0], LANES).astype(I32))
    ys = _moe(src, texp, used, hsl, w1, w3, w2)
    return _combine(dst, ys, x1, route, mod3, S)


def kernel(x, c, ada_w, ada_b, norm_mix_w, w_in, q_norm_w, k_norm_w, conv_w, conv_b, w_rec_gate, b_rec_gate,
           w_in_gate, b_in_gate, lru_lambda, w_proj_a, w_proj_b, w_out, norm_ffn_w, w_group, b_group,
           w_expert_router, b_expert_router, w1, w3, w2):
    B, S, D = x.shape
    x2 = x.reshape(B * S, D)
    for l in range(ada_w.shape[0]):
        mod3 = _ada(c, ada_w[l], ada_b[l]).reshape(B, 6, D)
        x2 = _layer(x2, mod3, B, S, norm_mix_w[l], w_in[l], q_norm_w[l], k_norm_w[l], conv_w[l], conv_b[l],
                    w_rec_gate[l], b_rec_gate[l], w_in_gate[l], b_in_gate[l], lru_lambda[l], w_proj_a[l],
                    w_proj_b[l], w_out[l], norm_ffn_w[l], w_group[l], b_group[l], w_expert_router[l],
                    b_expert_router[l], w1[l], w3[l], w2[l])
    return x2.reshape(B, S, D)
```

```python
import functools

import jax
import jax.numpy as jnp
import numpy as np
from jax import lax
from jax.experimental import pallas as pl
from jax.experimental.pallas import tpu as pltpu

F32 = jnp.float32
BF16 = jnp.bfloat16
I32 = jnp.int32

EPS = 1e-6
CHUNK = 64
Q_BLOCK = 128
N_HEADS_A = 8
HEAD_DIM_A = 64
KV_DIM_A = 64
ATTN_OUT = N_HEADS_A * KV_DIM_A
N_IDX_HEADS = 4
IDX_DIM = 64
TOPK_MAX = 256
LRU_WIDTH = 512
LRU_BLOCKS = 8
LRU_BLOCK_DIM = LRU_WIDTH // LRU_BLOCKS
CONV_WIDTH = 4
LRU_C = 8.0
N_GROUPS = 4
EXPERTS_PER_GROUP = 8
N_EXPERTS = N_GROUPS * EXPERTS_PER_GROUP
D_FF_EXPERT = 256

LANES = 128
INT_MIN = -2 ** 31
VMEM_LIMIT = 56 * 1024 * 1024

QA_W = N_HEADS_A * HEAD_DIM_A + N_IDX_HEADS * IDX_DIM
KV_W = 256
K_OFF, V_OFF, KI_OFF, WI_OFF = 0, 64, 128, 192

TM_PROJ = 256

TM_SORT = 512
GRAN = 16
TM_MOE = 256
GRAN_PER_TILE = TM_MOE // GRAN
LOCAL_ROWS = -(-(2 * TM_SORT + N_EXPERTS * (GRAN - 1)) // LANES) * LANES
LOCAL_GRAN = LOCAL_ROWS // GRAN
AUX_W = LANES
W_PIECES = 3

R_LP1, R_LP2 = 0, 1


def _dot(a, b, **kw):
    return jnp.dot(a, b, preferred_element_type=F32, **kw)


def _params(*sem):
    return pltpu.CompilerParams(dimension_semantics=sem, vmem_limit_bytes=VMEM_LIMIT)


def _ada_kernel(c_ref, w_ref, b_ref, o_ref):
    c = c_ref[...]
    cond = c * jax.nn.sigmoid(c)
    o_ref[...] = _dot(cond, w_ref[...], precision=lax.Precision.HIGHEST) + b_ref[...]


def _ada(c, ada_w, ada_b):
    B, D = c.shape
    n_out = ada_w.shape[1]
    tn = 1024
    return pl.pallas_call(
        _ada_kernel,
        grid=(n_out // tn,),
        in_specs=[pl.BlockSpec((B, D), lambda j: (0, 0)),
                  pl.BlockSpec((D, tn), lambda j: (0, j)),
                  pl.BlockSpec((1, tn), lambda j: (0, j))],
        out_specs=pl.BlockSpec((B, tn), lambda j: (0, j)),
        out_shape=jax.ShapeDtypeStruct((B, n_out), F32),
        compiler_params=_params("arbitrary"),
        name="ada",
    )(c, ada_w, ada_b.reshape(1, n_out))


def _rms_mod(x, w, shift, scale):
    ms = jnp.mean(x * x, axis=-1, keepdims=True)
    y = x * lax.rsqrt(ms + EPS) * w
    return y * (1.0 + scale) + shift


def _proj_kernel(x_ref, mod_ref, nw_ref, wa_ref, wb_ref, wg_ref, oqa_ref, okv_ref, olru_ref, og_ref):
    h = _rms_mod(x_ref[...], nw_ref[...], mod_ref[0, 0:1, :], mod_ref[0, 1:2, :])
    hb = h.astype(BF16)
    pa = _dot(hb, wa_ref[...])
    oqa_ref[...] = pa[:, :QA_W]
    okv_ref[...] = pa[:, QA_W:]
    olru_ref[...] = _dot(hb, wb_ref[...])
    og_ref[...] = _dot(hb, wg_ref[...])


def _proj(x2, mod3, norm_w, wa, wb, wg, S):
    N, D = x2.shape
    tm = TM_PROJ
    steps_per_batch = S // tm
    const = lambda t: (0, 0)
    return pl.pallas_call(
        _proj_kernel,
        grid=(N // tm,),
        in_specs=[pl.BlockSpec((tm, D), lambda t: (t, 0)),
                  pl.BlockSpec((1, 6, D), lambda t: (t // steps_per_batch, 0, 0)),
                  pl.BlockSpec((1, D), const),
                  pl.BlockSpec(wa.shape, const),
                  pl.BlockSpec(wb.shape, const),
                  pl.BlockSpec(wg.shape, const)],
        out_specs=[pl.BlockSpec((tm, QA_W), lambda t: (t, 0)),
                   pl.BlockSpec((tm, KV_W), lambda t: (t, 0)),
                   pl.BlockSpec((tm, 2 * LRU_WIDTH), lambda t: (t, 0)),
                   pl.BlockSpec((tm, 2 * D), lambda t: (t, 0))],
        out_shape=[jax.ShapeDtypeStruct((N, QA_W), F32),
                   jax.ShapeDtypeStruct((N, KV_W), F32),
                   jax.ShapeDtypeStruct((N, 2 * LRU_WIDTH), F32),
                   jax.ShapeDtypeStruct((N, 2 * D), F32)],
        compiler_params=_params("arbitrary"),
        name="proj",
    )(x2, mod3, norm_w.reshape(1, D), wa, wb, wg)


ATTN_BUCKETS = 8
ROW_FOLD = 64
KEY_ROWS = 256
LOG2E = float(np.log2(np.e))
POS_RADIX = 256
SLOPE_PIECES = 3


def _bf16_pieces(c, n):
    out = []
    rest = float(c)
    for _ in range(n):
        p = float(np.asarray(rest, np.float32).astype(BF16).astype(np.float32))
        out.append(p)
        rest -= p
    return out


def _reduce_rows(op, x):
    r = x.shape[0]
    if r > ROW_FOLD and r % ROW_FOLD == 0:
        x = op(x.reshape(r // ROW_FOLD, ROW_FOLD, x.shape[1]), axis=0)
    return op(x, axis=0, keepdims=True)


def _attn_block(s_eff, topk, i, qa_ref, kvq_ref, qnw_ref, o_ref, kn_s, vt_s, ki_s, key_s, bias_s, lg_s, ot_s):
    nq = N_HEADS_A * HEAD_DIM_A
    q_t = qa_ref[:, :nq].T
    qi_t = qa_ref[:, nq:].T
    wq_t = kvq_ref[...].T[WI_OFF:WI_OFF + N_IDX_HEADS, :] * (N_IDX_HEADS ** -0.5 * IDX_DIM ** -0.5)

    q_pos = i * Q_BLOCK + lax.broadcasted_iota(I32, (1, Q_BLOCK), 1)
    limit = (q_pos // CHUNK + 1) * CHUNK
    rows_per = KEY_ROWS if s_eff % KEY_ROWS == 0 else Q_BLOCK
    chunks = [slice(c * rows_per, (c + 1) * rows_per) for c in range(s_eff // rows_per)]

    qi = [qi_t[h * IDX_DIM:(h + 1) * IDX_DIM, :].astype(BF16) for h in range(N_IDX_HEADS)]
    for rows in chunks:
        ki = ki_s[rows, :]
        score = jnp.zeros((rows_per, Q_BLOCK), F32)
        for h in range(N_IDX_HEADS):
            score = score + jnp.maximum(_dot(ki, qi[h]), 0.0) * wq_t[h:h + 1, :]
        k_pos = rows.start + lax.broadcasted_iota(I32, (rows_per, Q_BLOCK), 0)
        key_s[rows, :] = jnp.where(k_pos < limit, score, -jnp.inf)

    def count(mask):
        return _reduce_rows(jnp.sum, jnp.where(mask, 1.0, 0.0))

    def ordered_to_float(c):
        s = c ^ INT_MIN
        return pltpu.bitcast(jnp.where(s < 0, INT_MIN | (-s), s), F32)

    def bisect(it, r):
        c = r | jnp.left_shift(jnp.int32(1), 31 - it)
        return jnp.where(count(key_s[:s_eff, :] >= ordered_to_float(c)) >= topk, c, r)

    thr = ordered_to_float(lax.fori_loop(0, 32, bisect, jnp.zeros((1, Q_BLOCK), I32)))

    n_ge = count(key_s[:s_eff, :] >= thr)
    clean = jnp.max(jnp.where((n_ge == topk) & (thr > -jnp.inf), 0.0, 1.0)) == 0.0

    @pl.when(clean)
    def _():
        for rows in chunks:
            bias_s[rows, :] = jnp.where(key_s[rows, :] >= thr, 0.0, -jnp.inf)

    @pl.when(jnp.logical_not(clean))
    def _():
        key = key_s[:s_eff, :]
        need = topk - count(key > thr)
        rr = lax.broadcasted_iota(I32, (LANES, LANES), 0)
        cc = lax.broadcasted_iota(I32, (LANES, LANES), 1)
        tri = jnp.where(cc < rr, 1.0, 0.0).astype(BF16)
        seen = jnp.zeros((1, Q_BLOCK), F32)
        for c in range(s_eff // LANES):
            kc = key[c * LANES:(c + 1) * LANES, :]
            eq = kc == thr
            eqf = jnp.where(eq, 1.0, 0.0)
            rank = _dot(tri, eqf.astype(BF16)) + seen
            seen = seen + jnp.sum(eqf, axis=0, keepdims=True)
            tie_bias = jnp.where(jnp.where(eq, rank, topk) < need, 0.0, -jnp.inf)
            bias = jnp.where(kc > thr, 0.0, tie_bias)
            k_pos = c * LANES + lax.broadcasted_iota(I32, (LANES, Q_BLOCK), 0)
            bias_s[c * LANES:(c + 1) * LANES, :] = jnp.where(k_pos < limit, bias, -jnp.inf)

    qnw = qnw_ref[...]
    crow = lax.broadcasted_iota(I32, (LANES - HEAD_DIM_A, Q_BLOCK), 0)
    r_k = lax.broadcasted_iota(I32, (Q_BLOCK, Q_BLOCK), 0)
    c_q = lax.broadcasted_iota(I32, (Q_BLOCK, Q_BLOCK), 1)
    after = 2.0 * jnp.maximum(r_k - c_q, 0).astype(F32)
    band = pl.ds(pl.multiple_of(i * Q_BLOCK, Q_BLOCK), Q_BLOCK)
    for h in range(N_HEADS_A):
        pieces = _bf16_pieces(LOG2E * 2.0 ** -(h + 1), SLOPE_PIECES)
        qh = q_t[h * HEAD_DIM_A:(h + 1) * HEAD_DIM_A, :]
        ms = jnp.mean(qh * qh, axis=0, keepdims=True)
        qn = (qh * lax.rsqrt(ms + EPS) * qnw) * (HEAD_DIM_A ** -0.5) * LOG2E
        slopes = jnp.zeros(crow.shape, F32)
        for k, piece in enumerate(pieces):
            slopes = jnp.where(crow == 2 * k, POS_RADIX * piece, jnp.where(crow == 2 * k + 1, piece, slopes))
        qa = jnp.concatenate([qn, slopes], axis=0).astype(BF16)
        maxes = []
        for rows in chunks:
            logits = _dot(kn_s[rows, :], qa) + bias_s[rows, :]
            lg_s[h, rows, :] = logits
            maxes.append(_reduce_rows(jnp.max, logits))
        m = functools.reduce(jnp.maximum, maxes)
        lg_s[h, band, :] = lg_s[h, band, :] - sum(pieces) * after
        l = jnp.zeros((1, Q_BLOCK), F32)
        acc = jnp.zeros((KV_DIM_A, Q_BLOCK), F32)
        for rows in chunks:
            p = jnp.exp2(lg_s[h, rows, :] - m)
            l = l + _reduce_rows(jnp.sum, p)
            acc = acc + _dot(vt_s[:, rows], p.astype(BF16))
        ot_s[h * KV_DIM_A:(h + 1) * KV_DIM_A, :] = acc / l
    o_ref[...] = ot_s[...].T


def _attn_kernel(qa_ref, kvq_ref, kv_ref, qnw_ref, knw_ref, o_ref, kn_s, vt_s, ki_s, key_s, bias_s, lg_s, ot_s):
    i = pl.program_id(1)
    S = kv_ref.shape[0]
    nb = S // Q_BLOCK
    topk = float(min(TOPK_MAX, S // 4))

    @pl.when(i == 0)
    def _():
        kv = kv_ref[:, :LANES]
        lane = lax.broadcasted_iota(I32, kv.shape, 1)
        pos = lax.broadcasted_iota(I32, kv.shape, 0)
        ms = jnp.sum(jnp.where(lane < KV_DIM_A, kv * kv, 0.0), axis=-1, keepdims=True) * (1.0 / KV_DIM_A)
        kn = kv * lax.rsqrt(ms + EPS) * knw_ref[...]
        digit = jnp.where((lane & 1) == 0, pos // POS_RADIX, pos % POS_RADIX).astype(F32)
        is_digit = (lane >= KV_DIM_A) & (lane < KV_DIM_A + 2 * SLOPE_PIECES)
        kn_s[...] = jnp.where(is_digit, digit, kn).astype(BF16)
        vt_s[...] = kv.T[V_OFF:V_OFF + KV_DIM_A, :].astype(BF16)
        ki_s[...] = kv_ref[:, KI_OFF:KI_OFF + IDX_DIM].astype(BF16)

    n_buckets = min(ATTN_BUCKETS, nb)
    per = nb // n_buckets
    for j in range(n_buckets):
        pl.when(i // per == j)(functools.partial(
            _attn_block, (j + 1) * per * Q_BLOCK, topk, i, qa_ref, kvq_ref, qnw_ref, o_ref,
            kn_s, vt_s, ki_s, key_s, bias_s, lg_s, ot_s))


def _attn(qa, kv, q_norm_w, k_norm_w, B, S):
    N = qa.shape[0]
    nb = S // Q_BLOCK
    return pl.pallas_call(
        _attn_kernel,
        grid=(B, nb),
        in_specs=[pl.BlockSpec((Q_BLOCK, QA_W), lambda b, i: (b * nb + i, 0)),
                  pl.BlockSpec((Q_BLOCK, KV_W), lambda b, i: (b * nb + i, 0)),
                  pl.BlockSpec((S, KV_W), lambda b, i: (b, 0)),
                  pl.BlockSpec((HEAD_DIM_A, 1), lambda b, i: (0, 0)),
                  pl.BlockSpec((1, LANES), lambda b, i: (0, 0))],
        out_specs=pl.BlockSpec((Q_BLOCK, ATTN_OUT), lambda b, i: (b * nb + i, 0)),
        out_shape=jax.ShapeDtypeStruct((N, ATTN_OUT), F32),
        scratch_shapes=[pltpu.VMEM((S, LANES), BF16),
                        pltpu.VMEM((KV_DIM_A, S), BF16),
                        pltpu.VMEM((S, IDX_DIM), BF16),
                        pltpu.VMEM((S, Q_BLOCK), F32),
                        pltpu.VMEM((S, Q_BLOCK), F32),
                        pltpu.VMEM((N_HEADS_A, S, Q_BLOCK), F32),
                        pltpu.VMEM((ATTN_OUT, Q_BLOCK), F32)],
        compiler_params=_params("arbitrary", "arbitrary"),
        name="attn",
    )(qa, kv, kv, q_norm_w.reshape(HEAD_DIM_A, 1),
      jnp.pad(k_norm_w, (0, LANES - KV_DIM_A)).reshape(1, LANES))


def _shift_rows(x, d, fill, row):
    return jnp.where(row >= d, pltpu.roll(x, d, 0), fill)


def _lru_kernel(xb_ref, gb_ref, cw_ref, cb_ref, wr_ref, br_ref, wi_ref, bi_ref, lam_ref, o_ref):
    x = xb_ref[...]
    S = x.shape[0]
    row = lax.broadcasted_iota(I32, x.shape, 0)
    xc = cb_ref[...] + _shift_rows(x, CONV_WIDTH - 1, 0.0, row) * cw_ref[0:1, :]
    for j in range(1, CONV_WIDTH):
        d = CONV_WIDTH - 1 - j
        xs = x if d == 0 else _shift_rows(x, d, 0.0, row)
        xc = xc + xs * cw_ref[j:j + 1, :]
    xcb = xc.astype(BF16)
    r = jax.nn.sigmoid(_dot(xcb, wr_ref[0]) + br_ref[...])
    ig = jax.nn.sigmoid(_dot(xcb, wi_ref[0]) + bi_ref[...])
    z = -lam_ref[...]
    softplus = jnp.maximum(z, 0.0) + jnp.log1p(jnp.exp(-jnp.abs(z)))
    log_a = -LRU_C * r * softplus
    a = jnp.exp(log_a)
    b = jnp.sqrt(-jnp.tanh(log_a) * (1.0 + a * a)) * (ig * xc)
    d = 1
    while d < S:
        b = a * _shift_rows(b, d, 0.0, row) + b
        if 2 * d < S:
            a = a * _shift_rows(a, d, 1.0, row)
        d *= 2
    g = gb_ref[...]
    gelu = 0.5 * g * (1.0 + jnp.tanh(float(np.sqrt(2.0 / np.pi)) * (g + 0.044715 * (g * g * g))))
    o_ref[...] = b * gelu


def _lru(lru, conv_w, conv_b, wr_bd, b_rec, wi_bd, b_in, lam, B, S):
    N = lru.shape[0]
    nt = LRU_WIDTH // LANES
    vec = lambda b, j: (0, j)
    return pl.pallas_call(
        _lru_kernel,
        grid=(B, nt),
        in_specs=[pl.BlockSpec((S, LANES), lambda b, j: (b, j)),
                  pl.BlockSpec((S, LANES), lambda b, j: (b, nt + j)),
                  pl.BlockSpec((CONV_WIDTH, LANES), vec),
                  pl.BlockSpec((1, LANES), vec),
                  pl.BlockSpec((1, LANES, LANES), lambda b, j: (j, 0, 0)),
                  pl.BlockSpec((1, LANES), vec),
                  pl.BlockSpec((1, LANES, LANES), lambda b, j: (j, 0, 0)),
                  pl.BlockSpec((1, LANES), vec),
                  pl.BlockSpec((1, LANES), vec)],
        out_specs=pl.BlockSpec((S, LANES), lambda b, j: (b, j)),
        out_shape=jax.ShapeDtypeStruct((N, LRU_WIDTH), F32),
        compiler_params=_params("arbitrary", "arbitrary"),
        name="lru",
    )(lru, lru, conv_w, conv_b.reshape(1, -1), wr_bd, b_rec.reshape(1, -1), wi_bd,
      b_in.reshape(1, -1), lam.reshape(1, -1))


def _block_diag_tiles(w):
    per = LANES // LRU_BLOCK_DIM
    nt = LRU_WIDTH // LANES
    w5 = w.reshape(nt, per, LRU_BLOCK_DIM, 1, LRU_BLOCK_DIM)
    eye = jnp.eye(per, dtype=w.dtype).reshape(1, per, 1, per, 1)
    return (w5 * eye).reshape(nt, LANES, LANES)


def _first_index_of_max(vals, lane):
    m = jnp.max(vals, axis=-1, keepdims=True)
    idx = jnp.min(jnp.where(vals == m, lane, 4 * LANES), axis=-1, keepdims=True)
    return m, idx


def _weight_pieces(w, lane):
    out = jnp.zeros(lane.shape, F32)
    rest = w
    for k in range(W_PIECES):
        piece = rest.astype(BF16).astype(F32)
        out = jnp.where(lane == k, piece, out)
        rest = rest - piece
    return out.astype(BF16)


def _one_hot_rows(n_rows, pos_row):
    rows = lax.broadcasted_iota(I32, (n_rows, pos_row.shape[1]), 0)
    return jnp.where(rows == pos_row, 1.0, 0.0).astype(BF16)


def _merge_kernel(attn_ref, lru_ref, g_ref, x_ref, mod_ref, wpa_ref, wpb_ref, wo_ref, nw_ref, wrh_ref, wrl_ref,
                  br_ref, x1_ref, hsl_ref, route_ref, gc_ref):
    D = x_ref.shape[1]
    tm = x_ref.shape[0]

    y_a = _dot(attn_ref[...].astype(BF16), wpa_ref[...])
    y_b = _dot(lru_ref[...].astype(BF16), wpb_ref[...])
    merged = jax.nn.sigmoid(g_ref[:, :D]) * y_a + jax.nn.sigmoid(g_ref[:, D:]) * y_b
    x1 = x_ref[...] + mod_ref[0, 2:3, :] * _dot(merged.astype(BF16), wo_ref[...])
    x1_ref[...] = x1
    h2 = _rms_mod(x1, nw_ref[...], mod_ref[0, 3:4, :], mod_ref[0, 4:5, :])
    h2b = h2.astype(BF16)

    h2l = (h2 - h2b.astype(F32)).astype(BF16)
    wrh = wrh_ref[...]
    logits = _dot(h2b, wrh) + (_dot(h2l, wrh) + _dot(h2b, wrl_ref[...])) + br_ref[...]
    lane = lax.broadcasted_iota(I32, logits.shape, 1)
    gl = jnp.where(lane < N_GROUPS, logits, -jnp.inf)
    gmax, g_sel = _first_index_of_max(gl, lane)
    g_weight = 1.0 / jnp.sum(jnp.exp(gl - gmax), axis=-1, keepdims=True)
    e_lo = N_GROUPS + g_sel * EXPERTS_PER_GROUP
    el = jnp.where((lane >= e_lo) & (lane < e_lo + EXPERTS_PER_GROUP), logits, -jnp.inf)
    v1, i1 = _first_index_of_max(el, lane)
    el2 = jnp.where(lane == i1, -jnp.inf, el)
    v2, i2 = _first_index_of_max(el2, lane)
    e2x = jnp.exp(v2 - v1)
    w1 = g_weight / (1.0 + e2x)
    w2 = g_weight * e2x / (1.0 + e2x)

    hot1 = lane == i1
    hot2 = lane == i2
    hot = jnp.where(hot1, 1.0, jnp.where(hot2, 1.0, 0.0))
    gcnt = jnp.floor((jnp.sum(hot, axis=0, keepdims=True) + (GRAN - 1.0)) * (1.0 / GRAN))
    gc_ref[0] = gcnt
    rr = lax.broadcasted_iota(I32, (LANES, LANES), 0)
    cc = lax.broadcasted_iota(I32, (LANES, LANES), 1)
    upper = jnp.where(rr < cc, 1.0, 0.0).astype(BF16)
    run_start = GRAN * _dot(jnp.broadcast_to(gcnt, (8, LANES)).astype(BF16), upper)[0:1, :]
    rr = lax.broadcasted_iota(I32, (tm, tm), 0)
    cc = lax.broadcasted_iota(I32, (tm, tm), 1)
    lower = jnp.where(cc < rr, 1.0, 0.0).astype(BF16)
    pos = _dot(lower, hot.astype(BF16)) + run_start
    pos1 = jnp.where(hot1, pos, 0.0)
    pos2 = jnp.where(hot2, pos, 0.0)
    lp1 = jnp.sum(pos1, axis=-1, keepdims=True)
    lp2 = jnp.sum(pos2, axis=-1, keepdims=True)
    route_ref[...] = jnp.where(lane == R_LP1, lp1, jnp.where(lane == R_LP2, lp2, 0.0))

    p1 = _one_hot_rows(LOCAL_ROWS, jnp.sum(pos1.T, axis=0, keepdims=True).astype(I32))
    p2 = _one_hot_rows(LOCAL_ROWS, jnp.sum(pos2.T, axis=0, keepdims=True).astype(I32))
    hsl_ref[:, :D] = _dot(p1 + p2, h2b).astype(BF16)
    aux = _dot(p1, _weight_pieces(w1, lane)) + _dot(p2, _weight_pieces(w2, lane))
    hsl_ref[:, D:] = aux.astype(BF16)


def _merge(attn, lru_out, gates, x2, mod3, wpa, wpb, wo, norm_w, wr_hi, wr_lo, b_router, S):
    N, D = x2.shape
    tm = TM_SORT
    n_tiles = N // tm
    steps_per_batch = S // tm
    const = lambda t: (0, 0)
    rows = lambda t: (t, 0)
    return pl.pallas_call(
        _merge_kernel,
        grid=(n_tiles,),
        in_specs=[pl.BlockSpec((tm, ATTN_OUT), rows),
                  pl.BlockSpec((tm, LRU_WIDTH), rows),
                  pl.BlockSpec((tm, 2 * D), rows),
                  pl.BlockSpec((tm, D), rows),
                  pl.BlockSpec((1, 6, D), lambda t: (t // steps_per_batch, 0, 0)),
                  pl.BlockSpec(wpa.shape, const),
                  pl.BlockSpec(wpb.shape, const),
                  pl.BlockSpec(wo.shape, const),
                  pl.BlockSpec((1, D), const),
                  pl.BlockSpec(wr_hi.shape, const),
                  pl.BlockSpec(wr_lo.shape, const),
                  pl.BlockSpec((1, LANES), const)],
        out_specs=[pl.BlockSpec((tm, D), rows),
                   pl.BlockSpec((LOCAL_ROWS, D + AUX_W), rows),
                   pl.BlockSpec((tm, LANES), rows),
                   pl.BlockSpec((1, 1, LANES), lambda t: (t, 0, 0))],
        out_shape=[jax.ShapeDtypeStruct((N, D), F32),
                   jax.ShapeDtypeStruct((n_tiles * LOCAL_ROWS, D + AUX_W), BF16),
                   jax.ShapeDtypeStruct((N, LANES), F32),
                   jax.ShapeDtypeStruct((n_tiles, 1, LANES), F32)],
        compiler_params=_params("arbitrary"),
        name="merge",
    )(attn, lru_out, gates, x2, mod3, wpa, wpb, wo, norm_w.reshape(1, D), wr_hi, wr_lo, b_router)


def _plan_sizes(n_tiles):
    max_gran = n_tiles * LOCAL_GRAN + N_EXPERTS * (GRAN_PER_TILE - 1)
    max_row_tiles = -(-max_gran // GRAN_PER_TILE)
    return max_row_tiles * GRAN_PER_TILE, max_row_tiles


def _plan_kernel(gc_ref, src_ref, texp_ref, nt_ref, dst_ref, local_next):
    n_tiles = gc_ref.shape[0]
    max_gran, max_row_tiles = _plan_sizes(n_tiles)

    def zero(ref, n):
        def body(k, c):
            ref[k] = 0
            return c
        lax.fori_loop(0, n, body, 0)

    zero(src_ref, max_gran)
    zero(dst_ref, n_tiles * LOCAL_GRAN)
    zero(local_next, n_tiles)

    def per_expert(e, g):
        first_tile = g // GRAN_PER_TILE

        def per_tile(t, g):
            n = gc_ref[t, N_GROUPS + e]
            base = t * LOCAL_GRAN + local_next[t]

            def per_granule(k, g):
                src_ref[g] = base + k
                dst_ref[base + k] = g
                return g + 1

            g = lax.fori_loop(0, n, per_granule, g)
            local_next[t] = local_next[t] + n
            return g

        g = lax.fori_loop(0, n_tiles, per_tile, g)
        g = (g + GRAN_PER_TILE - 1) // GRAN_PER_TILE * GRAN_PER_TILE

        def mark(j, c):
            texp_ref[j] = e
            return c

        lax.fori_loop(first_tile, g // GRAN_PER_TILE, mark, 0)
        return g

    g = lax.fori_loop(0, N_EXPERTS, per_expert, 0)
    used = g // GRAN_PER_TILE
    nt_ref[0] = used

    def mark_rest(j, c):
        texp_ref[j] = N_EXPERTS - 1
        return c

    lax.fori_loop(used, max_row_tiles, mark_rest, 0)


def _plan(gc):
    n_tiles = gc.shape[0]
    max_gran, max_row_tiles = _plan_sizes(n_tiles)
    smem = pl.BlockSpec(memory_space=pltpu.SMEM)
    return pl.pallas_call(
        _plan_kernel,
        in_specs=[smem],
        out_specs=[smem, smem, smem, smem],
        out_shape=[jax.ShapeDtypeStruct((max_gran,), I32),
                   jax.ShapeDtypeStruct((max_row_tiles,), I32),
                   jax.ShapeDtypeStruct((1,), I32),
                   jax.ShapeDtypeStruct((n_tiles * LOCAL_GRAN,), I32)],
        scratch_shapes=[pltpu.SMEM((n_tiles,), I32)],
        name="plan",
    )(gc)


def _granule_copies(idx_ref, first, count, src_hbm, dst_vmem, sem, wait):
    def body(k, c):
        g = idx_ref[first + k]
        cp = pltpu.make_async_copy(src_hbm.at[pl.ds(pl.multiple_of(g * GRAN, GRAN), GRAN)],
                                   dst_vmem.at[pl.ds(pl.multiple_of(k * GRAN, GRAN), GRAN)], sem)
        if wait:
            cp.wait()
        else:
            cp.start()
        return c
    lax.fori_loop(0, count, body, 0)


def _moe_kernel(src_ref, texp_ref, nt_ref, hsl_ref, w1_ref, w3_ref, w2_ref, ys_ref, xbuf, w1b, w3b, w2b, sem):
    i = pl.program_id(0)
    used = nt_ref[0]
    slot = i % 2
    D = ys_ref.shape[1]

    def gather(tile, slot_, wait):
        _granule_copies(src_ref, tile * GRAN_PER_TILE, GRAN_PER_TILE, hsl_ref, xbuf.at[slot_], sem.at[slot_], wait)

    @pl.when(i == 0)
    def _():
        gather(0, 0, False)

    @pl.when(i + 1 < used)
    def _():
        gather(i + 1, 1 - slot, False)

    @pl.when(i < used)
    def _():
        gather(i, slot, True)

        @pl.when((i == 0) | (texp_ref[i] != texp_ref[jnp.maximum(i - 1, 0)]))
        def _():
            w1b[...] = w1_ref[0].astype(BF16)
            w3b[...] = w3_ref[0].astype(BF16)
            w2b[...] = w2_ref[0].astype(BF16)

        xb = xbuf[slot]
        hb = xb[:, :D]
        aux = xb[:, D:].astype(F32)
        w_row = aux[:, 0:1]
        for k in range(1, W_PIECES):
            w_row = w_row + aux[:, k:k + 1]
        a = _dot(hb, w1b[...])
        act = (a * jax.nn.sigmoid(a)) * _dot(hb, w3b[...])
        ys_ref[...] = (w_row * _dot(act.astype(BF16), w2b[...])).astype(BF16)

    @pl.when(i >= used)
    def _():
        ys_ref[...] = jnp.zeros_like(ys_ref)


def _moe(src, texp, used, hsl, w1, w3, w2):
    max_row_tiles = texp.shape[0]
    D = w1.shape[1]
    F = w1.shape[2]
    grid_spec = pltpu.PrefetchScalarGridSpec(
        num_scalar_prefetch=3,
        grid=(max_row_tiles,),
        in_specs=[pl.BlockSpec(memory_space=pl.ANY),
                  pl.BlockSpec((1, D, F), lambda i, s, te, nt: (te[i], 0, 0)),
                  pl.BlockSpec((1, D, F), lambda i, s, te, nt: (te[i], 0, 0)),
                  pl.BlockSpec((1, F, D), lambda i, s, te, nt: (te[i], 0, 0))],
        out_specs=pl.BlockSpec((TM_MOE, D), lambda i, s, te, nt: (i, 0)),
        scratch_shapes=[pltpu.VMEM((2, TM_MOE, D + AUX_W), BF16),
                        pltpu.VMEM((D, F), BF16), pltpu.VMEM((D, F), BF16), pltpu.VMEM((F, D), BF16),
                        pltpu.SemaphoreType.DMA((2,))],
    )
    return pl.pallas_call(
        _moe_kernel,
        grid_spec=grid_spec,
        out_shape=jax.ShapeDtypeStruct((max_row_tiles * TM_MOE, D), BF16),
        compiler_params=_params("arbitrary"),
        name="moe",
    )(src, texp, used, hsl, w1, w3, w2)


def _combine_kernel(dst_ref, ys_ref, x1_ref, route_ref, mod_ref, o_ref, ybuf, sem):
    t = pl.program_id(0)
    slot = t % 2
    tm = x1_ref.shape[0]

    def gather(tile, slot_, wait):
        _granule_copies(dst_ref, tile * LOCAL_GRAN, LOCAL_GRAN, ys_ref, ybuf.at[slot_], sem.at[slot_], wait)

    @pl.when(t == 0)
    def _():
        gather(0, 0, False)

    @pl.when(t + 1 < pl.num_programs(0))
    def _():
        gather(t + 1, 1 - slot, False)

    gather(t, slot, True)
    lp1 = route_ref[:, R_LP1:R_LP1 + 1].astype(I32)
    lp2 = route_ref[:, R_LP2:R_LP2 + 1].astype(I32)
    col = lax.broadcasted_iota(I32, (tm, LOCAL_ROWS), 1)
    unsort = jnp.where(col == lp1, 1.0, jnp.where(col == lp2, 1.0, 0.0)).astype(BF16)
    o_ref[...] = x1_ref[...] + mod_ref[0, 5:6, :] * _dot(unsort, ybuf[slot])


def _combine(dst, ys, x1, route, mod3, S):
    N, D = x1.shape
    tm = TM_SORT
    steps_per_batch = S // tm
    grid_spec = pltpu.PrefetchScalarGridSpec(
        num_scalar_prefetch=1,
        grid=(N // tm,),
        in_specs=[pl.BlockSpec(memory_space=pl.ANY),
                  pl.BlockSpec((tm, D), lambda t, d: (t, 0)),
                  pl.BlockSpec((tm, LANES), lambda t, d: (t, 0)),
                  pl.BlockSpec((1, 6, D), lambda t, d: (t // steps_per_batch, 0, 0))],
        out_specs=pl.BlockSpec((tm, D), lambda t, d: (t, 0)),
        scratch_shapes=[pltpu.VMEM((2, LOCAL_ROWS, D), BF16), pltpu.SemaphoreType.DMA((2,))],
    )
    return pl.pallas_call(
        _combine_kernel,
        grid_spec=grid_spec,
        out_shape=jax.ShapeDtypeStruct((N, D), F32),
        compiler_params=_params("arbitrary"),
        name="combine",
    )(dst, ys, x1, route, mod3)


def _layer(x2, mod3, B, S, norm_mix_w, w_in, q_norm_w, k_norm_w, conv_w, conv_b, w_rec_gate, b_rec_gate,
           w_in_gate, b_in_gate, lru_lambda, w_proj_a, w_proj_b, w_out, norm_ffn_w, w_group, b_group,
           w_expert_router, b_expert_router, w1, w3, w2):
    N, D = x2.shape
    nq = N_HEADS_A * HEAD_DIM_A
    o_k = nq
    o_v = o_k + KV_DIM_A
    o_qi = o_v + KV_DIM_A
    o_ki = o_qi + N_IDX_HEADS * IDX_DIM
    o_wi = o_ki + IDX_DIM
    o_lx = o_wi + N_IDX_HEADS
    o_g = o_lx + 2 * LRU_WIDTH
    pad = jnp.zeros((D, KV_W - (WI_OFF + N_IDX_HEADS)), w_in.dtype)
    wa = jnp.concatenate([w_in[:, :nq], w_in[:, o_qi:o_ki], w_in[:, o_k:o_qi], w_in[:, o_ki:o_lx], pad],
                         axis=1).astype(BF16)
    wb = w_in[:, o_lx:o_g].astype(BF16)
    wg = w_in[:, o_g:].astype(BF16)

    qa, kv, lru, gates = _proj(x2, mod3, norm_mix_w, wa, wb, wg, S)
    attn = _attn(qa, kv, q_norm_w, k_norm_w, B, S)
    lru_out = _lru(lru, conv_w, conv_b, _block_diag_tiles(w_rec_gate).astype(BF16), b_rec_gate,
                   _block_diag_tiles(w_in_gate).astype(BF16), b_in_gate, lru_lambda, B, S)

    n_r = N_GROUPS + N_EXPERTS
    w_router = jnp.concatenate([w_group, w_expert_router, jnp.zeros((D, LANES - n_r), F32)], axis=1)
    wr_hi = w_router.astype(BF16)
    wr_lo = (w_router - wr_hi.astype(F32)).astype(BF16)
    b_router = jnp.concatenate([b_group, b_expert_router, jnp.zeros((LANES - n_r,), F32)]).reshape(1, LANES)
    x1, hsl, route, gc = _merge(attn, lru_out, gates, x2, mod3, w_proj_a.astype(BF16), w_proj_b.astype(BF16),
                                w_out.astype(BF16), norm_ffn_w, wr_hi, wr_lo, b_router, S)

    src, texp, used, dst = _plan(gc.reshape(gc.shape[0], LANES).astype(I32))
    ys = _moe(src, texp, used, hsl, w1, w3, w2)
    return _combine(dst, ys, x1, route, mod3, S)


def kernel(x, c, ada_w, ada_b, norm_mix_w, w_in, q_norm_w, k_norm_w, conv_w, conv_b, w_rec_gate, b_rec_gate,
           w_in_gate, b_in_gate, lru_lambda, w_proj_a, w_proj_b, w_out, norm_ffn_w, w_group, b_group,
           w_expert_router, b_expert_router, w1, w3, w2):
    B, S, D = x.shape
    x2 = x.reshape(B * S, D)
    for l in range(ada_w.shape[0]):
        mod3 = _ada(c, ada_w[l], ada_b[l]).reshape(B, 6, D)
        x2 = _layer(x2, mod3, B, S, norm_mix_w[l], w_in[l], q_norm_w[l], k_norm_w[l], conv_w[l], conv_b[l],
                    w_rec_gate[l], b_rec_gate[l], w_in_gate[l], b_in_gate[l], lru_lambda[l], w_proj_a[l],
                    w_proj_b[l], w_out[l], norm_ffn_w[l], w_group[l], b_group[l], w_expert_router[l],
                    b_expert_router[l], w1[l], w3[l], w2[l])
    return x2.reshape(B, S, D)
```

```python
import functools

import jax
import jax.numpy as jnp
import numpy as np
from jax import lax
from jax.experimental import pallas as pl
from jax.experimental.pallas import tpu as pltpu

F32 = jnp.float32
BF16 = jnp.bfloat16
I32 = jnp.int32

EPS = 1e-6
CHUNK = 64
Q_BLOCK = 128
N_HEADS_A = 8
HEAD_DIM_A = 64
KV_DIM_A = 64
ATTN_OUT = N_HEADS_A * KV_DIM_A
N_IDX_HEADS = 4
IDX_DIM = 64
TOPK_MAX = 256
LRU_WIDTH = 512
LRU_BLOCKS = 8
LRU_BLOCK_DIM = LRU_WIDTH // LRU_BLOCKS
CONV_WIDTH = 4
LRU_C = 8.0
N_GROUPS = 4
EXPERTS_PER_GROUP = 8
N_EXPERTS = N_GROUPS * EXPERTS_PER_GROUP
D_FF_EXPERT = 256

LANES = 128
INT_MIN = -2 ** 31
VMEM_LIMIT = 56 * 1024 * 1024

QA_W = N_HEADS_A * HEAD_DIM_A + N_IDX_HEADS * IDX_DIM
KV_W = 256
K_OFF, V_OFF, KI_OFF, WI_OFF = 0, 64, 128, 192

TM_PROJ = 256

TM_SORT = 512
GRAN = 16
TM_MOE = 256
GRAN_PER_TILE = TM_MOE // GRAN
MOE_BUFFERS = 3
LOCAL_ROWS = -(-(2 * TM_SORT + N_EXPERTS * (GRAN - 1)) // LANES) * LANES
LOCAL_GRAN = LOCAL_ROWS // GRAN
AUX_W = LANES
W_PIECES = 3

R_LP1, R_LP2 = 0, 1


def _dot(a, b, **kw):
    return jnp.dot(a, b, preferred_element_type=F32, **kw)


def _params(*sem):
    return pltpu.CompilerParams(dimension_semantics=sem, vmem_limit_bytes=VMEM_LIMIT)


def _ada_kernel(c_ref, w_ref, b_ref, o_ref):
    c = c_ref[...]
    cond = c * jax.nn.sigmoid(c)
    o_ref[...] = _dot(cond, w_ref[...], precision=lax.Precision.HIGHEST) + b_ref[...]


def _ada(c, ada_w, ada_b):
    B, D = c.shape
    n_out = ada_w.shape[1]
    tn = 1024
    return pl.pallas_call(
        _ada_kernel,
        grid=(n_out // tn,),
        in_specs=[pl.BlockSpec((B, D), lambda j: (0, 0)),
                  pl.BlockSpec((D, tn), lambda j: (0, j)),
                  pl.BlockSpec((1, tn), lambda j: (0, j))],
        out_specs=pl.BlockSpec((B, tn), lambda j: (0, j)),
        out_shape=jax.ShapeDtypeStruct((B, n_out), F32),
        compiler_params=_params("arbitrary"),
        name="ada",
    )(c, ada_w, ada_b.reshape(1, n_out))


def _rms_mod(x, w, shift, scale):
    ms = jnp.mean(x * x, axis=-1, keepdims=True)
    y = x * lax.rsqrt(ms + EPS) * w
    return y * (1.0 + scale) + shift


def _proj_kernel(x_ref, mod_ref, nw_ref, wa_ref, wb_ref, wg_ref, oqa_ref, okv_ref, olru_ref, og_ref):
    h = _rms_mod(x_ref[...], nw_ref[...], mod_ref[0, 0:1, :], mod_ref[0, 1:2, :])
    hb = h.astype(BF16)
    pa = _dot(hb, wa_ref[...])
    oqa_ref[...] = pa[:, :QA_W]
    okv_ref[...] = pa[:, QA_W:]
    olru_ref[...] = _dot(hb, wb_ref[...])
    og_ref[...] = _dot(hb, wg_ref[...])


def _proj(x2, mod3, norm_w, wa, wb, wg, S):
    N, D = x2.shape
    tm = TM_PROJ
    steps_per_batch = S // tm
    const = lambda t: (0, 0)
    return pl.pallas_call(
        _proj_kernel,
        grid=(N // tm,),
        in_specs=[pl.BlockSpec((tm, D), lambda t: (t, 0)),
                  pl.BlockSpec((1, 6, D), lambda t: (t // steps_per_batch, 0, 0)),
                  pl.BlockSpec((1, D), const),
                  pl.BlockSpec(wa.shape, const),
                  pl.BlockSpec(wb.shape, const),
                  pl.BlockSpec(wg.shape, const)],
        out_specs=[pl.BlockSpec((tm, QA_W), lambda t: (t, 0)),
                   pl.BlockSpec((tm, KV_W), lambda t: (t, 0)),
                   pl.BlockSpec((tm, 2 * LRU_WIDTH), lambda t: (t, 0)),
                   pl.BlockSpec((tm, 2 * D), lambda t: (t, 0))],
        out_shape=[jax.ShapeDtypeStruct((N, QA_W), F32),
                   jax.ShapeDtypeStruct((N, KV_W), F32),
                   jax.ShapeDtypeStruct((N, 2 * LRU_WIDTH), F32),
                   jax.ShapeDtypeStruct((N, 2 * D), F32)],
        compiler_params=_params("arbitrary"),
        name="proj",
    )(x2, mod3, norm_w.reshape(1, D), wa, wb, wg)


ATTN_BUCKETS = 8
ROW_FOLD = 64
KEY_ROWS = 256
LOG2E = float(np.log2(np.e))
POS_RADIX = 256
SLOPE_PIECES = 3


def _bf16_pieces(c, n):
    out = []
    rest = float(c)
    for _ in range(n):
        p = float(np.asarray(rest, np.float32).astype(BF16).astype(np.float32))
        out.append(p)
        rest -= p
    return out


def _reduce_rows(op, x):
    r = x.shape[0]
    if r > ROW_FOLD and r % ROW_FOLD == 0:
        x = op(x.reshape(r // ROW_FOLD, ROW_FOLD, x.shape[1]), axis=0)
    return op(x, axis=0, keepdims=True)


def _attn_block(s_eff, topk, i, qa_ref, kvq_ref, qnw_ref, o_ref, kn_s, vt_s, ki_s, key_s, bias_s, lg_s, ot_s):
    nq = N_HEADS_A * HEAD_DIM_A
    q_t = qa_ref[:, :nq].T
    qi_t = qa_ref[:, nq:].T
    wq_t = kvq_ref[...].T[WI_OFF:WI_OFF + N_IDX_HEADS, :] * (N_IDX_HEADS ** -0.5 * IDX_DIM ** -0.5)

    q_pos = i * Q_BLOCK + lax.broadcasted_iota(I32, (1, Q_BLOCK), 1)
    limit = (q_pos // CHUNK + 1) * CHUNK
    rows_per = KEY_ROWS if s_eff % KEY_ROWS == 0 else Q_BLOCK
    chunks = [slice(c * rows_per, (c + 1) * rows_per) for c in range(s_eff // rows_per)]

    qi = [qi_t[h * IDX_DIM:(h + 1) * IDX_DIM, :].astype(BF16) for h in range(N_IDX_HEADS)]
    for rows in chunks:
        ki = ki_s[rows, :]
        score = jnp.zeros((rows_per, Q_BLOCK), F32)
        for h in range(N_IDX_HEADS):
            score = score + jnp.maximum(_dot(ki, qi[h]), 0.0) * wq_t[h:h + 1, :]
        k_pos = rows.start + lax.broadcasted_iota(I32, (rows_per, Q_BLOCK), 0)
        key_s[rows, :] = jnp.where(k_pos < limit, score, -jnp.inf)

    def count(mask):
        return _reduce_rows(jnp.sum, jnp.where(mask, 1.0, 0.0))

    def ordered_to_float(c):
        s = c ^ INT_MIN
        return pltpu.bitcast(jnp.where(s < 0, INT_MIN | (-s), s), F32)

    def bisect(it, r):
        c = r | jnp.left_shift(jnp.int32(1), 31 - it)
        return jnp.where(count(key_s[:s_eff, :] >= ordered_to_float(c)) >= topk, c, r)

    thr = ordered_to_float(lax.fori_loop(0, 32, bisect, jnp.zeros((1, Q_BLOCK), I32)))

    n_ge = count(key_s[:s_eff, :] >= thr)
    clean = jnp.max(jnp.where((n_ge == topk) & (thr > -jnp.inf), 0.0, 1.0)) == 0.0

    @pl.when(clean)
    def _():
        for rows in chunks:
            bias_s[rows, :] = jnp.where(key_s[rows, :] >= thr, 0.0, -jnp.inf)

    @pl.when(jnp.logical_not(clean))
    def _():
        key = key_s[:s_eff, :]
        need = topk - count(key > thr)
        rr = lax.broadcasted_iota(I32, (LANES, LANES), 0)
        cc = lax.broadcasted_iota(I32, (LANES, LANES), 1)
        tri = jnp.where(cc < rr, 1.0, 0.0).astype(BF16)
        seen = jnp.zeros((1, Q_BLOCK), F32)
        for c in range(s_eff // LANES):
            kc = key[c * LANES:(c + 1) * LANES, :]
            eq = kc == thr
            eqf = jnp.where(eq, 1.0, 0.0)
            rank = _dot(tri, eqf.astype(BF16)) + seen
            seen = seen + jnp.sum(eqf, axis=0, keepdims=True)
            tie_bias = jnp.where(jnp.where(eq, rank, topk) < need, 0.0, -jnp.inf)
            bias = jnp.where(kc > thr, 0.0, tie_bias)
            k_pos = c * LANES + lax.broadcasted_iota(I32, (LANES, Q_BLOCK), 0)
            bias_s[c * LANES:(c + 1) * LANES, :] = jnp.where(k_pos < limit, bias, -jnp.inf)

    qnw = qnw_ref[...]
    crow = lax.broadcasted_iota(I32, (LANES - HEAD_DIM_A, Q_BLOCK), 0)
    r_k = lax.broadcasted_iota(I32, (Q_BLOCK, Q_BLOCK), 0)
    c_q = lax.broadcasted_iota(I32, (Q_BLOCK, Q_BLOCK), 1)
    after = 2.0 * jnp.maximum(r_k - c_q, 0).astype(F32)
    band = pl.ds(pl.multiple_of(i * Q_BLOCK, Q_BLOCK), Q_BLOCK)
    for h in range(N_HEADS_A):
        pieces = _bf16_pieces(LOG2E * 2.0 ** -(h + 1), SLOPE_PIECES)
        qh = q_t[h * HEAD_DIM_A:(h + 1) * HEAD_DIM_A, :]
        ms = jnp.mean(qh * qh, axis=0, keepdims=True)
        qn = (qh * lax.rsqrt(ms + EPS) * qnw) * (HEAD_DIM_A ** -0.5) * LOG2E
        slopes = jnp.zeros(crow.shape, F32)
        for k, piece in enumerate(pieces):
            slopes = jnp.where(crow == 2 * k, POS_RADIX * piece, jnp.where(crow == 2 * k + 1, piece, slopes))
        qa = jnp.concatenate([qn, slopes], axis=0).astype(BF16)
        maxes = []
        for rows in chunks:
            logits = _dot(kn_s[rows, :], qa) + bias_s[rows, :]
            lg_s[h, rows, :] = logits
            maxes.append(_reduce_rows(jnp.max, logits))
        m = functools.reduce(jnp.maximum, maxes)
        lg_s[h, band, :] = lg_s[h, band, :] - sum(pieces) * after
        l = jnp.zeros((1, Q_BLOCK), F32)
        acc = jnp.zeros((KV_DIM_A, Q_BLOCK), F32)
        for rows in chunks:
            p = jnp.exp2(lg_s[h, rows, :] - m)
            l = l + _reduce_rows(jnp.sum, p)
            acc = acc + _dot(vt_s[:, rows], p.astype(BF16))
        ot_s[h * KV_DIM_A:(h + 1) * KV_DIM_A, :] = acc / l
    o_ref[...] = ot_s[...].T


def _attn_kernel(qa_ref, kvq_ref, kv_ref, qnw_ref, knw_ref, o_ref, kn_s, vt_s, ki_s, key_s, bias_s, lg_s, ot_s):
    i = pl.program_id(1)
    S = kv_ref.shape[0]
    nb = S // Q_BLOCK
    topk = float(min(TOPK_MAX, S // 4))

    @pl.when(i == 0)
    def _():
        kv = kv_ref[:, :LANES]
        lane = lax.broadcasted_iota(I32, kv.shape, 1)
        pos = lax.broadcasted_iota(I32, kv.shape, 0)
        ms = jnp.sum(jnp.where(lane < KV_DIM_A, kv * kv, 0.0), axis=-1, keepdims=True) * (1.0 / KV_DIM_A)
        kn = kv * lax.rsqrt(ms + EPS) * knw_ref[...]
        digit = jnp.where((lane & 1) == 0, pos // POS_RADIX, pos % POS_RADIX).astype(F32)
        is_digit = (lane >= KV_DIM_A) & (lane < KV_DIM_A + 2 * SLOPE_PIECES)
        kn_s[...] = jnp.where(is_digit, digit, kn).astype(BF16)
        vt_s[...] = kv.T[V_OFF:V_OFF + KV_DIM_A, :].astype(BF16)
        ki_s[...] = kv_ref[:, KI_OFF:KI_OFF + IDX_DIM].astype(BF16)

    n_buckets = min(ATTN_BUCKETS, nb)
    per = nb // n_buckets
    for j in range(n_buckets):
        pl.when(i // per == j)(functools.partial(
            _attn_block, (j + 1) * per * Q_BLOCK, topk, i, qa_ref, kvq_ref, qnw_ref, o_ref,
            kn_s, vt_s, ki_s, key_s, bias_s, lg_s, ot_s))


def _attn(qa, kv, q_norm_w, k_norm_w, B, S):
    N = qa.shape[0]
    nb = S // Q_BLOCK
    return pl.pallas_call(
        _attn_kernel,
        grid=(B, nb),
        in_specs=[pl.BlockSpec((Q_BLOCK, QA_W), lambda b, i: (b * nb + i, 0)),
                  pl.BlockSpec((Q_BLOCK, KV_W), lambda b, i: (b * nb + i, 0)),
                  pl.BlockSpec((S, KV_W), lambda b, i: (b, 0)),
                  pl.BlockSpec((HEAD_DIM_A, 1), lambda b, i: (0, 0)),
                  pl.BlockSpec((1, LANES), lambda b, i: (0, 0))],
        out_specs=pl.BlockSpec((Q_BLOCK, ATTN_OUT), lambda b, i: (b * nb + i, 0)),
        out_shape=jax.ShapeDtypeStruct((N, ATTN_OUT), F32),
        scratch_shapes=[pltpu.VMEM((S, LANES), BF16),
                        pltpu.VMEM((KV_DIM_A, S), BF16),
                        pltpu.VMEM((S, IDX_DIM), BF16),
                        pltpu.VMEM((S, Q_BLOCK), F32),
                        pltpu.VMEM((S, Q_BLOCK), F32),
                        pltpu.VMEM((N_HEADS_A, S, Q_BLOCK), F32),
                        pltpu.VMEM((ATTN_OUT, Q_BLOCK), F32)],
        compiler_params=_params("arbitrary", "arbitrary"),
        name="attn",
    )(qa, kv, kv, q_norm_w.reshape(HEAD_DIM_A, 1),
      jnp.pad(k_norm_w, (0, LANES - KV_DIM_A)).reshape(1, LANES))


SUBLANES = 8


def _shift_rows(x, d, fill, row):
    if d % SUBLANES == 0:
        return jnp.concatenate([jnp.full((d, x.shape[1]), fill, x.dtype), x[:-d]], axis=0)
    return jnp.where(row >= d, pltpu.roll(x, d, 0), fill)


def _sigmoid(x):
    return 0.5 * (1.0 + jnp.tanh(0.5 * x))


def _lru_kernel(xb_ref, gb_ref, cw_ref, cb_ref, wr_ref, br_ref, wi_ref, bi_ref, lam_ref, o_ref):
    x = xb_ref[...]
    S = x.shape[0]
    row = lax.broadcasted_iota(I32, x.shape, 0)
    xc = cb_ref[...] + _shift_rows(x, CONV_WIDTH - 1, 0.0, row) * cw_ref[0:1, :]
    for j in range(1, CONV_WIDTH):
        d = CONV_WIDTH - 1 - j
        xs = x if d == 0 else _shift_rows(x, d, 0.0, row)
        xc = xc + xs * cw_ref[j:j + 1, :]
    xcb = xc.astype(BF16)
    r = _sigmoid(_dot(xcb, wr_ref[0]) + br_ref[...])
    ig = _sigmoid(_dot(xcb, wi_ref[0]) + bi_ref[...])
    z = -lam_ref[...]
    softplus = jnp.maximum(z, 0.0) + jnp.log1p(jnp.exp(-jnp.abs(z)))
    log_a = -LRU_C * r * softplus
    a = jnp.exp(log_a)
    b = jnp.sqrt(-jnp.tanh(log_a) * (1.0 + a * a)) * (ig * xc)
    d = 1
    while d < S:
        b = a * _shift_rows(b, d, 0.0, row) + b
        if 2 * d < S:
            a = a * _shift_rows(a, d, 1.0, row)
        d *= 2
    g = gb_ref[...]
    gelu = 0.5 * g * (1.0 + jnp.tanh(float(np.sqrt(2.0 / np.pi)) * (g + 0.044715 * (g * g * g))))
    o_ref[...] = b * gelu


def _lru(lru, conv_w, conv_b, wr_bd, b_rec, wi_bd, b_in, lam, B, S):
    N = lru.shape[0]
    nt = LRU_WIDTH // LANES
    vec = lambda b, j: (0, j)
    return pl.pallas_call(
        _lru_kernel,
        grid=(B, nt),
        in_specs=[pl.BlockSpec((S, LANES), lambda b, j: (b, j)),
                  pl.BlockSpec((S, LANES), lambda b, j: (b, nt + j)),
                  pl.BlockSpec((CONV_WIDTH, LANES), vec),
                  pl.BlockSpec((1, LANES), vec),
                  pl.BlockSpec((1, LANES, LANES), lambda b, j: (j, 0, 0)),
                  pl.BlockSpec((1, LANES), vec),
                  pl.BlockSpec((1, LANES, LANES), lambda b, j: (j, 0, 0)),
                  pl.BlockSpec((1, LANES), vec),
                  pl.BlockSpec((1, LANES), vec)],
        out_specs=pl.BlockSpec((S, LANES), lambda b, j: (b, j)),
        out_shape=jax.ShapeDtypeStruct((N, LRU_WIDTH), F32),
        compiler_params=_params("arbitrary", "arbitrary"),
        name="lru",
    )(lru, lru, conv_w, conv_b.reshape(1, -1), wr_bd, b_rec.reshape(1, -1), wi_bd,
      b_in.reshape(1, -1), lam.reshape(1, -1))


def _block_diag_tiles(w):
    per = LANES // LRU_BLOCK_DIM
    nt = LRU_WIDTH // LANES
    w5 = w.reshape(nt, per, LRU_BLOCK_DIM, 1, LRU_BLOCK_DIM)
    eye = jnp.eye(per, dtype=w.dtype).reshape(1, per, 1, per, 1)
    return (w5 * eye).reshape(nt, LANES, LANES)


def _first_index_of_max(vals, lane):
    m = jnp.max(vals, axis=-1, keepdims=True)
    idx = jnp.min(jnp.where(vals == m, lane, 4 * LANES), axis=-1, keepdims=True)
    return m, idx


def _weight_pieces(w, lane):
    out = jnp.zeros(lane.shape, F32)
    rest = w
    for k in range(W_PIECES):
        piece = rest.astype(BF16).astype(F32)
        out = jnp.where(lane == k, piece, out)
        rest = rest - piece
    return out.astype(BF16)


def _one_hot_rows(n_rows, pos_row):
    rows = lax.broadcasted_iota(I32, (n_rows, pos_row.shape[1]), 0)
    return jnp.where(rows == pos_row, 1.0, 0.0).astype(BF16)


def _merge_kernel(attn_ref, lru_ref, g_ref, x_ref, mod_ref, wpa_ref, wpb_ref, wo_ref, nw_ref, wrh_ref, wrl_ref,
                  br_ref, x1_ref, hsl_ref, route_ref, gc_ref):
    D = x_ref.shape[1]
    tm = x_ref.shape[0]

    y_a = _dot(attn_ref[...].astype(BF16), wpa_ref[...])
    y_b = _dot(lru_ref[...].astype(BF16), wpb_ref[...])
    merged = _sigmoid(g_ref[:, :D]) * y_a + _sigmoid(g_ref[:, D:]) * y_b
    x1 = x_ref[...] + mod_ref[0, 2:3, :] * _dot(merged.astype(BF16), wo_ref[...])
    x1_ref[...] = x1
    h2 = _rms_mod(x1, nw_ref[...], mod_ref[0, 3:4, :], mod_ref[0, 4:5, :])
    h2b = h2.astype(BF16)

    h2l = (h2 - h2b.astype(F32)).astype(BF16)
    wrh = wrh_ref[...]
    logits = _dot(h2b, wrh) + (_dot(h2l, wrh) + _dot(h2b, wrl_ref[...])) + br_ref[...]
    lane = lax.broadcasted_iota(I32, logits.shape, 1)
    gl = jnp.where(lane < N_GROUPS, logits, -jnp.inf)
    gmax, g_sel = _first_index_of_max(gl, lane)
    g_weight = 1.0 / jnp.sum(jnp.exp(gl - gmax), axis=-1, keepdims=True)
    e_lo = N_GROUPS + g_sel * EXPERTS_PER_GROUP
    el = jnp.where((lane >= e_lo) & (lane < e_lo + EXPERTS_PER_GROUP), logits, -jnp.inf)
    v1, i1 = _first_index_of_max(el, lane)
    el2 = jnp.where(lane == i1, -jnp.inf, el)
    v2, i2 = _first_index_of_max(el2, lane)
    e2x = jnp.exp(v2 - v1)
    w1 = g_weight / (1.0 + e2x)
    w2 = g_weight * e2x / (1.0 + e2x)

    hot1 = lane == i1
    hot2 = lane == i2
    hot = jnp.where(hot1, 1.0, jnp.where(hot2, 1.0, 0.0))
    gcnt = jnp.floor((jnp.sum(hot, axis=0, keepdims=True) + (GRAN - 1.0)) * (1.0 / GRAN))
    gc_ref[0] = gcnt
    rr = lax.broadcasted_iota(I32, (LANES, LANES), 0)
    cc = lax.broadcasted_iota(I32, (LANES, LANES), 1)
    upper = jnp.where(rr < cc, 1.0, 0.0).astype(BF16)
    run_start = GRAN * _dot(jnp.broadcast_to(gcnt, (8, LANES)).astype(BF16), upper)[0:1, :]
    rr = lax.broadcasted_iota(I32, (tm, tm), 0)
    cc = lax.broadcasted_iota(I32, (tm, tm), 1)
    lower = jnp.where(cc < rr, 1.0, 0.0).astype(BF16)
    pos = _dot(lower, hot.astype(BF16)) + run_start
    pos1 = jnp.where(hot1, pos, 0.0)
    pos2 = jnp.where(hot2, pos, 0.0)
    lp1 = jnp.sum(pos1, axis=-1, keepdims=True)
    lp2 = jnp.sum(pos2, axis=-1, keepdims=True)
    route_ref[...] = jnp.where(lane == R_LP1, lp1, jnp.where(lane == R_LP2, lp2, 0.0))

    p1 = _one_hot_rows(LOCAL_ROWS, jnp.sum(pos1.T, axis=0, keepdims=True).astype(I32))
    p2 = _one_hot_rows(LOCAL_ROWS, jnp.sum(pos2.T, axis=0, keepdims=True).astype(I32))
    hsl_ref[:, :D] = _dot(p1 + p2, h2b).astype(BF16)
    aux = _dot(p1, _weight_pieces(w1, lane)) + _dot(p2, _weight_pieces(w2, lane))
    hsl_ref[:, D:] = aux.astype(BF16)


def _merge(attn, lru_out, gates, x2, mod3, wpa, wpb, wo, norm_w, wr_hi, wr_lo, b_router, S):
    N, D = x2.shape
    tm = TM_SORT
    n_tiles = N // tm
    steps_per_batch = S // tm
    const = lambda t: (0, 0)
    rows = lambda t: (t, 0)
    return pl.pallas_call(
        _merge_kernel,
        grid=(n_tiles,),
        in_specs=[pl.BlockSpec((tm, ATTN_OUT), rows),
                  pl.BlockSpec((tm, LRU_WIDTH), rows),
                  pl.BlockSpec((tm, 2 * D), rows),
                  pl.BlockSpec((tm, D), rows),
                  pl.BlockSpec((1, 6, D), lambda t: (t // steps_per_batch, 0, 0)),
                  pl.BlockSpec(wpa.shape, const),
                  pl.BlockSpec(wpb.shape, const),
                  pl.BlockSpec(wo.shape, const),
                  pl.BlockSpec((1, D), const),
                  pl.BlockSpec(wr_hi.shape, const),
                  pl.BlockSpec(wr_lo.shape, const),
                  pl.BlockSpec((1, LANES), const)],
        out_specs=[pl.BlockSpec((tm, D), rows),
                   pl.BlockSpec((LOCAL_ROWS, D + AUX_W), rows),
                   pl.BlockSpec((tm, LANES), rows),
                   pl.BlockSpec((1, 1, LANES), lambda t: (t, 0, 0))],
        out_shape=[jax.ShapeDtypeStruct((N, D), F32),
                   jax.ShapeDtypeStruct((n_tiles * LOCAL_ROWS, D + AUX_W), BF16),
                   jax.ShapeDtypeStruct((N, LANES), F32),
                   jax.ShapeDtypeStruct((n_tiles, 1, LANES), F32)],
        compiler_params=_params("arbitrary"),
        name="merge",
    )(attn, lru_out, gates, x2, mod3, wpa, wpb, wo, norm_w.reshape(1, D), wr_hi, wr_lo, b_router)


def _plan_sizes(n_tiles):
    max_gran = n_tiles * LOCAL_GRAN + N_EXPERTS * (GRAN_PER_TILE - 1)
    max_row_tiles = -(-max_gran // GRAN_PER_TILE)
    return max_row_tiles * GRAN_PER_TILE, max_row_tiles


def _plan_kernel(gc_ref, src_ref, texp_ref, nt_ref, dst_ref, local_next):
    n_tiles = gc_ref.shape[0]
    max_gran, max_row_tiles = _plan_sizes(n_tiles)

    def zero(ref, lo, hi):
        def body(k, c):
            ref[k] = 0
            return c
        lax.fori_loop(lo, hi, body, 0)

    zero(local_next, 0, n_tiles)

    def per_expert(e, g):
        first_tile = g // GRAN_PER_TILE

        def per_tile(t, g):
            n = gc_ref[t, N_GROUPS + e]
            base = t * LOCAL_GRAN + local_next[t]

            def per_granule(k, g):
                src_ref[g] = base + k
                dst_ref[base + k] = g
                return g + 1

            g = lax.fori_loop(0, n, per_granule, g)
            local_next[t] = local_next[t] + n
            return g

        g = lax.fori_loop(0, n_tiles, per_tile, g)
        padded = (g + GRAN_PER_TILE - 1) // GRAN_PER_TILE * GRAN_PER_TILE
        zero(src_ref, g, padded)
        g = padded

        def mark(j, c):
            texp_ref[j] = e
            return c

        lax.fori_loop(first_tile, g // GRAN_PER_TILE, mark, 0)
        return g

    g = lax.fori_loop(0, N_EXPERTS, per_expert, 0)
    used = g // GRAN_PER_TILE
    nt_ref[0] = used
    zero(src_ref, g, max_gran)

    def clear_tail(t, c):
        zero(dst_ref, t * LOCAL_GRAN + local_next[t], (t + 1) * LOCAL_GRAN)
        return c

    lax.fori_loop(0, n_tiles, clear_tail, 0)

    def mark_rest(j, c):
        texp_ref[j] = N_EXPERTS - 1
        return c

    lax.fori_loop(used, max_row_tiles, mark_rest, 0)


def _plan(gc):
    n_tiles = gc.shape[0]
    max_gran, max_row_tiles = _plan_sizes(n_tiles)
    smem = pl.BlockSpec(memory_space=pltpu.SMEM)
    return pl.pallas_call(
        _plan_kernel,
        in_specs=[smem],
        out_specs=[smem, smem, smem, smem],
        out_shape=[jax.ShapeDtypeStruct((max_gran,), I32),
                   jax.ShapeDtypeStruct((max_row_tiles,), I32),
                   jax.ShapeDtypeStruct((1,), I32),
                   jax.ShapeDtypeStruct((n_tiles * LOCAL_GRAN,), I32)],
        scratch_shapes=[pltpu.SMEM((n_tiles,), I32)],
        name="plan",
    )(gc)


def _granule_copies(idx_ref, first, count, src_hbm, dst_vmem, sem, wait):
    def body(k, c):
        g = idx_ref[first + k]
        cp = pltpu.make_async_copy(src_hbm.at[pl.ds(pl.multiple_of(g * GRAN, GRAN), GRAN)],
                                   dst_vmem.at[pl.ds(pl.multiple_of(k * GRAN, GRAN), GRAN)], sem)
        if wait:
            cp.wait()
        else:
            cp.start()
        return c
    lax.fori_loop(0, count, body, 0)


def _moe_kernel(src_ref, texp_ref, nt_ref, hsl_ref, w1_ref, w3_ref, w2_ref, ys_ref, xbuf, w1b, w3b, w2b, sem):
    i = pl.program_id(0)
    used = nt_ref[0]
    slot = i % MOE_BUFFERS
    D = ys_ref.shape[1]

    def gather(tile, wait):
        slot_ = tile % MOE_BUFFERS
        _granule_copies(src_ref, tile * GRAN_PER_TILE, GRAN_PER_TILE, hsl_ref, xbuf.at[slot_], sem.at[slot_], wait)

    @pl.when(i == 0)
    def _():
        for ahead in range(MOE_BUFFERS - 1):
            pl.when(ahead < used)(functools.partial(gather, ahead, False))

    @pl.when(i + (MOE_BUFFERS - 1) < used)
    def _():
        gather(i + (MOE_BUFFERS - 1), False)

    @pl.when(i < used)
    def _():
        gather(i, True)

        @pl.when((i == 0) | (texp_ref[i] != texp_ref[jnp.maximum(i - 1, 0)]))
        def _():
            w1b[...] = w1_ref[0].astype(BF16)
            w3b[...] = w3_ref[0].astype(BF16)
            w2b[...] = w2_ref[0].astype(BF16)

        xb = xbuf[slot]
        hb = xb[:, :D]
        aux = xb[:, D:].astype(F32)
        w_row = aux[:, 0:1]
        for k in range(1, W_PIECES):
            w_row = w_row + aux[:, k:k + 1]
        a = _dot(hb, w1b[...])
        act = (a * _sigmoid(a)) * _dot(hb, w3b[...])
        ys_ref[...] = (w_row * _dot(act.astype(BF16), w2b[...])).astype(BF16)

    @pl.when(i >= used)
    def _():
        ys_ref[...] = jnp.zeros_like(ys_ref)


def _moe(src, texp, used, hsl, w1, w3, w2):
    max_row_tiles = texp.shape[0]
    D = w1.shape[1]
    F = w1.shape[2]
    grid_spec = pltpu.PrefetchScalarGridSpec(
        num_scalar_prefetch=3,
        grid=(max_row_tiles,),
        in_specs=[pl.BlockSpec(memory_space=pl.ANY),
                  pl.BlockSpec((1, D, F), lambda i, s, te, nt: (te[i], 0, 0)),
                  pl.BlockSpec((1, D, F), lambda i, s, te, nt: (te[i], 0, 0)),
                  pl.BlockSpec((1, F, D), lambda i, s, te, nt: (te[i], 0, 0))],
        out_specs=pl.BlockSpec((TM_MOE, D), lambda i, s, te, nt: (i, 0)),
        scratch_shapes=[pltpu.VMEM((MOE_BUFFERS, TM_MOE, D + AUX_W), BF16),
                        pltpu.VMEM((D, F), BF16), pltpu.VMEM((D, F), BF16), pltpu.VMEM((F, D), BF16),
                        pltpu.SemaphoreType.DMA((MOE_BUFFERS,))],
    )
    return pl.pallas_call(
        _moe_kernel,
        grid_spec=grid_spec,
        out_shape=jax.ShapeDtypeStruct((max_row_tiles * TM_MOE, D), BF16),
        compiler_params=_params("arbitrary"),
        name="moe",
    )(src, texp, used, hsl, w1, w3, w2)


def _combine_kernel(dst_ref, ys_ref, x1_ref, route_ref, mod_ref, o_ref, ybuf, sem):
    t = pl.program_id(0)
    slot = t % 2
    tm = x1_ref.shape[0]

    def gather(tile, slot_, wait):
        _granule_copies(dst_ref, tile * LOCAL_GRAN, LOCAL_GRAN, ys_ref, ybuf.at[slot_], sem.at[slot_], wait)

    @pl.when(t == 0)
    def _():
        gather(0, 0, False)

    @pl.when(t + 1 < pl.num_programs(0))
    def _():
        gather(t + 1, 1 - slot, False)

    gather(t, slot, True)
    lp1 = route_ref[:, R_LP1:R_LP1 + 1].astype(I32)
    lp2 = route_ref[:, R_LP2:R_LP2 + 1].astype(I32)
    col = lax.broadcasted_iota(I32, (tm, LOCAL_ROWS), 1)
    unsort = jnp.where(col == lp1, 1.0, jnp.where(col == lp2, 1.0, 0.0)).astype(BF16)
    o_ref[...] = x1_ref[...] + mod_ref[0, 5:6, :] * _dot(unsort, ybuf[slot])


def _combine(dst, ys, x1, route, mod3, S):
    N, D = x1.shape
    tm = TM_SORT
    steps_per_batch = S // tm
    grid_spec = pltpu.PrefetchScalarGridSpec(
        num_scalar_prefetch=1,
        grid=(N // tm,),
        in_specs=[pl.BlockSpec(memory_space=pl.ANY),
                  pl.BlockSpec((tm, D), lambda t, d: (t, 0)),
                  pl.BlockSpec((tm, LANES), lambda t, d: (t, 0)),
                  pl.BlockSpec((1, 6, D), lambda t, d: (t // steps_per_batch, 0, 0))],
        out_specs=pl.BlockSpec((tm, D), lambda t, d: (t, 0)),
        scratch_shapes=[pltpu.VMEM((2, LOCAL_ROWS, D), BF16), pltpu.SemaphoreType.DMA((2,))],
    )
    return pl.pallas_call(
        _combine_kernel,
        grid_spec=grid_spec,
        out_shape=jax.ShapeDtypeStruct((N, D), F32),
        compiler_params=_params("arbitrary"),
        name="combine",
    )(dst, ys, x1, route, mod3)


def _layer(x2, mod3, B, S, norm_mix_w, w_in, q_norm_w, k_norm_w, conv_w, conv_b, w_rec_gate, b_rec_gate,
           w_in_gate, b_in_gate, lru_lambda, w_proj_a, w_proj_b, w_out, norm_ffn_w, w_group, b_group,
           w_expert_router, b_expert_router, w1, w3, w2):
    N, D = x2.shape
    nq = N_HEADS_A * HEAD_DIM_A
    o_k = nq
    o_v = o_k + KV_DIM_A
    o_qi = o_v + KV_DIM_A
    o_ki = o_qi + N_IDX_HEADS * IDX_DIM
    o_wi = o_ki + IDX_DIM
    o_lx = o_wi + N_IDX_HEADS
    o_g = o_lx + 2 * LRU_WIDTH
    pad = jnp.zeros((D, KV_W - (WI_OFF + N_IDX_HEADS)), w_in.dtype)
    wa = jnp.concatenate([w_in[:, :nq], w_in[:, o_qi:o_ki], w_in[:, o_k:o_qi], w_in[:, o_ki:o_lx], pad],
                         axis=1).astype(BF16)
    wb = w_in[:, o_lx:o_g].astype(BF16)
    wg = w_in[:, o_g:].astype(BF16)

    qa, kv, lru, gates = _proj(x2, mod3, norm_mix_w, wa, wb, wg, S)
    attn = _attn(qa, kv, q_norm_w, k_norm_w, B, S)
    lru_out = _lru(lru, conv_w, conv_b, _block_diag_tiles(w_rec_gate).astype(BF16), b_rec_gate,
                   _block_diag_tiles(w_in_gate).astype(BF16), b_in_gate, lru_lambda, B, S)

    n_r = N_GROUPS + N_EXPERTS
    w_router = jnp.concatenate([w_group, w_expert_router, jnp.zeros((D, LANES - n_r), F32)], axis=1)
    wr_hi = w_router.astype(BF16)
    wr_lo = (w_router - wr_hi.astype(F32)).astype(BF16)
    b_router = jnp.concatenate([b_group, b_expert_router, jnp.zeros((LANES - n_r,), F32)]).reshape(1, LANES)
    x1, hsl, route, gc = _merge(attn, lru_out, gates, x2, mod3, w_proj_a.astype(BF16), w_proj_b.astype(BF16),
                                w_out.astype(BF16), norm_ffn_w, wr_hi, wr_lo, b_router, S)

    src, texp, used, dst = _plan(gc.reshape(gc.shape[0], LANES).astype(I32))
    ys = _moe(src, texp, used, hsl, w1, w3, w2)
    return _combine(dst, ys, x1, route, mod3, S)


def kernel(x, c, ada_w, ada_b, norm_mix_w, w_in, q_norm_w, k_norm_w, conv_w, conv_b, w_rec_gate, b_rec_gate,
           w_in_gate, b_in_gate, lru_lambda, w_proj_a, w_proj_b, w_out, norm_ffn_w, w_group, b_group,
           w_expert_router, b_expert_router, w1, w3, w2):
    B, S, D = x.shape
    x2 = x.reshape(B * S, D)
    for l in range(ada_w.shape[0]):
        mod3 = _ada(c, ada_w[l], ada_b[l]).reshape(B, 6, D)
        x2 = _layer(x2, mod3, B, S, norm_mix_w[l], w_in[l], q_norm_w[l], k_norm_w[l], conv_w[l], conv_b[l],
                    w_rec_gate[l], b_rec_gate[l], w_in_gate[l], b_in_gate[l], lru_lambda[l], w_proj_a[l],
                    w_proj_b[l], w_out[l], norm_ffn_w[l], w_group[l], b_group[l], w_expert_router[l],
                    b_expert_router[l], w1[l], w3[l], w2[l])
    return x2.reshape(B, S, D)
```

```python
import functools

import jax
import jax.numpy as jnp
import numpy as np
from jax import lax
from jax.experimental import pallas as pl
from jax.experimental.pallas import tpu as pltpu

F32 = jnp.float32
BF16 = jnp.bfloat16
I32 = jnp.int32

EPS = 1e-6
CHUNK = 64
Q_BLOCK = 128
N_HEADS_A = 8
HEAD_DIM_A = 64
KV_DIM_A = 64
ATTN_OUT = N_HEADS_A * KV_DIM_A
N_IDX_HEADS = 4
IDX_DIM = 64
TOPK_MAX = 256
LRU_WIDTH = 512
LRU_BLOCKS = 8
LRU_BLOCK_DIM = LRU_WIDTH // LRU_BLOCKS
CONV_WIDTH = 4
LRU_C = 8.0
N_GROUPS = 4
EXPERTS_PER_GROUP = 8
N_EXPERTS = N_GROUPS * EXPERTS_PER_GROUP
D_FF_EXPERT = 256

LANES = 128
INT_MIN = -2 ** 31
VMEM_LIMIT = 56 * 1024 * 1024

QA_W = N_HEADS_A * HEAD_DIM_A + N_IDX_HEADS * IDX_DIM
KV_W = 256
K_OFF, V_OFF, KI_OFF, WI_OFF = 0, 64, 128, 192

TM_PROJ = 256

TM_SORT = 512
GRAN = 16
TM_MOE = 256
GRAN_PER_TILE = TM_MOE // GRAN
MOE_BUFFERS = 3
LOCAL_ROWS = -(-(2 * TM_SORT + N_EXPERTS * (GRAN - 1)) // LANES) * LANES
LOCAL_GRAN = LOCAL_ROWS // GRAN
AUX_W = LANES
W_PIECES = 3
AUX_W1, AUX_W2 = 0, W_PIECES
AUX_E1 = 2 * W_PIECES

R_LP1, R_LP2 = 0, 1


def _dot(a, b, **kw):
    return jnp.dot(a, b, preferred_element_type=F32, **kw)


def _params(*sem):
    return pltpu.CompilerParams(dimension_semantics=sem, vmem_limit_bytes=VMEM_LIMIT)


def _ada_kernel(c_ref, w_ref, b_ref, o_ref):
    c = c_ref[...]
    cond = c * jax.nn.sigmoid(c)
    o_ref[...] = _dot(cond, w_ref[...], precision=lax.Precision.HIGHEST) + b_ref[...]


def _ada(c, ada_w, ada_b):
    B, D = c.shape
    n_out = ada_w.shape[1]
    tn = 1024
    return pl.pallas_call(
        _ada_kernel,
        grid=(n_out // tn,),
        in_specs=[pl.BlockSpec((B, D), lambda j: (0, 0)),
                  pl.BlockSpec((D, tn), lambda j: (0, j)),
                  pl.BlockSpec((1, tn), lambda j: (0, j))],
        out_specs=pl.BlockSpec((B, tn), lambda j: (0, j)),
        out_shape=jax.ShapeDtypeStruct((B, n_out), F32),
        compiler_params=_params("arbitrary"),
        name="ada",
    )(c, ada_w, ada_b.reshape(1, n_out))


def _rms_mod(x, w, shift, scale):
    ms = jnp.mean(x * x, axis=-1, keepdims=True)
    y = x * lax.rsqrt(ms + EPS) * w
    return y * (1.0 + scale) + shift


def _proj_kernel(x_ref, mod_ref, nw_ref, wa_ref, wb_ref, wg_ref, oqa_ref, okv_ref, olru_ref, og_ref):
    h = _rms_mod(x_ref[...], nw_ref[...], mod_ref[0, 0:1, :], mod_ref[0, 1:2, :])
    hb = h.astype(BF16)
    pa = _dot(hb, wa_ref[...])
    oqa_ref[...] = pa[:, :QA_W]
    okv_ref[...] = pa[:, QA_W:]
    olru_ref[...] = _dot(hb, wb_ref[...])
    og_ref[...] = _dot(hb, wg_ref[...])


def _proj(x2, mod3, norm_w, wa, wb, wg, S):
    N, D = x2.shape
    tm = TM_PROJ
    steps_per_batch = S // tm
    const = lambda t: (0, 0)
    return pl.pallas_call(
        _proj_kernel,
        grid=(N // tm,),
        in_specs=[pl.BlockSpec((tm, D), lambda t: (t, 0)),
                  pl.BlockSpec((1, 6, D), lambda t: (t // steps_per_batch, 0, 0)),
                  pl.BlockSpec((1, D), const),
                  pl.BlockSpec(wa.shape, const),
                  pl.BlockSpec(wb.shape, const),
                  pl.BlockSpec(wg.shape, const)],
        out_specs=[pl.BlockSpec((tm, QA_W), lambda t: (t, 0)),
                   pl.BlockSpec((tm, KV_W), lambda t: (t, 0)),
                   pl.BlockSpec((tm, 2 * LRU_WIDTH), lambda t: (t, 0)),
                   pl.BlockSpec((tm, 2 * D), lambda t: (t, 0))],
        out_shape=[jax.ShapeDtypeStruct((N, QA_W), F32),
                   jax.ShapeDtypeStruct((N, KV_W), F32),
                   jax.ShapeDtypeStruct((N, 2 * LRU_WIDTH), F32),
                   jax.ShapeDtypeStruct((N, 2 * D), F32)],
        compiler_params=_params("arbitrary"),
        name="proj",
    )(x2, mod3, norm_w.reshape(1, D), wa, wb, wg)


ATTN_BUCKETS = 8
ROW_FOLD = 64
KEY_ROWS = 256
LOG2E = float(np.log2(np.e))
POS_RADIX = 256
SLOPE_PIECES = 3
V_ONES = 16


def _bf16_pieces(c, n):
    out = []
    rest = float(c)
    for _ in range(n):
        p = float(np.asarray(rest, np.float32).astype(BF16).astype(np.float32))
        out.append(p)
        rest -= p
    return out


def _reduce_rows(op, x):
    r = x.shape[0]
    if r > ROW_FOLD and r % ROW_FOLD == 0:
        x = op(x.reshape(r // ROW_FOLD, ROW_FOLD, x.shape[1]), axis=0)
    return op(x, axis=0, keepdims=True)


def _attn_block(s_eff, topk, i, qa_ref, kvq_ref, qnw_ref, o_ref, kn_s, vt_s, ki_s, key_s, bias_s, lg_s, ot_s):
    nq = N_HEADS_A * HEAD_DIM_A
    q_t = qa_ref[:, :nq].T
    qi_t = qa_ref[:, nq:].T
    wq_t = kvq_ref[...].T[WI_OFF:WI_OFF + N_IDX_HEADS, :] * (N_IDX_HEADS ** -0.5 * IDX_DIM ** -0.5)

    q_pos = i * Q_BLOCK + lax.broadcasted_iota(I32, (1, Q_BLOCK), 1)
    limit = (q_pos // CHUNK + 1) * CHUNK
    rows_per = KEY_ROWS if s_eff % KEY_ROWS == 0 else Q_BLOCK
    chunks = [slice(c * rows_per, (c + 1) * rows_per) for c in range(s_eff // rows_per)]

    qi = [qi_t[h * IDX_DIM:(h + 1) * IDX_DIM, :].astype(BF16) for h in range(N_IDX_HEADS)]
    for rows in chunks:
        ki = ki_s[rows, :]
        score = jnp.zeros((rows_per, Q_BLOCK), F32)
        for h in range(N_IDX_HEADS):
            score = score + jnp.maximum(_dot(ki, qi[h]), 0.0) * wq_t[h:h + 1, :]
        k_pos = rows.start + lax.broadcasted_iota(I32, (rows_per, Q_BLOCK), 0)
        key_s[rows, :] = jnp.where(k_pos < limit, score, -jnp.inf)

    def count(mask):
        return _reduce_rows(jnp.sum, jnp.where(mask, 1.0, 0.0))

    def ordered_to_float(c):
        s = c ^ INT_MIN
        return pltpu.bitcast(jnp.where(s < 0, INT_MIN | (-s), s), F32)

    def bisect(it, r):
        c = r | jnp.left_shift(jnp.int32(1), 31 - it)
        return jnp.where(count(key_s[:s_eff, :] >= ordered_to_float(c)) >= topk, c, r)

    thr = ordered_to_float(lax.fori_loop(0, 32, bisect, jnp.zeros((1, Q_BLOCK), I32)))

    n_ge = count(key_s[:s_eff, :] >= thr)
    clean = jnp.max(jnp.where((n_ge == topk) & (thr > -jnp.inf), 0.0, 1.0)) == 0.0

    @pl.when(clean)
    def _():
        for rows in chunks:
            bias_s[rows, :] = jnp.where(key_s[rows, :] >= thr, 0.0, -jnp.inf)

    @pl.when(jnp.logical_not(clean))
    def _():
        key = key_s[:s_eff, :]
        need = topk - count(key > thr)
        rr = lax.broadcasted_iota(I32, (LANES, LANES), 0)
        cc = lax.broadcasted_iota(I32, (LANES, LANES), 1)
        tri = jnp.where(cc < rr, 1.0, 0.0).astype(BF16)
        seen = jnp.zeros((1, Q_BLOCK), F32)
        for c in range(s_eff // LANES):
            kc = key[c * LANES:(c + 1) * LANES, :]
            eq = kc == thr
            eqf = jnp.where(eq, 1.0, 0.0)
            rank = _dot(tri, eqf.astype(BF16)) + seen
            seen = seen + jnp.sum(eqf, axis=0, keepdims=True)
            tie_bias = jnp.where(jnp.where(eq, rank, topk) < need, 0.0, -jnp.inf)
            bias = jnp.where(kc > thr, 0.0, tie_bias)
            k_pos = c * LANES + lax.broadcasted_iota(I32, (LANES, Q_BLOCK), 0)
            bias_s[c * LANES:(c + 1) * LANES, :] = jnp.where(k_pos < limit, bias, -jnp.inf)

    qnw = qnw_ref[...]
    crow = lax.broadcasted_iota(I32, (LANES - HEAD_DIM_A, Q_BLOCK), 0)
    r_k = lax.broadcasted_iota(I32, (Q_BLOCK, Q_BLOCK), 0)
    c_q = lax.broadcasted_iota(I32, (Q_BLOCK, Q_BLOCK), 1)
    after = 2.0 * jnp.maximum(r_k - c_q, 0).astype(F32)
    band = pl.ds(pl.multiple_of(i * Q_BLOCK, Q_BLOCK), Q_BLOCK)
    slope = [sum(_bf16_pieces(LOG2E * 2.0 ** -(h + 1), SLOPE_PIECES)) for h in range(N_HEADS_A)]
    row_max = []
    for h in range(N_HEADS_A):
        pieces = _bf16_pieces(LOG2E * 2.0 ** -(h + 1), SLOPE_PIECES)
        qh = q_t[h * HEAD_DIM_A:(h + 1) * HEAD_DIM_A, :]
        ms = jnp.mean(qh * qh, axis=0, keepdims=True)
        qn = (qh * lax.rsqrt(ms + EPS) * qnw) * (HEAD_DIM_A ** -0.5) * LOG2E
        slopes = jnp.zeros(crow.shape, F32)
        for k, piece in enumerate(pieces):
            slopes = jnp.where(crow == 2 * k, POS_RADIX * piece, jnp.where(crow == 2 * k + 1, piece, slopes))
        qa = jnp.concatenate([qn, slopes], axis=0).astype(BF16)
        folded = []
        for rows in chunks:
            logits = _dot(kn_s[rows, :], qa) + bias_s[rows, :]
            lg_s[h, rows, :] = logits
            folded.append(jnp.max(logits.reshape(rows_per // ROW_FOLD, ROW_FOLD, Q_BLOCK), axis=0))
        row_max.append(jnp.max(functools.reduce(jnp.maximum, folded), axis=0, keepdims=True))
    for h in range(N_HEADS_A):
        lg_s[h, band, :] = lg_s[h, band, :] - slope[h] * after
    for h in range(N_HEADS_A):
        acc = jnp.zeros((KV_DIM_A + V_ONES, Q_BLOCK), F32)
        for rows in chunks:
            p = jnp.exp2(lg_s[h, rows, :] - row_max[h])
            acc = acc + _dot(vt_s[:, rows], p.astype(BF16))
        ot_s[h * KV_DIM_A:(h + 1) * KV_DIM_A, :] = acc[:KV_DIM_A, :] / acc[KV_DIM_A:KV_DIM_A + 1, :]
    o_ref[...] = ot_s[...].T


def _attn_kernel(qa_ref, kvq_ref, kv_ref, qnw_ref, knw_ref, o_ref, kn_s, vt_s, ki_s, key_s, bias_s, lg_s, ot_s):
    i = pl.program_id(1)
    S = kv_ref.shape[0]
    nb = S // Q_BLOCK
    topk = float(min(TOPK_MAX, S // 4))

    @pl.when(i == 0)
    def _():
        kv = kv_ref[:, :LANES]
        lane = lax.broadcasted_iota(I32, kv.shape, 1)
        pos = lax.broadcasted_iota(I32, kv.shape, 0)
        ms = jnp.sum(jnp.where(lane < KV_DIM_A, kv * kv, 0.0), axis=-1, keepdims=True) * (1.0 / KV_DIM_A)
        kn = kv * lax.rsqrt(ms + EPS) * knw_ref[...]
        digit = jnp.where((lane & 1) == 0, pos // POS_RADIX, pos % POS_RADIX).astype(F32)
        is_digit = (lane >= KV_DIM_A) & (lane < KV_DIM_A + 2 * SLOPE_PIECES)
        kn_s[...] = jnp.where(is_digit, digit, kn).astype(BF16)
        vt_s[:KV_DIM_A, :] = kv.T[V_OFF:V_OFF + KV_DIM_A, :].astype(BF16)
        vt_s[KV_DIM_A:, :] = jnp.ones((V_ONES, kv.shape[0]), BF16)
        ki_s[...] = kv_ref[:, KI_OFF:KI_OFF + IDX_DIM].astype(BF16)

    n_buckets = min(ATTN_BUCKETS, nb)
    per = nb // n_buckets
    for j in range(n_buckets):
        pl.when(i // per == j)(functools.partial(
            _attn_block, (j + 1) * per * Q_BLOCK, topk, i, qa_ref, kvq_ref, qnw_ref, o_ref,
            kn_s, vt_s, ki_s, key_s, bias_s, lg_s, ot_s))


def _attn(qa, kv, q_norm_w, k_norm_w, B, S):
    N = qa.shape[0]
    nb = S // Q_BLOCK
    return pl.pallas_call(
        _attn_kernel,
        grid=(B, nb),
        in_specs=[pl.BlockSpec((Q_BLOCK, QA_W), lambda b, i: (b * nb + i, 0)),
                  pl.BlockSpec((Q_BLOCK, KV_W), lambda b, i: (b * nb + i, 0)),
                  pl.BlockSpec((S, KV_W), lambda b, i: (b, 0)),
                  pl.BlockSpec((HEAD_DIM_A, 1), lambda b, i: (0, 0)),
                  pl.BlockSpec((1, LANES), lambda b, i: (0, 0))],
        out_specs=pl.BlockSpec((Q_BLOCK, ATTN_OUT), lambda b, i: (b * nb + i, 0)),
        out_shape=jax.ShapeDtypeStruct((N, ATTN_OUT), F32),
        scratch_shapes=[pltpu.VMEM((S, LANES), BF16),
                        pltpu.VMEM((KV_DIM_A + V_ONES, S), BF16),
                        pltpu.VMEM((S, IDX_DIM), BF16),
                        pltpu.VMEM((S, Q_BLOCK), F32),
                        pltpu.VMEM((S, Q_BLOCK), F32),
                        pltpu.VMEM((N_HEADS_A, S, Q_BLOCK), F32),
                        pltpu.VMEM((ATTN_OUT, Q_BLOCK), F32)],
        compiler_params=_params("arbitrary", "arbitrary"),
        name="attn",
    )(qa, kv, kv, q_norm_w.reshape(HEAD_DIM_A, 1),
      jnp.pad(k_norm_w, (0, LANES - KV_DIM_A)).reshape(1, LANES))


SUBLANES = 8


def _shift_rows(x, d, fill, row):
    if d % SUBLANES == 0:
        return jnp.concatenate([jnp.full((d, x.shape[1]), fill, x.dtype), x[:-d]], axis=0)
    return jnp.where(row >= d, pltpu.roll(x, d, 0), fill)


def _sigmoid(x):
    return 0.5 * (1.0 + jnp.tanh(0.5 * x))


def _lru_kernel(xb_ref, gb_ref, cw_ref, cb_ref, wr_ref, br_ref, wi_ref, bi_ref, lam_ref, o_ref):
    x = xb_ref[...]
    S = x.shape[0]
    row = lax.broadcasted_iota(I32, x.shape, 0)
    xc = cb_ref[...] + _shift_rows(x, CONV_WIDTH - 1, 0.0, row) * cw_ref[0:1, :]
    for j in range(1, CONV_WIDTH):
        d = CONV_WIDTH - 1 - j
        xs = x if d == 0 else _shift_rows(x, d, 0.0, row)
        xc = xc + xs * cw_ref[j:j + 1, :]
    xcb = xc.astype(BF16)
    r = _sigmoid(_dot(xcb, wr_ref[0]) + br_ref[...])
    ig = _sigmoid(_dot(xcb, wi_ref[0]) + bi_ref[...])
    z = -lam_ref[...]
    softplus = jnp.maximum(z, 0.0) + jnp.log1p(jnp.exp(-jnp.abs(z)))
    log_a = -LRU_C * r * softplus
    a = jnp.exp(log_a)
    b = jnp.sqrt(-jnp.tanh(log_a) * (1.0 + a * a)) * (ig * xc)
    d = 1
    while d < S:
        b = a * _shift_rows(b, d, 0.0, row) + b
        if 2 * d < S:
            a = a * _shift_rows(a, d, 1.0, row)
        d *= 2
    g = gb_ref[...]
    gelu = 0.5 * g * (1.0 + jnp.tanh(float(np.sqrt(2.0 / np.pi)) * (g + 0.044715 * (g * g * g))))
    o_ref[...] = b * gelu


def _lru(lru, conv_w, conv_b, wr_bd, b_rec, wi_bd, b_in, lam, B, S):
    N = lru.shape[0]
    nt = LRU_WIDTH // LANES
    vec = lambda b, j: (0, j)
    return pl.pallas_call(
        _lru_kernel,
        grid=(B, nt),
        in_specs=[pl.BlockSpec((S, LANES), lambda b, j: (b, j)),
                  pl.BlockSpec((S, LANES), lambda b, j: (b, nt + j)),
                  pl.BlockSpec((CONV_WIDTH, LANES), vec),
                  pl.BlockSpec((1, LANES), vec),
                  pl.BlockSpec((1, LANES, LANES), lambda b, j: (j, 0, 0)),
                  pl.BlockSpec((1, LANES), vec),
                  pl.BlockSpec((1, LANES, LANES), lambda b, j: (j, 0, 0)),
                  pl.BlockSpec((1, LANES), vec),
                  pl.BlockSpec((1, LANES), vec)],
        out_specs=pl.BlockSpec((S, LANES), lambda b, j: (b, j)),
        out_shape=jax.ShapeDtypeStruct((N, LRU_WIDTH), F32),
        compiler_params=_params("arbitrary", "arbitrary"),
        name="lru",
    )(lru, lru, conv_w, conv_b.reshape(1, -1), wr_bd, b_rec.reshape(1, -1), wi_bd,
      b_in.reshape(1, -1), lam.reshape(1, -1))


def _block_diag_tiles(w):
    per = LANES // LRU_BLOCK_DIM
    nt = LRU_WIDTH // LANES
    w5 = w.reshape(nt, per, LRU_BLOCK_DIM, 1, LRU_BLOCK_DIM)
    eye = jnp.eye(per, dtype=w.dtype).reshape(1, per, 1, per, 1)
    return (w5 * eye).reshape(nt, LANES, LANES)


def _first_index_of_max(vals, lane):
    m = jnp.max(vals, axis=-1, keepdims=True)
    idx = jnp.min(jnp.where(vals == m, lane, 4 * LANES), axis=-1, keepdims=True)
    return m, idx


def _weight_pieces(out, w, lane, first):
    rest = w
    for k in range(W_PIECES):
        piece = rest.astype(BF16).astype(F32)
        out = jnp.where(lane == first + k, piece, out)
        rest = rest - piece
    return out


def _merge_kernel(attn_ref, lru_ref, g_ref, x_ref, mod_ref, wpa_ref, wpb_ref, wo_ref, nw_ref, wr_ref,
                  br_ref, x1_ref, hsl_ref, route_ref, gc_ref):
    D = x_ref.shape[1]
    tm = x_ref.shape[0]

    y_a = _dot(attn_ref[...].astype(BF16), wpa_ref[...])
    y_b = _dot(lru_ref[...].astype(BF16), wpb_ref[...])
    merged = _sigmoid(g_ref[:, :D]) * y_a + _sigmoid(g_ref[:, D:]) * y_b
    x1 = x_ref[...] + mod_ref[0, 2:3, :] * _dot(merged.astype(BF16), wo_ref[...])
    x1_ref[...] = x1
    h2 = _rms_mod(x1, nw_ref[...], mod_ref[0, 3:4, :], mod_ref[0, 4:5, :])
    h2b = h2.astype(BF16)

    h2l = (h2 - h2b.astype(F32)).astype(BF16)
    hi_hi_lo = _dot(h2b, wr_ref[...])
    logits = hi_hi_lo[:, :LANES] + (_dot(h2l, wr_ref[:, :LANES]) + hi_hi_lo[:, LANES:]) + br_ref[...]
    lane = lax.broadcasted_iota(I32, logits.shape, 1)
    gl = jnp.where(lane < N_GROUPS, logits, -jnp.inf)
    gmax, g_sel = _first_index_of_max(gl, lane)
    g_weight = 1.0 / jnp.sum(jnp.exp(gl - gmax), axis=-1, keepdims=True)
    e_lo = N_GROUPS + g_sel * EXPERTS_PER_GROUP
    el = jnp.where((lane >= e_lo) & (lane < e_lo + EXPERTS_PER_GROUP), logits, -jnp.inf)
    v1, i1 = _first_index_of_max(el, lane)
    el2 = jnp.where(lane == i1, -jnp.inf, el)
    v2, i2 = _first_index_of_max(el2, lane)
    e2x = jnp.exp(v2 - v1)
    w1 = g_weight / (1.0 + e2x)
    w2 = g_weight * e2x / (1.0 + e2x)

    hot1 = lane == i1
    hot2 = lane == i2
    hot = jnp.where(hot1, 1.0, jnp.where(hot2, 1.0, 0.0))
    gcnt = jnp.floor((jnp.sum(hot, axis=0, keepdims=True) + (GRAN - 1.0)) * (1.0 / GRAN))
    gc_ref[0] = gcnt
    rr = lax.broadcasted_iota(I32, (LANES, LANES), 0)
    cc = lax.broadcasted_iota(I32, (LANES, LANES), 1)
    upper = jnp.where(rr < cc, 1.0, 0.0).astype(BF16)
    run_start = GRAN * _dot(jnp.broadcast_to(gcnt, (8, LANES)).astype(BF16), upper)[0:1, :]
    rr = lax.broadcasted_iota(I32, (tm, tm), 0)
    cc = lax.broadcasted_iota(I32, (tm, tm), 1)
    lower = jnp.where(cc < rr, 1.0, 0.0).astype(BF16)
    pos = _dot(lower, hot.astype(BF16)) + run_start
    pos1 = jnp.where(hot1, pos, 0.0)
    pos2 = jnp.where(hot2, pos, 0.0)
    lp1 = jnp.sum(pos1, axis=-1, keepdims=True)
    lp2 = jnp.sum(pos2, axis=-1, keepdims=True)
    route_ref[...] = jnp.where(lane == R_LP1, lp1, jnp.where(lane == R_LP2, lp2, 0.0))

    aux = jnp.where(lane == AUX_E1, (i1 - N_GROUPS).astype(F32), 0.0)
    aux = _weight_pieces(_weight_pieces(aux, w1, lane, AUX_W1), w2, lane, AUX_W2)
    lp1r = jnp.sum(pos1.T, axis=0, keepdims=True).astype(I32)
    lp2r = jnp.sum(pos2.T, axis=0, keepdims=True).astype(I32)
    rows = lax.broadcasted_iota(I32, (LOCAL_ROWS, tm), 0)
    sort = jnp.where(rows == lp1r, 1.0, jnp.where(rows == lp2r, 1.0, 0.0)).astype(BF16)
    hsl_ref[...] = _dot(sort, jnp.concatenate([h2b, aux.astype(BF16)], axis=1)).astype(BF16)


def _merge(attn, lru_out, gates, x2, mod3, wpa, wpb, wo, norm_w, wr_hi_lo, b_router, S):
    N, D = x2.shape
    tm = TM_SORT
    n_tiles = N // tm
    steps_per_batch = S // tm
    const = lambda t: (0, 0)
    rows = lambda t: (t, 0)
    return pl.pallas_call(
        _merge_kernel,
        grid=(n_tiles,),
        in_specs=[pl.BlockSpec((tm, ATTN_OUT), rows),
                  pl.BlockSpec((tm, LRU_WIDTH), rows),
                  pl.BlockSpec((tm, 2 * D), rows),
                  pl.BlockSpec((tm, D), rows),
                  pl.BlockSpec((1, 6, D), lambda t: (t // steps_per_batch, 0, 0)),
                  pl.BlockSpec(wpa.shape, const),
                  pl.BlockSpec(wpb.shape, const),
                  pl.BlockSpec(wo.shape, const),
                  pl.BlockSpec((1, D), const),
                  pl.BlockSpec(wr_hi_lo.shape, const),
                  pl.BlockSpec((1, LANES), const)],
        out_specs=[pl.BlockSpec((tm, D), rows),
                   pl.BlockSpec((LOCAL_ROWS, D + AUX_W), rows),
                   pl.BlockSpec((tm, LANES), rows),
                   pl.BlockSpec((1, 1, LANES), lambda t: (t, 0, 0))],
        out_shape=[jax.ShapeDtypeStruct((N, D), F32),
                   jax.ShapeDtypeStruct((n_tiles * LOCAL_ROWS, D + AUX_W), BF16),
                   jax.ShapeDtypeStruct((N, LANES), F32),
                   jax.ShapeDtypeStruct((n_tiles, 1, LANES), F32)],
        compiler_params=_params("arbitrary"),
        name="merge",
    )(attn, lru_out, gates, x2, mod3, wpa, wpb, wo, norm_w.reshape(1, D), wr_hi_lo, b_router)


def _plan_sizes(n_tiles):
    max_gran = n_tiles * LOCAL_GRAN + N_EXPERTS * (GRAN_PER_TILE - 1)
    max_row_tiles = -(-max_gran // GRAN_PER_TILE)
    return max_row_tiles * GRAN_PER_TILE, max_row_tiles


def _plan_kernel(gc_ref, src_ref, texp_ref, nt_ref, dst_ref, local_next):
    n_tiles = gc_ref.shape[0]
    max_gran, max_row_tiles = _plan_sizes(n_tiles)

    def zero(ref, lo, hi):
        def body(k, c):
            ref[k] = 0
            return c
        lax.fori_loop(lo, hi, body, 0)

    zero(local_next, 0, n_tiles)

    def per_expert(e, g):
        first_tile = g // GRAN_PER_TILE

        def per_tile(t, g):
            n = gc_ref[t, N_GROUPS + e]
            base = t * LOCAL_GRAN + local_next[t]

            def per_granule(k, g):
                src_ref[g] = base + k
                dst_ref[base + k] = g
                return g + 1

            g = lax.fori_loop(0, n, per_granule, g)
            local_next[t] = local_next[t] + n
            return g

        g = lax.fori_loop(0, n_tiles, per_tile, g)
        padded = (g + GRAN_PER_TILE - 1) // GRAN_PER_TILE * GRAN_PER_TILE
        zero(src_ref, g, padded)
        g = padded

        def mark(j, c):
            texp_ref[j] = e
            return c

        lax.fori_loop(first_tile, g // GRAN_PER_TILE, mark, 0)
        return g

    g = lax.fori_loop(0, N_EXPERTS, per_expert, 0)
    used = g // GRAN_PER_TILE
    nt_ref[0] = used
    zero(src_ref, g, max_gran)

    def clear_tail(t, c):
        zero(dst_ref, t * LOCAL_GRAN + local_next[t], (t + 1) * LOCAL_GRAN)
        return c

    lax.fori_loop(0, n_tiles, clear_tail, 0)

    def mark_rest(j, c):
        texp_ref[j] = N_EXPERTS - 1
        return c

    lax.fori_loop(used, max_row_tiles, mark_rest, 0)


def _plan(gc):
    n_tiles = gc.shape[0]
    max_gran, max_row_tiles = _plan_sizes(n_tiles)
    smem = pl.BlockSpec(memory_space=pltpu.SMEM)
    return pl.pallas_call(
        _plan_kernel,
        in_specs=[smem],
        out_specs=[smem, smem, smem, smem],
        out_shape=[jax.ShapeDtypeStruct((max_gran,), I32),
                   jax.ShapeDtypeStruct((max_row_tiles,), I32),
                   jax.ShapeDtypeStruct((1,), I32),
                   jax.ShapeDtypeStruct((n_tiles * LOCAL_GRAN,), I32)],
        scratch_shapes=[pltpu.SMEM((n_tiles,), I32)],
        name="plan",
    )(gc)


def _granule_copies(idx_ref, first, count, src_hbm, dst_vmem, sem, wait):
    def body(k, c):
        g = idx_ref[first + k]
        cp = pltpu.make_async_copy(src_hbm.at[pl.ds(pl.multiple_of(g * GRAN, GRAN), GRAN)],
                                   dst_vmem.at[pl.ds(pl.multiple_of(k * GRAN, GRAN), GRAN)], sem)
        if wait:
            cp.wait()
        else:
            cp.start()
        return c
    lax.fori_loop(0, count, body, 0)


def _moe_kernel(src_ref, texp_ref, nt_ref, hsl_ref, w1_ref, w3_ref, w2_ref, ys_ref, xbuf, w1b, w3b, w2b, sem):
    i = pl.program_id(0)
    used = nt_ref[0]
    slot = i % MOE_BUFFERS
    D = ys_ref.shape[1]

    def gather(tile, wait):
        slot_ = tile % MOE_BUFFERS
        _granule_copies(src_ref, tile * GRAN_PER_TILE, GRAN_PER_TILE, hsl_ref, xbuf.at[slot_], sem.at[slot_], wait)

    @pl.when(i == 0)
    def _():
        for ahead in range(MOE_BUFFERS - 1):
            pl.when(ahead < used)(functools.partial(gather, ahead, False))

    @pl.when(i + (MOE_BUFFERS - 1) < used)
    def _():
        gather(i + (MOE_BUFFERS - 1), False)

    @pl.when(i < used)
    def _():
        gather(i, True)

        @pl.when((i == 0) | (texp_ref[i] != texp_ref[jnp.maximum(i - 1, 0)]))
        def _():
            w1b[...] = w1_ref[0].astype(BF16)
            w3b[...] = w3_ref[0].astype(BF16)
            w2b[...] = w2_ref[0].astype(BF16)

        xb = xbuf[slot]
        hb = xb[:, :D]
        aux = xb[:, D:].astype(F32)
        first = aux[:, AUX_E1:AUX_E1 + 1] == texp_ref[i].astype(F32)
        w_row = jnp.zeros((TM_MOE, 1), F32)
        for k in range(W_PIECES):
            w_row = w_row + jnp.where(first, aux[:, AUX_W1 + k:AUX_W1 + k + 1], aux[:, AUX_W2 + k:AUX_W2 + k + 1])
        a = _dot(hb, w1b[...])
        act = (a * _sigmoid(a)) * _dot(hb, w3b[...])
        ys_ref[...] = (w_row * _dot(act.astype(BF16), w2b[...])).astype(BF16)

    @pl.when(i >= used)
    def _():
        ys_ref[...] = jnp.zeros_like(ys_ref)


def _moe(src, texp, used, hsl, w1, w3, w2):
    max_row_tiles = texp.shape[0]
    D = w1.shape[1]
    F = w1.shape[2]
    grid_spec = pltpu.PrefetchScalarGridSpec(
        num_scalar_prefetch=3,
        grid=(max_row_tiles,),
        in_specs=[pl.BlockSpec(memory_space=pl.ANY),
                  pl.BlockSpec((1, D, F), lambda i, s, te, nt: (te[i], 0, 0)),
                  pl.BlockSpec((1, D, F), lambda i, s, te, nt: (te[i], 0, 0)),
                  pl.BlockSpec((1, F, D), lambda i, s, te, nt: (te[i], 0, 0))],
        out_specs=pl.BlockSpec((TM_MOE, D), lambda i, s, te, nt: (i, 0)),
        scratch_shapes=[pltpu.VMEM((MOE_BUFFERS, TM_MOE, D + AUX_W), BF16),
                        pltpu.VMEM((D, F), BF16), pltpu.VMEM((D, F), BF16), pltpu.VMEM((F, D), BF16),
                        pltpu.SemaphoreType.DMA((MOE_BUFFERS,))],
    )
    return pl.pallas_call(
        _moe_kernel,
        grid_spec=grid_spec,
        out_shape=jax.ShapeDtypeStruct((max_row_tiles * TM_MOE, D), BF16),
        compiler_params=_params("arbitrary"),
        name="moe",
    )(src, texp, used, hsl, w1, w3, w2)


def _combine_kernel(dst_ref, ys_ref, x1_ref, route_ref, mod_ref, o_ref, ybuf, sem):
    t = pl.program_id(0)
    slot = t % 2
    tm = x1_ref.shape[0]

    def gather(tile, slot_, wait):
        _granule_copies(dst_ref, tile * LOCAL_GRAN, LOCAL_GRAN, ys_ref, ybuf.at[slot_], sem.at[slot_], wait)

    @pl.when(t == 0)
    def _():
        gather(0, 0, False)

    @pl.when(t + 1 < pl.num_programs(0))
    def _():
        gather(t + 1, 1 - slot, False)

    gather(t, slot, True)
    lp1 = route_ref[:, R_LP1:R_LP1 + 1].astype(I32)
    lp2 = route_ref[:, R_LP2:R_LP2 + 1].astype(I32)
    col = lax.broadcasted_iota(I32, (tm, LOCAL_ROWS), 1)
    unsort = jnp.where(col == lp1, 1.0, jnp.where(col == lp2, 1.0, 0.0)).astype(BF16)
    o_ref[...] = x1_ref[...] + mod_ref[0, 5:6, :] * _dot(unsort, ybuf[slot])


def _combine(dst, ys, x1, route, mod3, S):
    N, D = x1.shape
    tm = TM_SORT
    steps_per_batch = S // tm
    grid_spec = pltpu.PrefetchScalarGridSpec(
        num_scalar_prefetch=1,
        grid=(N // tm,),
        in_specs=[pl.BlockSpec(memory_space=pl.ANY),
                  pl.BlockSpec((tm, D), lambda t, d: (t, 0)),
                  pl.BlockSpec((tm, LANES), lambda t, d: (t, 0)),
                  pl.BlockSpec((1, 6, D), lambda t, d: (t // steps_per_batch, 0, 0))],
        out_specs=pl.BlockSpec((tm, D), lambda t, d: (t, 0)),
        scratch_shapes=[pltpu.VMEM((2, LOCAL_ROWS, D), BF16), pltpu.SemaphoreType.DMA((2,))],
    )
    return pl.pallas_call(
        _combine_kernel,
        grid_spec=grid_spec,
        out_shape=jax.ShapeDtypeStruct((N, D), F32),
        compiler_params=_params("arbitrary"),
        name="combine",
    )(dst, ys, x1, route, mod3)


def _layer(x2, mod3, B, S, norm_mix_w, w_in, q_norm_w, k_norm_w, conv_w, conv_b, w_rec_gate, b_rec_gate,
           w_in_gate, b_in_gate, lru_lambda, w_proj_a, w_proj_b, w_out, norm_ffn_w, w_group, b_group,
           w_expert_router, b_expert_router, w1, w3, w2):
    N, D = x2.shape
    nq = N_HEADS_A * HEAD_DIM_A
    o_k = nq
    o_v = o_k + KV_DIM_A
    o_qi = o_v + KV_DIM_A
    o_ki = o_qi + N_IDX_HEADS * IDX_DIM
    o_wi = o_ki + IDX_DIM
    o_lx = o_wi + N_IDX_HEADS
    o_g = o_lx + 2 * LRU_WIDTH
    pad = jnp.zeros((D, KV_W - (WI_OFF + N_IDX_HEADS)), w_in.dtype)
    wa = jnp.concatenate([w_in[:, :nq], w_in[:, o_qi:o_ki], w_in[:, o_k:o_qi], w_in[:, o_ki:o_lx], pad],
                         axis=1).astype(BF16)
    wb = w_in[:, o_lx:o_g].astype(BF16)
    wg = w_in[:, o_g:].astype(BF16)

    qa, kv, lru, gates = _proj(x2, mod3, norm_mix_w, wa, wb, wg, S)
    attn = _attn(qa, kv, q_norm_w, k_norm_w, B, S)
    lru_out = _lru(lru, conv_w, conv_b, _block_diag_tiles(w_rec_gate).astype(BF16), b_rec_gate,
                   _block_diag_tiles(w_in_gate).astype(BF16), b_in_gate, lru_lambda, B, S)

    n_r = N_GROUPS + N_EXPERTS
    w_router = jnp.concatenate([w_group, w_expert_router, jnp.zeros((D, LANES - n_r), F32)], axis=1)
    wr_hi = w_router.astype(BF16)
    wr_lo = (w_router - wr_hi.astype(F32)).astype(BF16)
    b_router = jnp.concatenate([b_group, b_expert_router, jnp.zeros((LANES - n_r,), F32)]).reshape(1, LANES)
    x1, hsl, route, gc = _merge(attn, lru_out, gates, x2, mod3, w_proj_a.astype(BF16), w_proj_b.astype(BF16),
                                w_out.astype(BF16), norm_ffn_w, jnp.concatenate([wr_hi, wr_lo], axis=1),
                                b_router, S)

    src, texp, used, dst = _plan(gc.reshape(gc.shape[0], LANES).astype(I32))
    ys = _moe(src, texp, used, hsl, w1, w3, w2)
    return _combine(dst, ys, x1, route, mod3, S)


def kernel(x, c, ada_w, ada_b, norm_mix_w, w_in, q_norm_w, k_norm_w, conv_w, conv_b, w_rec_gate, b_rec_gate,
           w_in_gate, b_in_gate, lru_lambda, w_proj_a, w_proj_b, w_out, norm_ffn_w, w_group, b_group,
           w_expert_router, b_expert_router, w1, w3, w2):
    B, S, D = x.shape
    x2 = x.reshape(B * S, D)
    for l in range(ada_w.shape[0]):
        mod3 = _ada(c, ada_w[l], ada_b[l]).reshape(B, 6, D)
        x2 = _layer(x2, mod3, B, S, norm_mix_w[l], w_in[l], q_norm_w[l], k_norm_w[l], conv_w[l], conv_b[l],
                    w_rec_gate[l], b_rec_gate[l], w_in_gate[l], b_in_gate[l], lru_lambda[l], w_proj_a[l],
                    w_proj_b[l], w_out[l], norm_ffn_w[l], w_group[l], b_group[l], w_expert_router[l],
                    b_expert_router[l], w1[l], w3[l], w2[l])
    return x2.reshape(B, S, D)
```

```python
import functools

import jax
import jax.numpy as jnp
import numpy as np
from jax import lax
from jax.experimental import pallas as pl
from jax.experimental.pallas import tpu as pltpu

F32 = jnp.float32
BF16 = jnp.bfloat16
I32 = jnp.int32

EPS = 1e-6
CHUNK = 64
Q_BLOCK = 128
N_HEADS_A = 8
HEAD_DIM_A = 64
KV_DIM_A = 64
ATTN_OUT = N_HEADS_A * KV_DIM_A
N_IDX_HEADS = 4
IDX_DIM = 64
TOPK_MAX = 256
LRU_WIDTH = 512
LRU_BLOCKS = 8
LRU_BLOCK_DIM = LRU_WIDTH // LRU_BLOCKS
CONV_WIDTH = 4
LRU_C = 8.0
N_GROUPS = 4
EXPERTS_PER_GROUP = 8
N_EXPERTS = N_GROUPS * EXPERTS_PER_GROUP
D_FF_EXPERT = 256

LANES = 128
INT_MIN = -2 ** 31
VMEM_LIMIT = 56 * 1024 * 1024

QA_W = N_HEADS_A * HEAD_DIM_A + N_IDX_HEADS * IDX_DIM
KV_W = 256
K_OFF, V_OFF, KI_OFF, WI_OFF = 0, 64, 128, 192

TM_PROJ = 512

TM_SORT = 512
GRAN = 16
TM_MOE = 256
GRAN_PER_TILE = TM_MOE // GRAN
MOE_BUFFERS = 3
LOCAL_ROWS = -(-(2 * TM_SORT + N_EXPERTS * (GRAN - 1)) // LANES) * LANES
LOCAL_GRAN = LOCAL_ROWS // GRAN
AUX_W = LANES
W_PIECES = 3
AUX_W1, AUX_W2 = 0, W_PIECES
AUX_E1 = 2 * W_PIECES

R_LP1, R_LP2 = 0, 1


def _dot(a, b, **kw):
    return jnp.dot(a, b, preferred_element_type=F32, **kw)


def _params(*sem):
    return pltpu.CompilerParams(dimension_semantics=sem, vmem_limit_bytes=VMEM_LIMIT)


def _ada_kernel(c_ref, w_ref, b_ref, o_ref):
    c = c_ref[...]
    cond = c * jax.nn.sigmoid(c)
    o_ref[...] = _dot(cond, w_ref[...], precision=lax.Precision.HIGHEST) + b_ref[...]


def _ada(c, ada_w, ada_b):
    B, D = c.shape
    n_out = ada_w.shape[1]
    tn = 1024
    return pl.pallas_call(
        _ada_kernel,
        grid=(n_out // tn,),
        in_specs=[pl.BlockSpec((B, D), lambda j: (0, 0)),
                  pl.BlockSpec((D, tn), lambda j: (0, j)),
                  pl.BlockSpec((1, tn), lambda j: (0, j))],
        out_specs=pl.BlockSpec((B, tn), lambda j: (0, j)),
        out_shape=jax.ShapeDtypeStruct((B, n_out), F32),
        compiler_params=_params("arbitrary"),
        name="ada",
    )(c, ada_w, ada_b.reshape(1, n_out))


def _rms_mod(x, w, shift, scale):
    ms = jnp.mean(x * x, axis=-1, keepdims=True)
    y = x * lax.rsqrt(ms + EPS) * w
    return y * (1.0 + scale) + shift


def _proj_kernel(x_ref, mod_ref, nw_ref, wa_ref, wb_ref, wg_ref, oqa_ref, okv_ref, olru_ref, og_ref):
    h = _rms_mod(x_ref[...], nw_ref[...], mod_ref[0, 0:1, :], mod_ref[0, 1:2, :])
    hb = h.astype(BF16)
    pa = _dot(hb, wa_ref[...])
    oqa_ref[...] = pa[:, :QA_W]
    okv_ref[...] = pa[:, QA_W:]
    olru_ref[...] = _dot(hb, wb_ref[...])
    og_ref[...] = _sigmoid(_dot(hb, wg_ref[...])).astype(BF16)


def _proj(x2, mod3, norm_w, wa, wb, wg, S):
    N, D = x2.shape
    tm = TM_PROJ
    steps_per_batch = S // tm
    const = lambda t: (0, 0)
    return pl.pallas_call(
        _proj_kernel,
        grid=(N // tm,),
        in_specs=[pl.BlockSpec((tm, D), lambda t: (t, 0)),
                  pl.BlockSpec((1, 6, D), lambda t: (t // steps_per_batch, 0, 0)),
                  pl.BlockSpec((1, D), const),
                  pl.BlockSpec(wa.shape, const),
                  pl.BlockSpec(wb.shape, const),
                  pl.BlockSpec(wg.shape, const)],
        out_specs=[pl.BlockSpec((tm, QA_W), lambda t: (t, 0)),
                   pl.BlockSpec((tm, KV_W), lambda t: (t, 0)),
                   pl.BlockSpec((tm, 2 * LRU_WIDTH), lambda t: (t, 0)),
                   pl.BlockSpec((tm, 2 * D), lambda t: (t, 0))],
        out_shape=[jax.ShapeDtypeStruct((N, QA_W), F32),
                   jax.ShapeDtypeStruct((N, KV_W), F32),
                   jax.ShapeDtypeStruct((N, 2 * LRU_WIDTH), F32),
                   jax.ShapeDtypeStruct((N, 2 * D), BF16)],
        compiler_params=_params("arbitrary"),
        name="proj",
    )(x2, mod3, norm_w.reshape(1, D), wa, wb, wg)


ATTN_BUCKETS = 8
ROW_FOLD = 64
KEY_ROWS = 256
LOG2E = float(np.log2(np.e))
POS_RADIX = 256
SLOPE_PIECES = 3
V_ONES = 16


def _bf16_pieces(c, n):
    out = []
    rest = float(c)
    for _ in range(n):
        p = float(np.asarray(rest, np.float32).astype(BF16).astype(np.float32))
        out.append(p)
        rest -= p
    return out


def _reduce_rows(op, x):
    r = x.shape[0]
    if r > ROW_FOLD and r % ROW_FOLD == 0:
        x = op(x.reshape(r // ROW_FOLD, ROW_FOLD, x.shape[1]), axis=0)
    return op(x, axis=0, keepdims=True)


def _attn_block(s_eff, topk, i, qa_ref, kvq_ref, qnw_ref, o_ref, kn_s, vt_s, ki_s, key_s, bias_s, lg_s, ot_s):
    nq = N_HEADS_A * HEAD_DIM_A
    q_t = qa_ref[:, :nq].T
    qi_t = qa_ref[:, nq:].T
    wq_t = kvq_ref[...].T[WI_OFF:WI_OFF + N_IDX_HEADS, :] * (N_IDX_HEADS ** -0.5 * IDX_DIM ** -0.5)

    q_pos = i * Q_BLOCK + lax.broadcasted_iota(I32, (1, Q_BLOCK), 1)
    limit = (q_pos // CHUNK + 1) * CHUNK
    rows_per = KEY_ROWS if s_eff % KEY_ROWS == 0 else Q_BLOCK
    chunks = [slice(c * rows_per, (c + 1) * rows_per) for c in range(s_eff // rows_per)]

    qi = [qi_t[h * IDX_DIM:(h + 1) * IDX_DIM, :].astype(BF16) for h in range(N_IDX_HEADS)]
    for rows in chunks:
        ki = ki_s[rows, :]
        score = jnp.zeros((rows_per, Q_BLOCK), F32)
        for h in range(N_IDX_HEADS):
            score = score + jnp.maximum(_dot(ki, qi[h]), 0.0) * wq_t[h:h + 1, :]
        k_pos = rows.start + lax.broadcasted_iota(I32, (rows_per, Q_BLOCK), 0)
        key_s[rows, :] = jnp.where(k_pos < limit, score, -jnp.inf)

    def count(mask):
        return _reduce_rows(jnp.sum, jnp.where(mask, 1.0, 0.0))

    def ordered_to_float(c):
        s = c ^ INT_MIN
        return pltpu.bitcast(jnp.where(s < 0, INT_MIN | (-s), s), F32)

    def bisect(it, r):
        c = r | jnp.left_shift(jnp.int32(1), 31 - it)
        return jnp.where(count(key_s[:s_eff, :] >= ordered_to_float(c)) >= topk, c, r)

    thr = ordered_to_float(lax.fori_loop(0, 32, bisect, jnp.zeros((1, Q_BLOCK), I32)))

    n_ge = count(key_s[:s_eff, :] >= thr)
    clean = jnp.max(jnp.where((n_ge == topk) & (thr > -jnp.inf), 0.0, 1.0)) == 0.0

    @pl.when(clean)
    def _():
        for rows in chunks:
            bias_s[rows, :] = jnp.where(key_s[rows, :] >= thr, 0.0, -jnp.inf)

    @pl.when(jnp.logical_not(clean))
    def _():
        key = key_s[:s_eff, :]
        need = topk - count(key > thr)
        rr = lax.broadcasted_iota(I32, (LANES, LANES), 0)
        cc = lax.broadcasted_iota(I32, (LANES, LANES), 1)
        tri = jnp.where(cc < rr, 1.0, 0.0).astype(BF16)
        seen = jnp.zeros((1, Q_BLOCK), F32)
        for c in range(s_eff // LANES):
            kc = key[c * LANES:(c + 1) * LANES, :]
            eq = kc == thr
            eqf = jnp.where(eq, 1.0, 0.0)
            rank = _dot(tri, eqf.astype(BF16)) + seen
            seen = seen + jnp.sum(eqf, axis=0, keepdims=True)
            tie_bias = jnp.where(jnp.where(eq, rank, topk) < need, 0.0, -jnp.inf)
            bias = jnp.where(kc > thr, 0.0, tie_bias)
            k_pos = c * LANES + lax.broadcasted_iota(I32, (LANES, Q_BLOCK), 0)
            bias_s[c * LANES:(c + 1) * LANES, :] = jnp.where(k_pos < limit, bias, -jnp.inf)

    qnw = qnw_ref[...]
    crow = lax.broadcasted_iota(I32, (LANES - HEAD_DIM_A, Q_BLOCK), 0)
    r_k = lax.broadcasted_iota(I32, (Q_BLOCK, Q_BLOCK), 0)
    c_q = lax.broadcasted_iota(I32, (Q_BLOCK, Q_BLOCK), 1)
    after = 2.0 * jnp.maximum(r_k - c_q, 0).astype(F32)
    band = pl.ds(pl.multiple_of(i * Q_BLOCK, Q_BLOCK), Q_BLOCK)
    slope = [sum(_bf16_pieces(LOG2E * 2.0 ** -(h + 1), SLOPE_PIECES)) for h in range(N_HEADS_A)]
    row_max = []
    for h in range(N_HEADS_A):
        pieces = _bf16_pieces(LOG2E * 2.0 ** -(h + 1), SLOPE_PIECES)
        qh = q_t[h * HEAD_DIM_A:(h + 1) * HEAD_DIM_A, :]
        ms = jnp.mean(qh * qh, axis=0, keepdims=True)
        qn = (qh * lax.rsqrt(ms + EPS) * qnw) * (HEAD_DIM_A ** -0.5) * LOG2E
        slopes = jnp.zeros(crow.shape, F32)
        for k, piece in enumerate(pieces):
            slopes = jnp.where(crow == 2 * k, POS_RADIX * piece, jnp.where(crow == 2 * k + 1, piece, slopes))
        qa = jnp.concatenate([qn, slopes], axis=0).astype(BF16)
        folded = []
        for rows in chunks:
            logits = _dot(kn_s[rows, :], qa) + bias_s[rows, :]
            lg_s[h, rows, :] = logits
            folded.append(jnp.max(logits.reshape(rows_per // ROW_FOLD, ROW_FOLD, Q_BLOCK), axis=0))
        row_max.append(jnp.max(functools.reduce(jnp.maximum, folded), axis=0, keepdims=True))
    for h in range(N_HEADS_A):
        lg_s[h, band, :] = lg_s[h, band, :] - slope[h] * after
    for h in range(N_HEADS_A):
        acc = jnp.zeros((KV_DIM_A + V_ONES, Q_BLOCK), F32)
        for rows in chunks:
            p = jnp.exp2(lg_s[h, rows, :] - row_max[h])
            acc = acc + _dot(vt_s[:, rows], p.astype(BF16))
        ot_s[h * KV_DIM_A:(h + 1) * KV_DIM_A, :] = acc[:KV_DIM_A, :] / acc[KV_DIM_A:KV_DIM_A + 1, :]
    o_ref[...] = ot_s[...].T


def _attn_kernel(qa_ref, kvq_ref, kv_ref, qnw_ref, knw_ref, o_ref, kn_s, vt_s, ki_s, key_s, bias_s, lg_s, ot_s):
    i = pl.program_id(1)
    S = kv_ref.shape[0]
    nb = S // Q_BLOCK
    topk = float(min(TOPK_MAX, S // 4))

    @pl.when(i == 0)
    def _():
        kv = kv_ref[:, :LANES]
        lane = lax.broadcasted_iota(I32, kv.shape, 1)
        pos = lax.broadcasted_iota(I32, kv.shape, 0)
        ms = jnp.sum(jnp.where(lane < KV_DIM_A, kv * kv, 0.0), axis=-1, keepdims=True) * (1.0 / KV_DIM_A)
        kn = kv * lax.rsqrt(ms + EPS) * knw_ref[...]
        digit = jnp.where((lane & 1) == 0, pos // POS_RADIX, pos % POS_RADIX).astype(F32)
        is_digit = (lane >= KV_DIM_A) & (lane < KV_DIM_A + 2 * SLOPE_PIECES)
        kn_s[...] = jnp.where(is_digit, digit, kn).astype(BF16)
        vt_s[:KV_DIM_A, :] = kv.T[V_OFF:V_OFF + KV_DIM_A, :].astype(BF16)
        vt_s[KV_DIM_A:, :] = jnp.ones((V_ONES, kv.shape[0]), BF16)
        ki_s[...] = kv_ref[:, KI_OFF:KI_OFF + IDX_DIM].astype(BF16)

    n_buckets = min(ATTN_BUCKETS, nb)
    per = nb // n_buckets
    for j in range(n_buckets):
        pl.when(i // per == j)(functools.partial(
            _attn_block, (j + 1) * per * Q_BLOCK, topk, i, qa_ref, kvq_ref, qnw_ref, o_ref,
            kn_s, vt_s, ki_s, key_s, bias_s, lg_s, ot_s))


def _attn(qa, kv, q_norm_w, k_norm_w, B, S):
    N = qa.shape[0]
    nb = S // Q_BLOCK
    return pl.pallas_call(
        _attn_kernel,
        grid=(B, nb),
        in_specs=[pl.BlockSpec((Q_BLOCK, QA_W), lambda b, i: (b * nb + i, 0)),
                  pl.BlockSpec((Q_BLOCK, KV_W), lambda b, i: (b * nb + i, 0)),
                  pl.BlockSpec((S, KV_W), lambda b, i: (b, 0)),
                  pl.BlockSpec((HEAD_DIM_A, 1), lambda b, i: (0, 0)),
                  pl.BlockSpec((1, LANES), lambda b, i: (0, 0))],
        out_specs=pl.BlockSpec((Q_BLOCK, ATTN_OUT), lambda b, i: (b * nb + i, 0)),
        out_shape=jax.ShapeDtypeStruct((N, ATTN_OUT), F32),
        scratch_shapes=[pltpu.VMEM((S, LANES), BF16),
                        pltpu.VMEM((KV_DIM_A + V_ONES, S), BF16),
                        pltpu.VMEM((S, IDX_DIM), BF16),
                        pltpu.VMEM((S, Q_BLOCK), F32),
                        pltpu.VMEM((S, Q_BLOCK), F32),
                        pltpu.VMEM((N_HEADS_A, S, Q_BLOCK), F32),
                        pltpu.VMEM((ATTN_OUT, Q_BLOCK), F32)],
        compiler_params=_params("arbitrary", "arbitrary"),
        name="attn",
    )(qa, kv, kv, q_norm_w.reshape(HEAD_DIM_A, 1),
      jnp.pad(k_norm_w, (0, LANES - KV_DIM_A)).reshape(1, LANES))


SUBLANES = 8


def _shift_rows(x, d, fill, row):
    if d % SUBLANES == 0:
        return jnp.concatenate([jnp.full((d, x.shape[1]), fill, x.dtype), x[:-d]], axis=0)
    return jnp.where(row >= d, pltpu.roll(x, d, 0), fill)


def _sigmoid(x):
    return 0.5 * (1.0 + jnp.tanh(0.5 * x))


def _doubling_scan(a, b, row):
    n = a.shape[0]
    d = 1
    while d < n:
        b = a * _shift_rows(b, d, 0.0, row) + b
        if 2 * d < n:
            a = a * _shift_rows(a, d, 1.0, row)
        d *= 2
    return b


def _lru_kernel(xb_ref, gb_ref, cw_ref, cb_ref, wr_ref, br_ref, wi_ref, bi_ref, lam_ref, o_ref, a_s, b_s):
    x = xb_ref[...]
    S = x.shape[0]
    row = lax.broadcasted_iota(I32, x.shape, 0)
    xc = cb_ref[...] + _shift_rows(x, CONV_WIDTH - 1, 0.0, row) * cw_ref[0:1, :]
    for j in range(1, CONV_WIDTH):
        d = CONV_WIDTH - 1 - j
        xs = x if d == 0 else _shift_rows(x, d, 0.0, row)
        xc = xc + xs * cw_ref[j:j + 1, :]
    xcb = xc.astype(BF16)
    r = _sigmoid(_dot(xcb, wr_ref[0]) + br_ref[...])
    ig = _sigmoid(_dot(xcb, wi_ref[0]) + bi_ref[...])
    z = -lam_ref[...]
    softplus = jnp.maximum(z, 0.0) + jnp.log1p(jnp.exp(-jnp.abs(z)))
    log_a = -LRU_C * r * softplus
    a = jnp.exp(log_a)
    b = jnp.sqrt(-jnp.tanh(log_a) * (1.0 + a * a)) * (ig * xc)
    sub = row & (SUBLANES - 1)
    d = 1
    while d < SUBLANES:
        keep = sub >= d
        b = a * jnp.where(keep, pltpu.roll(b, d, 0), 0.0) + b
        a = a * jnp.where(keep, pltpu.roll(a, d, 0), 1.0)
        d *= 2
    a_s[...] = a
    b_s[...] = b
    groups = S // SUBLANES
    last = pl.ds(SUBLANES - 1, groups, stride=SUBLANES)
    g_row = lax.broadcasted_iota(I32, (groups, x.shape[1]), 0)
    h_last = _doubling_scan(a_s[last, :], b_s[last, :], g_row)
    carry = _shift_rows(h_last, 1, 0.0, g_row)
    carry = jnp.broadcast_to(carry[:, None, :], (groups, SUBLANES, x.shape[1])).reshape(x.shape)
    h = b_s[...] + a_s[...] * carry
    g = gb_ref[...]
    gelu = 0.5 * g * (1.0 + jnp.tanh(float(np.sqrt(2.0 / np.pi)) * (g + 0.044715 * (g * g * g))))
    o_ref[...] = h * gelu


def _lru(lru, conv_w, conv_b, wr_bd, b_rec, wi_bd, b_in, lam, B, S):
    N = lru.shape[0]
    nt = LRU_WIDTH // LANES
    vec = lambda b, j: (0, j)
    return pl.pallas_call(
        _lru_kernel,
        grid=(B, nt),
        in_specs=[pl.BlockSpec((S, LANES), lambda b, j: (b, j)),
                  pl.BlockSpec((S, LANES), lambda b, j: (b, nt + j)),
                  pl.BlockSpec((CONV_WIDTH, LANES), vec),
                  pl.BlockSpec((1, LANES), vec),
                  pl.BlockSpec((1, LANES, LANES), lambda b, j: (j, 0, 0)),
                  pl.BlockSpec((1, LANES), vec),
                  pl.BlockSpec((1, LANES, LANES), lambda b, j: (j, 0, 0)),
                  pl.BlockSpec((1, LANES), vec),
                  pl.BlockSpec((1, LANES), vec)],
        out_specs=pl.BlockSpec((S, LANES), lambda b, j: (b, j)),
        out_shape=jax.ShapeDtypeStruct((N, LRU_WIDTH), F32),
        scratch_shapes=[pltpu.VMEM((S, LANES), F32), pltpu.VMEM((S, LANES), F32)],
        compiler_params=_params("arbitrary", "arbitrary"),
        name="lru",
    )(lru, lru, conv_w, conv_b.reshape(1, -1), wr_bd, b_rec.reshape(1, -1), wi_bd,
      b_in.reshape(1, -1), lam.reshape(1, -1))


def _block_diag_tiles(w):
    per = LANES // LRU_BLOCK_DIM
    nt = LRU_WIDTH // LANES
    w5 = w.reshape(nt, per, LRU_BLOCK_DIM, 1, LRU_BLOCK_DIM)
    eye = jnp.eye(per, dtype=w.dtype).reshape(1, per, 1, per, 1)
    return (w5 * eye).reshape(nt, LANES, LANES)


def _first_index_of_max(vals, lane):
    m = jnp.max(vals, axis=-1, keepdims=True)
    idx = jnp.min(jnp.where(vals == m, lane, 4 * LANES), axis=-1, keepdims=True)
    return m, idx


def _weight_pieces(out, w, lane, first):
    rest = w
    for k in range(W_PIECES):
        piece = rest.astype(BF16).astype(F32)
        out = jnp.where(lane == first + k, piece, out)
        rest = rest - piece
    return out


def _merge_kernel(attn_ref, lru_ref, g_ref, x_ref, mod_ref, wpa_ref, wpb_ref, wo_ref, nw_ref, wr_ref,
                  br_ref, x1_ref, hsl_ref, route_ref, gc_ref):
    D = x_ref.shape[1]
    tm = x_ref.shape[0]

    y_a = _dot(attn_ref[...].astype(BF16), wpa_ref[...])
    y_b = _dot(lru_ref[...].astype(BF16), wpb_ref[...])
    merged = g_ref[:, :D].astype(F32) * y_a + g_ref[:, D:].astype(F32) * y_b
    x1 = x_ref[...] + mod_ref[0, 2:3, :] * _dot(merged.astype(BF16), wo_ref[...])
    x1_ref[...] = x1
    h2 = _rms_mod(x1, nw_ref[...], mod_ref[0, 3:4, :], mod_ref[0, 4:5, :])
    h2b = h2.astype(BF16)

    h2l = (h2 - h2b.astype(F32)).astype(BF16)
    hi_hi_lo = _dot(h2b, wr_ref[...])
    logits = hi_hi_lo[:, :LANES] + (_dot(h2l, wr_ref[:, :LANES]) + hi_hi_lo[:, LANES:]) + br_ref[...]
    lane = lax.broadcasted_iota(I32, logits.shape, 1)
    gl = jnp.where(lane < N_GROUPS, logits, -jnp.inf)
    gmax, g_sel = _first_index_of_max(gl, lane)
    g_weight = 1.0 / jnp.sum(jnp.exp(gl - gmax), axis=-1, keepdims=True)
    e_lo = N_GROUPS + g_sel * EXPERTS_PER_GROUP
    el = jnp.where((lane >= e_lo) & (lane < e_lo + EXPERTS_PER_GROUP), logits, -jnp.inf)
    v1, i1 = _first_index_of_max(el, lane)
    el2 = jnp.where(lane == i1, -jnp.inf, el)
    v2, i2 = _first_index_of_max(el2, lane)
    e2x = jnp.exp(v2 - v1)
    w1 = g_weight / (1.0 + e2x)
    w2 = g_weight * e2x / (1.0 + e2x)

    hot1 = lane == i1
    hot2 = lane == i2
    hot = jnp.where(hot1, 1.0, jnp.where(hot2, 1.0, 0.0))
    gcnt = jnp.floor((jnp.sum(hot, axis=0, keepdims=True) + (GRAN - 1.0)) * (1.0 / GRAN))
    gc_ref[0] = gcnt
    rr = lax.broadcasted_iota(I32, (LANES, LANES), 0)
    cc = lax.broadcasted_iota(I32, (LANES, LANES), 1)
    upper = jnp.where(rr < cc, 1.0, 0.0).astype(BF16)
    run_start = GRAN * _dot(jnp.broadcast_to(gcnt, (8, LANES)).astype(BF16), upper)[0:1, :]
    rr = lax.broadcasted_iota(I32, (tm, tm), 0)
    cc = lax.broadcasted_iota(I32, (tm, tm), 1)
    lower = jnp.where(cc < rr, 1.0, 0.0).astype(BF16)
    pos = _dot(lower, hot.astype(BF16)) + run_start
    pos1 = jnp.where(hot1, pos, 0.0)
    pos2 = jnp.where(hot2, pos, 0.0)
    lp1 = jnp.sum(pos1, axis=-1, keepdims=True)
    lp2 = jnp.sum(pos2, axis=-1, keepdims=True)
    route_ref[...] = jnp.where(lane == R_LP1, lp1, jnp.where(lane == R_LP2, lp2, 0.0))

    aux = jnp.where(lane == AUX_E1, (i1 - N_GROUPS).astype(F32), 0.0)
    aux = _weight_pieces(_weight_pieces(aux, w1, lane, AUX_W1), w2, lane, AUX_W2)
    lp1r = jnp.sum(pos1.T, axis=0, keepdims=True).astype(I32)
    lp2r = jnp.sum(pos2.T, axis=0, keepdims=True).astype(I32)
    rows = lax.broadcasted_iota(I32, (LOCAL_ROWS, tm), 0)
    sort = jnp.where(rows == lp1r, 1.0, jnp.where(rows == lp2r, 1.0, 0.0)).astype(BF16)
    hsl_ref[...] = _dot(sort, jnp.concatenate([h2b, aux.astype(BF16)], axis=1)).astype(BF16)


def _merge(attn, lru_out, gates, x2, mod3, wpa, wpb, wo, norm_w, wr_hi_lo, b_router, S):
    N, D = x2.shape
    tm = TM_SORT
    n_tiles = N // tm
    steps_per_batch = S // tm
    const = lambda t: (0, 0)
    rows = lambda t: (t, 0)
    return pl.pallas_call(
        _merge_kernel,
        grid=(n_tiles,),
        in_specs=[pl.BlockSpec((tm, ATTN_OUT), rows),
                  pl.BlockSpec((tm, LRU_WIDTH), rows),
                  pl.BlockSpec((tm, 2 * D), rows),
                  pl.BlockSpec((tm, D), rows),
                  pl.BlockSpec((1, 6, D), lambda t: (t // steps_per_batch, 0, 0)),
                  pl.BlockSpec(wpa.shape, const),
                  pl.BlockSpec(wpb.shape, const),
                  pl.BlockSpec(wo.shape, const),
                  pl.BlockSpec((1, D), const),
                  pl.BlockSpec(wr_hi_lo.shape, const),
                  pl.BlockSpec((1, LANES), const)],
        out_specs=[pl.BlockSpec((tm, D), rows),
                   pl.BlockSpec((LOCAL_ROWS, D + AUX_W), rows),
                   pl.BlockSpec((tm, LANES), rows),
                   pl.BlockSpec((1, 1, LANES), lambda t: (t, 0, 0))],
        out_shape=[jax.ShapeDtypeStruct((N, D), F32),
                   jax.ShapeDtypeStruct((n_tiles * LOCAL_ROWS, D + AUX_W), BF16),
                   jax.ShapeDtypeStruct((N, LANES), F32),
                   jax.ShapeDtypeStruct((n_tiles, 1, LANES), F32)],
        compiler_params=_params("arbitrary"),
        name="merge",
    )(attn, lru_out, gates, x2, mod3, wpa, wpb, wo, norm_w.reshape(1, D), wr_hi_lo, b_router)


def _plan_sizes(n_tiles):
    max_gran = n_tiles * LOCAL_GRAN + N_EXPERTS * (GRAN_PER_TILE - 1)
    max_row_tiles = -(-max_gran // GRAN_PER_TILE)
    return max_row_tiles * GRAN_PER_TILE, max_row_tiles


def _plan_kernel(gc_ref, src_ref, texp_ref, nt_ref, dst_ref, local_next):
    n_tiles = gc_ref.shape[0]
    max_gran, max_row_tiles = _plan_sizes(n_tiles)

    def zero(ref, lo, hi):
        def body(k, c):
            ref[k] = 0
            return c
        lax.fori_loop(lo, hi, body, 0)

    zero(local_next, 0, n_tiles)

    def per_expert(e, g):
        first_tile = g // GRAN_PER_TILE

        def per_tile(t, g):
            n = gc_ref[t, N_GROUPS + e]
            base = t * LOCAL_GRAN + local_next[t]

            def per_granule(k, g):
                src_ref[g] = base + k
                dst_ref[base + k] = g
                return g + 1

            g = lax.fori_loop(0, n, per_granule, g)
            local_next[t] = local_next[t] + n
            return g

        g = lax.fori_loop(0, n_tiles, per_tile, g)
        padded = (g + GRAN_PER_TILE - 1) // GRAN_PER_TILE * GRAN_PER_TILE
        zero(src_ref, g, padded)
        g = padded

        def mark(j, c):
            texp_ref[j] = e
            return c

        lax.fori_loop(first_tile, g // GRAN_PER_TILE, mark, 0)
        return g

    g = lax.fori_loop(0, N_EXPERTS, per_expert, 0)
    used = g // GRAN_PER_TILE
    nt_ref[0] = used
    zero(src_ref, g, max_gran)

    def clear_tail(t, c):
        zero(dst_ref, t * LOCAL_GRAN + local_next[t], (t + 1) * LOCAL_GRAN)
        return c

    lax.fori_loop(0, n_tiles, clear_tail, 0)

    def mark_rest(j, c):
        texp_ref[j] = N_EXPERTS - 1
        return c

    lax.fori_loop(used, max_row_tiles, mark_rest, 0)


def _plan(gc):
    n_tiles = gc.shape[0]
    max_gran, max_row_tiles = _plan_sizes(n_tiles)
    smem = pl.BlockSpec(memory_space=pltpu.SMEM)
    return pl.pallas_call(
        _plan_kernel,
        in_specs=[smem],
        out_specs=[smem, smem, smem, smem, smem],
        out_shape=[jax.ShapeDtypeStruct((max_gran,), I32),
                   jax.ShapeDtypeStruct((max_row_tiles,), I32),
                   jax.ShapeDtypeStruct((1,), I32),
                   jax.ShapeDtypeStruct((n_tiles * LOCAL_GRAN,), I32),
                   jax.ShapeDtypeStruct((n_tiles,), I32)],
        name="plan",
    )(gc)


def _granule_copies(idx_ref, first, count, src_hbm, dst_vmem, sem, wait):
    def body(k, c):
        g = idx_ref[first + k]
        cp = pltpu.make_async_copy(src_hbm.at[pl.ds(pl.multiple_of(g * GRAN, GRAN), GRAN)],
                                   dst_vmem.at[pl.ds(pl.multiple_of(k * GRAN, GRAN), GRAN)], sem)
        if wait:
            cp.wait()
        else:
            cp.start()
        return c
    lax.fori_loop(0, count, body, 0)


def _moe_kernel(src_ref, texp_ref, nt_ref, hsl_ref, w1_ref, w3_ref, w2_ref, ys_ref, xbuf, w1b, w3b, w2b, sem):
    i = pl.program_id(0)
    used = nt_ref[0]
    slot = i % MOE_BUFFERS
    D = ys_ref.shape[1]

    def gather(tile, wait):
        slot_ = tile % MOE_BUFFERS
        _granule_copies(src_ref, tile * GRAN_PER_TILE, GRAN_PER_TILE, hsl_ref, xbuf.at[slot_], sem.at[slot_], wait)

    @pl.when(i == 0)
    def _():
        for ahead in range(MOE_BUFFERS - 1):
            pl.when(ahead < used)(functools.partial(gather, ahead, False))

    @pl.when(i + (MOE_BUFFERS - 1) < used)
    def _():
        gather(i + (MOE_BUFFERS - 1), False)

    @pl.when(i < used)
    def _():
        gather(i, True)

        @pl.when((i == 0) | (texp_ref[i] != texp_ref[jnp.maximum(i - 1, 0)]))
        def _():
            w1b[...] = w1_ref[0].astype(BF16)
            w3b[...] = w3_ref[0].astype(BF16)
            w2b[...] = w2_ref[0].astype(BF16)

        xb = xbuf[slot]
        hb = xb[:, :D]
        aux = xb[:, D:].astype(F32)
        first = aux[:, AUX_E1:AUX_E1 + 1] == texp_ref[i].astype(F32)
        w_row = jnp.zeros((TM_MOE, 1), F32)
        for k in range(W_PIECES):
            w_row = w_row + jnp.where(first, aux[:, AUX_W1 + k:AUX_W1 + k + 1], aux[:, AUX_W2 + k:AUX_W2 + k + 1])
        a = _dot(hb, w1b[...])
        act = (a * _sigmoid(a)) * _dot(hb, w3b[...])
        ys_ref[...] = (w_row * _dot(act.astype(BF16), w2b[...])).astype(BF16)

    @pl.when(i >= used)
    def _():
        ys_ref[...] = jnp.zeros_like(ys_ref)


def _moe(src, texp, used, hsl, w1, w3, w2):
    max_row_tiles = texp.shape[0]
    D = w1.shape[1]
    F = w1.shape[2]
    grid_spec = pltpu.PrefetchScalarGridSpec(
        num_scalar_prefetch=3,
        grid=(max_row_tiles,),
        in_specs=[pl.BlockSpec(memory_space=pl.ANY),
                  pl.BlockSpec((1, D, F), lambda i, s, te, nt: (te[i], 0, 0)),
                  pl.BlockSpec((1, D, F), lambda i, s, te, nt: (te[i], 0, 0)),
                  pl.BlockSpec((1, F, D), lambda i, s, te, nt: (te[i], 0, 0))],
        out_specs=pl.BlockSpec((TM_MOE, D), lambda i, s, te, nt: (i, 0)),
        scratch_shapes=[pltpu.VMEM((MOE_BUFFERS, TM_MOE, D + AUX_W), BF16),
                        pltpu.VMEM((D, F), BF16), pltpu.VMEM((D, F), BF16), pltpu.VMEM((F, D), BF16),
                        pltpu.SemaphoreType.DMA((MOE_BUFFERS,))],
    )
    return pl.pallas_call(
        _moe_kernel,
        grid_spec=grid_spec,
        out_shape=jax.ShapeDtypeStruct((max_row_tiles * TM_MOE, D), BF16),
        compiler_params=_params("arbitrary"),
        name="moe",
    )(src, texp, used, hsl, w1, w3, w2)


def _combine_kernel(dst_ref, lcnt_ref, ys_ref, x1_ref, route_ref, mod_ref, o_ref, ybuf, sem):
    t = pl.program_id(0)
    slot = t % 2
    tm = x1_ref.shape[0]

    def gather(tile, slot_, wait):
        _granule_copies(dst_ref, tile * LOCAL_GRAN, lcnt_ref[tile], ys_ref, ybuf.at[slot_], sem.at[slot_], wait)

    @pl.when(t == 0)
    def _():
        ybuf[...] = jnp.zeros_like(ybuf)
        gather(0, 0, False)

    @pl.when(t + 1 < pl.num_programs(0))
    def _():
        gather(t + 1, 1 - slot, False)

    gather(t, slot, True)
    lp1 = route_ref[:, R_LP1:R_LP1 + 1].astype(I32)
    lp2 = route_ref[:, R_LP2:R_LP2 + 1].astype(I32)
    col = lax.broadcasted_iota(I32, (tm, LOCAL_ROWS), 1)
    unsort = jnp.where(col == lp1, 1.0, jnp.where(col == lp2, 1.0, 0.0)).astype(BF16)
    o_ref[...] = x1_ref[...] + mod_ref[0, 5:6, :] * _dot(unsort, ybuf[slot])


def _combine(dst, lcnt, ys, x1, route, mod3, S):
    N, D = x1.shape
    tm = TM_SORT
    steps_per_batch = S // tm
    grid_spec = pltpu.PrefetchScalarGridSpec(
        num_scalar_prefetch=2,
        grid=(N // tm,),
        in_specs=[pl.BlockSpec(memory_space=pl.ANY),
                  pl.BlockSpec((tm, D), lambda t, d, c: (t, 0)),
                  pl.BlockSpec((tm, LANES), lambda t, d, c: (t, 0)),
                  pl.BlockSpec((1, 6, D), lambda t, d, c: (t // steps_per_batch, 0, 0))],
        out_specs=pl.BlockSpec((tm, D), lambda t, d, c: (t, 0)),
        scratch_shapes=[pltpu.VMEM((2, LOCAL_ROWS, D), BF16), pltpu.SemaphoreType.DMA((2,))],
    )
    return pl.pallas_call(
        _combine_kernel,
        grid_spec=grid_spec,
        out_shape=jax.ShapeDtypeStruct((N, D), F32),
        compiler_params=_params("arbitrary"),
        name="combine",
    )(dst, lcnt, ys, x1, route, mod3)


def _layer(x2, mod3, B, S, norm_mix_w, w_in, q_norm_w, k_norm_w, conv_w, conv_b, w_rec_gate, b_rec_gate,
           w_in_gate, b_in_gate, lru_lambda, w_proj_a, w_proj_b, w_out, norm_ffn_w, w_group, b_group,
           w_expert_router, b_expert_router, w1, w3, w2):
    N, D = x2.shape
    nq = N_HEADS_A * HEAD_DIM_A
    o_k = nq
    o_v = o_k + KV_DIM_A
    o_qi = o_v + KV_DIM_A
    o_ki = o_qi + N_IDX_HEADS * IDX_DIM
    o_wi = o_ki + IDX_DIM
    o_lx = o_wi + N_IDX_HEADS
    o_g = o_lx + 2 * LRU_WIDTH
    pad = jnp.zeros((D, KV_W - (WI_OFF + N_IDX_HEADS)), w_in.dtype)
    wa = jnp.concatenate([w_in[:, :nq], w_in[:, o_qi:o_ki], w_in[:, o_k:o_qi], w_in[:, o_ki:o_lx], pad],
                         axis=1).astype(BF16)
    wb = w_in[:, o_lx:o_g].astype(BF16)
    wg = w_in[:, o_g:].astype(BF16)

    qa, kv, lru, gates = _proj(x2, mod3, norm_mix_w, wa, wb, wg, S)
    attn = _attn(qa, kv, q_norm_w, k_norm_w, B, S)
    lru_out = _lru(lru, conv_w, conv_b, _block_diag_tiles(w_rec_gate).astype(BF16), b_rec_gate,
                   _block_diag_tiles(w_in_gate).astype(BF16), b_in_gate, lru_lambda, B, S)

    n_r = N_GROUPS + N_EXPERTS
    w_router = jnp.concatenate([w_group, w_expert_router, jnp.zeros((D, LANES - n_r), F32)], axis=1)
    wr_hi = w_router.astype(BF16)
    wr_lo = (w_router - wr_hi.astype(F32)).astype(BF16)
    b_router = jnp.concatenate([b_group, b_expert_router, jnp.zeros((LANES - n_r,), F32)]).reshape(1, LANES)
    x1, hsl, route, gc = _merge(attn, lru_out, gates, x2, mod3, w_proj_a.astype(BF16), w_proj_b.astype(BF16),
                                w_out.astype(BF16), norm_ffn_w, jnp.concatenate([wr_hi, wr_lo], axis=1),
                                b_router, S)

    src, texp, used, dst, lcnt = _plan(gc.reshape(gc.shape[0], LANES).astype(I32))
    ys = _moe(src, texp, used, hsl, w1, w3, w2)
    return _combine(dst, lcnt, ys, x1, route, mod3, S)


def kernel(x, c, ada_w, ada_b, norm_mix_w, w_in, q_norm_w, k_norm_w, conv_w, conv_b, w_rec_gate, b_rec_gate,
           w_in_gate, b_in_gate, lru_lambda, w_proj_a, w_proj_b, w_out, norm_ffn_w, w_group, b_group,
           w_expert_router, b_expert_router, w1, w3, w2):
    B, S, D = x.shape
    x2 = x.reshape(B * S, D)
    for l in range(ada_w.shape[0]):
        mod3 = _ada(c, ada_w[l], ada_b[l]).reshape(B, 6, D)
        x2 = _layer(x2, mod3, B, S, norm_mix_w[l], w_in[l], q_norm_w[l], k_norm_w[l], conv_w[l], conv_b[l],
                    w_rec_gate[l], b_rec_gate[l], w_in_gate[l], b_in_gate[l], lru_lambda[l], w_proj_a[l],
                    w_proj_b[l], w_out[l], norm_ffn_w[l], w_group[l], b_group[l], w_expert_router[l],
                    b_expert_router[l], w1[l], w3[l], w2[l])
    return x2.reshape(B, S, D)
```

```python
import functools

import jax
import jax.numpy as jnp
import numpy as np
from jax import lax
from jax.experimental import pallas as pl
from jax.experimental.pallas import tpu as pltpu

F32 = jnp.float32
BF16 = jnp.bfloat16
I32 = jnp.int32

EPS = 1e-6
CHUNK = 64
Q_BLOCK = 128
N_HEADS_A = 8
HEAD_DIM_A = 64
KV_DIM_A = 64
ATTN_OUT = N_HEADS_A * KV_DIM_A
N_IDX_HEADS = 4
IDX_DIM = 64
TOPK_MAX = 256
LRU_WIDTH = 512
LRU_BLOCKS = 8
LRU_BLOCK_DIM = LRU_WIDTH // LRU_BLOCKS
CONV_WIDTH = 4
LRU_C = 8.0
N_GROUPS = 4
EXPERTS_PER_GROUP = 8
N_EXPERTS = N_GROUPS * EXPERTS_PER_GROUP
D_FF_EXPERT = 256

LANES = 128
INT_MIN = -2 ** 31
VMEM_LIMIT = 56 * 1024 * 1024

QA_W = N_HEADS_A * HEAD_DIM_A + N_IDX_HEADS * IDX_DIM
KV_W = 256
K_OFF, V_OFF, KI_OFF, WI_OFF = 0, 64, 128, 192

TM_PROJ = 512

TM_SORT = 512
GRAN = 16
TM_MOE = 256
GRAN_PER_TILE = TM_MOE // GRAN
MOE_BUFFERS = 3
LOCAL_ROWS = -(-(2 * TM_SORT + N_EXPERTS * (GRAN - 1)) // LANES) * LANES
LOCAL_GRAN = LOCAL_ROWS // GRAN
AUX_W = LANES
W_PIECES = 3
AUX_W1, AUX_W2 = 0, W_PIECES
AUX_E1 = 2 * W_PIECES

R_LP1, R_LP2 = 0, 1


def _dot(a, b, **kw):
    return jnp.dot(a, b, preferred_element_type=F32, **kw)


def _params(*sem):
    return pltpu.CompilerParams(dimension_semantics=sem, vmem_limit_bytes=VMEM_LIMIT)


def _ada_kernel(c_ref, w_ref, b_ref, o_ref):
    c = c_ref[...]
    cond = c * jax.nn.sigmoid(c)
    o_ref[...] = _dot(cond, w_ref[...], precision=lax.Precision.HIGHEST) + b_ref[...]


def _ada(c, ada_w, ada_b):
    B, D = c.shape
    n_out = ada_w.shape[1]
    tn = 1024
    return pl.pallas_call(
        _ada_kernel,
        grid=(n_out // tn,),
        in_specs=[pl.BlockSpec((B, D), lambda j: (0, 0)),
                  pl.BlockSpec((D, tn), lambda j: (0, j)),
                  pl.BlockSpec((1, tn), lambda j: (0, j))],
        out_specs=pl.BlockSpec((B, tn), lambda j: (0, j)),
        out_shape=jax.ShapeDtypeStruct((B, n_out), F32),
        compiler_params=_params("arbitrary"),
        name="ada",
    )(c, ada_w, ada_b.reshape(1, n_out))


def _rms_mod(x, w, shift, scale):
    ms = jnp.mean(x * x, axis=-1, keepdims=True)
    y = x * lax.rsqrt(ms + EPS) * w
    return y * (1.0 + scale) + shift


def _proj_kernel(x_ref, mod_ref, nw_ref, wa_ref, wb_ref, wg_ref, oqa_ref, okv_ref, olru_ref, og_ref):
    h = _rms_mod(x_ref[...], nw_ref[...], mod_ref[0, 0:1, :], mod_ref[0, 1:2, :])
    hb = h.astype(BF16)
    pa = _dot(hb, wa_ref[...])
    oqa_ref[...] = pa[:, :QA_W]
    okv_ref[...] = pa[:, QA_W:]
    olru_ref[...] = _dot(hb, wb_ref[...])
    og_ref[...] = _sigmoid(_dot(hb, wg_ref[...])).astype(BF16)


def _proj(x2, mod3, norm_w, wa, wb, wg, S):
    N, D = x2.shape
    tm = TM_PROJ
    steps_per_batch = S // tm
    const = lambda t: (0, 0)
    return pl.pallas_call(
        _proj_kernel,
        grid=(N // tm,),
        in_specs=[pl.BlockSpec((tm, D), lambda t: (t, 0)),
                  pl.BlockSpec((1, 6, D), lambda t: (t // steps_per_batch, 0, 0)),
                  pl.BlockSpec((1, D), const),
                  pl.BlockSpec(wa.shape, const),
                  pl.BlockSpec(wb.shape, const),
                  pl.BlockSpec(wg.shape, const)],
        out_specs=[pl.BlockSpec((tm, QA_W), lambda t: (t, 0)),
                   pl.BlockSpec((tm, KV_W), lambda t: (t, 0)),
                   pl.BlockSpec((tm, 2 * LRU_WIDTH), lambda t: (t, 0)),
                   pl.BlockSpec((tm, 2 * D), lambda t: (t, 0))],
        out_shape=[jax.ShapeDtypeStruct((N, QA_W), F32),
                   jax.ShapeDtypeStruct((N, KV_W), F32),
                   jax.ShapeDtypeStruct((N, 2 * LRU_WIDTH), F32),
                   jax.ShapeDtypeStruct((N, 2 * D), BF16)],
        compiler_params=_params("arbitrary"),
        name="proj",
    )(x2, mod3, norm_w.reshape(1, D), wa, wb, wg)


ATTN_BUCKETS = 8
ROW_FOLD = 64
KEY_ROWS = 256
COUNT_CHAINS = 2
LOG2E = float(np.log2(np.e))
POS_RADIX = 256
SLOPE_PIECES = 3
V_ONES = 16


def _bf16_pieces(c, n):
    out = []
    rest = float(c)
    for _ in range(n):
        p = float(np.asarray(rest, np.float32).astype(BF16).astype(np.float32))
        out.append(p)
        rest -= p
    return out


def _reduce_rows(op, x):
    r = x.shape[0]
    if r > ROW_FOLD and r % ROW_FOLD == 0:
        x = op(x.reshape(r // ROW_FOLD, ROW_FOLD, x.shape[1]), axis=0)
    return op(x, axis=0, keepdims=True)


def _attn_block(s_eff, topk, i, qa_ref, kvq_ref, qnw_ref, o_ref, kn_s, vt_s, ki_s, key_s, bias_s, lg_s, ot_s):
    nq = N_HEADS_A * HEAD_DIM_A
    q_t = qa_ref[:, :nq].T
    qi_t = qa_ref[:, nq:].T
    wq_t = kvq_ref[...].T[WI_OFF:WI_OFF + N_IDX_HEADS, :] * (N_IDX_HEADS ** -0.5 * IDX_DIM ** -0.5)

    q_pos = i * Q_BLOCK + lax.broadcasted_iota(I32, (1, Q_BLOCK), 1)
    limit = (q_pos // CHUNK + 1) * CHUNK
    rows_per = KEY_ROWS if s_eff % KEY_ROWS == 0 else Q_BLOCK
    chunks = [slice(c * rows_per, (c + 1) * rows_per) for c in range(s_eff // rows_per)]

    qi = [qi_t[h * IDX_DIM:(h + 1) * IDX_DIM, :].astype(BF16) for h in range(N_IDX_HEADS)]
    for rows in chunks:
        ki = ki_s[rows, :]
        score = jnp.zeros((rows_per, Q_BLOCK), F32)
        for h in range(N_IDX_HEADS):
            score = score + jnp.maximum(_dot(ki, qi[h]), 0.0) * wq_t[h:h + 1, :]
        k_pos = rows.start + lax.broadcasted_iota(I32, (rows_per, Q_BLOCK), 0)
        key_s[rows, :] = jnp.where(k_pos < limit, score, -jnp.inf)

    def count(mask):
        return _reduce_rows(jnp.sum, jnp.where(mask, 1.0, 0.0))

    def count_keys(test):
        acc = [jnp.zeros((ROW_FOLD, Q_BLOCK), F32) for _ in range(COUNT_CHAINS)]
        for n, r0 in enumerate(range(0, s_eff, ROW_FOLD)):
            hit = jnp.where(test(key_s[r0:r0 + ROW_FOLD, :]), 1.0, 0.0)
            acc[n % COUNT_CHAINS] = acc[n % COUNT_CHAINS] + hit
        return jnp.sum(functools.reduce(lambda x, y: x + y, acc), axis=0, keepdims=True)

    def ordered_to_float(c):
        s = c ^ INT_MIN
        return pltpu.bitcast(jnp.where(s < 0, INT_MIN | (-s), s), F32)

    def bisect(it, r):
        c = r | jnp.left_shift(jnp.int32(1), 31 - it)
        cand = ordered_to_float(c)
        return jnp.where(count_keys(lambda slab: slab >= cand) >= topk, c, r)

    thr = ordered_to_float(lax.fori_loop(0, 32, bisect, jnp.zeros((1, Q_BLOCK), I32)))

    n_ge = count_keys(lambda slab: slab >= thr)
    clean = jnp.max(jnp.where((n_ge == topk) & (thr > -jnp.inf), 0.0, 1.0)) == 0.0

    @pl.when(clean)
    def _():
        for rows in chunks:
            bias_s[rows, :] = jnp.where(key_s[rows, :] >= thr, 0.0, -jnp.inf)

    @pl.when(jnp.logical_not(clean))
    def _():
        key = key_s[:s_eff, :]
        need = topk - count(key > thr)
        rr = lax.broadcasted_iota(I32, (LANES, LANES), 0)
        cc = lax.broadcasted_iota(I32, (LANES, LANES), 1)
        tri = jnp.where(cc < rr, 1.0, 0.0).astype(BF16)
        seen = jnp.zeros((1, Q_BLOCK), F32)
        for c in range(s_eff // LANES):
            kc = key[c * LANES:(c + 1) * LANES, :]
            eq = kc == thr
            eqf = jnp.where(eq, 1.0, 0.0)
            rank = _dot(tri, eqf.astype(BF16)) + seen
            seen = seen + jnp.sum(eqf, axis=0, keepdims=True)
            tie_bias = jnp.where(jnp.where(eq, rank, topk) < need, 0.0, -jnp.inf)
            bias = jnp.where(kc > thr, 0.0, tie_bias)
            k_pos = c * LANES + lax.broadcasted_iota(I32, (LANES, Q_BLOCK), 0)
            bias_s[c * LANES:(c + 1) * LANES, :] = jnp.where(k_pos < limit, bias, -jnp.inf)

    qnw = qnw_ref[...]
    crow = lax.broadcasted_iota(I32, (LANES - HEAD_DIM_A, Q_BLOCK), 0)
    r_k = lax.broadcasted_iota(I32, (Q_BLOCK, Q_BLOCK), 0)
    c_q = lax.broadcasted_iota(I32, (Q_BLOCK, Q_BLOCK), 1)
    after = 2.0 * jnp.maximum(r_k - c_q, 0).astype(F32)
    band = pl.ds(pl.multiple_of(i * Q_BLOCK, Q_BLOCK), Q_BLOCK)
    slope = [sum(_bf16_pieces(LOG2E * 2.0 ** -(h + 1), SLOPE_PIECES)) for h in range(N_HEADS_A)]
    row_max = []
    for h in range(N_HEADS_A):
        pieces = _bf16_pieces(LOG2E * 2.0 ** -(h + 1), SLOPE_PIECES)
        qh = q_t[h * HEAD_DIM_A:(h + 1) * HEAD_DIM_A, :]
        ms = jnp.mean(qh * qh, axis=0, keepdims=True)
        qn = (qh * lax.rsqrt(ms + EPS) * qnw) * (HEAD_DIM_A ** -0.5) * LOG2E
        slopes = jnp.zeros(crow.shape, F32)
        for k, piece in enumerate(pieces):
            slopes = jnp.where(crow == 2 * k, POS_RADIX * piece, jnp.where(crow == 2 * k + 1, piece, slopes))
        qa = jnp.concatenate([qn, slopes], axis=0).astype(BF16)
        folded = []
        for rows in chunks:
            logits = _dot(kn_s[rows, :], qa) + bias_s[rows, :]
            lg_s[h, rows, :] = logits
            folded.append(jnp.max(logits.reshape(rows_per // ROW_FOLD, ROW_FOLD, Q_BLOCK), axis=0))
        row_max.append(jnp.max(functools.reduce(jnp.maximum, folded), axis=0, keepdims=True))
    for h in range(N_HEADS_A):
        lg_s[h, band, :] = lg_s[h, band, :] - slope[h] * after
    for h in range(N_HEADS_A):
        acc = jnp.zeros((KV_DIM_A + V_ONES, Q_BLOCK), F32)
        for rows in chunks:
            p = jnp.exp2(lg_s[h, rows, :] - row_max[h])
            acc = acc + _dot(vt_s[:, rows], p.astype(BF16))
        ot_s[h * KV_DIM_A:(h + 1) * KV_DIM_A, :] = acc[:KV_DIM_A, :] / acc[KV_DIM_A:KV_DIM_A + 1, :]
    o_ref[...] = ot_s[...].T


def _attn_kernel(qa_ref, kvq_ref, kv_ref, qnw_ref, knw_ref, o_ref, kn_s, vt_s, ki_s, key_s, bias_s, lg_s, ot_s):
    i = pl.program_id(1)
    S = kv_ref.shape[0]
    nb = S // Q_BLOCK
    topk = float(min(TOPK_MAX, S // 4))

    @pl.when(i == 0)
    def _():
        kv = kv_ref[:, :LANES]
        lane = lax.broadcasted_iota(I32, kv.shape, 1)
        pos = lax.broadcasted_iota(I32, kv.shape, 0)
        ms = jnp.sum(jnp.where(lane < KV_DIM_A, kv * kv, 0.0), axis=-1, keepdims=True) * (1.0 / KV_DIM_A)
        kn = kv * lax.rsqrt(ms + EPS) * knw_ref[...]
        digit = jnp.where((lane & 1) == 0, pos // POS_RADIX, pos % POS_RADIX).astype(F32)
        is_digit = (lane >= KV_DIM_A) & (lane < KV_DIM_A + 2 * SLOPE_PIECES)
        kn_s[...] = jnp.where(is_digit, digit, kn).astype(BF16)
        vt_s[:KV_DIM_A, :] = kv.T[V_OFF:V_OFF + KV_DIM_A, :].astype(BF16)
        vt_s[KV_DIM_A:, :] = jnp.ones((V_ONES, kv.shape[0]), BF16)
        ki_s[...] = kv_ref[:, KI_OFF:KI_OFF + IDX_DIM].astype(BF16)

    n_buckets = min(ATTN_BUCKETS, nb)
    per = nb // n_buckets
    for j in range(n_buckets):
        pl.when(i // per == j)(functools.partial(
            _attn_block, (j + 1) * per * Q_BLOCK, topk, i, qa_ref, kvq_ref, qnw_ref, o_ref,
            kn_s, vt_s, ki_s, key_s, bias_s, lg_s, ot_s))


def _attn(qa, kv, q_norm_w, k_norm_w, B, S):
    N = qa.shape[0]
    nb = S // Q_BLOCK
    return pl.pallas_call(
        _attn_kernel,
        grid=(B, nb),
        in_specs=[pl.BlockSpec((Q_BLOCK, QA_W), lambda b, i: (b * nb + i, 0)),
                  pl.BlockSpec((Q_BLOCK, KV_W), lambda b, i: (b * nb + i, 0)),
                  pl.BlockSpec((S, KV_W), lambda b, i: (b, 0)),
                  pl.BlockSpec((HEAD_DIM_A, 1), lambda b, i: (0, 0)),
                  pl.BlockSpec((1, LANES), lambda b, i: (0, 0))],
        out_specs=pl.BlockSpec((Q_BLOCK, ATTN_OUT), lambda b, i: (b * nb + i, 0)),
        out_shape=jax.ShapeDtypeStruct((N, ATTN_OUT), F32),
        scratch_shapes=[pltpu.VMEM((S, LANES), BF16),
                        pltpu.VMEM((KV_DIM_A + V_ONES, S), BF16),
                        pltpu.VMEM((S, IDX_DIM), BF16),
                        pltpu.VMEM((S, Q_BLOCK), F32),
                        pltpu.VMEM((S, Q_BLOCK), F32),
                        pltpu.VMEM((N_HEADS_A, S, Q_BLOCK), F32),
                        pltpu.VMEM((ATTN_OUT, Q_BLOCK), F32)],
        compiler_params=_params("arbitrary", "arbitrary"),
        name="attn",
    )(qa, kv, kv, q_norm_w.reshape(HEAD_DIM_A, 1),
      jnp.pad(k_norm_w, (0, LANES - KV_DIM_A)).reshape(1, LANES))


SUBLANES = 8


def _shift_rows(x, d, fill, row):
    if d % SUBLANES == 0:
        return jnp.concatenate([jnp.full((d, x.shape[1]), fill, x.dtype), x[:-d]], axis=0)
    return jnp.where(row >= d, pltpu.roll(x, d, 0), fill)


def _sigmoid(x):
    return 0.5 * (1.0 + jnp.tanh(0.5 * x))


def _doubling_scan(a, b, row):
    n = a.shape[0]
    d = 1
    while d < n:
        b = a * _shift_rows(b, d, 0.0, row) + b
        if 2 * d < n:
            a = a * _shift_rows(a, d, 1.0, row)
        d *= 2
    return b


def _lru_kernel(xb_ref, gb_ref, cw_ref, cb_ref, wr_ref, br_ref, wi_ref, bi_ref, lam_ref, o_ref, a_s, b_s):
    x = xb_ref[...]
    S = x.shape[0]
    row = lax.broadcasted_iota(I32, x.shape, 0)
    xc = cb_ref[...] + _shift_rows(x, CONV_WIDTH - 1, 0.0, row) * cw_ref[0:1, :]
    for j in range(1, CONV_WIDTH):
        d = CONV_WIDTH - 1 - j
        xs = x if d == 0 else _shift_rows(x, d, 0.0, row)
        xc = xc + xs * cw_ref[j:j + 1, :]
    xcb = xc.astype(BF16)
    r = _sigmoid(_dot(xcb, wr_ref[0]) + br_ref[...])
    ig = _sigmoid(_dot(xcb, wi_ref[0]) + bi_ref[...])
    z = -lam_ref[...]
    softplus = jnp.maximum(z, 0.0) + jnp.log1p(jnp.exp(-jnp.abs(z)))
    log_a = -LRU_C * r * softplus
    a = jnp.exp(log_a)
    b = jnp.sqrt(-jnp.tanh(log_a) * (1.0 + a * a)) * (ig * xc)
    sub = row & (SUBLANES - 1)
    d = 1
    while d < SUBLANES:
        keep = sub >= d
        b = a * jnp.where(keep, pltpu.roll(b, d, 0), 0.0) + b
        a = a * jnp.where(keep, pltpu.roll(a, d, 0), 1.0)
        d *= 2
    a_s[...] = a
    b_s[...] = b
    groups = S // SUBLANES
    last = pl.ds(SUBLANES - 1, groups, stride=SUBLANES)
    g_row = lax.broadcasted_iota(I32, (groups, x.shape[1]), 0)
    h_last = _doubling_scan(a_s[last, :], b_s[last, :], g_row)
    carry = _shift_rows(h_last, 1, 0.0, g_row)
    carry = jnp.broadcast_to(carry[:, None, :], (groups, SUBLANES, x.shape[1])).reshape(x.shape)
    h = b_s[...] + a_s[...] * carry
    g = gb_ref[...]
    gelu = 0.5 * g * (1.0 + jnp.tanh(float(np.sqrt(2.0 / np.pi)) * (g + 0.044715 * (g * g * g))))
    o_ref[...] = h * gelu


def _lru(lru, conv_w, conv_b, wr_bd, b_rec, wi_bd, b_in, lam, B, S):
    N = lru.shape[0]
    nt = LRU_WIDTH // LANES
    vec = lambda b, j: (0, j)
    return pl.pallas_call(
        _lru_kernel,
        grid=(B, nt),
        in_specs=[pl.BlockSpec((S, LANES), lambda b, j: (b, j)),
                  pl.BlockSpec((S, LANES), lambda b, j: (b, nt + j)),
                  pl.BlockSpec((CONV_WIDTH, LANES), vec),
                  pl.BlockSpec((1, LANES), vec),
                  pl.BlockSpec((1, LANES, LANES), lambda b, j: (j, 0, 0)),
                  pl.BlockSpec((1, LANES), vec),
                  pl.BlockSpec((1, LANES, LANES), lambda b, j: (j, 0, 0)),
                  pl.BlockSpec((1, LANES), vec),
                  pl.BlockSpec((1, LANES), vec)],
        out_specs=pl.BlockSpec((S, LANES), lambda b, j: (b, j)),
        out_shape=jax.ShapeDtypeStruct((N, LRU_WIDTH), F32),
        scratch_shapes=[pltpu.VMEM((S, LANES), F32), pltpu.VMEM((S, LANES), F32)],
        compiler_params=_params("arbitrary", "arbitrary"),
        name="lru",
    )(lru, lru, conv_w, conv_b.reshape(1, -1), wr_bd, b_rec.reshape(1, -1), wi_bd,
      b_in.reshape(1, -1), lam.reshape(1, -1))


def _block_diag_tiles(w):
    per = LANES // LRU_BLOCK_DIM
    nt = LRU_WIDTH // LANES
    w5 = w.reshape(nt, per, LRU_BLOCK_DIM, 1, LRU_BLOCK_DIM)
    eye = jnp.eye(per, dtype=w.dtype).reshape(1, per, 1, per, 1)
    return (w5 * eye).reshape(nt, LANES, LANES)


def _first_index_of_max(vals, lane):
    m = jnp.max(vals, axis=-1, keepdims=True)
    idx = jnp.min(jnp.where(vals == m, lane, 4 * LANES), axis=-1, keepdims=True)
    return m, idx


def _weight_pieces(out, w, lane, first):
    rest = w
    for k in range(W_PIECES):
        piece = rest.astype(BF16).astype(F32)
        out = jnp.where(lane == first + k, piece, out)
        rest = rest - piece
    return out


def _merge_kernel(attn_ref, lru_ref, g_ref, x_ref, mod_ref, wpa_ref, wpb_ref, wo_ref, nw_ref, wr_ref,
                  br_ref, x1_ref, hsl_ref, route_ref, gc_ref):
    D = x_ref.shape[1]
    tm = x_ref.shape[0]

    y_a = _dot(attn_ref[...].astype(BF16), wpa_ref[...])
    y_b = _dot(lru_ref[...].astype(BF16), wpb_ref[...])
    merged = g_ref[:, :D].astype(F32) * y_a + g_ref[:, D:].astype(F32) * y_b
    x1 = x_ref[...] + mod_ref[0, 2:3, :] * _dot(merged.astype(BF16), wo_ref[...])
    x1_ref[...] = x1
    h2 = _rms_mod(x1, nw_ref[...], mod_ref[0, 3:4, :], mod_ref[0, 4:5, :])
    h2b = h2.astype(BF16)

    h2l = (h2 - h2b.astype(F32)).astype(BF16)
    hi_hi_lo = _dot(h2b, wr_ref[...])
    logits = hi_hi_lo[:, :LANES] + (_dot(h2l, wr_ref[:, :LANES]) + hi_hi_lo[:, LANES:]) + br_ref[...]
    lane = lax.broadcasted_iota(I32, logits.shape, 1)
    gl = jnp.where(lane < N_GROUPS, logits, -jnp.inf)
    gmax, g_sel = _first_index_of_max(gl, lane)
    g_weight = 1.0 / jnp.sum(jnp.exp(gl - gmax), axis=-1, keepdims=True)
    e_lo = N_GROUPS + g_sel * EXPERTS_PER_GROUP
    el = jnp.where((lane >= e_lo) & (lane < e_lo + EXPERTS_PER_GROUP), logits, -jnp.inf)
    v1, i1 = _first_index_of_max(el, lane)
    el2 = jnp.where(lane == i1, -jnp.inf, el)
    v2, i2 = _first_index_of_max(el2, lane)
    e2x = jnp.exp(v2 - v1)
    w1 = g_weight / (1.0 + e2x)
    w2 = g_weight * e2x / (1.0 + e2x)

    hot1 = lane == i1
    hot2 = lane == i2
    hot = jnp.where(hot1, 1.0, jnp.where(hot2, 1.0, 0.0))
    gcnt = jnp.floor((jnp.sum(hot, axis=0, keepdims=True) + (GRAN - 1.0)) * (1.0 / GRAN))
    gc_ref[0] = gcnt
    rr = lax.broadcasted_iota(I32, (LANES, LANES), 0)
    cc = lax.broadcasted_iota(I32, (LANES, LANES), 1)
    upper = jnp.where(rr < cc, 1.0, 0.0).astype(BF16)
    run_start = GRAN * _dot(jnp.broadcast_to(gcnt, (8, LANES)).astype(BF16), upper)[0:1, :]
    rr = lax.broadcasted_iota(I32, (tm, tm), 0)
    cc = lax.broadcasted_iota(I32, (tm, tm), 1)
    lower = jnp.where(cc < rr, 1.0, 0.0).astype(BF16)
    pos = _dot(lower, hot.astype(BF16)) + run_start
    pos1 = jnp.where(hot1, pos, 0.0)
    pos2 = jnp.where(hot2, pos, 0.0)
    lp1 = jnp.sum(pos1, axis=-1, keepdims=True)
    lp2 = jnp.sum(pos2, axis=-1, keepdims=True)
    route_ref[...] = jnp.where(lane == R_LP1, lp1, jnp.where(lane == R_LP2, lp2, 0.0))

    aux = jnp.where(lane == AUX_E1, (i1 - N_GROUPS).astype(F32), 0.0)
    aux = _weight_pieces(_weight_pieces(aux, w1, lane, AUX_W1), w2, lane, AUX_W2)
    lp1r = jnp.sum(pos1.T, axis=0, keepdims=True).astype(I32)
    lp2r = jnp.sum(pos2.T, axis=0, keepdims=True).astype(I32)
    rows = lax.broadcasted_iota(I32, (LOCAL_ROWS, tm), 0)
    sort = jnp.where(rows == lp1r, 1.0, jnp.where(rows == lp2r, 1.0, 0.0)).astype(BF16)
    hsl_ref[...] = _dot(sort, jnp.concatenate([h2b, aux.astype(BF16)], axis=1)).astype(BF16)


def _merge(attn, lru_out, gates, x2, mod3, wpa, wpb, wo, norm_w, wr_hi_lo, b_router, S):
    N, D = x2.shape
    tm = TM_SORT
    n_tiles = N // tm
    steps_per_batch = S // tm
    const = lambda t: (0, 0)
    rows = lambda t: (t, 0)
    return pl.pallas_call(
        _merge_kernel,
        grid=(n_tiles,),
        in_specs=[pl.BlockSpec((tm, ATTN_OUT), rows),
                  pl.BlockSpec((tm, LRU_WIDTH), rows),
                  pl.BlockSpec((tm, 2 * D), rows),
                  pl.BlockSpec((tm, D), rows),
                  pl.BlockSpec((1, 6, D), lambda t: (t // steps_per_batch, 0, 0)),
                  pl.BlockSpec(wpa.shape, const),
                  pl.BlockSpec(wpb.shape, const),
                  pl.BlockSpec(wo.shape, const),
                  pl.BlockSpec((1, D), const),
                  pl.BlockSpec(wr_hi_lo.shape, const),
                  pl.BlockSpec((1, LANES), const)],
        out_specs=[pl.BlockSpec((tm, D), rows),
                   pl.BlockSpec((LOCAL_ROWS, D + AUX_W), rows),
                   pl.BlockSpec((tm, LANES), rows),
                   pl.BlockSpec((1, 1, LANES), lambda t: (t, 0, 0))],
        out_shape=[jax.ShapeDtypeStruct((N, D), F32),
                   jax.ShapeDtypeStruct((n_tiles * LOCAL_ROWS, D + AUX_W), BF16),
                   jax.ShapeDtypeStruct((N, LANES), F32),
                   jax.ShapeDtypeStruct((n_tiles, 1, LANES), F32)],
        compiler_params=_params("arbitrary"),
        name="merge",
    )(attn, lru_out, gates, x2, mod3, wpa, wpb, wo, norm_w.reshape(1, D), wr_hi_lo, b_router)


def _plan_sizes(n_tiles):
    max_gran = n_tiles * LOCAL_GRAN + N_EXPERTS * (GRAN_PER_TILE - 1)
    max_row_tiles = -(-max_gran // GRAN_PER_TILE)
    return max_row_tiles * GRAN_PER_TILE, max_row_tiles


def _plan_kernel(gc_ref, src_ref, texp_ref, nt_ref, dst_ref, local_next):
    n_tiles = gc_ref.shape[0]
    max_gran, max_row_tiles = _plan_sizes(n_tiles)

    def zero(ref, lo, hi):
        def body(k, c):
            ref[k] = 0
            return c
        lax.fori_loop(lo, hi, body, 0)

    zero(local_next, 0, n_tiles)

    def per_expert(e, g):
        first_tile = g // GRAN_PER_TILE

        def per_tile(t, g):
            n = gc_ref[t, N_GROUPS + e]
            base = t * LOCAL_GRAN + local_next[t]

            def per_granule(k, g):
                src_ref[g] = base + k
                dst_ref[base + k] = g
                return g + 1

            g = lax.fori_loop(0, n, per_granule, g)
            local_next[t] = local_next[t] + n
            return g

        g = lax.fori_loop(0, n_tiles, per_tile, g)
        padded = (g + GRAN_PER_TILE - 1) // GRAN_PER_TILE * GRAN_PER_TILE
        zero(src_ref, g, padded)
        g = padded

        def mark(j, c):
            texp_ref[j] = e
            return c

        lax.fori_loop(first_tile, g // GRAN_PER_TILE, mark, 0)
        return g

    g = lax.fori_loop(0, N_EXPERTS, per_expert, 0)
    used = g // GRAN_PER_TILE
    nt_ref[0] = used
    zero(src_ref, g, max_gran)

    def clear_tail(t, c):
        zero(dst_ref, t * LOCAL_GRAN + local_next[t], (t + 1) * LOCAL_GRAN)
        return c

    lax.fori_loop(0, n_tiles, clear_tail, 0)

    def mark_rest(j, c):
        texp_ref[j] = N_EXPERTS - 1
        return c

    lax.fori_loop(used, max_row_tiles, mark_rest, 0)


def _plan(gc):
    n_tiles = gc.shape[0]
    max_gran, max_row_tiles = _plan_sizes(n_tiles)
    smem = pl.BlockSpec(memory_space=pltpu.SMEM)
    return pl.pallas_call(
        _plan_kernel,
        in_specs=[smem],
        out_specs=[smem, smem, smem, smem, smem],
        out_shape=[jax.ShapeDtypeStruct((max_gran,), I32),
                   jax.ShapeDtypeStruct((max_row_tiles,), I32),
                   jax.ShapeDtypeStruct((1,), I32),
                   jax.ShapeDtypeStruct((n_tiles * LOCAL_GRAN,), I32),
                   jax.ShapeDtypeStruct((n_tiles,), I32)],
        name="plan",
    )(gc)


def _granule_copies(idx_ref, first, count, src_hbm, dst_vmem, sem, wait):
    def body(k, c):
        g = idx_ref[first + k]
        cp = pltpu.make_async_copy(src_hbm.at[pl.ds(pl.multiple_of(g * GRAN, GRAN), GRAN)],
                                   dst_vmem.at[pl.ds(pl.multiple_of(k * GRAN, GRAN), GRAN)], sem)
        if wait:
            cp.wait()
        else:
            cp.start()
        return c
    lax.fori_loop(0, count, body, 0)


def _moe_kernel(src_ref, texp_ref, nt_ref, hsl_ref, w1_ref, w3_ref, w2_ref, ys_ref, xbuf, w1b, w3b, w2b, sem):
    i = pl.program_id(0)
    used = nt_ref[0]
    slot = i % MOE_BUFFERS
    D = ys_ref.shape[1]

    def gather(tile, wait):
        slot_ = tile % MOE_BUFFERS
        _granule_copies(src_ref, tile * GRAN_PER_TILE, GRAN_PER_TILE, hsl_ref, xbuf.at[slot_], sem.at[slot_], wait)

    @pl.when(i == 0)
    def _():
        for ahead in range(MOE_BUFFERS - 1):
            pl.when(ahead < used)(functools.partial(gather, ahead, False))

    @pl.when(i + (MOE_BUFFERS - 1) < used)
    def _():
        gather(i + (MOE_BUFFERS - 1), False)

    @pl.when(i < used)
    def _():
        gather(i, True)

        @pl.when((i == 0) | (texp_ref[i] != texp_ref[jnp.maximum(i - 1, 0)]))
        def _():
            w1b[...] = w1_ref[0].astype(BF16)
            w3b[...] = w3_ref[0].astype(BF16)
            w2b[...] = w2_ref[0].astype(BF16)

        xb = xbuf[slot]
        hb = xb[:, :D]
        aux = xb[:, D:].astype(F32)
        first = aux[:, AUX_E1:AUX_E1 + 1] == texp_ref[i].astype(F32)
        w_row = jnp.zeros((TM_MOE, 1), F32)
        for k in range(W_PIECES):
            w_row = w_row + jnp.where(first, aux[:, AUX_W1 + k:AUX_W1 + k + 1], aux[:, AUX_W2 + k:AUX_W2 + k + 1])
        a = _dot(hb, w1b[...])
        act = (a * _sigmoid(a)) * _dot(hb, w3b[...])
        ys_ref[...] = (w_row * _dot(act.astype(BF16), w2b[...])).astype(BF16)

    @pl.when(i >= used)
    def _():
        ys_ref[...] = jnp.zeros_like(ys_ref)


def _moe(src, texp, used, hsl, w1, w3, w2):
    max_row_tiles = texp.shape[0]
    D = w1.shape[1]
    F = w1.shape[2]
    grid_spec = pltpu.PrefetchScalarGridSpec(
        num_scalar_prefetch=3,
        grid=(max_row_tiles,),
        in_specs=[pl.BlockSpec(memory_space=pl.ANY),
                  pl.BlockSpec((1, D, F), lambda i, s, te, nt: (te[i], 0, 0)),
                  pl.BlockSpec((1, D, F), lambda i, s, te, nt: (te[i], 0, 0)),
                  pl.BlockSpec((1, F, D), lambda i, s, te, nt: (te[i], 0, 0))],
        out_specs=pl.BlockSpec((TM_MOE, D), lambda i, s, te, nt: (i, 0)),
        scratch_shapes=[pltpu.VMEM((MOE_BUFFERS, TM_MOE, D + AUX_W), BF16),
                        pltpu.VMEM((D, F), BF16), pltpu.VMEM((D, F), BF16), pltpu.VMEM((F, D), BF16),
                        pltpu.SemaphoreType.DMA((MOE_BUFFERS,))],
    )
    return pl.pallas_call(
        _moe_kernel,
        grid_spec=grid_spec,
        out_shape=jax.ShapeDtypeStruct((max_row_tiles * TM_MOE, D), BF16),
        compiler_params=_params("arbitrary"),
        name="moe",
    )(src, texp, used, hsl, w1, w3, w2)


def _combine_kernel(dst_ref, lcnt_ref, ys_ref, x1_ref, route_ref, mod_ref, o_ref, ybuf, sem):
    t = pl.program_id(0)
    slot = t % 2
    tm = x1_ref.shape[0]

    def gather(tile, slot_, wait):
        _granule_copies(dst_ref, tile * LOCAL_GRAN, lcnt_ref[tile], ys_ref, ybuf.at[slot_], sem.at[slot_], wait)

    @pl.when(t == 0)
    def _():
        ybuf[...] = jnp.zeros_like(ybuf)
        gather(0, 0, False)

    @pl.when(t + 1 < pl.num_programs(0))
    def _():
        gather(t + 1, 1 - slot, False)

    gather(t, slot, True)
    lp1 = route_ref[:, R_LP1:R_LP1 + 1].astype(I32)
    lp2 = route_ref[:, R_LP2:R_LP2 + 1].astype(I32)
    col = lax.broadcasted_iota(I32, (tm, LOCAL_ROWS), 1)
    unsort = jnp.where(col == lp1, 1.0, jnp.where(col == lp2, 1.0, 0.0)).astype(BF16)
    o_ref[...] = x1_ref[...] + mod_ref[0, 5:6, :] * _dot(unsort, ybuf[slot])


def _combine(dst, lcnt, ys, x1, route, mod3, S):
    N, D = x1.shape
    tm = TM_SORT
    steps_per_batch = S // tm
    grid_spec = pltpu.PrefetchScalarGridSpec(
        num_scalar_prefetch=2,
        grid=(N // tm,),
        in_specs=[pl.BlockSpec(memory_space=pl.ANY),
                  pl.BlockSpec((tm, D), lambda t, d, c: (t, 0)),
                  pl.BlockSpec((tm, LANES), lambda t, d, c: (t, 0)),
                  pl.BlockSpec((1, 6, D), lambda t, d, c: (t // steps_per_batch, 0, 0))],
        out_specs=pl.BlockSpec((tm, D), lambda t, d, c: (t, 0)),
        scratch_shapes=[pltpu.VMEM((2, LOCAL_ROWS, D), BF16), pltpu.SemaphoreType.DMA((2,))],
    )
    return pl.pallas_call(
        _combine_kernel,
        grid_spec=grid_spec,
        out_shape=jax.ShapeDtypeStruct((N, D), F32),
        compiler_params=_params("arbitrary"),
        name="combine",
    )(dst, lcnt, ys, x1, route, mod3)


def _layer(x2, mod3, B, S, norm_mix_w, w_in, q_norm_w, k_norm_w, conv_w, conv_b, w_rec_gate, b_rec_gate,
           w_in_gate, b_in_gate, lru_lambda, w_proj_a, w_proj_b, w_out, norm_ffn_w, w_group, b_group,
           w_expert_router, b_expert_router, w1, w3, w2):
    N, D = x2.shape
    nq = N_HEADS_A * HEAD_DIM_A
    o_k = nq
    o_v = o_k + KV_DIM_A
    o_qi = o_v + KV_DIM_A
    o_ki = o_qi + N_IDX_HEADS * IDX_DIM
    o_wi = o_ki + IDX_DIM
    o_lx = o_wi + N_IDX_HEADS
    o_g = o_lx + 2 * LRU_WIDTH
    pad = jnp.zeros((D, KV_W - (WI_OFF + N_IDX_HEADS)), w_in.dtype)
    wa = jnp.concatenate([w_in[:, :nq], w_in[:, o_qi:o_ki], w_in[:, o_k:o_qi], w_in[:, o_ki:o_lx], pad],
                         axis=1).astype(BF16)
    wb = w_in[:, o_lx:o_g].astype(BF16)
    wg = w_in[:, o_g:].astype(BF16)

    qa, kv, lru, gates = _proj(x2, mod3, norm_mix_w, wa, wb, wg, S)
    attn = _attn(qa, kv, q_norm_w, k_norm_w, B, S)
    lru_out = _lru(lru, conv_w, conv_b, _block_diag_tiles(w_rec_gate).astype(BF16), b_rec_gate,
                   _block_diag_tiles(w_in_gate).astype(BF16), b_in_gate, lru_lambda, B, S)

    n_r = N_GROUPS + N_EXPERTS
    w_router = jnp.concatenate([w_group, w_expert_router, jnp.zeros((D, LANES - n_r), F32)], axis=1)
    wr_hi = w_router.astype(BF16)
    wr_lo = (w_router - wr_hi.astype(F32)).astype(BF16)
    b_router = jnp.concatenate([b_group, b_expert_router, jnp.zeros((LANES - n_r,), F32)]).reshape(1, LANES)
    x1, hsl, route, gc = _merge(attn, lru_out, gates, x2, mod3, w_proj_a.astype(BF16), w_proj_b.astype(BF16),
                                w_out.astype(BF16), norm_ffn_w, jnp.concatenate([wr_hi, wr_lo], axis=1),
                                b_router, S)

    src, texp, used, dst, lcnt = _plan(gc.reshape(gc.shape[0], LANES).astype(I32))
    ys = _moe(src, texp, used, hsl, w1, w3, w2)
    return _combine(dst, lcnt, ys, x1, route, mod3, S)


def kernel(x, c, ada_w, ada_b, norm_mix_w, w_in, q_norm_w, k_norm_w, conv_w, conv_b, w_rec_gate, b_rec_gate,
           w_in_gate, b_in_gate, lru_lambda, w_proj_a, w_proj_b, w_out, norm_ffn_w, w_group, b_group,
           w_expert_router, b_expert_router, w1, w3, w2):
    B, S, D = x.shape
    x2 = x.reshape(B * S, D)
    for l in range(ada_w.shape[0]):
        mod3 = _ada(c, ada_w[l], ada_b[l]).reshape(B, 6, D)
        x2 = _layer(x2, mod3, B, S, norm_mix_w[l], w_in[l], q_norm_w[l], k_norm_w[l], conv_w[l], conv_b[l],
                    w_rec_gate[l], b_rec_gate[l], w_in_gate[l], b_in_gate[l], lru_lambda[l], w_proj_a[l],
                    w_proj_b[l], w_out[l], norm_ffn_w[l], w_group[l], b_group[l], w_expert_router[l],
                    b_expert_router[l], w1[l], w3[l], w2[l])
    return x2.reshape(B, S, D)
```

```python
import functools

import jax
import jax.numpy as jnp
import numpy as np
from jax import lax
from jax.experimental import pallas as pl
from jax.experimental.pallas import tpu as pltpu

F32 = jnp.float32
BF16 = jnp.bfloat16
I32 = jnp.int32

EPS = 1e-6
CHUNK = 64
Q_BLOCK = 128
N_HEADS_A = 8
HEAD_DIM_A = 64
KV_DIM_A = 64
ATTN_OUT = N_HEADS_A * KV_DIM_A
N_IDX_HEADS = 4
IDX_DIM = 64
TOPK_MAX = 256
LRU_WIDTH = 512
LRU_BLOCKS = 8
LRU_BLOCK_DIM = LRU_WIDTH // LRU_BLOCKS
CONV_WIDTH = 4
LRU_C = 8.0
N_GROUPS = 4
EXPERTS_PER_GROUP = 8
N_EXPERTS = N_GROUPS * EXPERTS_PER_GROUP
D_FF_EXPERT = 256

LANES = 128
INT_MIN = -2 ** 31
VMEM_LIMIT = 56 * 1024 * 1024

QA_W = N_HEADS_A * HEAD_DIM_A + N_IDX_HEADS * IDX_DIM
KV_W = 256
K_OFF, V_OFF, KI_OFF, WI_OFF = 0, 64, 128, 192

TM_PROJ = 512

TM_SORT = 512
GRAN = 16
TM_MOE = 256
GRAN_PER_TILE = TM_MOE // GRAN
MOE_BUFFERS = 3
LOCAL_ROWS = -(-(2 * TM_SORT + N_EXPERTS * (GRAN - 1)) // LANES) * LANES
LOCAL_GRAN = LOCAL_ROWS // GRAN
SORT_ROWS = 256
AUX_W = LANES
W_PIECES = 3
AUX_W1, AUX_W2 = 0, W_PIECES
AUX_E1 = 2 * W_PIECES

R_LP1, R_LP2 = 0, 1


def _dot(a, b, **kw):
    return jnp.dot(a, b, preferred_element_type=F32, **kw)


def _params(*sem):
    return pltpu.CompilerParams(dimension_semantics=sem, vmem_limit_bytes=VMEM_LIMIT)


def _ada_kernel(c_ref, w_ref, b_ref, o_ref):
    c = c_ref[...]
    cond = c * jax.nn.sigmoid(c)
    o_ref[...] = _dot(cond, w_ref[...], precision=lax.Precision.HIGHEST) + b_ref[...]


def _ada(c, ada_w, ada_b):
    B, D = c.shape
    n_out = ada_w.shape[1]
    tn = 1024
    return pl.pallas_call(
        _ada_kernel,
        grid=(n_out // tn,),
        in_specs=[pl.BlockSpec((B, D), lambda j: (0, 0)),
                  pl.BlockSpec((D, tn), lambda j: (0, j)),
                  pl.BlockSpec((1, tn), lambda j: (0, j))],
        out_specs=pl.BlockSpec((B, tn), lambda j: (0, j)),
        out_shape=jax.ShapeDtypeStruct((B, n_out), F32),
        compiler_params=_params("arbitrary"),
        name="ada",
    )(c, ada_w, ada_b.reshape(1, n_out))


def _rms_mod(x, w, shift, scale):
    ms = jnp.mean(x * x, axis=-1, keepdims=True)
    y = x * lax.rsqrt(ms + EPS) * w
    return y * (1.0 + scale) + shift


def _proj_kernel(x_ref, mod_ref, nw_ref, wa_ref, wb_ref, wg_ref, oqa_ref, okv_ref, olru_ref, og_ref):
    h = _rms_mod(x_ref[...], nw_ref[...], mod_ref[0, 0:1, :], mod_ref[0, 1:2, :])
    hb = h.astype(BF16)
    pa = _dot(hb, wa_ref[...])
    oqa_ref[...] = pa[:, :QA_W]
    okv_ref[...] = pa[:, QA_W:]
    olru_ref[...] = _dot(hb, wb_ref[...])
    og_ref[...] = _sigmoid(_dot(hb, wg_ref[...])).astype(BF16)


def _proj(x2, mod3, norm_w, wa, wb, wg, S):
    N, D = x2.shape
    tm = TM_PROJ
    steps_per_batch = S // tm
    const = lambda t: (0, 0)
    return pl.pallas_call(
        _proj_kernel,
        grid=(N // tm,),
        in_specs=[pl.BlockSpec((tm, D), lambda t: (t, 0)),
                  pl.BlockSpec((1, 6, D), lambda t: (t // steps_per_batch, 0, 0)),
                  pl.BlockSpec((1, D), const),
                  pl.BlockSpec(wa.shape, const),
                  pl.BlockSpec(wb.shape, const),
                  pl.BlockSpec(wg.shape, const)],
        out_specs=[pl.BlockSpec((tm, QA_W), lambda t: (t, 0)),
                   pl.BlockSpec((tm, KV_W), lambda t: (t, 0)),
                   pl.BlockSpec((tm, 2 * LRU_WIDTH), lambda t: (t, 0)),
                   pl.BlockSpec((tm, 2 * D), lambda t: (t, 0))],
        out_shape=[jax.ShapeDtypeStruct((N, QA_W), F32),
                   jax.ShapeDtypeStruct((N, KV_W), F32),
                   jax.ShapeDtypeStruct((N, 2 * LRU_WIDTH), F32),
                   jax.ShapeDtypeStruct((N, 2 * D), BF16)],
        compiler_params=_params("arbitrary"),
        name="proj",
    )(x2, mod3, norm_w.reshape(1, D), wa, wb, wg)


ATTN_BUCKETS = 16
ROW_FOLD = 64
KEY_ROWS = 256
COUNT_CHAINS = 2
LOG2E = float(np.log2(np.e))
POS_RADIX = 256
SLOPE_PIECES = 3
V_ONES = 16


def _bf16_pieces(c, n):
    out = []
    rest = float(c)
    for _ in range(n):
        p = float(np.asarray(rest, np.float32).astype(BF16).astype(np.float32))
        out.append(p)
        rest -= p
    return out


def _reduce_rows(op, x):
    r = x.shape[0]
    if r > ROW_FOLD and r % ROW_FOLD == 0:
        x = op(x.reshape(r // ROW_FOLD, ROW_FOLD, x.shape[1]), axis=0)
    return op(x, axis=0, keepdims=True)


def _attn_block(s_eff, topk, i, qa_ref, kvq_ref, qnw_ref, o_ref, kn_s, vt_s, ki_s, key_s, bias_s, lg_s, ot_s):
    nq = N_HEADS_A * HEAD_DIM_A
    q_t = qa_ref[:, :nq].T
    qi_t = qa_ref[:, nq:].T
    wq_t = kvq_ref[...].T[WI_OFF:WI_OFF + N_IDX_HEADS, :] * (N_IDX_HEADS ** -0.5 * IDX_DIM ** -0.5)

    q_pos = i * Q_BLOCK + lax.broadcasted_iota(I32, (1, Q_BLOCK), 1)
    limit = (q_pos // CHUNK + 1) * CHUNK
    chunks = [slice(r0, min(r0 + KEY_ROWS, s_eff)) for r0 in range(0, s_eff, KEY_ROWS)]

    qi = [qi_t[h * IDX_DIM:(h + 1) * IDX_DIM, :].astype(BF16) for h in range(N_IDX_HEADS)]
    for rows in chunks:
        ki = ki_s[rows, :]
        score = jnp.zeros((rows.stop - rows.start, Q_BLOCK), F32)
        for h in range(N_IDX_HEADS):
            score = score + jnp.maximum(_dot(ki, qi[h]), 0.0) * wq_t[h:h + 1, :]
        k_pos = rows.start + lax.broadcasted_iota(I32, score.shape, 0)
        key_s[rows, :] = jnp.where(k_pos < limit, score, -jnp.inf)

    def count(mask):
        return _reduce_rows(jnp.sum, jnp.where(mask, 1.0, 0.0))

    def count_keys(test):
        acc = [jnp.zeros((ROW_FOLD, Q_BLOCK), F32) for _ in range(COUNT_CHAINS)]
        for n, r0 in enumerate(range(0, s_eff, ROW_FOLD)):
            hit = jnp.where(test(key_s[r0:r0 + ROW_FOLD, :]), 1.0, 0.0)
            acc[n % COUNT_CHAINS] = acc[n % COUNT_CHAINS] + hit
        return jnp.sum(functools.reduce(lambda x, y: x + y, acc), axis=0, keepdims=True)

    def ordered_to_float(c):
        s = c ^ INT_MIN
        return pltpu.bitcast(jnp.where(s < 0, INT_MIN | (-s), s), F32)

    def bisect(it, r):
        c = r | jnp.left_shift(jnp.int32(1), 31 - it)
        cand = ordered_to_float(c)
        return jnp.where(count_keys(lambda slab: slab >= cand) >= topk, c, r)

    def select_by_threshold():
        thr = ordered_to_float(lax.fori_loop(0, 32, bisect, jnp.zeros((1, Q_BLOCK), I32)))

        n_ge = count_keys(lambda slab: slab >= thr)
        clean = jnp.max(jnp.where((n_ge == topk) & (thr > -jnp.inf), 0.0, 1.0)) == 0.0

        @pl.when(clean)
        def _():
            for rows in chunks:
                bias_s[rows, :] = jnp.where(key_s[rows, :] >= thr, 0.0, -jnp.inf)

        @pl.when(jnp.logical_not(clean))
        def _():
            key = key_s[:s_eff, :]
            need = topk - count(key > thr)
            rr = lax.broadcasted_iota(I32, (LANES, LANES), 0)
            cc = lax.broadcasted_iota(I32, (LANES, LANES), 1)
            tri = jnp.where(cc < rr, 1.0, 0.0).astype(BF16)
            seen = jnp.zeros((1, Q_BLOCK), F32)
            for c in range(s_eff // LANES):
                kc = key[c * LANES:(c + 1) * LANES, :]
                eq = kc == thr
                eqf = jnp.where(eq, 1.0, 0.0)
                rank = _dot(tri, eqf.astype(BF16)) + seen
                seen = seen + jnp.sum(eqf, axis=0, keepdims=True)
                tie_bias = jnp.where(jnp.where(eq, rank, topk) < need, 0.0, -jnp.inf)
                bias = jnp.where(kc > thr, 0.0, tie_bias)
                k_pos = c * LANES + lax.broadcasted_iota(I32, (LANES, Q_BLOCK), 0)
                bias_s[c * LANES:(c + 1) * LANES, :] = jnp.where(k_pos < limit, bias, -jnp.inf)

    if s_eff > topk:
        select_by_threshold()
    else:
        for rows in chunks:
            k_pos = rows.start + lax.broadcasted_iota(I32, (rows.stop - rows.start, Q_BLOCK), 0)
            bias_s[rows, :] = jnp.where(k_pos < limit, 0.0, -jnp.inf)

    qnw = qnw_ref[...]
    crow = lax.broadcasted_iota(I32, (LANES - HEAD_DIM_A, Q_BLOCK), 0)
    r_k = lax.broadcasted_iota(I32, (Q_BLOCK, Q_BLOCK), 0)
    c_q = lax.broadcasted_iota(I32, (Q_BLOCK, Q_BLOCK), 1)
    after = 2.0 * jnp.maximum(r_k - c_q, 0).astype(F32)
    band = pl.ds(pl.multiple_of(i * Q_BLOCK, Q_BLOCK), Q_BLOCK)
    slope = [sum(_bf16_pieces(LOG2E * 2.0 ** -(h + 1), SLOPE_PIECES)) for h in range(N_HEADS_A)]
    row_max = []
    for h in range(N_HEADS_A):
        pieces = _bf16_pieces(LOG2E * 2.0 ** -(h + 1), SLOPE_PIECES)
        qh = q_t[h * HEAD_DIM_A:(h + 1) * HEAD_DIM_A, :]
        ms = jnp.mean(qh * qh, axis=0, keepdims=True)
        qn = (qh * lax.rsqrt(ms + EPS) * qnw) * (HEAD_DIM_A ** -0.5) * LOG2E
        slopes = jnp.zeros(crow.shape, F32)
        for k, piece in enumerate(pieces):
            slopes = jnp.where(crow == 2 * k, POS_RADIX * piece, jnp.where(crow == 2 * k + 1, piece, slopes))
        qa = jnp.concatenate([qn, slopes], axis=0).astype(BF16)
        folded = []
        for rows in chunks:
            logits = _dot(kn_s[rows, :], qa) + bias_s[rows, :]
            lg_s[h, rows, :] = logits
            folded.append(jnp.max(logits.reshape(-1, ROW_FOLD, Q_BLOCK), axis=0))
        row_max.append(jnp.max(functools.reduce(jnp.maximum, folded), axis=0, keepdims=True))
    for h in range(N_HEADS_A):
        lg_s[h, band, :] = lg_s[h, band, :] - slope[h] * after
    for h in range(N_HEADS_A):
        acc = jnp.zeros((KV_DIM_A + V_ONES, Q_BLOCK), F32)
        for rows in chunks:
            p = jnp.exp2(lg_s[h, rows, :] - row_max[h])
            acc = acc + _dot(vt_s[:, rows], p.astype(BF16))
        ot_s[h * KV_DIM_A:(h + 1) * KV_DIM_A, :] = acc[:KV_DIM_A, :] / acc[KV_DIM_A:KV_DIM_A + 1, :]
    o_ref[...] = ot_s[...].T


def _attn_kernel(qa_ref, kvq_ref, kv_ref, qnw_ref, knw_ref, o_ref, kn_s, vt_s, ki_s, key_s, bias_s, lg_s, ot_s):
    i = pl.program_id(1)
    S = kv_ref.shape[0]
    nb = S // Q_BLOCK
    topk = float(min(TOPK_MAX, S // 4))

    @pl.when(i == 0)
    def _():
        kv = kv_ref[:, :LANES]
        lane = lax.broadcasted_iota(I32, kv.shape, 1)
        pos = lax.broadcasted_iota(I32, kv.shape, 0)
        ms = jnp.sum(jnp.where(lane < KV_DIM_A, kv * kv, 0.0), axis=-1, keepdims=True) * (1.0 / KV_DIM_A)
        kn = kv * lax.rsqrt(ms + EPS) * knw_ref[...]
        digit = jnp.where((lane & 1) == 0, pos // POS_RADIX, pos % POS_RADIX).astype(F32)
        is_digit = (lane >= KV_DIM_A) & (lane < KV_DIM_A + 2 * SLOPE_PIECES)
        kn_s[...] = jnp.where(is_digit, digit, kn).astype(BF16)
        vt_s[:KV_DIM_A, :] = kv.T[V_OFF:V_OFF + KV_DIM_A, :].astype(BF16)
        vt_s[KV_DIM_A:, :] = jnp.ones((V_ONES, kv.shape[0]), BF16)
        ki_s[...] = kv_ref[:, KI_OFF:KI_OFF + IDX_DIM].astype(BF16)

    n_buckets = min(ATTN_BUCKETS, nb)
    per = nb // n_buckets
    for j in range(n_buckets):
        pl.when(i // per == j)(functools.partial(
            _attn_block, (j + 1) * per * Q_BLOCK, topk, i, qa_ref, kvq_ref, qnw_ref, o_ref,
            kn_s, vt_s, ki_s, key_s, bias_s, lg_s, ot_s))


def _attn(qa, kv, q_norm_w, k_norm_w, B, S):
    N = qa.shape[0]
    nb = S // Q_BLOCK
    return pl.pallas_call(
        _attn_kernel,
        grid=(B, nb),
        in_specs=[pl.BlockSpec((Q_BLOCK, QA_W), lambda b, i: (b * nb + i, 0)),
                  pl.BlockSpec((Q_BLOCK, KV_W), lambda b, i: (b * nb + i, 0)),
                  pl.BlockSpec((S, KV_W), lambda b, i: (b, 0)),
                  pl.BlockSpec((HEAD_DIM_A, 1), lambda b, i: (0, 0)),
                  pl.BlockSpec((1, LANES), lambda b, i: (0, 0))],
        out_specs=pl.BlockSpec((Q_BLOCK, ATTN_OUT), lambda b, i: (b * nb + i, 0)),
        out_shape=jax.ShapeDtypeStruct((N, ATTN_OUT), F32),
        scratch_shapes=[pltpu.VMEM((S, LANES), BF16),
                        pltpu.VMEM((KV_DIM_A + V_ONES, S), BF16),
                        pltpu.VMEM((S, IDX_DIM), BF16),
                        pltpu.VMEM((S, Q_BLOCK), F32),
                        pltpu.VMEM((S, Q_BLOCK), F32),
                        pltpu.VMEM((N_HEADS_A, S, Q_BLOCK), F32),
                        pltpu.VMEM((ATTN_OUT, Q_BLOCK), F32)],
        compiler_params=_params("arbitrary", "arbitrary"),
        name="attn",
    )(qa, kv, kv, q_norm_w.reshape(HEAD_DIM_A, 1),
      jnp.pad(k_norm_w, (0, LANES - KV_DIM_A)).reshape(1, LANES))


SUBLANES = 8


def _shift_rows(x, d, fill, row):
    if d % SUBLANES == 0:
        return jnp.concatenate([jnp.full((d, x.shape[1]), fill, x.dtype), x[:-d]], axis=0)
    return jnp.where(row >= d, pltpu.roll(x, d, 0), fill)


def _sigmoid(x):
    return 0.5 * (1.0 + jnp.tanh(0.5 * x))


def _doubling_scan(a, b, row):
    n = a.shape[0]
    d = 1
    while d < n:
        b = a * _shift_rows(b, d, 0.0, row) + b
        if 2 * d < n:
            a = a * _shift_rows(a, d, 1.0, row)
        d *= 2
    return b


def _lru_kernel(xb_ref, gb_ref, cw_ref, cb_ref, wr_ref, br_ref, wi_ref, bi_ref, lam_ref, o_ref, a_s, b_s):
    x = xb_ref[...]
    S = x.shape[0]
    row = lax.broadcasted_iota(I32, x.shape, 0)
    xc = cb_ref[...] + _shift_rows(x, CONV_WIDTH - 1, 0.0, row) * cw_ref[0:1, :]
    for j in range(1, CONV_WIDTH):
        d = CONV_WIDTH - 1 - j
        xs = x if d == 0 else _shift_rows(x, d, 0.0, row)
        xc = xc + xs * cw_ref[j:j + 1, :]
    xcb = xc.astype(BF16)
    r = _sigmoid(_dot(xcb, wr_ref[0]) + br_ref[...])
    ig = _sigmoid(_dot(xcb, wi_ref[0]) + bi_ref[...])
    z = -lam_ref[...]
    softplus = jnp.maximum(z, 0.0) + jnp.log1p(jnp.exp(-jnp.abs(z)))
    log_a = -LRU_C * r * softplus
    a = jnp.exp(log_a)
    b = jnp.sqrt(-jnp.tanh(log_a) * (1.0 + a * a)) * (ig * xc)
    sub = row & (SUBLANES - 1)
    d = 1
    while d < SUBLANES:
        keep = sub >= d
        b = a * jnp.where(keep, pltpu.roll(b, d, 0), 0.0) + b
        a = a * jnp.where(keep, pltpu.roll(a, d, 0), 1.0)
        d *= 2
    a_s[...] = a
    b_s[...] = b
    groups = S // SUBLANES
    last = pl.ds(SUBLANES - 1, groups, stride=SUBLANES)
    g_row = lax.broadcasted_iota(I32, (groups, x.shape[1]), 0)
    h_last = _doubling_scan(a_s[last, :], b_s[last, :], g_row)
    carry = _shift_rows(h_last, 1, 0.0, g_row)
    carry = jnp.broadcast_to(carry[:, None, :], (groups, SUBLANES, x.shape[1])).reshape(x.shape)
    h = b_s[...] + a_s[...] * carry
    g = gb_ref[...]
    gelu = 0.5 * g * (1.0 + jnp.tanh(float(np.sqrt(2.0 / np.pi)) * (g + 0.044715 * (g * g * g))))
    o_ref[...] = h * gelu


def _lru(lru, conv_w, conv_b, wr_bd, b_rec, wi_bd, b_in, lam, B, S):
    N = lru.shape[0]
    nt = LRU_WIDTH // LANES
    vec = lambda b, j: (0, j)
    return pl.pallas_call(
        _lru_kernel,
        grid=(B, nt),
        in_specs=[pl.BlockSpec((S, LANES), lambda b, j: (b, j)),
                  pl.BlockSpec((S, LANES), lambda b, j: (b, nt + j)),
                  pl.BlockSpec((CONV_WIDTH, LANES), vec),
                  pl.BlockSpec((1, LANES), vec),
                  pl.BlockSpec((1, LANES, LANES), lambda b, j: (j, 0, 0)),
                  pl.BlockSpec((1, LANES), vec),
                  pl.BlockSpec((1, LANES, LANES), lambda b, j: (j, 0, 0)),
                  pl.BlockSpec((1, LANES), vec),
                  pl.BlockSpec((1, LANES), vec)],
        out_specs=pl.BlockSpec((S, LANES), lambda b, j: (b, j)),
        out_shape=jax.ShapeDtypeStruct((N, LRU_WIDTH), F32),
        scratch_shapes=[pltpu.VMEM((S, LANES), F32), pltpu.VMEM((S, LANES), F32)],
        compiler_params=_params("arbitrary", "arbitrary"),
        name="lru",
    )(lru, lru, conv_w, conv_b.reshape(1, -1), wr_bd, b_rec.reshape(1, -1), wi_bd,
      b_in.reshape(1, -1), lam.reshape(1, -1))


def _block_diag_tiles(w):
    per = LANES // LRU_BLOCK_DIM
    nt = LRU_WIDTH // LANES
    w5 = w.reshape(nt, per, LRU_BLOCK_DIM, 1, LRU_BLOCK_DIM)
    eye = jnp.eye(per, dtype=w.dtype).reshape(1, per, 1, per, 1)
    return (w5 * eye).reshape(nt, LANES, LANES)


def _first_index_of_max(vals, lane):
    m = jnp.max(vals, axis=-1, keepdims=True)
    idx = jnp.min(jnp.where(vals == m, lane, 4 * LANES), axis=-1, keepdims=True)
    return m, idx


def _weight_pieces(out, w, lane, first):
    rest = w
    for k in range(W_PIECES):
        piece = rest.astype(BF16).astype(F32)
        out = jnp.where(lane == first + k, piece, out)
        rest = rest - piece
    return out


def _merge_kernel(attn_ref, lru_ref, g_ref, x_ref, mod_ref, wpa_ref, wpb_ref, wo_ref, nw_ref, wr_ref,
                  br_ref, x1_ref, hsl_ref, route_ref, gc_ref):
    D = x_ref.shape[1]
    tm = x_ref.shape[0]

    y_a = _dot(attn_ref[...].astype(BF16), wpa_ref[...])
    y_b = _dot(lru_ref[...].astype(BF16), wpb_ref[...])
    merged = g_ref[:, :D].astype(F32) * y_a + g_ref[:, D:].astype(F32) * y_b
    x1 = x_ref[...] + mod_ref[0, 2:3, :] * _dot(merged.astype(BF16), wo_ref[...])
    x1_ref[...] = x1
    h2 = _rms_mod(x1, nw_ref[...], mod_ref[0, 3:4, :], mod_ref[0, 4:5, :])
    h2b = h2.astype(BF16)

    h2l = (h2 - h2b.astype(F32)).astype(BF16)
    hi_hi_lo = _dot(h2b, wr_ref[...])
    logits = hi_hi_lo[:, :LANES] + (_dot(h2l, wr_ref[:, :LANES]) + hi_hi_lo[:, LANES:]) + br_ref[...]
    lane = lax.broadcasted_iota(I32, logits.shape, 1)
    gl = jnp.where(lane < N_GROUPS, logits, -jnp.inf)
    gmax, g_sel = _first_index_of_max(gl, lane)
    g_weight = 1.0 / jnp.sum(jnp.exp(gl - gmax), axis=-1, keepdims=True)
    e_lo = N_GROUPS + g_sel * EXPERTS_PER_GROUP
    el = jnp.where((lane >= e_lo) & (lane < e_lo + EXPERTS_PER_GROUP), logits, -jnp.inf)
    v1, i1 = _first_index_of_max(el, lane)
    el2 = jnp.where(lane == i1, -jnp.inf, el)
    v2, i2 = _first_index_of_max(el2, lane)
    e2x = jnp.exp(v2 - v1)
    w1 = g_weight / (1.0 + e2x)
    w2 = g_weight * e2x / (1.0 + e2x)

    hot1 = lane == i1
    hot2 = lane == i2
    hot = jnp.where(hot1, 1.0, jnp.where(hot2, 1.0, 0.0))
    gcnt = jnp.floor((jnp.sum(hot, axis=0, keepdims=True) + (GRAN - 1.0)) * (1.0 / GRAN))
    gc_ref[0] = gcnt
    rr = lax.broadcasted_iota(I32, (LANES, LANES), 0)
    cc = lax.broadcasted_iota(I32, (LANES, LANES), 1)
    upper = jnp.where(rr < cc, 1.0, 0.0).astype(BF16)
    run_start = GRAN * _dot(jnp.broadcast_to(gcnt, (8, LANES)).astype(BF16), upper)[0:1, :]
    rr = lax.broadcasted_iota(I32, (tm, tm), 0)
    cc = lax.broadcasted_iota(I32, (tm, tm), 1)
    lower = jnp.where(cc < rr, 1.0, 0.0).astype(BF16)
    pos = _dot(lower, hot.astype(BF16)) + run_start
    pos1 = jnp.where(hot1, pos, 0.0)
    pos2 = jnp.where(hot2, pos, 0.0)
    lp1 = jnp.sum(pos1, axis=-1, keepdims=True)
    lp2 = jnp.sum(pos2, axis=-1, keepdims=True)
    route_ref[...] = jnp.where(lane == R_LP1, lp1, jnp.where(lane == R_LP2, lp2, 0.0))

    aux = jnp.where(lane == AUX_E1, (i1 - N_GROUPS).astype(F32), 0.0)
    aux = _weight_pieces(_weight_pieces(aux, w1, lane, AUX_W1), w2, lane, AUX_W2)
    lp1r = jnp.sum(pos1.T, axis=0, keepdims=True).astype(I32)
    lp2r = jnp.sum(pos2.T, axis=0, keepdims=True).astype(I32)
    payload = jnp.concatenate([h2b, aux.astype(BF16)], axis=1)
    used_rows = (GRAN * jnp.sum(gcnt)).astype(I32)
    for r0 in range(0, LOCAL_ROWS, SORT_ROWS):
        @pl.when(r0 < used_rows)
        def _():
            rows = r0 + lax.broadcasted_iota(I32, (SORT_ROWS, tm), 0)
            sort = jnp.where(rows == lp1r, 1.0, jnp.where(rows == lp2r, 1.0, 0.0)).astype(BF16)
            hsl_ref[r0:r0 + SORT_ROWS, :] = _dot(sort, payload).astype(BF16)

        @pl.when(r0 >= used_rows)
        def _():
            hsl_ref[r0:r0 + SORT_ROWS, :] = jnp.zeros((SORT_ROWS, D + AUX_W), BF16)


def _merge(attn, lru_out, gates, x2, mod3, wpa, wpb, wo, norm_w, wr_hi_lo, b_router, S):
    N, D = x2.shape
    tm = TM_SORT
    n_tiles = N // tm
    steps_per_batch = S // tm
    const = lambda t: (0, 0)
    rows = lambda t: (t, 0)
    return pl.pallas_call(
        _merge_kernel,
        grid=(n_tiles,),
        in_specs=[pl.BlockSpec((tm, ATTN_OUT), rows),
                  pl.BlockSpec((tm, LRU_WIDTH), rows),
                  pl.BlockSpec((tm, 2 * D), rows),
                  pl.BlockSpec((tm, D), rows),
                  pl.BlockSpec((1, 6, D), lambda t: (t // steps_per_batch, 0, 0)),
                  pl.BlockSpec(wpa.shape, const),
                  pl.BlockSpec(wpb.shape, const),
                  pl.BlockSpec(wo.shape, const),
                  pl.BlockSpec((1, D), const),
                  pl.BlockSpec(wr_hi_lo.shape, const),
                  pl.BlockSpec((1, LANES), const)],
        out_specs=[pl.BlockSpec((tm, D), rows),
                   pl.BlockSpec((LOCAL_ROWS, D + AUX_W), rows),
                   pl.BlockSpec((tm, LANES), rows),
                   pl.BlockSpec((1, 1, LANES), lambda t: (t, 0, 0))],
        out_shape=[jax.ShapeDtypeStruct((N, D), F32),
                   jax.ShapeDtypeStruct((n_tiles * LOCAL_ROWS, D + AUX_W), BF16),
                   jax.ShapeDtypeStruct((N, LANES), F32),
                   jax.ShapeDtypeStruct((n_tiles, 1, LANES), F32)],
        compiler_params=_params("arbitrary"),
        name="merge",
    )(attn, lru_out, gates, x2, mod3, wpa, wpb, wo, norm_w.reshape(1, D), wr_hi_lo, b_router)


def _plan_sizes(n_tiles):
    max_gran = n_tiles * LOCAL_GRAN + N_EXPERTS * (GRAN_PER_TILE - 1)
    max_row_tiles = -(-max_gran // GRAN_PER_TILE)
    return max_row_tiles * GRAN_PER_TILE, max_row_tiles


def _plan_kernel(gc_ref, src_ref, texp_ref, nt_ref, dst_ref, local_next):
    n_tiles = gc_ref.shape[0]
    max_gran, max_row_tiles = _plan_sizes(n_tiles)

    def zero(ref, lo, hi):
        def body(k, c):
            ref[k] = 0
            return c
        lax.fori_loop(lo, hi, body, 0)

    zero(local_next, 0, n_tiles)

    def per_expert(e, g):
        first_tile = g // GRAN_PER_TILE

        def per_tile(t, g):
            n = gc_ref[t, N_GROUPS + e]
            base = t * LOCAL_GRAN + local_next[t]

            def per_granule(k, g):
                src_ref[g] = base + k
                dst_ref[base + k] = g
                return g + 1

            g = lax.fori_loop(0, n, per_granule, g)
            local_next[t] = local_next[t] + n
            return g

        g = lax.fori_loop(0, n_tiles, per_tile, g)
        padded = (g + GRAN_PER_TILE - 1) // GRAN_PER_TILE * GRAN_PER_TILE
        zero(src_ref, g, padded)
        g = padded

        def mark(j, c):
            texp_ref[j] = e
            return c

        lax.fori_loop(first_tile, g // GRAN_PER_TILE, mark, 0)
        return g

    g = lax.fori_loop(0, N_EXPERTS, per_expert, 0)
    used = g // GRAN_PER_TILE
    nt_ref[0] = used
    zero(src_ref, g, max_gran)

    def clear_tail(t, c):
        zero(dst_ref, t * LOCAL_GRAN + local_next[t], (t + 1) * LOCAL_GRAN)
        return c

    lax.fori_loop(0, n_tiles, clear_tail, 0)

    def mark_rest(j, c):
        texp_ref[j] = N_EXPERTS - 1
        return c

    lax.fori_loop(used, max_row_tiles, mark_rest, 0)


def _plan(gc):
    n_tiles = gc.shape[0]
    max_gran, max_row_tiles = _plan_sizes(n_tiles)
    smem = pl.BlockSpec(memory_space=pltpu.SMEM)
    return pl.pallas_call(
        _plan_kernel,
        in_specs=[smem],
        out_specs=[smem, smem, smem, smem, smem],
        out_shape=[jax.ShapeDtypeStruct((max_gran,), I32),
                   jax.ShapeDtypeStruct((max_row_tiles,), I32),
                   jax.ShapeDtypeStruct((1,), I32),
                   jax.ShapeDtypeStruct((n_tiles * LOCAL_GRAN,), I32),
                   jax.ShapeDtypeStruct((n_tiles,), I32)],
        name="plan",
    )(gc)


def _granule_copies(idx_ref, first, count, src_hbm, dst_vmem, sem, wait):
    def body(k, c):
        g = idx_ref[first + k]
        cp = pltpu.make_async_copy(src_hbm.at[pl.ds(pl.multiple_of(g * GRAN, GRAN), GRAN)],
                                   dst_vmem.at[pl.ds(pl.multiple_of(k * GRAN, GRAN), GRAN)], sem)
        if wait:
            cp.wait()
        else:
            cp.start()
        return c
    lax.fori_loop(0, count, body, 0)


def _moe_kernel(src_ref, texp_ref, nt_ref, hsl_ref, w1_ref, w3_ref, w2_ref, ys_ref, xbuf, w1b, w3b, w2b, sem):
    i = pl.program_id(0)
    used = nt_ref[0]
    slot = i % MOE_BUFFERS
    D = ys_ref.shape[1]

    def gather(tile, wait):
        slot_ = tile % MOE_BUFFERS
        _granule_copies(src_ref, tile * GRAN_PER_TILE, GRAN_PER_TILE, hsl_ref, xbuf.at[slot_], sem.at[slot_], wait)

    @pl.when(i == 0)
    def _():
        for ahead in range(MOE_BUFFERS - 1):
            pl.when(ahead < used)(functools.partial(gather, ahead, False))

    @pl.when(i + (MOE_BUFFERS - 1) < used)
    def _():
        gather(i + (MOE_BUFFERS - 1), False)

    @pl.when(i < used)
    def _():
        gather(i, True)

        @pl.when((i == 0) | (texp_ref[i] != texp_ref[jnp.maximum(i - 1, 0)]))
        def _():
            w1b[...] = w1_ref[0].astype(BF16)
            w3b[...] = w3_ref[0].astype(BF16)
            w2b[...] = w2_ref[0].astype(BF16)

        xb = xbuf[slot]
        hb = xb[:, :D]
        aux = xb[:, D:].astype(F32)
        first = aux[:, AUX_E1:AUX_E1 + 1] == texp_ref[i].astype(F32)
        w_row = jnp.zeros((TM_MOE, 1), F32)
        for k in range(W_PIECES):
            w_row = w_row + jnp.where(first, aux[:, AUX_W1 + k:AUX_W1 + k + 1], aux[:, AUX_W2 + k:AUX_W2 + k + 1])
        a = _dot(hb, w1b[...])
        act = (a * _sigmoid(a)) * _dot(hb, w3b[...])
        ys_ref[...] = (w_row * _dot(act.astype(BF16), w2b[...])).astype(BF16)

    @pl.when(i >= used)
    def _():
        ys_ref[...] = jnp.zeros_like(ys_ref)


def _moe(src, texp, used, hsl, w1, w3, w2):
    max_row_tiles = texp.shape[0]
    D = w1.shape[1]
    F = w1.shape[2]
    grid_spec = pltpu.PrefetchScalarGridSpec(
        num_scalar_prefetch=3,
        grid=(max_row_tiles,),
        in_specs=[pl.BlockSpec(memory_space=pl.ANY),
                  pl.BlockSpec((1, D, F), lambda i, s, te, nt: (te[i], 0, 0)),
                  pl.BlockSpec((1, D, F), lambda i, s, te, nt: (te[i], 0, 0)),
                  pl.BlockSpec((1, F, D), lambda i, s, te, nt: (te[i], 0, 0))],
        out_specs=pl.BlockSpec((TM_MOE, D), lambda i, s, te, nt: (i, 0)),
        scratch_shapes=[pltpu.VMEM((MOE_BUFFERS, TM_MOE, D + AUX_W), BF16),
                        pltpu.VMEM((D, F), BF16), pltpu.VMEM((D, F), BF16), pltpu.VMEM((F, D), BF16),
                        pltpu.SemaphoreType.DMA((MOE_BUFFERS,))],
    )
    return pl.pallas_call(
        _moe_kernel,
        grid_spec=grid_spec,
        out_shape=jax.ShapeDtypeStruct((max_row_tiles * TM_MOE, D), BF16),
        compiler_params=_params("arbitrary"),
        name="moe",
    )(src, texp, used, hsl, w1, w3, w2)


def _combine_kernel(dst_ref, lcnt_ref, ys_ref, x1_ref, route_ref, mod_ref, o_ref, ybuf, acc_s, sem):
    t = pl.program_id(0)
    slot = t % 2
    tm = x1_ref.shape[0]

    def gather(tile, slot_, wait):
        _granule_copies(dst_ref, tile * LOCAL_GRAN, lcnt_ref[tile], ys_ref, ybuf.at[slot_], sem.at[slot_], wait)

    @pl.when(t == 0)
    def _():
        ybuf[...] = jnp.zeros_like(ybuf)
        gather(0, 0, False)

    @pl.when(t + 1 < pl.num_programs(0))
    def _():
        gather(t + 1, 1 - slot, False)

    gather(t, slot, True)
    lp1 = route_ref[:, R_LP1:R_LP1 + 1].astype(I32)
    lp2 = route_ref[:, R_LP2:R_LP2 + 1].astype(I32)
    used_rows = lcnt_ref[t] * GRAN
    acc_s[...] = jnp.zeros_like(acc_s)
    for r0 in range(0, LOCAL_ROWS, SORT_ROWS):
        @pl.when(r0 < used_rows)
        def _():
            col = r0 + lax.broadcasted_iota(I32, (tm, SORT_ROWS), 1)
            unsort = jnp.where(col == lp1, 1.0, jnp.where(col == lp2, 1.0, 0.0)).astype(BF16)
            acc_s[...] += _dot(unsort, ybuf[slot, r0:r0 + SORT_ROWS, :])
    o_ref[...] = x1_ref[...] + mod_ref[0, 5:6, :] * acc_s[...]


def _combine(dst, lcnt, ys, x1, route, mod3, S):
    N, D = x1.shape
    tm = TM_SORT
    steps_per_batch = S // tm
    grid_spec = pltpu.PrefetchScalarGridSpec(
        num_scalar_prefetch=2,
        grid=(N // tm,),
        in_specs=[pl.BlockSpec(memory_space=pl.ANY),
                  pl.BlockSpec((tm, D), lambda t, d, c: (t, 0)),
                  pl.BlockSpec((tm, LANES), lambda t, d, c: (t, 0)),
                  pl.BlockSpec((1, 6, D), lambda t, d, c: (t // steps_per_batch, 0, 0))],
        out_specs=pl.BlockSpec((tm, D), lambda t, d, c: (t, 0)),
        scratch_shapes=[pltpu.VMEM((2, LOCAL_ROWS, D), BF16), pltpu.VMEM((tm, D), F32),
                        pltpu.SemaphoreType.DMA((2,))],
    )
    return pl.pallas_call(
        _combine_kernel,
        grid_spec=grid_spec,
        out_shape=jax.ShapeDtypeStruct((N, D), F32),
        compiler_params=_params("arbitrary"),
        name="combine",
    )(dst, lcnt, ys, x1, route, mod3)


def _layer(x2, mod3, B, S, norm_mix_w, w_in, q_norm_w, k_norm_w, conv_w, conv_b, w_rec_gate, b_rec_gate,
           w_in_gate, b_in_gate, lru_lambda, w_proj_a, w_proj_b, w_out, norm_ffn_w, w_group, b_group,
           w_expert_router, b_expert_router, w1, w3, w2):
    N, D = x2.shape
    nq = N_HEADS_A * HEAD_DIM_A
    o_k = nq
    o_v = o_k + KV_DIM_A
    o_qi = o_v + KV_DIM_A
    o_ki = o_qi + N_IDX_HEADS * IDX_DIM
    o_wi = o_ki + IDX_DIM
    o_lx = o_wi + N_IDX_HEADS
    o_g = o_lx + 2 * LRU_WIDTH
    pad = jnp.zeros((D, KV_W - (WI_OFF + N_IDX_HEADS)), w_in.dtype)
    wa = jnp.concatenate([w_in[:, :nq], w_in[:, o_qi:o_ki], w_in[:, o_k:o_qi], w_in[:, o_ki:o_lx], pad],
                         axis=1).astype(BF16)
    wb = w_in[:, o_lx:o_g].astype(BF16)
    wg = w_in[:, o_g:].astype(BF16)

    qa, kv, lru, gates = _proj(x2, mod3, norm_mix_w, wa, wb, wg, S)
    attn = _attn(qa, kv, q_norm_w, k_norm_w, B, S)
    lru_out = _lru(lru, conv_w, conv_b, _block_diag_tiles(w_rec_gate).astype(BF16), b_rec_gate,
                   _block_diag_tiles(w_in_gate).astype(BF16), b_in_gate, lru_lambda, B, S)

    n_r = N_GROUPS + N_EXPERTS
    w_router = jnp.concatenate([w_group, w_expert_router, jnp.zeros((D, LANES - n_r), F32)], axis=1)
    wr_hi = w_router.astype(BF16)
    wr_lo = (w_router - wr_hi.astype(F32)).astype(BF16)
    b_router = jnp.concatenate([b_group, b_expert_router, jnp.zeros((LANES - n_r,), F32)]).reshape(1, LANES)
    x1, hsl, route, gc = _merge(attn, lru_out, gates, x2, mod3, w_proj_a.astype(BF16), w_proj_b.astype(BF16),
                                w_out.astype(BF16), norm_ffn_w, jnp.concatenate([wr_hi, wr_lo], axis=1),
                                b_router, S)

    src, texp, used, dst, lcnt = _plan(gc.reshape(gc.shape[0], LANES).astype(I32))
    ys = _moe(src, texp, used, hsl, w1, w3, w2)
    return _combine(dst, lcnt, ys, x1, route, mod3, S)


def kernel(x, c, ada_w, ada_b, norm_mix_w, w_in, q_norm_w, k_norm_w, conv_w, conv_b, w_rec_gate, b_rec_gate,
           w_in_gate, b_in_gate, lru_lambda, w_proj_a, w_proj_b, w_out, norm_ffn_w, w_group, b_group,
           w_expert_router, b_expert_router, w1, w3, w2):
    B, S, D = x.shape
    x2 = x.reshape(B * S, D)
    for l in range(ada_w.shape[0]):
        mod3 = _ada(c, ada_w[l], ada_b[l]).reshape(B, 6, D)
        x2 = _layer(x2, mod3, B, S, norm_mix_w[l], w_in[l], q_norm_w[l], k_norm_w[l], conv_w[l], conv_b[l],
                    w_rec_gate[l], b_rec_gate[l], w_in_gate[l], b_in_gate[l], lru_lambda[l], w_proj_a[l],
                    w_proj_b[l], w_out[l], norm_ffn_w[l], w_group[l], b_group[l], w_expert_router[l],
                    b_expert_router[l], w1[l], w3[l], w2[l])
    return x2.reshape(B, S, D)
```

```python
import functools

import jax
import jax.numpy as jnp
import numpy as np
from jax import lax
from jax.experimental import pallas as pl
from jax.experimental.pallas import tpu as pltpu

F32 = jnp.float32
BF16 = jnp.bfloat16
I32 = jnp.int32

EPS = 1e-6
CHUNK = 64
Q_BLOCK = 128
N_HEADS_A = 8
HEAD_DIM_A = 64
KV_DIM_A = 64
ATTN_OUT = N_HEADS_A * KV_DIM_A
N_IDX_HEADS = 4
IDX_DIM = 64
TOPK_MAX = 256
LRU_WIDTH = 512
LRU_BLOCKS = 8
LRU_BLOCK_DIM = LRU_WIDTH // LRU_BLOCKS
CONV_WIDTH = 4
LRU_C = 8.0
N_GROUPS = 4
EXPERTS_PER_GROUP = 8
N_EXPERTS = N_GROUPS * EXPERTS_PER_GROUP
D_FF_EXPERT = 256

LANES = 128
INT_MIN = -2 ** 31
VMEM_LIMIT = 56 * 1024 * 1024

QA_W = N_HEADS_A * HEAD_DIM_A + N_IDX_HEADS * IDX_DIM
KV_W = 256
K_OFF, V_OFF, KI_OFF, WI_OFF = 0, 64, 128, 192

TM_PROJ = 512

TM_SORT = 512
GRAN = 16
TM_MOE = 256
GRAN_PER_TILE = TM_MOE // GRAN
MOE_BUFFERS = 3
LOCAL_ROWS = -(-(2 * TM_SORT + N_EXPERTS * (GRAN - 1)) // LANES) * LANES
LOCAL_GRAN = LOCAL_ROWS // GRAN
AUX_W = LANES
W_PIECES = 3
AUX_W1, AUX_W2 = 0, W_PIECES
AUX_E1 = 2 * W_PIECES

R_LP1, R_LP2 = 0, 1


def _dot(a, b, **kw):
    return jnp.dot(a, b, preferred_element_type=F32, **kw)


def _params(*sem):
    return pltpu.CompilerParams(dimension_semantics=sem, vmem_limit_bytes=VMEM_LIMIT)


def _ada_kernel(c_ref, w_ref, b_ref, o_ref):
    c = c_ref[...]
    cond = c * jax.nn.sigmoid(c)
    o_ref[...] = _dot(cond, w_ref[...], precision=lax.Precision.HIGHEST) + b_ref[...]


def _ada(c, ada_w, ada_b):
    B, D = c.shape
    n_out = ada_w.shape[1]
    tn = 1024
    return pl.pallas_call(
        _ada_kernel,
        grid=(n_out // tn,),
        in_specs=[pl.BlockSpec((B, D), lambda j: (0, 0)),
                  pl.BlockSpec((D, tn), lambda j: (0, j)),
                  pl.BlockSpec((1, tn), lambda j: (0, j))],
        out_specs=pl.BlockSpec((B, tn), lambda j: (0, j)),
        out_shape=jax.ShapeDtypeStruct((B, n_out), F32),
        compiler_params=_params("arbitrary"),
        name="ada",
    )(c, ada_w, ada_b.reshape(1, n_out))


def _rms_mod(x, w, shift, scale):
    ms = jnp.mean(x * x, axis=-1, keepdims=True)
    y = x * lax.rsqrt(ms + EPS) * w
    return y * (1.0 + scale) + shift


def _proj_kernel(x_ref, mod_ref, nw_ref, wa_ref, wb_ref, wg_ref, oqa_ref, okv_ref, olru_ref, og_ref):
    h = _rms_mod(x_ref[...], nw_ref[...], mod_ref[0, 0:1, :], mod_ref[0, 1:2, :])
    hb = h.astype(BF16)
    pa = _dot(hb, wa_ref[...])
    oqa_ref[...] = pa[:, :QA_W]
    okv_ref[...] = pa[:, QA_W:]
    olru_ref[...] = _dot(hb, wb_ref[...])
    og_ref[...] = _sigmoid(_dot(hb, wg_ref[...])).astype(BF16)


def _proj(x2, mod3, norm_w, wa, wb, wg, S):
    N, D = x2.shape
    tm = TM_PROJ
    steps_per_batch = S // tm
    const = lambda t: (0, 0)
    return pl.pallas_call(
        _proj_kernel,
        grid=(N // tm,),
        in_specs=[pl.BlockSpec((tm, D), lambda t: (t, 0)),
                  pl.BlockSpec((1, 6, D), lambda t: (t // steps_per_batch, 0, 0)),
                  pl.BlockSpec((1, D), const),
                  pl.BlockSpec(wa.shape, const),
                  pl.BlockSpec(wb.shape, const),
                  pl.BlockSpec(wg.shape, const)],
        out_specs=[pl.BlockSpec((tm, QA_W), lambda t: (t, 0)),
                   pl.BlockSpec((tm, KV_W), lambda t: (t, 0)),
                   pl.BlockSpec((tm, 2 * LRU_WIDTH), lambda t: (t, 0)),
                   pl.BlockSpec((tm, 2 * D), lambda t: (t, 0))],
        out_shape=[jax.ShapeDtypeStruct((N, QA_W), F32),
                   jax.ShapeDtypeStruct((N, KV_W), F32),
                   jax.ShapeDtypeStruct((N, 2 * LRU_WIDTH), F32),
                   jax.ShapeDtypeStruct((N, 2 * D), BF16)],
        compiler_params=_params("arbitrary"),
        name="proj",
    )(x2, mod3, norm_w.reshape(1, D), wa, wb, wg)


ATTN_BUCKETS = 8
ROW_FOLD = 64
KEY_ROWS = 256
COUNT_CHAINS = 2
BRACKET_MARGIN = 4
LOG2E = float(np.log2(np.e))
POS_RADIX = 256
SLOPE_PIECES = 3
V_ONES = 16


def _bf16_pieces(c, n):
    out = []
    rest = float(c)
    for _ in range(n):
        p = float(np.asarray(rest, np.float32).astype(BF16).astype(np.float32))
        out.append(p)
        rest -= p
    return out


def _reduce_rows(op, x):
    r = x.shape[0]
    if r > ROW_FOLD and r % ROW_FOLD == 0:
        x = op(x.reshape(r // ROW_FOLD, ROW_FOLD, x.shape[1]), axis=0)
    return op(x, axis=0, keepdims=True)


def _attn_block(s_eff, topk, i, qa_ref, kvq_ref, qnw_ref, o_ref, kn_s, vt_s, ki_s, key_s, bias_s, lg_s, ot_s):
    nq = N_HEADS_A * HEAD_DIM_A
    q_t = qa_ref[:, :nq].T
    qi_t = qa_ref[:, nq:].T
    wq_t = kvq_ref[...].T[WI_OFF:WI_OFF + N_IDX_HEADS, :] * (N_IDX_HEADS ** -0.5 * IDX_DIM ** -0.5)

    q_pos = i * Q_BLOCK + lax.broadcasted_iota(I32, (1, Q_BLOCK), 1)
    limit = (q_pos // CHUNK + 1) * CHUNK
    chunks = [slice(r0, min(r0 + KEY_ROWS, s_eff)) for r0 in range(0, s_eff, KEY_ROWS)]

    qi = [qi_t[h * IDX_DIM:(h + 1) * IDX_DIM, :].astype(BF16) for h in range(N_IDX_HEADS)]
    for rows in chunks:
        ki = ki_s[rows, :]
        score = jnp.zeros((rows.stop - rows.start, Q_BLOCK), F32)
        for h in range(N_IDX_HEADS):
            score = score + jnp.maximum(_dot(ki, qi[h]), 0.0) * wq_t[h:h + 1, :]
        k_pos = rows.start + lax.broadcasted_iota(I32, score.shape, 0)
        key_s[rows, :] = jnp.where(k_pos < limit, score, -jnp.inf)

    def count(mask):
        return _reduce_rows(jnp.sum, jnp.where(mask, 1.0, 0.0))

    def count_keys(test):
        acc = [jnp.zeros((ROW_FOLD, Q_BLOCK), F32) for _ in range(COUNT_CHAINS)]
        for n, r0 in enumerate(range(0, s_eff, ROW_FOLD)):
            hit = jnp.where(test(key_s[r0:r0 + ROW_FOLD, :]), 1.0, 0.0)
            acc[n % COUNT_CHAINS] = acc[n % COUNT_CHAINS] + hit
        return jnp.sum(functools.reduce(lambda x, y: x + y, acc), axis=0, keepdims=True)

    def ordered_to_float(c):
        s = c ^ INT_MIN
        return pltpu.bitcast(jnp.where(s < 0, INT_MIN | (-s), s), F32)

    def float_to_ordered(x):
        b = pltpu.bitcast(x, I32)
        return jnp.where(b < 0, -(b & 0x7FFFFFFF), b) ^ INT_MIN

    def bisect(it, r):
        c = r | jnp.left_shift(jnp.int32(1), 31 - it)
        cand = ordered_to_float(c)
        return jnp.where(count_keys(lambda slab: slab >= cand) >= topk, c, r)

    def bracketed_threshold():
        k = int(topk)
        gmax = key_s[0:k, :]
        for r0 in range(k, s_eff, k):
            gmax = jnp.maximum(gmax, key_s[r0:r0 + k, :])
        lo = float_to_ordered(_reduce_rows(jnp.min, gmax)) - BRACKET_MARGIN
        hi = float_to_ordered(_reduce_rows(jnp.max, gmax)) + BRACKET_MARGIN
        passes = jnp.max(32 - lax.clz(hi - lo))

        def halve(_, lo_hi):
            lo, hi = lo_hi
            mid = lo + lax.shift_right_logical(hi - lo + 1, 1)
            cand = ordered_to_float(mid)
            ok = count_keys(lambda slab: slab >= cand) >= topk
            return jnp.where(ok, mid, lo), jnp.where(ok, hi, mid - 1)

        return ordered_to_float(lax.fori_loop(0, passes, halve, (lo, hi))[0])

    def select_by_threshold():
        if s_eff % int(topk) == 0:
            thr = bracketed_threshold()
        else:
            thr = ordered_to_float(lax.fori_loop(0, 32, bisect, jnp.zeros((1, Q_BLOCK), I32)))

        n_ge = count_keys(lambda slab: slab >= thr)
        clean = jnp.max(jnp.where((n_ge == topk) & (thr > -jnp.inf), 0.0, 1.0)) == 0.0

        @pl.when(clean)
        def _():
            for rows in chunks:
                bias_s[rows, :] = jnp.where(key_s[rows, :] >= thr, 0.0, -jnp.inf)

        @pl.when(jnp.logical_not(clean))
        def _():
            key = key_s[:s_eff, :]
            need = topk - count(key > thr)
            rr = lax.broadcasted_iota(I32, (LANES, LANES), 0)
            cc = lax.broadcasted_iota(I32, (LANES, LANES), 1)
            tri = jnp.where(cc < rr, 1.0, 0.0).astype(BF16)
            seen = jnp.zeros((1, Q_BLOCK), F32)
            for c in range(s_eff // LANES):
                kc = key[c * LANES:(c + 1) * LANES, :]
                eq = kc == thr
                eqf = jnp.where(eq, 1.0, 0.0)
                rank = _dot(tri, eqf.astype(BF16)) + seen
                seen = seen + jnp.sum(eqf, axis=0, keepdims=True)
                tie_bias = jnp.where(jnp.where(eq, rank, topk) < need, 0.0, -jnp.inf)
                bias = jnp.where(kc > thr, 0.0, tie_bias)
                k_pos = c * LANES + lax.broadcasted_iota(I32, (LANES, Q_BLOCK), 0)
                bias_s[c * LANES:(c + 1) * LANES, :] = jnp.where(k_pos < limit, bias, -jnp.inf)

    if s_eff > topk:
        select_by_threshold()
    else:
        for rows in chunks:
            k_pos = rows.start + lax.broadcasted_iota(I32, (rows.stop - rows.start, Q_BLOCK), 0)
            bias_s[rows, :] = jnp.where(k_pos < limit, 0.0, -jnp.inf)

    qnw = qnw_ref[...]
    crow = lax.broadcasted_iota(I32, (LANES - HEAD_DIM_A, Q_BLOCK), 0)
    r_k = lax.broadcasted_iota(I32, (Q_BLOCK, Q_BLOCK), 0)
    c_q = lax.broadcasted_iota(I32, (Q_BLOCK, Q_BLOCK), 1)
    after = 2.0 * jnp.maximum(r_k - c_q, 0).astype(F32)
    band = pl.ds(pl.multiple_of(i * Q_BLOCK, Q_BLOCK), Q_BLOCK)
    slope = [sum(_bf16_pieces(LOG2E * 2.0 ** -(h + 1), SLOPE_PIECES)) for h in range(N_HEADS_A)]
    row_max = []
    for h in range(N_HEADS_A):
        pieces = _bf16_pieces(LOG2E * 2.0 ** -(h + 1), SLOPE_PIECES)
        qh = q_t[h * HEAD_DIM_A:(h + 1) * HEAD_DIM_A, :]
        ms = jnp.mean(qh * qh, axis=0, keepdims=True)
        qn = (qh * lax.rsqrt(ms + EPS) * qnw) * (HEAD_DIM_A ** -0.5) * LOG2E
        slopes = jnp.zeros(crow.shape, F32)
        for k, piece in enumerate(pieces):
            slopes = jnp.where(crow == 2 * k, POS_RADIX * piece, jnp.where(crow == 2 * k + 1, piece, slopes))
        qa = jnp.concatenate([qn, slopes], axis=0).astype(BF16)
        folded = []
        for rows in chunks:
            logits = _dot(kn_s[rows, :], qa) + bias_s[rows, :]
            lg_s[h, rows, :] = logits
            folded.append(jnp.max(logits.reshape(-1, ROW_FOLD, Q_BLOCK), axis=0))
        row_max.append(jnp.max(functools.reduce(jnp.maximum, folded), axis=0, keepdims=True))
    for h in range(N_HEADS_A):
        lg_s[h, band, :] = lg_s[h, band, :] - slope[h] * after
    for h in range(N_HEADS_A):
        acc = jnp.zeros((KV_DIM_A + V_ONES, Q_BLOCK), F32)
        for rows in chunks:
            p = jnp.exp2(lg_s[h, rows, :] - row_max[h])
            acc = acc + _dot(vt_s[:, rows], p.astype(BF16))
        ot_s[h * KV_DIM_A:(h + 1) * KV_DIM_A, :] = acc[:KV_DIM_A, :] / acc[KV_DIM_A:KV_DIM_A + 1, :]
    o_ref[...] = ot_s[...].T


def _attn_kernel(qa_ref, kvq_ref, kv_ref, qnw_ref, knw_ref, o_ref, kn_s, vt_s, ki_s, key_s, bias_s, lg_s, ot_s):
    i = pl.program_id(1)
    S = kv_ref.shape[0]
    nb = S // Q_BLOCK
    topk = float(min(TOPK_MAX, S // 4))

    @pl.when(i == 0)
    def _():
        kv = kv_ref[:, :LANES]
        lane = lax.broadcasted_iota(I32, kv.shape, 1)
        pos = lax.broadcasted_iota(I32, kv.shape, 0)
        ms = jnp.sum(jnp.where(lane < KV_DIM_A, kv * kv, 0.0), axis=-1, keepdims=True) * (1.0 / KV_DIM_A)
        kn = kv * lax.rsqrt(ms + EPS) * knw_ref[...]
        digit = jnp.where((lane & 1) == 0, pos // POS_RADIX, pos % POS_RADIX).astype(F32)
        is_digit = (lane >= KV_DIM_A) & (lane < KV_DIM_A + 2 * SLOPE_PIECES)
        kn_s[...] = jnp.where(is_digit, digit, kn).astype(BF16)
        vt_s[:KV_DIM_A, :] = kv.T[V_OFF:V_OFF + KV_DIM_A, :].astype(BF16)
        vt_s[KV_DIM_A:, :] = jnp.ones((V_ONES, kv.shape[0]), BF16)
        ki_s[...] = kv_ref[:, KI_OFF:KI_OFF + IDX_DIM].astype(BF16)

    n_buckets = min(ATTN_BUCKETS, nb)
    per = nb // n_buckets
    for j in range(n_buckets):
        pl.when(i // per == j)(functools.partial(
            _attn_block, (j + 1) * per * Q_BLOCK, topk, i, qa_ref, kvq_ref, qnw_ref, o_ref,
            kn_s, vt_s, ki_s, key_s, bias_s, lg_s, ot_s))


def _attn(qa, kv, q_norm_w, k_norm_w, B, S):
    N = qa.shape[0]
    nb = S // Q_BLOCK
    return pl.pallas_call(
        _attn_kernel,
        grid=(B, nb),
        in_specs=[pl.BlockSpec((Q_BLOCK, QA_W), lambda b, i: (b * nb + i, 0)),
                  pl.BlockSpec((Q_BLOCK, KV_W), lambda b, i: (b * nb + i, 0)),
                  pl.BlockSpec((S, KV_W), lambda b, i: (b, 0)),
                  pl.BlockSpec((HEAD_DIM_A, 1), lambda b, i: (0, 0)),
                  pl.BlockSpec((1, LANES), lambda b, i: (0, 0))],
        out_specs=pl.BlockSpec((Q_BLOCK, ATTN_OUT), lambda b, i: (b * nb + i, 0)),
        out_shape=jax.ShapeDtypeStruct((N, ATTN_OUT), F32),
        scratch_shapes=[pltpu.VMEM((S, LANES), BF16),
                        pltpu.VMEM((KV_DIM_A + V_ONES, S), BF16),
                        pltpu.VMEM((S, IDX_DIM), BF16),
                        pltpu.VMEM((S, Q_BLOCK), F32),
                        pltpu.VMEM((S, Q_BLOCK), F32),
                        pltpu.VMEM((N_HEADS_A, S, Q_BLOCK), F32),
                        pltpu.VMEM((ATTN_OUT, Q_BLOCK), F32)],
        compiler_params=_params("arbitrary", "arbitrary"),
        name="attn",
    )(qa, kv, kv, q_norm_w.reshape(HEAD_DIM_A, 1),
      jnp.pad(k_norm_w, (0, LANES - KV_DIM_A)).reshape(1, LANES))


SUBLANES = 8


def _shift_rows(x, d, fill, row):
    if d % SUBLANES == 0:
        return jnp.concatenate([jnp.full((d, x.shape[1]), fill, x.dtype), x[:-d]], axis=0)
    return jnp.where(row >= d, pltpu.roll(x, d, 0), fill)


def _sigmoid(x):
    return 0.5 * (1.0 + jnp.tanh(0.5 * x))


def _doubling_scan(a, b, row):
    n = a.shape[0]
    d = 1
    while d < n:
        b = a * _shift_rows(b, d, 0.0, row) + b
        if 2 * d < n:
            a = a * _shift_rows(a, d, 1.0, row)
        d *= 2
    return b


def _lru_kernel(xb_ref, gb_ref, cw_ref, cb_ref, wr_ref, br_ref, wi_ref, bi_ref, lam_ref, o_ref, a_s, b_s):
    x = xb_ref[...]
    S = x.shape[0]
    row = lax.broadcasted_iota(I32, x.shape, 0)
    xc = cb_ref[...] + _shift_rows(x, CONV_WIDTH - 1, 0.0, row) * cw_ref[0:1, :]
    for j in range(1, CONV_WIDTH):
        d = CONV_WIDTH - 1 - j
        xs = x if d == 0 else _shift_rows(x, d, 0.0, row)
        xc = xc + xs * cw_ref[j:j + 1, :]
    xcb = xc.astype(BF16)
    r = _sigmoid(_dot(xcb, wr_ref[0]) + br_ref[...])
    ig = _sigmoid(_dot(xcb, wi_ref[0]) + bi_ref[...])
    z = -lam_ref[...]
    softplus = jnp.maximum(z, 0.0) + jnp.log1p(jnp.exp(-jnp.abs(z)))
    log_a = -LRU_C * r * softplus
    a = jnp.exp(log_a)
    b = jnp.sqrt(-jnp.tanh(log_a) * (1.0 + a * a)) * (ig * xc)
    sub = row & (SUBLANES - 1)
    d = 1
    while d < SUBLANES:
        keep = sub >= d
        b = a * jnp.where(keep, pltpu.roll(b, d, 0), 0.0) + b
        a = a * jnp.where(keep, pltpu.roll(a, d, 0), 1.0)
        d *= 2
    a_s[...] = a
    b_s[...] = b
    groups = S // SUBLANES
    last = pl.ds(SUBLANES - 1, groups, stride=SUBLANES)
    g_row = lax.broadcasted_iota(I32, (groups, x.shape[1]), 0)
    h_last = _doubling_scan(a_s[last, :], b_s[last, :], g_row)
    carry = _shift_rows(h_last, 1, 0.0, g_row)
    carry = jnp.broadcast_to(carry[:, None, :], (groups, SUBLANES, x.shape[1])).reshape(x.shape)
    h = b_s[...] + a_s[...] * carry
    g = gb_ref[...]
    gelu = 0.5 * g * (1.0 + jnp.tanh(float(np.sqrt(2.0 / np.pi)) * (g + 0.044715 * (g * g * g))))
    o_ref[...] = h * gelu


def _lru(lru, conv_w, conv_b, wr_bd, b_rec, wi_bd, b_in, lam, B, S):
    N = lru.shape[0]
    nt = LRU_WIDTH // LANES
    vec = lambda b, j: (0, j)
    return pl.pallas_call(
        _lru_kernel,
        grid=(B, nt),
        in_specs=[pl.BlockSpec((S, LANES), lambda b, j: (b, j)),
                  pl.BlockSpec((S, LANES), lambda b, j: (b, nt + j)),
                  pl.BlockSpec((CONV_WIDTH, LANES), vec),
                  pl.BlockSpec((1, LANES), vec),
                  pl.BlockSpec((1, LANES, LANES), lambda b, j: (j, 0, 0)),
                  pl.BlockSpec((1, LANES), vec),
                  pl.BlockSpec((1, LANES, LANES), lambda b, j: (j, 0, 0)),
                  pl.BlockSpec((1, LANES), vec),
                  pl.BlockSpec((1, LANES), vec)],
        out_specs=pl.BlockSpec((S, LANES), lambda b, j: (b, j)),
        out_shape=jax.ShapeDtypeStruct((N, LRU_WIDTH), F32),
        scratch_shapes=[pltpu.VMEM((S, LANES), F32), pltpu.VMEM((S, LANES), F32)],
        compiler_params=_params("arbitrary", "arbitrary"),
        name="lru",
    )(lru, lru, conv_w, conv_b.reshape(1, -1), wr_bd, b_rec.reshape(1, -1), wi_bd,
      b_in.reshape(1, -1), lam.reshape(1, -1))


def _block_diag_tiles(w):
    per = LANES // LRU_BLOCK_DIM
    nt = LRU_WIDTH // LANES
    w5 = w.reshape(nt, per, LRU_BLOCK_DIM, 1, LRU_BLOCK_DIM)
    eye = jnp.eye(per, dtype=w.dtype).reshape(1, per, 1, per, 1)
    return (w5 * eye).reshape(nt, LANES, LANES)


def _first_index_of_max(vals, lane):
    m = jnp.max(vals, axis=-1, keepdims=True)
    idx = jnp.min(jnp.where(vals == m, lane, 4 * LANES), axis=-1, keepdims=True)
    return m, idx


def _weight_pieces(out, w, lane, first):
    rest = w
    for k in range(W_PIECES):
        piece = rest.astype(BF16).astype(F32)
        out = jnp.where(lane == first + k, piece, out)
        rest = rest - piece
    return out


def _merge_kernel(attn_ref, lru_ref, g_ref, x_ref, mod_ref, wpa_ref, wpb_ref, wo_ref, nw_ref, wr_ref,
                  br_ref, x1_ref, hsl_ref, route_ref, gc_ref):
    D = x_ref.shape[1]
    tm = x_ref.shape[0]

    y_a = _dot(attn_ref[...].astype(BF16), wpa_ref[...])
    y_b = _dot(lru_ref[...].astype(BF16), wpb_ref[...])
    merged = g_ref[:, :D].astype(F32) * y_a + g_ref[:, D:].astype(F32) * y_b
    x1 = x_ref[...] + mod_ref[0, 2:3, :] * _dot(merged.astype(BF16), wo_ref[...])
    x1_ref[...] = x1
    h2 = _rms_mod(x1, nw_ref[...], mod_ref[0, 3:4, :], mod_ref[0, 4:5, :])
    h2b = h2.astype(BF16)

    h2l = (h2 - h2b.astype(F32)).astype(BF16)
    hi_hi_lo = _dot(h2b, wr_ref[...])
    logits = hi_hi_lo[:, :LANES] + (_dot(h2l, wr_ref[:, :LANES]) + hi_hi_lo[:, LANES:]) + br_ref[...]
    lane = lax.broadcasted_iota(I32, logits.shape, 1)
    gl = jnp.where(lane < N_GROUPS, logits, -jnp.inf)
    gmax, g_sel = _first_index_of_max(gl, lane)
    g_weight = 1.0 / jnp.sum(jnp.exp(gl - gmax), axis=-1, keepdims=True)
    e_lo = N_GROUPS + g_sel * EXPERTS_PER_GROUP
    el = jnp.where((lane >= e_lo) & (lane < e_lo + EXPERTS_PER_GROUP), logits, -jnp.inf)
    v1, i1 = _first_index_of_max(el, lane)
    el2 = jnp.where(lane == i1, -jnp.inf, el)
    v2, i2 = _first_index_of_max(el2, lane)
    e2x = jnp.exp(v2 - v1)
    w1 = g_weight / (1.0 + e2x)
    w2 = g_weight * e2x / (1.0 + e2x)

    hot1 = lane == i1
    hot2 = lane == i2
    hot = jnp.where(hot1, 1.0, jnp.where(hot2, 1.0, 0.0))
    gcnt = jnp.floor((jnp.sum(hot, axis=0, keepdims=True) + (GRAN - 1.0)) * (1.0 / GRAN))
    gc_ref[0] = gcnt
    rr = lax.broadcasted_iota(I32, (LANES, LANES), 0)
    cc = lax.broadcasted_iota(I32, (LANES, LANES), 1)
    upper = jnp.where(rr < cc, 1.0, 0.0).astype(BF16)
    run_start = GRAN * _dot(jnp.broadcast_to(gcnt, (8, LANES)).astype(BF16), upper)[0:1, :]
    rr = lax.broadcasted_iota(I32, (tm, tm), 0)
    cc = lax.broadcasted_iota(I32, (tm, tm), 1)
    lower = jnp.where(cc < rr, 1.0, 0.0).astype(BF16)
    pos = _dot(lower, hot.astype(BF16)) + run_start
    pos1 = jnp.where(hot1, pos, 0.0)
    pos2 = jnp.where(hot2, pos, 0.0)
    lp1 = jnp.sum(pos1, axis=-1, keepdims=True)
    lp2 = jnp.sum(pos2, axis=-1, keepdims=True)
    route_ref[...] = jnp.where(lane == R_LP1, lp1, jnp.where(lane == R_LP2, lp2, 0.0))

    aux = jnp.where(lane == AUX_E1, (i1 - N_GROUPS).astype(F32), 0.0)
    aux = _weight_pieces(_weight_pieces(aux, w1, lane, AUX_W1), w2, lane, AUX_W2)
    lp1r = jnp.sum(pos1.T, axis=0, keepdims=True).astype(I32)
    lp2r = jnp.sum(pos2.T, axis=0, keepdims=True).astype(I32)
    rows = lax.broadcasted_iota(I32, (LOCAL_ROWS, tm), 0)
    sort = jnp.where(rows == lp1r, 1.0, jnp.where(rows == lp2r, 1.0, 0.0)).astype(BF16)
    hsl_ref[...] = _dot(sort, jnp.concatenate([h2b, aux.astype(BF16)], axis=1)).astype(BF16)


def _merge(attn, lru_out, gates, x2, mod3, wpa, wpb, wo, norm_w, wr_hi_lo, b_router, S):
    N, D = x2.shape
    tm = TM_SORT
    n_tiles = N // tm
    steps_per_batch = S // tm
    const = lambda t: (0, 0)
    rows = lambda t: (t, 0)
    return pl.pallas_call(
        _merge_kernel,
        grid=(n_tiles,),
        in_specs=[pl.BlockSpec((tm, ATTN_OUT), rows),
                  pl.BlockSpec((tm, LRU_WIDTH), rows),
                  pl.BlockSpec((tm, 2 * D), rows),
                  pl.BlockSpec((tm, D), rows),
                  pl.BlockSpec((1, 6, D), lambda t: (t // steps_per_batch, 0, 0)),
                  pl.BlockSpec(wpa.shape, const),
                  pl.BlockSpec(wpb.shape, const),
                  pl.BlockSpec(wo.shape, const),
                  pl.BlockSpec((1, D), const),
                  pl.BlockSpec(wr_hi_lo.shape, const),
                  pl.BlockSpec((1, LANES), const)],
        out_specs=[pl.BlockSpec((tm, D), rows),
                   pl.BlockSpec((LOCAL_ROWS, D + AUX_W), rows),
                   pl.BlockSpec((tm, LANES), rows),
                   pl.BlockSpec((1, 1, LANES), lambda t: (t, 0, 0))],
        out_shape=[jax.ShapeDtypeStruct((N, D), F32),
                   jax.ShapeDtypeStruct((n_tiles * LOCAL_ROWS, D + AUX_W), BF16),
                   jax.ShapeDtypeStruct((N, LANES), F32),
                   jax.ShapeDtypeStruct((n_tiles, 1, LANES), F32)],
        compiler_params=_params("arbitrary"),
        name="merge",
    )(attn, lru_out, gates, x2, mod3, wpa, wpb, wo, norm_w.reshape(1, D), wr_hi_lo, b_router)


def _plan_sizes(n_tiles):
    max_gran = n_tiles * LOCAL_GRAN + N_EXPERTS * (GRAN_PER_TILE - 1)
    max_row_tiles = -(-max_gran // GRAN_PER_TILE)
    return max_row_tiles * GRAN_PER_TILE, max_row_tiles


def _plan_kernel(gc_ref, src_ref, texp_ref, nt_ref, dst_ref, local_next):
    n_tiles = gc_ref.shape[0]
    max_gran, max_row_tiles = _plan_sizes(n_tiles)

    def zero(ref, lo, hi):
        def body(k, c):
            ref[k] = 0
            return c
        lax.fori_loop(lo, hi, body, 0)

    zero(local_next, 0, n_tiles)

    def per_expert(e, g):
        first_tile = g // GRAN_PER_TILE

        def per_tile(t, g):
            n = gc_ref[t, N_GROUPS + e]
            base = t * LOCAL_GRAN + local_next[t]

            def per_granule(k, g):
                src_ref[g] = base + k
                dst_ref[base + k] = g
                return g + 1

            g = lax.fori_loop(0, n, per_granule, g)
            local_next[t] = local_next[t] + n
            return g

        g = lax.fori_loop(0, n_tiles, per_tile, g)
        padded = (g + GRAN_PER_TILE - 1) // GRAN_PER_TILE * GRAN_PER_TILE
        zero(src_ref, g, padded)
        g = padded

        def mark(j, c):
            texp_ref[j] = e
            return c

        lax.fori_loop(first_tile, g // GRAN_PER_TILE, mark, 0)
        return g

    g = lax.fori_loop(0, N_EXPERTS, per_expert, 0)
    used = g // GRAN_PER_TILE
    nt_ref[0] = used
    zero(src_ref, g, max_gran)

    def clear_tail(t, c):
        zero(dst_ref, t * LOCAL_GRAN + local_next[t], (t + 1) * LOCAL_GRAN)
        return c

    lax.fori_loop(0, n_tiles, clear_tail, 0)

    def mark_rest(j, c):
        texp_ref[j] = N_EXPERTS - 1
        return c

    lax.fori_loop(used, max_row_tiles, mark_rest, 0)


def _plan(gc):
    n_tiles = gc.shape[0]
    max_gran, max_row_tiles = _plan_sizes(n_tiles)
    smem = pl.BlockSpec(memory_space=pltpu.SMEM)
    return pl.pallas_call(
        _plan_kernel,
        in_specs=[smem],
        out_specs=[smem, smem, smem, smem, smem],
        out_shape=[jax.ShapeDtypeStruct((max_gran,), I32),
                   jax.ShapeDtypeStruct((max_row_tiles,), I32),
                   jax.ShapeDtypeStruct((1,), I32),
                   jax.ShapeDtypeStruct((n_tiles * LOCAL_GRAN,), I32),
                   jax.ShapeDtypeStruct((n_tiles,), I32)],
        name="plan",
    )(gc)


def _granule_copies(idx_ref, first, count, src_hbm, dst_vmem, sem, wait):
    def body(k, c):
        g = idx_ref[first + k]
        cp = pltpu.make_async_copy(src_hbm.at[pl.ds(pl.multiple_of(g * GRAN, GRAN), GRAN)],
                                   dst_vmem.at[pl.ds(pl.multiple_of(k * GRAN, GRAN), GRAN)], sem)
        if wait:
            cp.wait()
        else:
            cp.start()
        return c
    lax.fori_loop(0, count, body, 0)


def _moe_kernel(src_ref, texp_ref, nt_ref, hsl_ref, w1_ref, w3_ref, w2_ref, ys_ref, xbuf, w1b, w3b, w2b, sem):
    i = pl.program_id(0)
    used = nt_ref[0]
    slot = i % MOE_BUFFERS
    D = ys_ref.shape[1]

    def gather(tile, wait):
        slot_ = tile % MOE_BUFFERS
        _granule_copies(src_ref, tile * GRAN_PER_TILE, GRAN_PER_TILE, hsl_ref, xbuf.at[slot_], sem.at[slot_], wait)

    @pl.when(i == 0)
    def _():
        for ahead in range(MOE_BUFFERS - 1):
            pl.when(ahead < used)(functools.partial(gather, ahead, False))

    @pl.when(i + (MOE_BUFFERS - 1) < used)
    def _():
        gather(i + (MOE_BUFFERS - 1), False)

    @pl.when(i < used)
    def _():
        gather(i, True)

        @pl.when((i == 0) | (texp_ref[i] != texp_ref[jnp.maximum(i - 1, 0)]))
        def _():
            w1b[...] = w1_ref[0].astype(BF16)
            w3b[...] = w3_ref[0].astype(BF16)
            w2b[...] = w2_ref[0].astype(BF16)

        xb = xbuf[slot]
        hb = xb[:, :D]
        aux = xb[:, D:].astype(F32)
        first = aux[:, AUX_E1:AUX_E1 + 1] == texp_ref[i].astype(F32)
        w_row = jnp.zeros((TM_MOE, 1), F32)
        for k in range(W_PIECES):
            w_row = w_row + jnp.where(first, aux[:, AUX_W1 + k:AUX_W1 + k + 1], aux[:, AUX_W2 + k:AUX_W2 + k + 1])
        a = _dot(hb, w1b[...])
        act = (a * _sigmoid(a)) * _dot(hb, w3b[...])
        ys_ref[...] = (w_row * _dot(act.astype(BF16), w2b[...])).astype(BF16)

    @pl.when(i >= used)
    def _():
        ys_ref[...] = jnp.zeros_like(ys_ref)


def _moe(src, texp, used, hsl, w1, w3, w2):
    max_row_tiles = texp.shape[0]
    D = w1.shape[1]
    F = w1.shape[2]
    grid_spec = pltpu.PrefetchScalarGridSpec(
        num_scalar_prefetch=3,
        grid=(max_row_tiles,),
        in_specs=[pl.BlockSpec(memory_space=pl.ANY),
                  pl.BlockSpec((1, D, F), lambda i, s, te, nt: (te[i], 0, 0)),
                  pl.BlockSpec((1, D, F), lambda i, s, te, nt: (te[i], 0, 0)),
                  pl.BlockSpec((1, F, D), lambda i, s, te, nt: (te[i], 0, 0))],
        out_specs=pl.BlockSpec((TM_MOE, D), lambda i, s, te, nt: (i, 0)),
        scratch_shapes=[pltpu.VMEM((MOE_BUFFERS, TM_MOE, D + AUX_W), BF16),
                        pltpu.VMEM((D, F), BF16), pltpu.VMEM((D, F), BF16), pltpu.VMEM((F, D), BF16),
                        pltpu.SemaphoreType.DMA((MOE_BUFFERS,))],
    )
    return pl.pallas_call(
        _moe_kernel,
        grid_spec=grid_spec,
        out_shape=jax.ShapeDtypeStruct((max_row_tiles * TM_MOE, D), BF16),
        compiler_params=_params("arbitrary"),
        name="moe",
    )(src, texp, used, hsl, w1, w3, w2)


def _combine_kernel(dst_ref, lcnt_ref, ys_ref, x1_ref, route_ref, mod_ref, o_ref, ybuf, sem):
    t = pl.program_id(0)
    slot = t % 2
    tm = x1_ref.shape[0]

    def gather(tile, slot_, wait):
        _granule_copies(dst_ref, tile * LOCAL_GRAN, lcnt_ref[tile], ys_ref, ybuf.at[slot_], sem.at[slot_], wait)

    @pl.when(t == 0)
    def _():
        ybuf[...] = jnp.zeros_like(ybuf)
        gather(0, 0, False)

    @pl.when(t + 1 < pl.num_programs(0))
    def _():
        gather(t + 1, 1 - slot, False)

    gather(t, slot, True)
    lp1 = route_ref[:, R_LP1:R_LP1 + 1].astype(I32)
    lp2 = route_ref[:, R_LP2:R_LP2 + 1].astype(I32)
    col = lax.broadcasted_iota(I32, (tm, LOCAL_ROWS), 1)
    unsort = jnp.where(col == lp1, 1.0, jnp.where(col == lp2, 1.0, 0.0)).astype(BF16)
    o_ref[...] = x1_ref[...] + mod_ref[0, 5:6, :] * _dot(unsort, ybuf[slot])


def _combine(dst, lcnt, ys, x1, route, mod3, S):
    N, D = x1.shape
    tm = TM_SORT
    steps_per_batch = S // tm
    grid_spec = pltpu.PrefetchScalarGridSpec(
        num_scalar_prefetch=2,
        grid=(N // tm,),
        in_specs=[pl.BlockSpec(memory_space=pl.ANY),
                  pl.BlockSpec((tm, D), lambda t, d, c: (t, 0)),
                  pl.BlockSpec((tm, LANES), lambda t, d, c: (t, 0)),
                  pl.BlockSpec((1, 6, D), lambda t, d, c: (t // steps_per_batch, 0, 0))],
        out_specs=pl.BlockSpec((tm, D), lambda t, d, c: (t, 0)),
        scratch_shapes=[pltpu.VMEM((2, LOCAL_ROWS, D), BF16), pltpu.SemaphoreType.DMA((2,))],
    )
    return pl.pallas_call(
        _combine_kernel,
        grid_spec=grid_spec,
        out_shape=jax.ShapeDtypeStruct((N, D), F32),
        compiler_params=_params("arbitrary"),
        name="combine",
    )(dst, lcnt, ys, x1, route, mod3)


def _layer(x2, mod3, B, S, norm_mix_w, w_in, q_norm_w, k_norm_w, conv_w, conv_b, w_rec_gate, b_rec_gate,
           w_in_gate, b_in_gate, lru_lambda, w_proj_a, w_proj_b, w_out, norm_ffn_w, w_group, b_group,
           w_expert_router, b_expert_router, w1, w3, w2):
    N, D = x2.shape
    nq = N_HEADS_A * HEAD_DIM_A
    o_k = nq
    o_v = o_k + KV_DIM_A
    o_qi = o_v + KV_DIM_A
    o_ki = o_qi + N_IDX_HEADS * IDX_DIM
    o_wi = o_ki + IDX_DIM
    o_lx = o_wi + N_IDX_HEADS
    o_g = o_lx + 2 * LRU_WIDTH
    pad = jnp.zeros((D, KV_W - (WI_OFF + N_IDX_HEADS)), w_in.dtype)
    wa = jnp.concatenate([w_in[:, :nq], w_in[:, o_qi:o_ki], w_in[:, o_k:o_qi], w_in[:, o_ki:o_lx], pad],
                         axis=1).astype(BF16)
    wb = w_in[:, o_lx:o_g].astype(BF16)
    wg = w_in[:, o_g:].astype(BF16)

    qa, kv, lru, gates = _proj(x2, mod3, norm_mix_w, wa, wb, wg, S)
    attn = _attn(qa, kv, q_norm_w, k_norm_w, B, S)
    lru_out = _lru(lru, conv_w, conv_b, _block_diag_tiles(w_rec_gate).astype(BF16), b_rec_gate,
                   _block_diag_tiles(w_in_gate).astype(BF16), b_in_gate, lru_lambda, B, S)

    n_r = N_GROUPS + N_EXPERTS
    w_router = jnp.concatenate([w_group, w_expert_router, jnp.zeros((D, LANES - n_r), F32)], axis=1)
    wr_hi = w_router.astype(BF16)
    wr_lo = (w_router - wr_hi.astype(F32)).astype(BF16)
    b_router = jnp.concatenate([b_group, b_expert_router, jnp.zeros((LANES - n_r,), F32)]).reshape(1, LANES)
    x1, hsl, route, gc = _merge(attn, lru_out, gates, x2, mod3, w_proj_a.astype(BF16), w_proj_b.astype(BF16),
                                w_out.astype(BF16), norm_ffn_w, jnp.concatenate([wr_hi, wr_lo], axis=1),
                                b_router, S)

    src, texp, used, dst, lcnt = _plan(gc.reshape(gc.shape[0], LANES).astype(I32))
    ys = _moe(src, texp, used, hsl, w1, w3, w2)
    return _combine(dst, lcnt, ys, x1, route, mod3, S)


def kernel(x, c, ada_w, ada_b, norm_mix_w, w_in, q_norm_w, k_norm_w, conv_w, conv_b, w_rec_gate, b_rec_gate,
           w_in_gate, b_in_gate, lru_lambda, w_proj_a, w_proj_b, w_out, norm_ffn_w, w_group, b_group,
           w_expert_router, b_expert_router, w1, w3, w2):
    B, S, D = x.shape
    x2 = x.reshape(B * S, D)
    for l in range(ada_w.shape[0]):
        mod3 = _ada(c, ada_w[l], ada_b[l]).reshape(B, 6, D)
        x2 = _layer(x2, mod3, B, S, norm_mix_w[l], w_in[l], q_norm_w[l], k_norm_w[l], conv_w[l], conv_b[l],
                    w_rec_gate[l], b_rec_gate[l], w_in_gate[l], b_in_gate[l], lru_lambda[l], w_proj_a[l],
                    w_proj_b[l], w_out[l], norm_ffn_w[l], w_group[l], b_group[l], w_expert_router[l],
                    b_expert_router[l], w1[l], w3[l], w2[l])
    return x2.reshape(B, S, D)
```

```python
import functools

import jax
import jax.numpy as jnp
import numpy as np
from jax import lax
from jax.experimental import pallas as pl
from jax.experimental.pallas import tpu as pltpu

F32 = jnp.float32
BF16 = jnp.bfloat16
I32 = jnp.int32

EPS = 1e-6
CHUNK = 64
Q_BLOCK = 128
N_HEADS_A = 8
HEAD_DIM_A = 64
KV_DIM_A = 64
ATTN_OUT = N_HEADS_A * KV_DIM_A
N_IDX_HEADS = 4
IDX_DIM = 64
TOPK_MAX = 256
LRU_WIDTH = 512
LRU_BLOCKS = 8
LRU_BLOCK_DIM = LRU_WIDTH // LRU_BLOCKS
CONV_WIDTH = 4
LRU_C = 8.0
N_GROUPS = 4
EXPERTS_PER_GROUP = 8
N_EXPERTS = N_GROUPS * EXPERTS_PER_GROUP
D_FF_EXPERT = 256

LANES = 128
INT_MIN = -2 ** 31
VMEM_LIMIT = 56 * 1024 * 1024

D_MODEL = 1024
COL_Q = 0
COL_K = COL_Q + N_HEADS_A * HEAD_DIM_A
COL_V = COL_K + KV_DIM_A
COL_QI = COL_V + KV_DIM_A
COL_KI = COL_QI + N_IDX_HEADS * IDX_DIM
COL_WI = COL_KI + IDX_DIM
COL_LX = COL_WI + N_IDX_HEADS
D_IN = COL_LX + 2 * LRU_WIDTH + 2 * D_MODEL
ATTN_COLS = -(-COL_LX // LANES) * LANES
TAIL_COLS = D_IN - COL_LX
TAIL_SHIFT = COL_LX % LANES
W_IN_PAD = (COL_LX // LANES) * LANES + TAIL_COLS + LANES
assert COL_K % LANES == 0 and COL_QI % LANES == 0 and COL_KI % LANES == 0 and TAIL_COLS % LANES == 0

QA_W = N_HEADS_A * HEAD_DIM_A + N_IDX_HEADS * IDX_DIM
KV_W = (COL_QI - COL_K) + (ATTN_COLS - COL_KI)
K_OFF, V_OFF = 0, KV_DIM_A
KI_OFF = COL_QI - COL_K
WI_OFF = KI_OFF + IDX_DIM

TM_PROJ = 512

TM_SORT = 512
GRAN = 16
TM_MOE = 256
GRAN_PER_TILE = TM_MOE // GRAN
MOE_BUFFERS = 3
LOCAL_ROWS = -(-(2 * TM_SORT + N_EXPERTS * (GRAN - 1)) // LANES) * LANES
LOCAL_GRAN = LOCAL_ROWS // GRAN
AUX_W = LANES
W_PIECES = 3
AUX_W1, AUX_W2 = 0, W_PIECES
AUX_E1 = 2 * W_PIECES

R_LP1, R_LP2 = 0, 1


def _dot(a, b, **kw):
    return jnp.dot(a, b, preferred_element_type=F32, **kw)


def _params(*sem):
    return pltpu.CompilerParams(dimension_semantics=sem, vmem_limit_bytes=VMEM_LIMIT)


def _ada_kernel(c_ref, w_ref, b_ref, o_ref):
    c = c_ref[...]
    cond = c * jax.nn.sigmoid(c)
    o_ref[...] = _dot(cond, w_ref[...], precision=lax.Precision.HIGHEST) + b_ref[...]


def _ada(c, ada_w, ada_b):
    B, D = c.shape
    n_out = ada_w.shape[1]
    tn = 1024
    return pl.pallas_call(
        _ada_kernel,
        grid=(n_out // tn,),
        in_specs=[pl.BlockSpec((B, D), lambda j: (0, 0)),
                  pl.BlockSpec((D, tn), lambda j: (0, j)),
                  pl.BlockSpec((1, tn), lambda j: (0, j))],
        out_specs=pl.BlockSpec((B, tn), lambda j: (0, j)),
        out_shape=jax.ShapeDtypeStruct((B, n_out), F32),
        compiler_params=_params("arbitrary"),
        name="ada",
    )(c, ada_w, ada_b.reshape(1, n_out))


def _rms_mod(x, w, shift, scale):
    ms = jnp.mean(x * x, axis=-1, keepdims=True)
    y = x * lax.rsqrt(ms + EPS) * w
    return y * (1.0 + scale) + shift


def _proj_kernel(x_ref, mod_ref, nw_ref, w_ref, oqa_ref, okv_ref, olru_ref, og_ref, wt_s):
    @pl.when(pl.program_id(0) == 0)
    def _():
        rr = lax.broadcasted_iota(I32, (2 * LANES, LANES), 0)
        cc = lax.broadcasted_iota(I32, (2 * LANES, LANES), 1)
        shift = jnp.where(rr == cc + TAIL_SHIFT, 1.0, 0.0).astype(BF16)
        for j in range(TAIL_COLS // LANES):
            src = (COL_LX // LANES + j) * LANES
            wt_s[:, j * LANES:(j + 1) * LANES] = _dot(w_ref[:, src:src + 2 * LANES], shift).astype(BF16)

    h = _rms_mod(x_ref[...], nw_ref[...], mod_ref[0, 0:1, :], mod_ref[0, 1:2, :])
    hb = h.astype(BF16)
    pa = _dot(hb, w_ref[:, :ATTN_COLS])
    oqa_ref[...] = jnp.concatenate([pa[:, COL_Q:COL_K], pa[:, COL_QI:COL_KI]], axis=1)
    okv_ref[...] = jnp.concatenate([pa[:, COL_K:COL_QI], pa[:, COL_KI:ATTN_COLS]], axis=1)
    olru_ref[...] = _dot(hb, wt_s[:, :2 * LRU_WIDTH])
    og_ref[...] = _sigmoid(_dot(hb, wt_s[:, 2 * LRU_WIDTH:])).astype(BF16)


def _proj(x2, mod3, norm_w, w_all, S):
    N, D = x2.shape
    tm = TM_PROJ
    steps_per_batch = S // tm
    const = lambda t: (0, 0)
    return pl.pallas_call(
        _proj_kernel,
        grid=(N // tm,),
        in_specs=[pl.BlockSpec((tm, D), lambda t: (t, 0)),
                  pl.BlockSpec((1, 6, D), lambda t: (t // steps_per_batch, 0, 0)),
                  pl.BlockSpec((1, D), const),
                  pl.BlockSpec(w_all.shape, const)],
        out_specs=[pl.BlockSpec((tm, QA_W), lambda t: (t, 0)),
                   pl.BlockSpec((tm, KV_W), lambda t: (t, 0)),
                   pl.BlockSpec((tm, 2 * LRU_WIDTH), lambda t: (t, 0)),
                   pl.BlockSpec((tm, 2 * D), lambda t: (t, 0))],
        out_shape=[jax.ShapeDtypeStruct((N, QA_W), F32),
                   jax.ShapeDtypeStruct((N, KV_W), F32),
                   jax.ShapeDtypeStruct((N, 2 * LRU_WIDTH), F32),
                   jax.ShapeDtypeStruct((N, 2 * D), BF16)],
        scratch_shapes=[pltpu.VMEM((D, TAIL_COLS), BF16)],
        compiler_params=_params("arbitrary"),
        name="proj",
    )(x2, mod3, norm_w.reshape(1, D), w_all)


ATTN_BUCKETS = 8
ROW_FOLD = 64
KEY_ROWS = 256
COUNT_CHAINS = 2
LOG2E = float(np.log2(np.e))
POS_RADIX = 256
SLOPE_PIECES = 3
V_ONES = 16


def _bf16_pieces(c, n):
    out = []
    rest = float(c)
    for _ in range(n):
        p = float(np.asarray(rest, np.float32).astype(BF16).astype(np.float32))
        out.append(p)
        rest -= p
    return out


def _reduce_rows(op, x):
    r = x.shape[0]
    if r > ROW_FOLD and r % ROW_FOLD == 0:
        x = op(x.reshape(r // ROW_FOLD, ROW_FOLD, x.shape[1]), axis=0)
    return op(x, axis=0, keepdims=True)


def _attn_block(s_eff, topk, i, qa_ref, kvq_ref, qnw_ref, o_ref, kn_s, vt_s, ki_s, key_s, bias_s, lg_s, ot_s):
    nq = N_HEADS_A * HEAD_DIM_A
    q_t = qa_ref[:, :nq].T
    qi_t = qa_ref[:, nq:].T
    wq_t = kvq_ref[...].T[WI_OFF:WI_OFF + N_IDX_HEADS, :] * (N_IDX_HEADS ** -0.5 * IDX_DIM ** -0.5)

    q_pos = i * Q_BLOCK + lax.broadcasted_iota(I32, (1, Q_BLOCK), 1)
    limit = (q_pos // CHUNK + 1) * CHUNK
    chunks = [slice(r0, min(r0 + KEY_ROWS, s_eff)) for r0 in range(0, s_eff, KEY_ROWS)]

    qi = [qi_t[h * IDX_DIM:(h + 1) * IDX_DIM, :].astype(BF16) for h in range(N_IDX_HEADS)]
    for rows in chunks:
        ki = ki_s[rows, :]
        score = jnp.zeros((rows.stop - rows.start, Q_BLOCK), F32)
        for h in range(N_IDX_HEADS):
            score = score + jnp.maximum(_dot(ki, qi[h]), 0.0) * wq_t[h:h + 1, :]
        k_pos = rows.start + lax.broadcasted_iota(I32, score.shape, 0)
        key_s[rows, :] = jnp.where(k_pos < limit, score, -jnp.inf)

    def count(mask):
        return _reduce_rows(jnp.sum, jnp.where(mask, 1.0, 0.0))

    def count_keys(test):
        acc = [jnp.zeros((ROW_FOLD, Q_BLOCK), F32) for _ in range(COUNT_CHAINS)]
        for n, r0 in enumerate(range(0, s_eff, ROW_FOLD)):
            hit = jnp.where(test(key_s[r0:r0 + ROW_FOLD, :]), 1.0, 0.0)
            acc[n % COUNT_CHAINS] = acc[n % COUNT_CHAINS] + hit
        return jnp.sum(functools.reduce(lambda x, y: x + y, acc), axis=0, keepdims=True)

    def ordered_to_float(c):
        s = c ^ INT_MIN
        return pltpu.bitcast(jnp.where(s < 0, INT_MIN | (-s), s), F32)

    def bisect(it, r):
        c = r | jnp.left_shift(jnp.int32(1), 31 - it)
        cand = ordered_to_float(c)
        return jnp.where(count_keys(lambda slab: slab >= cand) >= topk, c, r)

    def select_by_threshold():
        thr = ordered_to_float(lax.fori_loop(0, 32, bisect, jnp.zeros((1, Q_BLOCK), I32)))

        n_ge = count_keys(lambda slab: slab >= thr)
        clean = jnp.max(jnp.where((n_ge == topk) & (thr > -jnp.inf), 0.0, 1.0)) == 0.0

        @pl.when(clean)
        def _():
            for rows in chunks:
                bias_s[rows, :] = jnp.where(key_s[rows, :] >= thr, 0.0, -jnp.inf)

        @pl.when(jnp.logical_not(clean))
        def _():
            key = key_s[:s_eff, :]
            need = topk - count(key > thr)
            rr = lax.broadcasted_iota(I32, (LANES, LANES), 0)
            cc = lax.broadcasted_iota(I32, (LANES, LANES), 1)
            tri = jnp.where(cc < rr, 1.0, 0.0).astype(BF16)
            seen = jnp.zeros((1, Q_BLOCK), F32)
            for c in range(s_eff // LANES):
                kc = key[c * LANES:(c + 1) * LANES, :]
                eq = kc == thr
                eqf = jnp.where(eq, 1.0, 0.0)
                rank = _dot(tri, eqf.astype(BF16)) + seen
                seen = seen + jnp.sum(eqf, axis=0, keepdims=True)
                tie_bias = jnp.where(jnp.where(eq, rank, topk) < need, 0.0, -jnp.inf)
                bias = jnp.where(kc > thr, 0.0, tie_bias)
                k_pos = c * LANES + lax.broadcasted_iota(I32, (LANES, Q_BLOCK), 0)
                bias_s[c * LANES:(c + 1) * LANES, :] = jnp.where(k_pos < limit, bias, -jnp.inf)

    if s_eff > topk:
        select_by_threshold()
    else:
        for rows in chunks:
            k_pos = rows.start + lax.broadcasted_iota(I32, (rows.stop - rows.start, Q_BLOCK), 0)
            bias_s[rows, :] = jnp.where(k_pos < limit, 0.0, -jnp.inf)

    qnw = qnw_ref[...]
    crow = lax.broadcasted_iota(I32, (LANES - HEAD_DIM_A, Q_BLOCK), 0)
    r_k = lax.broadcasted_iota(I32, (Q_BLOCK, Q_BLOCK), 0)
    c_q = lax.broadcasted_iota(I32, (Q_BLOCK, Q_BLOCK), 1)
    after = 2.0 * jnp.maximum(r_k - c_q, 0).astype(F32)
    band = pl.ds(pl.multiple_of(i * Q_BLOCK, Q_BLOCK), Q_BLOCK)
    slope = [sum(_bf16_pieces(LOG2E * 2.0 ** -(h + 1), SLOPE_PIECES)) for h in range(N_HEADS_A)]
    row_max = []
    for h in range(N_HEADS_A):
        pieces = _bf16_pieces(LOG2E * 2.0 ** -(h + 1), SLOPE_PIECES)
        qh = q_t[h * HEAD_DIM_A:(h + 1) * HEAD_DIM_A, :]
        ms = jnp.mean(qh * qh, axis=0, keepdims=True)
        qn = (qh * lax.rsqrt(ms + EPS) * qnw) * (HEAD_DIM_A ** -0.5) * LOG2E
        slopes = jnp.zeros(crow.shape, F32)
        for k, piece in enumerate(pieces):
            slopes = jnp.where(crow == 2 * k, POS_RADIX * piece, jnp.where(crow == 2 * k + 1, piece, slopes))
        qa = jnp.concatenate([qn, slopes], axis=0).astype(BF16)
        folded = []
        for rows in chunks:
            logits = _dot(kn_s[rows, :], qa) + bias_s[rows, :]
            lg_s[h, rows, :] = logits
            folded.append(jnp.max(logits.reshape(-1, ROW_FOLD, Q_BLOCK), axis=0))
        row_max.append(jnp.max(functools.reduce(jnp.maximum, folded), axis=0, keepdims=True))
    for h in range(N_HEADS_A):
        lg_s[h, band, :] = lg_s[h, band, :] - slope[h] * after
    for h in range(N_HEADS_A):
        acc = jnp.zeros((KV_DIM_A + V_ONES, Q_BLOCK), F32)
        for rows in chunks:
            p = jnp.exp2(lg_s[h, rows, :] - row_max[h])
            acc = acc + _dot(vt_s[:, rows], p.astype(BF16))
        ot_s[h * KV_DIM_A:(h + 1) * KV_DIM_A, :] = acc[:KV_DIM_A, :] / acc[KV_DIM_A:KV_DIM_A + 1, :]
    o_ref[...] = ot_s[...].T


def _attn_kernel(qa_ref, kvq_ref, kv_ref, qnw_ref, knw_ref, o_ref, kn_s, vt_s, ki_s, key_s, bias_s, lg_s, ot_s):
    i = pl.program_id(1)
    S = kv_ref.shape[0]
    nb = S // Q_BLOCK
    topk = float(min(TOPK_MAX, S // 4))

    @pl.when(i == 0)
    def _():
        kv = kv_ref[:, :LANES]
        lane = lax.broadcasted_iota(I32, kv.shape, 1)
        pos = lax.broadcasted_iota(I32, kv.shape, 0)
        ms = jnp.sum(jnp.where(lane < KV_DIM_A, kv * kv, 0.0), axis=-1, keepdims=True) * (1.0 / KV_DIM_A)
        kn = kv * lax.rsqrt(ms + EPS) * knw_ref[...]
        digit = jnp.where((lane & 1) == 0, pos // POS_RADIX, pos % POS_RADIX).astype(F32)
        is_digit = (lane >= KV_DIM_A) & (lane < KV_DIM_A + 2 * SLOPE_PIECES)
        kn_s[...] = jnp.where(is_digit, digit, kn).astype(BF16)
        vt_s[:KV_DIM_A, :] = kv.T[V_OFF:V_OFF + KV_DIM_A, :].astype(BF16)
        vt_s[KV_DIM_A:, :] = jnp.ones((V_ONES, kv.shape[0]), BF16)
        ki_s[...] = kv_ref[:, KI_OFF:KI_OFF + IDX_DIM].astype(BF16)

    n_buckets = min(ATTN_BUCKETS, nb)
    per = nb // n_buckets
    for j in range(n_buckets):
        pl.when(i // per == j)(functools.partial(
            _attn_block, (j + 1) * per * Q_BLOCK, topk, i, qa_ref, kvq_ref, qnw_ref, o_ref,
            kn_s, vt_s, ki_s, key_s, bias_s, lg_s, ot_s))


def _attn(qa, kv, q_norm_w, k_norm_w, B, S):
    N = qa.shape[0]
    nb = S // Q_BLOCK
    return pl.pallas_call(
        _attn_kernel,
        grid=(B, nb),
        in_specs=[pl.BlockSpec((Q_BLOCK, QA_W), lambda b, i: (b * nb + i, 0)),
                  pl.BlockSpec((Q_BLOCK, KV_W), lambda b, i: (b * nb + i, 0)),
                  pl.BlockSpec((S, KV_W), lambda b, i: (b, 0)),
                  pl.BlockSpec((HEAD_DIM_A, 1), lambda b, i: (0, 0)),
                  pl.BlockSpec((1, LANES), lambda b, i: (0, 0))],
        out_specs=pl.BlockSpec((Q_BLOCK, ATTN_OUT), lambda b, i: (b * nb + i, 0)),
        out_shape=jax.ShapeDtypeStruct((N, ATTN_OUT), F32),
        scratch_shapes=[pltpu.VMEM((S, LANES), BF16),
                        pltpu.VMEM((KV_DIM_A + V_ONES, S), BF16),
                        pltpu.VMEM((S, IDX_DIM), BF16),
                        pltpu.VMEM((S, Q_BLOCK), F32),
                        pltpu.VMEM((S, Q_BLOCK), F32),
                        pltpu.VMEM((N_HEADS_A, S, Q_BLOCK), F32),
                        pltpu.VMEM((ATTN_OUT, Q_BLOCK), F32)],
        compiler_params=_params("arbitrary", "arbitrary"),
        name="attn",
    )(qa, kv, kv, q_norm_w.reshape(HEAD_DIM_A, 1),
      jnp.pad(k_norm_w, (0, LANES - KV_DIM_A)).reshape(1, LANES))


SUBLANES = 8


def _shift_rows(x, d, fill, row):
    if d % SUBLANES == 0:
        return jnp.concatenate([jnp.full((d, x.shape[1]), fill, x.dtype), x[:-d]], axis=0)
    return jnp.where(row >= d, pltpu.roll(x, d, 0), fill)


def _sigmoid(x):
    return 0.5 * (1.0 + jnp.tanh(0.5 * x))


def _doubling_scan(a, b, row):
    n = a.shape[0]
    d = 1
    while d < n:
        b = a * _shift_rows(b, d, 0.0, row) + b
        if 2 * d < n:
            a = a * _shift_rows(a, d, 1.0, row)
        d *= 2
    return b


def _lru_kernel(xb_ref, gb_ref, cw_ref, cb_ref, wr_ref, br_ref, wi_ref, bi_ref, lam_ref, o_ref, a_s, b_s):
    x = xb_ref[...]
    S = x.shape[0]
    row = lax.broadcasted_iota(I32, x.shape, 0)
    xc = cb_ref[...] + _shift_rows(x, CONV_WIDTH - 1, 0.0, row) * cw_ref[0:1, :]
    for j in range(1, CONV_WIDTH):
        d = CONV_WIDTH - 1 - j
        xs = x if d == 0 else _shift_rows(x, d, 0.0, row)
        xc = xc + xs * cw_ref[j:j + 1, :]
    xcb = xc.astype(BF16)
    r = _sigmoid(_dot(xcb, wr_ref[0]) + br_ref[...])
    ig = _sigmoid(_dot(xcb, wi_ref[0]) + bi_ref[...])
    z = -lam_ref[...]
    softplus = jnp.maximum(z, 0.0) + jnp.log1p(jnp.exp(-jnp.abs(z)))
    log_a = -LRU_C * r * softplus
    a = jnp.exp(log_a)
    b = jnp.sqrt(-jnp.tanh(log_a) * (1.0 + a * a)) * (ig * xc)
    sub = row & (SUBLANES - 1)
    d = 1
    while d < SUBLANES:
        keep = sub >= d
        b = a * jnp.where(keep, pltpu.roll(b, d, 0), 0.0) + b
        a = a * jnp.where(keep, pltpu.roll(a, d, 0), 1.0)
        d *= 2
    a_s[...] = a
    b_s[...] = b
    groups = S // SUBLANES
    last = pl.ds(SUBLANES - 1, groups, stride=SUBLANES)
    g_row = lax.broadcasted_iota(I32, (groups, x.shape[1]), 0)
    h_last = _doubling_scan(a_s[last, :], b_s[last, :], g_row)
    carry = _shift_rows(h_last, 1, 0.0, g_row)
    carry = jnp.broadcast_to(carry[:, None, :], (groups, SUBLANES, x.shape[1])).reshape(x.shape)
    h = b_s[...] + a_s[...] * carry
    g = gb_ref[...]
    gelu = 0.5 * g * (1.0 + jnp.tanh(float(np.sqrt(2.0 / np.pi)) * (g + 0.044715 * (g * g * g))))
    o_ref[...] = h * gelu


def _lru(lru, conv_w, conv_b, wr_bd, b_rec, wi_bd, b_in, lam, B, S):
    N = lru.shape[0]
    nt = LRU_WIDTH // LANES
    vec = lambda b, j: (0, j)
    return pl.pallas_call(
        _lru_kernel,
        grid=(B, nt),
        in_specs=[pl.BlockSpec((S, LANES), lambda b, j: (b, j)),
                  pl.BlockSpec((S, LANES), lambda b, j: (b, nt + j)),
                  pl.BlockSpec((CONV_WIDTH, LANES), vec),
                  pl.BlockSpec((1, LANES), vec),
                  pl.BlockSpec((1, LANES, LANES), lambda b, j: (j, 0, 0)),
                  pl.BlockSpec((1, LANES), vec),
                  pl.BlockSpec((1, LANES, LANES), lambda b, j: (j, 0, 0)),
                  pl.BlockSpec((1, LANES), vec),
                  pl.BlockSpec((1, LANES), vec)],
        out_specs=pl.BlockSpec((S, LANES), lambda b, j: (b, j)),
        out_shape=jax.ShapeDtypeStruct((N, LRU_WIDTH), F32),
        scratch_shapes=[pltpu.VMEM((S, LANES), F32), pltpu.VMEM((S, LANES), F32)],
        compiler_params=_params("arbitrary", "arbitrary"),
        name="lru",
    )(lru, lru, conv_w, conv_b.reshape(1, -1), wr_bd, b_rec.reshape(1, -1), wi_bd,
      b_in.reshape(1, -1), lam.reshape(1, -1))


def _block_diag_tiles(w):
    per = LANES // LRU_BLOCK_DIM
    nt = LRU_WIDTH // LANES
    w5 = w.reshape(nt, per, LRU_BLOCK_DIM, 1, LRU_BLOCK_DIM)
    eye = jnp.eye(per, dtype=w.dtype).reshape(1, per, 1, per, 1)
    return (w5 * eye).reshape(nt, LANES, LANES)


def _first_index_of_max(vals, lane):
    m = jnp.max(vals, axis=-1, keepdims=True)
    idx = jnp.min(jnp.where(vals == m, lane, 4 * LANES), axis=-1, keepdims=True)
    return m, idx


def _weight_pieces(out, w, lane, first):
    rest = w
    for k in range(W_PIECES):
        piece = rest.astype(BF16).astype(F32)
        out = jnp.where(lane == first + k, piece, out)
        rest = rest - piece
    return out


def _merge_kernel(attn_ref, lru_ref, g_ref, x_ref, mod_ref, wpa_ref, wpb_ref, wo_ref, nw_ref, wr_ref,
                  br_ref, x1_ref, hsl_ref, route_ref, gc_ref):
    D = x_ref.shape[1]
    tm = x_ref.shape[0]

    y_a = _dot(attn_ref[...].astype(BF16), wpa_ref[...])
    y_b = _dot(lru_ref[...].astype(BF16), wpb_ref[...])
    merged = g_ref[:, :D].astype(F32) * y_a + g_ref[:, D:].astype(F32) * y_b
    x1 = x_ref[...] + mod_ref[0, 2:3, :] * _dot(merged.astype(BF16), wo_ref[...])
    x1_ref[...] = x1
    h2 = _rms_mod(x1, nw_ref[...], mod_ref[0, 3:4, :], mod_ref[0, 4:5, :])
    h2b = h2.astype(BF16)

    h2l = (h2 - h2b.astype(F32)).astype(BF16)
    hi_hi_lo = _dot(h2b, wr_ref[...])
    logits = hi_hi_lo[:, :LANES] + (_dot(h2l, wr_ref[:, :LANES]) + hi_hi_lo[:, LANES:]) + br_ref[...]
    lane = lax.broadcasted_iota(I32, logits.shape, 1)
    gl = jnp.where(lane < N_GROUPS, logits, -jnp.inf)
    gmax, g_sel = _first_index_of_max(gl, lane)
    g_weight = 1.0 / jnp.sum(jnp.exp(gl - gmax), axis=-1, keepdims=True)
    e_lo = N_GROUPS + g_sel * EXPERTS_PER_GROUP
    el = jnp.where((lane >= e_lo) & (lane < e_lo + EXPERTS_PER_GROUP), logits, -jnp.inf)
    v1, i1 = _first_index_of_max(el, lane)
    el2 = jnp.where(lane == i1, -jnp.inf, el)
    v2, i2 = _first_index_of_max(el2, lane)
    e2x = jnp.exp(v2 - v1)
    w1 = g_weight / (1.0 + e2x)
    w2 = g_weight * e2x / (1.0 + e2x)

    hot1 = lane == i1
    hot2 = lane == i2
    hot = jnp.where(hot1, 1.0, jnp.where(hot2, 1.0, 0.0))
    gcnt = jnp.floor((jnp.sum(hot, axis=0, keepdims=True) + (GRAN - 1.0)) * (1.0 / GRAN))
    gc_ref[0] = gcnt
    rr = lax.broadcasted_iota(I32, (LANES, LANES), 0)
    cc = lax.broadcasted_iota(I32, (LANES, LANES), 1)
    upper = jnp.where(rr < cc, 1.0, 0.0).astype(BF16)
    run_start = GRAN * _dot(jnp.broadcast_to(gcnt, (8, LANES)).astype(BF16), upper)[0:1, :]
    rr = lax.broadcasted_iota(I32, (tm, tm), 0)
    cc = lax.broadcasted_iota(I32, (tm, tm), 1)
    lower = jnp.where(cc < rr, 1.0, 0.0).astype(BF16)
    pos = _dot(lower, hot.astype(BF16)) + run_start
    pos1 = jnp.where(hot1, pos, 0.0)
    pos2 = jnp.where(hot2, pos, 0.0)
    lp1 = jnp.sum(pos1, axis=-1, keepdims=True)
    lp2 = jnp.sum(pos2, axis=-1, keepdims=True)
    route_ref[...] = jnp.where(lane == R_LP1, lp1, jnp.where(lane == R_LP2, lp2, 0.0))

    aux = jnp.where(lane == AUX_E1, (i1 - N_GROUPS).astype(F32), 0.0)
    aux = _weight_pieces(_weight_pieces(aux, w1, lane, AUX_W1), w2, lane, AUX_W2)
    lp1r = jnp.sum(pos1.T, axis=0, keepdims=True).astype(I32)
    lp2r = jnp.sum(pos2.T, axis=0, keepdims=True).astype(I32)
    rows = lax.broadcasted_iota(I32, (LOCAL_ROWS, tm), 0)
    sort = jnp.where(rows == lp1r, 1.0, jnp.where(rows == lp2r, 1.0, 0.0)).astype(BF16)
    hsl_ref[...] = _dot(sort, jnp.concatenate([h2b, aux.astype(BF16)], axis=1)).astype(BF16)


def _merge(attn, lru_out, gates, x2, mod3, wpa, wpb, wo, norm_w, wr_hi_lo, b_router, S):
    N, D = x2.shape
    tm = TM_SORT
    n_tiles = N // tm
    steps_per_batch = S // tm
    const = lambda t: (0, 0)
    rows = lambda t: (t, 0)
    return pl.pallas_call(
        _merge_kernel,
        grid=(n_tiles,),
        in_specs=[pl.BlockSpec((tm, ATTN_OUT), rows),
                  pl.BlockSpec((tm, LRU_WIDTH), rows),
                  pl.BlockSpec((tm, 2 * D), rows),
                  pl.BlockSpec((tm, D), rows),
                  pl.BlockSpec((1, 6, D), lambda t: (t // steps_per_batch, 0, 0)),
                  pl.BlockSpec(wpa.shape, const),
                  pl.BlockSpec(wpb.shape, const),
                  pl.BlockSpec(wo.shape, const),
                  pl.BlockSpec((1, D), const),
                  pl.BlockSpec(wr_hi_lo.shape, const),
                  pl.BlockSpec((1, LANES), const)],
        out_specs=[pl.BlockSpec((tm, D), rows),
                   pl.BlockSpec((LOCAL_ROWS, D + AUX_W), rows),
                   pl.BlockSpec((tm, LANES), rows),
                   pl.BlockSpec((1, 1, LANES), lambda t: (t, 0, 0))],
        out_shape=[jax.ShapeDtypeStruct((N, D), F32),
                   jax.ShapeDtypeStruct((n_tiles * LOCAL_ROWS, D + AUX_W), BF16),
                   jax.ShapeDtypeStruct((N, LANES), F32),
                   jax.ShapeDtypeStruct((n_tiles, 1, LANES), F32)],
        compiler_params=_params("arbitrary"),
        name="merge",
    )(attn, lru_out, gates, x2, mod3, wpa, wpb, wo, norm_w.reshape(1, D), wr_hi_lo, b_router)


def _plan_sizes(n_tiles):
    max_gran = n_tiles * LOCAL_GRAN + N_EXPERTS * (GRAN_PER_TILE - 1)
    max_row_tiles = -(-max_gran // GRAN_PER_TILE)
    return max_row_tiles * GRAN_PER_TILE, max_row_tiles


def _plan_kernel(gc_ref, src_ref, texp_ref, nt_ref, dst_ref, local_next):
    n_tiles = gc_ref.shape[0]
    max_gran, max_row_tiles = _plan_sizes(n_tiles)

    def zero(ref, lo, hi):
        def body(k, c):
            ref[k] = 0
            return c
        lax.fori_loop(lo, hi, body, 0)

    zero(local_next, 0, n_tiles)

    def per_expert(e, g):
        first_tile = g // GRAN_PER_TILE

        def per_tile(t, g):
            n = gc_ref[t, N_GROUPS + e]
            base = t * LOCAL_GRAN + local_next[t]

            def per_granule(k, g):
                src_ref[g] = base + k
                dst_ref[base + k] = g
                return g + 1

            g = lax.fori_loop(0, n, per_granule, g)
            local_next[t] = local_next[t] + n
            return g

        g = lax.fori_loop(0, n_tiles, per_tile, g)
        padded = (g + GRAN_PER_TILE - 1) // GRAN_PER_TILE * GRAN_PER_TILE
        zero(src_ref, g, padded)
        g = padded

        def mark(j, c):
            texp_ref[j] = e
            return c

        lax.fori_loop(first_tile, g // GRAN_PER_TILE, mark, 0)
        return g

    g = lax.fori_loop(0, N_EXPERTS, per_expert, 0)
    used = g // GRAN_PER_TILE
    nt_ref[0] = used
    zero(src_ref, g, max_gran)

    def clear_tail(t, c):
        zero(dst_ref, t * LOCAL_GRAN + local_next[t], (t + 1) * LOCAL_GRAN)
        return c

    lax.fori_loop(0, n_tiles, clear_tail, 0)

    def mark_rest(j, c):
        texp_ref[j] = N_EXPERTS - 1
        return c

    lax.fori_loop(used, max_row_tiles, mark_rest, 0)


def _plan(gc):
    n_tiles = gc.shape[0]
    max_gran, max_row_tiles = _plan_sizes(n_tiles)
    smem = pl.BlockSpec(memory_space=pltpu.SMEM)
    return pl.pallas_call(
        _plan_kernel,
        in_specs=[smem],
        out_specs=[smem, smem, smem, smem, smem],
        out_shape=[jax.ShapeDtypeStruct((max_gran,), I32),
                   jax.ShapeDtypeStruct((max_row_tiles,), I32),
                   jax.ShapeDtypeStruct((1,), I32),
                   jax.ShapeDtypeStruct((n_tiles * LOCAL_GRAN,), I32),
                   jax.ShapeDtypeStruct((n_tiles,), I32)],
        name="plan",
    )(gc)


def _granule_copies(idx_ref, first, count, src_hbm, dst_vmem, sem, wait):
    def body(k, c):
        g = idx_ref[first + k]
        cp = pltpu.make_async_copy(src_hbm.at[pl.ds(pl.multiple_of(g * GRAN, GRAN), GRAN)],
                                   dst_vmem.at[pl.ds(pl.multiple_of(k * GRAN, GRAN), GRAN)], sem)
        if wait:
            cp.wait()
        else:
            cp.start()
        return c
    lax.fori_loop(0, count, body, 0)


def _moe_kernel(src_ref, texp_ref, nt_ref, hsl_ref, w1_ref, w3_ref, w2_ref, ys_ref, xbuf, w1b, w3b, w2b, sem):
    i = pl.program_id(0)
    used = nt_ref[0]
    slot = i % MOE_BUFFERS
    D = ys_ref.shape[1]

    def gather(tile, wait):
        slot_ = tile % MOE_BUFFERS
        _granule_copies(src_ref, tile * GRAN_PER_TILE, GRAN_PER_TILE, hsl_ref, xbuf.at[slot_], sem.at[slot_], wait)

    @pl.when(i == 0)
    def _():
        for ahead in range(MOE_BUFFERS - 1):
            pl.when(ahead < used)(functools.partial(gather, ahead, False))

    @pl.when(i + (MOE_BUFFERS - 1) < used)
    def _():
        gather(i + (MOE_BUFFERS - 1), False)

    @pl.when(i < used)
    def _():
        gather(i, True)

        @pl.when((i == 0) | (texp_ref[i] != texp_ref[jnp.maximum(i - 1, 0)]))
        def _():
            w1b[...] = w1_ref[0].astype(BF16)
            w3b[...] = w3_ref[0].astype(BF16)
            w2b[...] = w2_ref[0].astype(BF16)

        xb = xbuf[slot]
        hb = xb[:, :D]
        aux = xb[:, D:].astype(F32)
        first = aux[:, AUX_E1:AUX_E1 + 1] == texp_ref[i].astype(F32)
        w_row = jnp.zeros((TM_MOE, 1), F32)
        for k in range(W_PIECES):
            w_row = w_row + jnp.where(first, aux[:, AUX_W1 + k:AUX_W1 + k + 1], aux[:, AUX_W2 + k:AUX_W2 + k + 1])
        a = _dot(hb, w1b[...])
        act = (a * _sigmoid(a)) * _dot(hb, w3b[...])
        ys_ref[...] = (w_row * _dot(act.astype(BF16), w2b[...])).astype(BF16)

    @pl.when(i >= used)
    def _():
        ys_ref[...] = jnp.zeros_like(ys_ref)


def _moe(src, texp, used, hsl, w1, w3, w2):
    max_row_tiles = texp.shape[0]
    D = w1.shape[1]
    F = w1.shape[2]
    grid_spec = pltpu.PrefetchScalarGridSpec(
        num_scalar_prefetch=3,
        grid=(max_row_tiles,),
        in_specs=[pl.BlockSpec(memory_space=pl.ANY),
                  pl.BlockSpec((1, D, F), lambda i, s, te, nt: (te[i], 0, 0)),
                  pl.BlockSpec((1, D, F), lambda i, s, te, nt: (te[i], 0, 0)),
                  pl.BlockSpec((1, F, D), lambda i, s, te, nt: (te[i], 0, 0))],
        out_specs=pl.BlockSpec((TM_MOE, D), lambda i, s, te, nt: (i, 0)),
        scratch_shapes=[pltpu.VMEM((MOE_BUFFERS, TM_MOE, D + AUX_W), BF16),
                        pltpu.VMEM((D, F), BF16), pltpu.VMEM((D, F), BF16), pltpu.VMEM((F, D), BF16),
                        pltpu.SemaphoreType.DMA((MOE_BUFFERS,))],
    )
    return pl.pallas_call(
        _moe_kernel,
        grid_spec=grid_spec,
        out_shape=jax.ShapeDtypeStruct((max_row_tiles * TM_MOE, D), BF16),
        compiler_params=_params("arbitrary"),
        name="moe",
    )(src, texp, used, hsl, w1, w3, w2)


def _combine_kernel(dst_ref, lcnt_ref, ys_ref, x1_ref, route_ref, mod_ref, o_ref, ybuf, sem):
    t = pl.program_id(0)
    slot = t % 2
    tm = x1_ref.shape[0]

    def gather(tile, slot_, wait):
        _granule_copies(dst_ref, tile * LOCAL_GRAN, lcnt_ref[tile], ys_ref, ybuf.at[slot_], sem.at[slot_], wait)

    @pl.when(t == 0)
    def _():
        ybuf[...] = jnp.zeros_like(ybuf)
        gather(0, 0, False)

    @pl.when(t + 1 < pl.num_programs(0))
    def _():
        gather(t + 1, 1 - slot, False)

    gather(t, slot, True)
    lp1 = route_ref[:, R_LP1:R_LP1 + 1].astype(I32)
    lp2 = route_ref[:, R_LP2:R_LP2 + 1].astype(I32)
    col = lax.broadcasted_iota(I32, (tm, LOCAL_ROWS), 1)
    unsort = jnp.where(col == lp1, 1.0, jnp.where(col == lp2, 1.0, 0.0)).astype(BF16)
    o_ref[...] = x1_ref[...] + mod_ref[0, 5:6, :] * _dot(unsort, ybuf[slot])


def _combine(dst, lcnt, ys, x1, route, mod3, S):
    N, D = x1.shape
    tm = TM_SORT
    steps_per_batch = S // tm
    grid_spec = pltpu.PrefetchScalarGridSpec(
        num_scalar_prefetch=2,
        grid=(N // tm,),
        in_specs=[pl.BlockSpec(memory_space=pl.ANY),
                  pl.BlockSpec((tm, D), lambda t, d, c: (t, 0)),
                  pl.BlockSpec((tm, LANES), lambda t, d, c: (t, 0)),
                  pl.BlockSpec((1, 6, D), lambda t, d, c: (t // steps_per_batch, 0, 0))],
        out_specs=pl.BlockSpec((tm, D), lambda t, d, c: (t, 0)),
        scratch_shapes=[pltpu.VMEM((2, LOCAL_ROWS, D), BF16), pltpu.SemaphoreType.DMA((2,))],
    )
    return pl.pallas_call(
        _combine_kernel,
        grid_spec=grid_spec,
        out_shape=jax.ShapeDtypeStruct((N, D), F32),
        compiler_params=_params("arbitrary"),
        name="combine",
    )(dst, lcnt, ys, x1, route, mod3)


def _layer(x2, mod3, B, S, norm_mix_w, w_in, q_norm_w, k_norm_w, conv_w, conv_b, w_rec_gate, b_rec_gate,
           w_in_gate, b_in_gate, lru_lambda, w_proj_a, w_proj_b, w_out, norm_ffn_w, w_group, b_group,
           w_expert_router, b_expert_router, w1, w3, w2):
    N, D = x2.shape
    w_all = jnp.pad(w_in.astype(BF16), ((0, 0), (0, W_IN_PAD - D_IN)))
    qa, kv, lru, gates = _proj(x2, mod3, norm_mix_w, w_all, S)
    attn = _attn(qa, kv, q_norm_w, k_norm_w, B, S)
    lru_out = _lru(lru, conv_w, conv_b, _block_diag_tiles(w_rec_gate).astype(BF16), b_rec_gate,
                   _block_diag_tiles(w_in_gate).astype(BF16), b_in_gate, lru_lambda, B, S)

    n_r = N_GROUPS + N_EXPERTS
    w_router = jnp.concatenate([w_group, w_expert_router, jnp.zeros((D, LANES - n_r), F32)], axis=1)
    wr_hi = w_router.astype(BF16)
    wr_lo = (w_router - wr_hi.astype(F32)).astype(BF16)
    b_router = jnp.concatenate([b_group, b_expert_router, jnp.zeros((LANES - n_r,), F32)]).reshape(1, LANES)
    x1, hsl, route, gc = _merge(attn, lru_out, gates, x2, mod3, w_proj_a.astype(BF16), w_proj_b.astype(BF16),
                                w_out.astype(BF16), norm_ffn_w, jnp.concatenate([wr_hi, wr_lo], axis=1),
                                b_router, S)

    src, texp, used, dst, lcnt = _plan(gc.reshape(gc.shape[0], LANES).astype(I32))
    ys = _moe(src, texp, used, hsl, w1, w3, w2)
    return _combine(dst, lcnt, ys, x1, route, mod3, S)


def kernel(x, c, ada_w, ada_b, norm_mix_w, w_in, q_norm_w, k_norm_w, conv_w, conv_b, w_rec_gate, b_rec_gate,
           w_in_gate, b_in_gate, lru_lambda, w_proj_a, w_proj_b, w_out, norm_ffn_w, w_group, b_group,
           w_expert_router, b_expert_router, w1, w3, w2):
    B, S, D = x.shape
    x2 = x.reshape(B * S, D)
    for l in range(ada_w.shape[0]):
        mod3 = _ada(c, ada_w[l], ada_b[l]).reshape(B, 6, D)
        x2 = _layer(x2, mod3, B, S, norm_mix_w[l], w_in[l], q_norm_w[l], k_norm_w[l], conv_w[l], conv_b[l],
                    w_rec_gate[l], b_rec_gate[l], w_in_gate[l], b_in_gate[l], lru_lambda[l], w_proj_a[l],
                    w_proj_b[l], w_out[l], norm_ffn_w[l], w_group[l], b_group[l], w_expert_router[l],
                    b_expert_router[l], w1[l], w3[l], w2[l])
    return x2.reshape(B, S, D)
```

```python
import functools

import jax
import jax.numpy as jnp
import numpy as np
from jax import lax
from jax.experimental import pallas as pl
from jax.experimental.pallas import tpu as pltpu

F32 = jnp.float32
BF16 = jnp.bfloat16
I32 = jnp.int32

EPS = 1e-6
CHUNK = 64
Q_BLOCK = 128
N_HEADS_A = 8
HEAD_DIM_A = 64
KV_DIM_A = 64
ATTN_OUT = N_HEADS_A * KV_DIM_A
N_IDX_HEADS = 4
IDX_DIM = 64
TOPK_MAX = 256
LRU_WIDTH = 512
LRU_BLOCKS = 8
LRU_BLOCK_DIM = LRU_WIDTH // LRU_BLOCKS
CONV_WIDTH = 4
LRU_C = 8.0
N_GROUPS = 4
EXPERTS_PER_GROUP = 8
N_EXPERTS = N_GROUPS * EXPERTS_PER_GROUP
D_FF_EXPERT = 256

LANES = 128
INT_MIN = -2 ** 31
VMEM_LIMIT = 56 * 1024 * 1024

D_MODEL = 1024
COL_Q = 0
COL_K = COL_Q + N_HEADS_A * HEAD_DIM_A
COL_V = COL_K + KV_DIM_A
COL_QI = COL_V + KV_DIM_A
COL_KI = COL_QI + N_IDX_HEADS * IDX_DIM
COL_WI = COL_KI + IDX_DIM
COL_LX = COL_WI + N_IDX_HEADS
D_IN = COL_LX + 2 * LRU_WIDTH + 2 * D_MODEL
ATTN_COLS = -(-COL_LX // LANES) * LANES
TAIL_COLS = D_IN - COL_LX
TAIL_SHIFT = COL_LX % LANES
W_IN_PAD = (COL_LX // LANES) * LANES + TAIL_COLS + LANES
assert COL_K % LANES == 0 and COL_QI % LANES == 0 and COL_KI % LANES == 0 and TAIL_COLS % LANES == 0

QA_W = N_HEADS_A * HEAD_DIM_A + N_IDX_HEADS * IDX_DIM
KV_W = (COL_QI - COL_K) + (ATTN_COLS - COL_KI)
K_OFF, V_OFF = 0, KV_DIM_A
KI_OFF = COL_QI - COL_K
WI_OFF = KI_OFF + IDX_DIM

TM_PROJ = 512

TM_SORT = 512
GRAN = 16
TM_MOE = 256
GRAN_PER_TILE = TM_MOE // GRAN
MOE_BUFFERS = 3
LOCAL_ROWS = -(-(2 * TM_SORT + N_EXPERTS * (GRAN - 1)) // LANES) * LANES
LOCAL_GRAN = LOCAL_ROWS // GRAN
AUX_W = LANES
W_PIECES = 3
AUX_W1, AUX_W2 = 0, W_PIECES
AUX_E1 = 2 * W_PIECES

R_LP1, R_LP2 = 0, 1


def _dot(a, b, **kw):
    return jnp.dot(a, b, preferred_element_type=F32, **kw)


def _params(*sem):
    return pltpu.CompilerParams(dimension_semantics=sem, vmem_limit_bytes=VMEM_LIMIT)


def _ada_kernel(c_ref, w_ref, b_ref, o_ref):
    c = c_ref[...]
    cond = c * jax.nn.sigmoid(c)
    o_ref[...] = _dot(cond, w_ref[...], precision=lax.Precision.HIGHEST) + b_ref[...]


def _ada(c, ada_w, ada_b):
    B, D = c.shape
    n_out = ada_w.shape[1]
    tn = 1024
    return pl.pallas_call(
        _ada_kernel,
        grid=(n_out // tn,),
        in_specs=[pl.BlockSpec((B, D), lambda j: (0, 0)),
                  pl.BlockSpec((D, tn), lambda j: (0, j)),
                  pl.BlockSpec((1, tn), lambda j: (0, j))],
        out_specs=pl.BlockSpec((B, tn), lambda j: (0, j)),
        out_shape=jax.ShapeDtypeStruct((B, n_out), F32),
        compiler_params=_params("arbitrary"),
        name="ada",
    )(c, ada_w, ada_b.reshape(1, n_out))


W_CAST_COLS = 512


def _pad_cast_kernel(w_ref, o_ref):
    col = pl.program_id(0) * W_CAST_COLS + lax.broadcasted_iota(I32, w_ref.shape, 1)
    o_ref[...] = jnp.where(col < D_IN, w_ref[...], 0.0).astype(BF16)


def _pad_cast(w_in):
    D = w_in.shape[0]
    assert w_in.shape[1] == D_IN and W_IN_PAD % W_CAST_COLS == 0
    return pl.pallas_call(
        _pad_cast_kernel,
        grid=(W_IN_PAD // W_CAST_COLS,),
        in_specs=[pl.BlockSpec((D, W_CAST_COLS), lambda j: (0, j))],
        out_specs=pl.BlockSpec((D, W_CAST_COLS), lambda j: (0, j)),
        out_shape=jax.ShapeDtypeStruct((D, W_IN_PAD), BF16),
        compiler_params=_params("arbitrary"),
        name="pad_cast",
    )(w_in)


def _rms_mod(x, w, shift, scale):
    ms = jnp.mean(x * x, axis=-1, keepdims=True)
    y = x * lax.rsqrt(ms + EPS) * w
    return y * (1.0 + scale) + shift


def _proj_kernel(x_ref, mod_ref, nw_ref, w_ref, oqa_ref, okv_ref, olru_ref, og_ref, wt_s):
    @pl.when(pl.program_id(0) == 0)
    def _():
        rr = lax.broadcasted_iota(I32, (2 * LANES, LANES), 0)
        cc = lax.broadcasted_iota(I32, (2 * LANES, LANES), 1)
        shift = jnp.where(rr == cc + TAIL_SHIFT, 1.0, 0.0).astype(BF16)
        for j in range(TAIL_COLS // LANES):
            src = (COL_LX // LANES + j) * LANES
            wt_s[:, j * LANES:(j + 1) * LANES] = _dot(w_ref[:, src:src + 2 * LANES], shift).astype(BF16)

    h = _rms_mod(x_ref[...], nw_ref[...], mod_ref[0, 0:1, :], mod_ref[0, 1:2, :])
    hb = h.astype(BF16)
    pa = _dot(hb, w_ref[:, :ATTN_COLS])
    oqa_ref[...] = jnp.concatenate([pa[:, COL_Q:COL_K], pa[:, COL_QI:COL_KI]], axis=1)
    okv_ref[...] = jnp.concatenate([pa[:, COL_K:COL_QI], pa[:, COL_KI:ATTN_COLS]], axis=1)
    olru_ref[...] = _dot(hb, wt_s[:, :2 * LRU_WIDTH])
    og_ref[...] = _sigmoid(_dot(hb, wt_s[:, 2 * LRU_WIDTH:])).astype(BF16)


def _proj(x2, mod3, norm_w, w_all, S):
    N, D = x2.shape
    tm = TM_PROJ
    steps_per_batch = S // tm
    const = lambda t: (0, 0)
    return pl.pallas_call(
        _proj_kernel,
        grid=(N // tm,),
        in_specs=[pl.BlockSpec((tm, D), lambda t: (t, 0)),
                  pl.BlockSpec((1, 6, D), lambda t: (t // steps_per_batch, 0, 0)),
                  pl.BlockSpec((1, D), const),
                  pl.BlockSpec(w_all.shape, const)],
        out_specs=[pl.BlockSpec((tm, QA_W), lambda t: (t, 0)),
                   pl.BlockSpec((tm, KV_W), lambda t: (t, 0)),
                   pl.BlockSpec((tm, 2 * LRU_WIDTH), lambda t: (t, 0)),
                   pl.BlockSpec((tm, 2 * D), lambda t: (t, 0))],
        out_shape=[jax.ShapeDtypeStruct((N, QA_W), F32),
                   jax.ShapeDtypeStruct((N, KV_W), F32),
                   jax.ShapeDtypeStruct((N, 2 * LRU_WIDTH), F32),
                   jax.ShapeDtypeStruct((N, 2 * D), BF16)],
        scratch_shapes=[pltpu.VMEM((D, TAIL_COLS), BF16)],
        compiler_params=_params("arbitrary"),
        name="proj",
    )(x2, mod3, norm_w.reshape(1, D), w_all)


ATTN_BUCKETS = 8
ROW_FOLD = 64
KEY_ROWS = 256
COUNT_CHAINS = 2
LOG2E = float(np.log2(np.e))
POS_RADIX = 256
SLOPE_PIECES = 3
V_ONES = 16


def _bf16_pieces(c, n):
    out = []
    rest = float(c)
    for _ in range(n):
        p = float(np.asarray(rest, np.float32).astype(BF16).astype(np.float32))
        out.append(p)
        rest -= p
    return out


def _reduce_rows(op, x):
    r = x.shape[0]
    if r > ROW_FOLD and r % ROW_FOLD == 0:
        x = op(x.reshape(r // ROW_FOLD, ROW_FOLD, x.shape[1]), axis=0)
    return op(x, axis=0, keepdims=True)


def _attn_block(s_eff, topk, i, qa_ref, kvq_ref, qnw_ref, o_ref, kn_s, vt_s, ki_s, key_s, bias_s, lg_s, ot_s):
    nq = N_HEADS_A * HEAD_DIM_A
    q_t = qa_ref[:, :nq].T
    qi_t = qa_ref[:, nq:].T
    wq_t = kvq_ref[...].T[WI_OFF:WI_OFF + N_IDX_HEADS, :] * (N_IDX_HEADS ** -0.5 * IDX_DIM ** -0.5)

    q_pos = i * Q_BLOCK + lax.broadcasted_iota(I32, (1, Q_BLOCK), 1)
    limit = (q_pos // CHUNK + 1) * CHUNK
    chunks = [slice(r0, min(r0 + KEY_ROWS, s_eff)) for r0 in range(0, s_eff, KEY_ROWS)]

    qi = [qi_t[h * IDX_DIM:(h + 1) * IDX_DIM, :].astype(BF16) for h in range(N_IDX_HEADS)]
    for rows in chunks:
        ki = ki_s[rows, :]
        score = jnp.zeros((rows.stop - rows.start, Q_BLOCK), F32)
        for h in range(N_IDX_HEADS):
            score = score + jnp.maximum(_dot(ki, qi[h]), 0.0) * wq_t[h:h + 1, :]
        k_pos = rows.start + lax.broadcasted_iota(I32, score.shape, 0)
        key_s[rows, :] = jnp.where(k_pos < limit, score, -jnp.inf)

    def count(mask):
        return _reduce_rows(jnp.sum, jnp.where(mask, 1.0, 0.0))

    def count_keys(test):
        acc = [jnp.zeros((ROW_FOLD, Q_BLOCK), F32) for _ in range(COUNT_CHAINS)]
        for n, r0 in enumerate(range(0, s_eff, ROW_FOLD)):
            hit = jnp.where(test(key_s[r0:r0 + ROW_FOLD, :]), 1.0, 0.0)
            acc[n % COUNT_CHAINS] = acc[n % COUNT_CHAINS] + hit
        return jnp.sum(functools.reduce(lambda x, y: x + y, acc), axis=0, keepdims=True)

    def ordered_to_float(c):
        s = c ^ INT_MIN
        return pltpu.bitcast(jnp.where(s < 0, INT_MIN | (-s), s), F32)

    def bisect(it, r):
        c = r | jnp.left_shift(jnp.int32(1), 31 - it)
        cand = ordered_to_float(c)
        return jnp.where(count_keys(lambda slab: slab >= cand) >= topk, c, r)

    def select_by_threshold():
        thr = ordered_to_float(lax.fori_loop(0, 32, bisect, jnp.zeros((1, Q_BLOCK), I32)))

        n_ge = count_keys(lambda slab: slab >= thr)
        clean = jnp.max(jnp.where((n_ge == topk) & (thr > -jnp.inf), 0.0, 1.0)) == 0.0

        @pl.when(clean)
        def _():
            for rows in chunks:
                bias_s[rows, :] = jnp.where(key_s[rows, :] >= thr, 0.0, -jnp.inf)

        @pl.when(jnp.logical_not(clean))
        def _():
            key = key_s[:s_eff, :]
            need = topk - count(key > thr)
            rr = lax.broadcasted_iota(I32, (LANES, LANES), 0)
            cc = lax.broadcasted_iota(I32, (LANES, LANES), 1)
            tri = jnp.where(cc < rr, 1.0, 0.0).astype(BF16)
            seen = jnp.zeros((1, Q_BLOCK), F32)
            for c in range(s_eff // LANES):
                kc = key[c * LANES:(c + 1) * LANES, :]
                eq = kc == thr
                eqf = jnp.where(eq, 1.0, 0.0)
                rank = _dot(tri, eqf.astype(BF16)) + seen
                seen = seen + jnp.sum(eqf, axis=0, keepdims=True)
                tie_bias = jnp.where(jnp.where(eq, rank, topk) < need, 0.0, -jnp.inf)
                bias = jnp.where(kc > thr, 0.0, tie_bias)
                k_pos = c * LANES + lax.broadcasted_iota(I32, (LANES, Q_BLOCK), 0)
                bias_s[c * LANES:(c + 1) * LANES, :] = jnp.where(k_pos < limit, bias, -jnp.inf)

    if s_eff > topk:
        select_by_threshold()
    else:
        for rows in chunks:
            k_pos = rows.start + lax.broadcasted_iota(I32, (rows.stop - rows.start, Q_BLOCK), 0)
            bias_s[rows, :] = jnp.where(k_pos < limit, 0.0, -jnp.inf)

    qnw = qnw_ref[...]
    crow = lax.broadcasted_iota(I32, (LANES - HEAD_DIM_A, Q_BLOCK), 0)
    r_k = lax.broadcasted_iota(I32, (Q_BLOCK, Q_BLOCK), 0)
    c_q = lax.broadcasted_iota(I32, (Q_BLOCK, Q_BLOCK), 1)
    after = 2.0 * jnp.maximum(r_k - c_q, 0).astype(F32)
    band = pl.ds(pl.multiple_of(i * Q_BLOCK, Q_BLOCK), Q_BLOCK)
    slope = [sum(_bf16_pieces(LOG2E * 2.0 ** -(h + 1), SLOPE_PIECES)) for h in range(N_HEADS_A)]
    row_max = []
    for h in range(N_HEADS_A):
        pieces = _bf16_pieces(LOG2E * 2.0 ** -(h + 1), SLOPE_PIECES)
        qh = q_t[h * HEAD_DIM_A:(h + 1) * HEAD_DIM_A, :]
        ms = jnp.mean(qh * qh, axis=0, keepdims=True)
        qn = (qh * lax.rsqrt(ms + EPS) * qnw) * (HEAD_DIM_A ** -0.5) * LOG2E
        slopes = jnp.zeros(crow.shape, F32)
        for k, piece in enumerate(pieces):
            slopes = jnp.where(crow == 2 * k, POS_RADIX * piece, jnp.where(crow == 2 * k + 1, piece, slopes))
        qa = jnp.concatenate([qn, slopes], axis=0).astype(BF16)
        folded = []
        for rows in chunks:
            logits = _dot(kn_s[rows, :], qa) + bias_s[rows, :]
            lg_s[h, rows, :] = logits
            folded.append(jnp.max(logits.reshape(-1, ROW_FOLD, Q_BLOCK), axis=0))
        row_max.append(jnp.max(functools.reduce(jnp.maximum, folded), axis=0, keepdims=True))
    for h in range(N_HEADS_A):
        lg_s[h, band, :] = lg_s[h, band, :] - slope[h] * after
    for h in range(N_HEADS_A):
        acc = jnp.zeros((KV_DIM_A + V_ONES, Q_BLOCK), F32)
        for rows in chunks:
            p = jnp.exp2(lg_s[h, rows, :] - row_max[h])
            acc = acc + _dot(vt_s[:, rows], p.astype(BF16))
        ot_s[h * KV_DIM_A:(h + 1) * KV_DIM_A, :] = acc[:KV_DIM_A, :] / acc[KV_DIM_A:KV_DIM_A + 1, :]
    o_ref[...] = ot_s[...].T


def _attn_kernel(qa_ref, kvq_ref, kv_ref, qnw_ref, knw_ref, o_ref, kn_s, vt_s, ki_s, key_s, bias_s, lg_s, ot_s):
    i = pl.program_id(1)
    S = kv_ref.shape[0]
    nb = S // Q_BLOCK
    topk = float(min(TOPK_MAX, S // 4))

    @pl.when(i == 0)
    def _():
        kv = kv_ref[:, :LANES]
        lane = lax.broadcasted_iota(I32, kv.shape, 1)
        pos = lax.broadcasted_iota(I32, kv.shape, 0)
        ms = jnp.sum(jnp.where(lane < KV_DIM_A, kv * kv, 0.0), axis=-1, keepdims=True) * (1.0 / KV_DIM_A)
        kn = kv * lax.rsqrt(ms + EPS) * knw_ref[...]
        digit = jnp.where((lane & 1) == 0, pos // POS_RADIX, pos % POS_RADIX).astype(F32)
        is_digit = (lane >= KV_DIM_A) & (lane < KV_DIM_A + 2 * SLOPE_PIECES)
        kn_s[...] = jnp.where(is_digit, digit, kn).astype(BF16)
        vt_s[:KV_DIM_A, :] = kv.T[V_OFF:V_OFF + KV_DIM_A, :].astype(BF16)
        vt_s[KV_DIM_A:, :] = jnp.ones((V_ONES, kv.shape[0]), BF16)
        ki_s[...] = kv_ref[:, KI_OFF:KI_OFF + IDX_DIM].astype(BF16)

    n_buckets = min(ATTN_BUCKETS, nb)
    per = nb // n_buckets
    for j in range(n_buckets):
        pl.when(i // per == j)(functools.partial(
            _attn_block, (j + 1) * per * Q_BLOCK, topk, i, qa_ref, kvq_ref, qnw_ref, o_ref,
            kn_s, vt_s, ki_s, key_s, bias_s, lg_s, ot_s))


def _attn(qa, kv, q_norm_w, k_norm_w, B, S):
    N = qa.shape[0]
    nb = S // Q_BLOCK
    return pl.pallas_call(
        _attn_kernel,
        grid=(B, nb),
        in_specs=[pl.BlockSpec((Q_BLOCK, QA_W), lambda b, i: (b * nb + i, 0)),
                  pl.BlockSpec((Q_BLOCK, KV_W), lambda b, i: (b * nb + i, 0)),
                  pl.BlockSpec((S, KV_W), lambda b, i: (b, 0)),
                  pl.BlockSpec((HEAD_DIM_A, 1), lambda b, i: (0, 0)),
                  pl.BlockSpec((1, LANES), lambda b, i: (0, 0))],
        out_specs=pl.BlockSpec((Q_BLOCK, ATTN_OUT), lambda b, i: (b * nb + i, 0)),
        out_shape=jax.ShapeDtypeStruct((N, ATTN_OUT), F32),
        scratch_shapes=[pltpu.VMEM((S, LANES), BF16),
                        pltpu.VMEM((KV_DIM_A + V_ONES, S), BF16),
                        pltpu.VMEM((S, IDX_DIM), BF16),
                        pltpu.VMEM((S, Q_BLOCK), F32),
                        pltpu.VMEM((S, Q_BLOCK), F32),
                        pltpu.VMEM((N_HEADS_A, S, Q_BLOCK), F32),
                        pltpu.VMEM((ATTN_OUT, Q_BLOCK), F32)],
        compiler_params=_params("arbitrary", "arbitrary"),
        name="attn",
    )(qa, kv, kv, q_norm_w.reshape(HEAD_DIM_A, 1),
      jnp.pad(k_norm_w, (0, LANES - KV_DIM_A)).reshape(1, LANES))


SUBLANES = 8


def _shift_rows(x, d, fill, row):
    if d % SUBLANES == 0:
        return jnp.concatenate([jnp.full((d, x.shape[1]), fill, x.dtype), x[:-d]], axis=0)
    return jnp.where(row >= d, pltpu.roll(x, d, 0), fill)


def _sigmoid(x):
    return 0.5 * (1.0 + jnp.tanh(0.5 * x))


def _doubling_scan(a, b, row):
    n = a.shape[0]
    d = 1
    while d < n:
        b = a * _shift_rows(b, d, 0.0, row) + b
        if 2 * d < n:
            a = a * _shift_rows(a, d, 1.0, row)
        d *= 2
    return b


def _lru_kernel(xb_ref, gb_ref, cw_ref, cb_ref, wr_ref, br_ref, wi_ref, bi_ref, lam_ref, o_ref, a_s, b_s):
    x = xb_ref[...]
    S = x.shape[0]
    row = lax.broadcasted_iota(I32, x.shape, 0)
    xc = cb_ref[...] + _shift_rows(x, CONV_WIDTH - 1, 0.0, row) * cw_ref[0:1, :]
    for j in range(1, CONV_WIDTH):
        d = CONV_WIDTH - 1 - j
        xs = x if d == 0 else _shift_rows(x, d, 0.0, row)
        xc = xc + xs * cw_ref[j:j + 1, :]
    xcb = xc.astype(BF16)
    r = _sigmoid(_dot(xcb, wr_ref[0]) + br_ref[...])
    ig = _sigmoid(_dot(xcb, wi_ref[0]) + bi_ref[...])
    z = -lam_ref[...]
    softplus = jnp.maximum(z, 0.0) + jnp.log1p(jnp.exp(-jnp.abs(z)))
    log_a = -LRU_C * r * softplus
    a = jnp.exp(log_a)
    b = jnp.sqrt(-jnp.tanh(log_a) * (1.0 + a * a)) * (ig * xc)
    sub = row & (SUBLANES - 1)
    d = 1
    while d < SUBLANES:
        keep = sub >= d
        b = a * jnp.where(keep, pltpu.roll(b, d, 0), 0.0) + b
        a = a * jnp.where(keep, pltpu.roll(a, d, 0), 1.0)
        d *= 2
    a_s[...] = a
    b_s[...] = b
    groups = S // SUBLANES
    last = pl.ds(SUBLANES - 1, groups, stride=SUBLANES)
    g_row = lax.broadcasted_iota(I32, (groups, x.shape[1]), 0)
    h_last = _doubling_scan(a_s[last, :], b_s[last, :], g_row)
    carry = _shift_rows(h_last, 1, 0.0, g_row)
    carry = jnp.broadcast_to(carry[:, None, :], (groups, SUBLANES, x.shape[1])).reshape(x.shape)
    h = b_s[...] + a_s[...] * carry
    g = gb_ref[...]
    gelu = 0.5 * g * (1.0 + jnp.tanh(float(np.sqrt(2.0 / np.pi)) * (g + 0.044715 * (g * g * g))))
    o_ref[...] = h * gelu


def _lru(lru, conv_w, conv_b, wr_bd, b_rec, wi_bd, b_in, lam, B, S):
    N = lru.shape[0]
    nt = LRU_WIDTH // LANES
    vec = lambda b, j: (0, j)
    return pl.pallas_call(
        _lru_kernel,
        grid=(B, nt),
        in_specs=[pl.BlockSpec((S, LANES), lambda b, j: (b, j)),
                  pl.BlockSpec((S, LANES), lambda b, j: (b, nt + j)),
                  pl.BlockSpec((CONV_WIDTH, LANES), vec),
                  pl.BlockSpec((1, LANES), vec),
                  pl.BlockSpec((1, LANES, LANES), lambda b, j: (j, 0, 0)),
                  pl.BlockSpec((1, LANES), vec),
                  pl.BlockSpec((1, LANES, LANES), lambda b, j: (j, 0, 0)),
                  pl.BlockSpec((1, LANES), vec),
                  pl.BlockSpec((1, LANES), vec)],
        out_specs=pl.BlockSpec((S, LANES), lambda b, j: (b, j)),
        out_shape=jax.ShapeDtypeStruct((N, LRU_WIDTH), F32),
        scratch_shapes=[pltpu.VMEM((S, LANES), F32), pltpu.VMEM((S, LANES), F32)],
        compiler_params=_params("arbitrary", "arbitrary"),
        name="lru",
    )(lru, lru, conv_w, conv_b.reshape(1, -1), wr_bd, b_rec.reshape(1, -1), wi_bd,
      b_in.reshape(1, -1), lam.reshape(1, -1))


def _block_diag_tiles(w):
    per = LANES // LRU_BLOCK_DIM
    nt = LRU_WIDTH // LANES
    w5 = w.reshape(nt, per, LRU_BLOCK_DIM, 1, LRU_BLOCK_DIM)
    eye = jnp.eye(per, dtype=w.dtype).reshape(1, per, 1, per, 1)
    return (w5 * eye).reshape(nt, LANES, LANES)


def _first_index_of_max(vals, lane):
    m = jnp.max(vals, axis=-1, keepdims=True)
    idx = jnp.min(jnp.where(vals == m, lane, 4 * LANES), axis=-1, keepdims=True)
    return m, idx


def _weight_pieces(out, w, lane, first):
    rest = w
    for k in range(W_PIECES):
        piece = rest.astype(BF16).astype(F32)
        out = jnp.where(lane == first + k, piece, out)
        rest = rest - piece
    return out


def _merge_kernel(attn_ref, lru_ref, g_ref, x_ref, mod_ref, wpa_ref, wpb_ref, wo_ref, nw_ref, wr_ref,
                  br_ref, x1_ref, hsl_ref, route_ref, gc_ref):
    D = x_ref.shape[1]
    tm = x_ref.shape[0]

    y_a = _dot(attn_ref[...].astype(BF16), wpa_ref[...])
    y_b = _dot(lru_ref[...].astype(BF16), wpb_ref[...])
    merged = g_ref[:, :D].astype(F32) * y_a + g_ref[:, D:].astype(F32) * y_b
    x1 = x_ref[...] + mod_ref[0, 2:3, :] * _dot(merged.astype(BF16), wo_ref[...])
    x1_ref[...] = x1
    h2 = _rms_mod(x1, nw_ref[...], mod_ref[0, 3:4, :], mod_ref[0, 4:5, :])
    h2b = h2.astype(BF16)

    h2l = (h2 - h2b.astype(F32)).astype(BF16)
    hi_hi_lo = _dot(h2b, wr_ref[...])
    logits = hi_hi_lo[:, :LANES] + (_dot(h2l, wr_ref[:, :LANES]) + hi_hi_lo[:, LANES:]) + br_ref[...]
    lane = lax.broadcasted_iota(I32, logits.shape, 1)
    gl = jnp.where(lane < N_GROUPS, logits, -jnp.inf)
    gmax, g_sel = _first_index_of_max(gl, lane)
    g_weight = 1.0 / jnp.sum(jnp.exp(gl - gmax), axis=-1, keepdims=True)
    e_lo = N_GROUPS + g_sel * EXPERTS_PER_GROUP
    el = jnp.where((lane >= e_lo) & (lane < e_lo + EXPERTS_PER_GROUP), logits, -jnp.inf)
    v1, i1 = _first_index_of_max(el, lane)
    el2 = jnp.where(lane == i1, -jnp.inf, el)
    v2, i2 = _first_index_of_max(el2, lane)
    e2x = jnp.exp(v2 - v1)
    w1 = g_weight / (1.0 + e2x)
    w2 = g_weight * e2x / (1.0 + e2x)

    hot1 = lane == i1
    hot2 = lane == i2
    hot = jnp.where(hot1, 1.0, jnp.where(hot2, 1.0, 0.0))
    gcnt = jnp.floor((jnp.sum(hot, axis=0, keepdims=True) + (GRAN - 1.0)) * (1.0 / GRAN))
    gc_ref[0] = gcnt
    rr = lax.broadcasted_iota(I32, (LANES, LANES), 0)
    cc = lax.broadcasted_iota(I32, (LANES, LANES), 1)
    upper = jnp.where(rr < cc, 1.0, 0.0).astype(BF16)
    run_start = GRAN * _dot(jnp.broadcast_to(gcnt, (8, LANES)).astype(BF16), upper)[0:1, :]
    rr = lax.broadcasted_iota(I32, (tm, tm), 0)
    cc = lax.broadcasted_iota(I32, (tm, tm), 1)
    lower = jnp.where(cc < rr, 1.0, 0.0).astype(BF16)
    pos = _dot(lower, hot.astype(BF16)) + run_start
    pos1 = jnp.where(hot1, pos, 0.0)
    pos2 = jnp.where(hot2, pos, 0.0)
    lp1 = jnp.sum(pos1, axis=-1, keepdims=True)
    lp2 = jnp.sum(pos2, axis=-1, keepdims=True)
    route_ref[...] = jnp.where(lane == R_LP1, lp1, jnp.where(lane == R_LP2, lp2, 0.0))

    aux = jnp.where(lane == AUX_E1, (i1 - N_GROUPS).astype(F32), 0.0)
    aux = _weight_pieces(_weight_pieces(aux, w1, lane, AUX_W1), w2, lane, AUX_W2)
    lp1r = jnp.sum(pos1.T, axis=0, keepdims=True).astype(I32)
    lp2r = jnp.sum(pos2.T, axis=0, keepdims=True).astype(I32)
    rows = lax.broadcasted_iota(I32, (LOCAL_ROWS, tm), 0)
    sort = jnp.where(rows == lp1r, 1.0, jnp.where(rows == lp2r, 1.0, 0.0)).astype(BF16)
    hsl_ref[...] = _dot(sort, jnp.concatenate([h2b, aux.astype(BF16)], axis=1)).astype(BF16)


def _merge(attn, lru_out, gates, x2, mod3, wpa, wpb, wo, norm_w, wr_hi_lo, b_router, S):
    N, D = x2.shape
    tm = TM_SORT
    n_tiles = N // tm
    steps_per_batch = S // tm
    const = lambda t: (0, 0)
    rows = lambda t: (t, 0)
    return pl.pallas_call(
        _merge_kernel,
        grid=(n_tiles,),
        in_specs=[pl.BlockSpec((tm, ATTN_OUT), rows),
                  pl.BlockSpec((tm, LRU_WIDTH), rows),
                  pl.BlockSpec((tm, 2 * D), rows),
                  pl.BlockSpec((tm, D), rows),
                  pl.BlockSpec((1, 6, D), lambda t: (t // steps_per_batch, 0, 0)),
                  pl.BlockSpec(wpa.shape, const),
                  pl.BlockSpec(wpb.shape, const),
                  pl.BlockSpec(wo.shape, const),
                  pl.BlockSpec((1, D), const),
                  pl.BlockSpec(wr_hi_lo.shape, const),
                  pl.BlockSpec((1, LANES), const)],
        out_specs=[pl.BlockSpec((tm, D), rows),
                   pl.BlockSpec((LOCAL_ROWS, D + AUX_W), rows),
                   pl.BlockSpec((tm, LANES), rows),
                   pl.BlockSpec((1, 1, LANES), lambda t: (t, 0, 0))],
        out_shape=[jax.ShapeDtypeStruct((N, D), F32),
                   jax.ShapeDtypeStruct((n_tiles * LOCAL_ROWS, D + AUX_W), BF16),
                   jax.ShapeDtypeStruct((N, LANES), F32),
                   jax.ShapeDtypeStruct((n_tiles, 1, LANES), F32)],
        compiler_params=_params("arbitrary"),
        name="merge",
    )(attn, lru_out, gates, x2, mod3, wpa, wpb, wo, norm_w.reshape(1, D), wr_hi_lo, b_router)


def _plan_sizes(n_tiles):
    max_gran = n_tiles * LOCAL_GRAN + N_EXPERTS * (GRAN_PER_TILE - 1)
    max_row_tiles = -(-max_gran // GRAN_PER_TILE)
    return max_row_tiles * GRAN_PER_TILE, max_row_tiles


def _plan_kernel(gc_ref, src_ref, texp_ref, nt_ref, dst_ref, local_next):
    n_tiles = gc_ref.shape[0]
    max_gran, max_row_tiles = _plan_sizes(n_tiles)

    def zero(ref, lo, hi):
        def body(k, c):
            ref[k] = 0
            return c
        lax.fori_loop(lo, hi, body, 0)

    zero(local_next, 0, n_tiles)

    def per_expert(e, g):
        first_tile = g // GRAN_PER_TILE

        def per_tile(t, g):
            n = gc_ref[t, N_GROUPS + e]
            base = t * LOCAL_GRAN + local_next[t]

            def per_granule(k, g):
                src_ref[g] = base + k
                dst_ref[base + k] = g
                return g + 1

            g = lax.fori_loop(0, n, per_granule, g)
            local_next[t] = local_next[t] + n
            return g

        g = lax.fori_loop(0, n_tiles, per_tile, g)
        padded = (g + GRAN_PER_TILE - 1) // GRAN_PER_TILE * GRAN_PER_TILE
        zero(src_ref, g, padded)
        g = padded

        def mark(j, c):
            texp_ref[j] = e
            return c

        lax.fori_loop(first_tile, g // GRAN_PER_TILE, mark, 0)
        return g

    g = lax.fori_loop(0, N_EXPERTS, per_expert, 0)
    used = g // GRAN_PER_TILE
    nt_ref[0] = used
    zero(src_ref, g, max_gran)

    def clear_tail(t, c):
        zero(dst_ref, t * LOCAL_GRAN + local_next[t], (t + 1) * LOCAL_GRAN)
        return c

    lax.fori_loop(0, n_tiles, clear_tail, 0)

    def mark_rest(j, c):
        texp_ref[j] = N_EXPERTS - 1
        return c

    lax.fori_loop(used, max_row_tiles, mark_rest, 0)


def _plan(gc):
    n_tiles = gc.shape[0]
    max_gran, max_row_tiles = _plan_sizes(n_tiles)
    smem = pl.BlockSpec(memory_space=pltpu.SMEM)
    return pl.pallas_call(
        _plan_kernel,
        in_specs=[smem],
        out_specs=[smem, smem, smem, smem, smem],
        out_shape=[jax.ShapeDtypeStruct((max_gran,), I32),
                   jax.ShapeDtypeStruct((max_row_tiles,), I32),
                   jax.ShapeDtypeStruct((1,), I32),
                   jax.ShapeDtypeStruct((n_tiles * LOCAL_GRAN,), I32),
                   jax.ShapeDtypeStruct((n_tiles,), I32)],
        name="plan",
    )(gc)


def _granule_copies(idx_ref, first, count, src_hbm, dst_vmem, sem, wait):
    def body(k, c):
        g = idx_ref[first + k]
        cp = pltpu.make_async_copy(src_hbm.at[pl.ds(pl.multiple_of(g * GRAN, GRAN), GRAN)],
                                   dst_vmem.at[pl.ds(pl.multiple_of(k * GRAN, GRAN), GRAN)], sem)
        if wait:
            cp.wait()
        else:
            cp.start()
        return c
    lax.fori_loop(0, count, body, 0)


def _moe_kernel(src_ref, texp_ref, nt_ref, hsl_ref, w1_ref, w3_ref, w2_ref, ys_ref, xbuf, w1b, w3b, w2b, sem):
    i = pl.program_id(0)
    used = nt_ref[0]
    slot = i % MOE_BUFFERS
    D = ys_ref.shape[1]

    def gather(tile, wait):
        slot_ = tile % MOE_BUFFERS
        _granule_copies(src_ref, tile * GRAN_PER_TILE, GRAN_PER_TILE, hsl_ref, xbuf.at[slot_], sem.at[slot_], wait)

    @pl.when(i == 0)
    def _():
        for ahead in range(MOE_BUFFERS - 1):
            pl.when(ahead < used)(functools.partial(gather, ahead, False))

    @pl.when(i + (MOE_BUFFERS - 1) < used)
    def _():
        gather(i + (MOE_BUFFERS - 1), False)

    @pl.when(i < used)
    def _():
        gather(i, True)

        @pl.when((i == 0) | (texp_ref[i] != texp_ref[jnp.maximum(i - 1, 0)]))
        def _():
            w1b[...] = w1_ref[0].astype(BF16)
            w3b[...] = w3_ref[0].astype(BF16)
            w2b[...] = w2_ref[0].astype(BF16)

        xb = xbuf[slot]
        hb = xb[:, :D]
        aux = xb[:, D:].astype(F32)
        first = aux[:, AUX_E1:AUX_E1 + 1] == texp_ref[i].astype(F32)
        w_row = jnp.zeros((TM_MOE, 1), F32)
        for k in range(W_PIECES):
            w_row = w_row + jnp.where(first, aux[:, AUX_W1 + k:AUX_W1 + k + 1], aux[:, AUX_W2 + k:AUX_W2 + k + 1])
        a = _dot(hb, w1b[...])
        act = (a * _sigmoid(a)) * _dot(hb, w3b[...])
        ys_ref[...] = (w_row * _dot(act.astype(BF16), w2b[...])).astype(BF16)

    @pl.when(i >= used)
    def _():
        ys_ref[...] = jnp.zeros_like(ys_ref)


def _moe(src, texp, used, hsl, w1, w3, w2):
    max_row_tiles = texp.shape[0]
    D = w1.shape[1]
    F = w1.shape[2]
    grid_spec = pltpu.PrefetchScalarGridSpec(
        num_scalar_prefetch=3,
        grid=(max_row_tiles,),
        in_specs=[pl.BlockSpec(memory_space=pl.ANY),
                  pl.BlockSpec((1, D, F), lambda i, s, te, nt: (te[i], 0, 0)),
                  pl.BlockSpec((1, D, F), lambda i, s, te, nt: (te[i], 0, 0)),
                  pl.BlockSpec((1, F, D), lambda i, s, te, nt: (te[i], 0, 0))],
        out_specs=pl.BlockSpec((TM_MOE, D), lambda i, s, te, nt: (i, 0)),
        scratch_shapes=[pltpu.VMEM((MOE_BUFFERS, TM_MOE, D + AUX_W), BF16),
                        pltpu.VMEM((D, F), BF16), pltpu.VMEM((D, F), BF16), pltpu.VMEM((F, D), BF16),
                        pltpu.SemaphoreType.DMA((MOE_BUFFERS,))],
    )
    return pl.pallas_call(
        _moe_kernel,
        grid_spec=grid_spec,
        out_shape=jax.ShapeDtypeStruct((max_row_tiles * TM_MOE, D), BF16),
        compiler_params=_params("arbitrary"),
        name="moe",
    )(src, texp, used, hsl, w1, w3, w2)


def _combine_kernel(dst_ref, lcnt_ref, ys_ref, x1_ref, route_ref, mod_ref, o_ref, ybuf, sem):
    t = pl.program_id(0)
    slot = t % 2
    tm = x1_ref.shape[0]

    def gather(tile, slot_, wait):
        _granule_copies(dst_ref, tile * LOCAL_GRAN, lcnt_ref[tile], ys_ref, ybuf.at[slot_], sem.at[slot_], wait)

    @pl.when(t == 0)
    def _():
        ybuf[...] = jnp.zeros_like(ybuf)
        gather(0, 0, False)

    @pl.when(t + 1 < pl.num_programs(0))
    def _():
        gather(t + 1, 1 - slot, False)

    gather(t, slot, True)
    lp1 = route_ref[:, R_LP1:R_LP1 + 1].astype(I32)
    lp2 = route_ref[:, R_LP2:R_LP2 + 1].astype(I32)
    col = lax.broadcasted_iota(I32, (tm, LOCAL_ROWS), 1)
    unsort = jnp.where(col == lp1, 1.0, jnp.where(col == lp2, 1.0, 0.0)).astype(BF16)
    o_ref[...] = x1_ref[...] + mod_ref[0, 5:6, :] * _dot(unsort, ybuf[slot])


def _combine(dst, lcnt, ys, x1, route, mod3, S):
    N, D = x1.shape
    tm = TM_SORT
    steps_per_batch = S // tm
    grid_spec = pltpu.PrefetchScalarGridSpec(
        num_scalar_prefetch=2,
        grid=(N // tm,),
        in_specs=[pl.BlockSpec(memory_space=pl.ANY),
                  pl.BlockSpec((tm, D), lambda t, d, c: (t, 0)),
                  pl.BlockSpec((tm, LANES), lambda t, d, c: (t, 0)),
                  pl.BlockSpec((1, 6, D), lambda t, d, c: (t // steps_per_batch, 0, 0))],
        out_specs=pl.BlockSpec((tm, D), lambda t, d, c: (t, 0)),
        scratch_shapes=[pltpu.VMEM((2, LOCAL_ROWS, D), BF16), pltpu.SemaphoreType.DMA((2,))],
    )
    return pl.pallas_call(
        _combine_kernel,
        grid_spec=grid_spec,
        out_shape=jax.ShapeDtypeStruct((N, D), F32),
        compiler_params=_params("arbitrary"),
        name="combine",
    )(dst, lcnt, ys, x1, route, mod3)


def _layer(x2, mod3, B, S, norm_mix_w, w_in, q_norm_w, k_norm_w, conv_w, conv_b, w_rec_gate, b_rec_gate,
           w_in_gate, b_in_gate, lru_lambda, w_proj_a, w_proj_b, w_out, norm_ffn_w, w_group, b_group,
           w_expert_router, b_expert_router, w1, w3, w2):
    N, D = x2.shape
    qa, kv, lru, gates = _proj(x2, mod3, norm_mix_w, _pad_cast(w_in), S)
    attn = _attn(qa, kv, q_norm_w, k_norm_w, B, S)
    lru_out = _lru(lru, conv_w, conv_b, _block_diag_tiles(w_rec_gate).astype(BF16), b_rec_gate,
                   _block_diag_tiles(w_in_gate).astype(BF16), b_in_gate, lru_lambda, B, S)

    n_r = N_GROUPS + N_EXPERTS
    w_router = jnp.concatenate([w_group, w_expert_router, jnp.zeros((D, LANES - n_r), F32)], axis=1)
    wr_hi = w_router.astype(BF16)
    wr_lo = (w_router - wr_hi.astype(F32)).astype(BF16)
    b_router = jnp.concatenate([b_group, b_expert_router, jnp.zeros((LANES - n_r,), F32)]).reshape(1, LANES)
    x1, hsl, route, gc = _merge(attn, lru_out, gates, x2, mod3, w_proj_a.astype(BF16), w_proj_b.astype(BF16),
                                w_out.astype(BF16), norm_ffn_w, jnp.concatenate([wr_hi, wr_lo], axis=1),
                                b_router, S)

    src, texp, used, dst, lcnt = _plan(gc.reshape(gc.shape[0], LANES).astype(I32))
    ys = _moe(src, texp, used, hsl, w1, w3, w2)
    return _combine(dst, lcnt, ys, x1, route, mod3, S)


def kernel(x, c, ada_w, ada_b, norm_mix_w, w_in, q_norm_w, k_norm_w, conv_w, conv_b, w_rec_gate, b_rec_gate,
           w_in_gate, b_in_gate, lru_lambda, w_proj_a, w_proj_b, w_out, norm_ffn_w, w_group, b_group,
           w_expert_router, b_expert_router, w1, w3, w2):
    B, S, D = x.shape
    x2 = x.reshape(B * S, D)
    for l in range(ada_w.shape[0]):
        mod3 = _ada(c, ada_w[l], ada_b[l]).reshape(B, 6, D)
        x2 = _layer(x2, mod3, B, S, norm_mix_w[l], w_in[l], q_norm_w[l], k_norm_w[l], conv_w[l], conv_b[l],
                    w_rec_gate[l], b_rec_gate[l], w_in_gate[l], b_in_gate[l], lru_lambda[l], w_proj_a[l],
                    w_proj_b[l], w_out[l], norm_ffn_w[l], w_group[l], b_group[l], w_expert_router[l],
                    b_expert_router[l], w1[l], w3[l], w2[l])
    return x2.reshape(B, S, D)
```

```python
import functools

import jax
import jax.numpy as jnp
import numpy as np
from jax import lax
from jax.experimental import pallas as pl
from jax.experimental.pallas import tpu as pltpu

F32 = jnp.float32
BF16 = jnp.bfloat16
I32 = jnp.int32

EPS = 1e-6
CHUNK = 64
Q_BLOCK = 128
N_HEADS_A = 8
HEAD_DIM_A = 64
KV_DIM_A = 64
ATTN_OUT = N_HEADS_A * KV_DIM_A
N_IDX_HEADS = 4
IDX_DIM = 64
TOPK_MAX = 256
LRU_WIDTH = 512
LRU_BLOCKS = 8
LRU_BLOCK_DIM = LRU_WIDTH // LRU_BLOCKS
CONV_WIDTH = 4
LRU_C = 8.0
N_GROUPS = 4
EXPERTS_PER_GROUP = 8
N_EXPERTS = N_GROUPS * EXPERTS_PER_GROUP
D_FF_EXPERT = 256

LANES = 128
INT_MIN = -2 ** 31
VMEM_LIMIT = 56 * 1024 * 1024

D_MODEL = 1024
COL_Q = 0
COL_K = COL_Q + N_HEADS_A * HEAD_DIM_A
COL_V = COL_K + KV_DIM_A
COL_QI = COL_V + KV_DIM_A
COL_KI = COL_QI + N_IDX_HEADS * IDX_DIM
COL_WI = COL_KI + IDX_DIM
COL_LX = COL_WI + N_IDX_HEADS
D_IN = COL_LX + 2 * LRU_WIDTH + 2 * D_MODEL
ATTN_COLS = -(-COL_LX // LANES) * LANES
TAIL_COLS = D_IN - COL_LX
TAIL_SHIFT = COL_LX % LANES
W_IN_PAD = (COL_LX // LANES) * LANES + TAIL_COLS + LANES
assert COL_K % LANES == 0 and COL_QI % LANES == 0 and COL_KI % LANES == 0 and TAIL_COLS % LANES == 0

QA_W = N_HEADS_A * HEAD_DIM_A + N_IDX_HEADS * IDX_DIM
KV_W = (COL_QI - COL_K) + (ATTN_COLS - COL_KI)
K_OFF, V_OFF = 0, KV_DIM_A
KI_OFF = COL_QI - COL_K
WI_OFF = KI_OFF + IDX_DIM

TM_PROJ = 512

TM_SORT = 512
GRAN = 16
TM_MOE = 256
GRAN_PER_TILE = TM_MOE // GRAN
MOE_BUFFERS = 3
LOCAL_ROWS = -(-(2 * TM_SORT + N_EXPERTS * (GRAN - 1)) // LANES) * LANES
LOCAL_GRAN = LOCAL_ROWS // GRAN
AUX_W = LANES
W_PIECES = 3
AUX_W1, AUX_W2 = 0, W_PIECES
AUX_E1 = 2 * W_PIECES

R_LP1, R_LP2 = 0, 1


def _dot(a, b, **kw):
    return jnp.dot(a, b, preferred_element_type=F32, **kw)


def _params(*sem):
    return pltpu.CompilerParams(dimension_semantics=sem, vmem_limit_bytes=VMEM_LIMIT)


def _ada_kernel(c_ref, w_ref, b_ref, o_ref):
    c = c_ref[...]
    cond = c * jax.nn.sigmoid(c)
    o_ref[...] = _dot(cond, w_ref[...], precision=lax.Precision.HIGHEST) + b_ref[...]


def _ada(c, ada_w, ada_b):
    B, D = c.shape
    n_out = ada_w.shape[1]
    tn = 1024
    return pl.pallas_call(
        _ada_kernel,
        grid=(n_out // tn,),
        in_specs=[pl.BlockSpec((B, D), lambda j: (0, 0)),
                  pl.BlockSpec((D, tn), lambda j: (0, j)),
                  pl.BlockSpec((1, tn), lambda j: (0, j))],
        out_specs=pl.BlockSpec((B, tn), lambda j: (0, j)),
        out_shape=jax.ShapeDtypeStruct((B, n_out), F32),
        compiler_params=_params("arbitrary"),
        name="ada",
    )(c, ada_w, ada_b.reshape(1, n_out))


def _rms_mod(x, w, shift, scale):
    ms = jnp.mean(x * x, axis=-1, keepdims=True)
    y = x * lax.rsqrt(ms + EPS) * w
    return y * (1.0 + scale) + shift


def _proj_kernel(x_ref, mod_ref, nw_ref, w_ref, oqa_ref, okv_ref, olru_ref, og_ref, wt_s):
    @pl.when(pl.program_id(0) == 0)
    def _():
        rr = lax.broadcasted_iota(I32, (2 * LANES, LANES), 0)
        cc = lax.broadcasted_iota(I32, (2 * LANES, LANES), 1)
        shift = jnp.where(rr == cc + TAIL_SHIFT, 1.0, 0.0).astype(BF16)
        for j in range(TAIL_COLS // LANES):
            src = (COL_LX // LANES + j) * LANES
            wt_s[:, j * LANES:(j + 1) * LANES] = _dot(w_ref[:, src:src + 2 * LANES], shift).astype(BF16)

    h = _rms_mod(x_ref[...], nw_ref[...], mod_ref[0, 0:1, :], mod_ref[0, 1:2, :])
    hb = h.astype(BF16)
    pa = _dot(hb, w_ref[:, :ATTN_COLS])
    oqa_ref[...] = jnp.concatenate([pa[:, COL_Q:COL_K], pa[:, COL_QI:COL_KI]], axis=1)
    okv_ref[...] = jnp.concatenate([pa[:, COL_K:COL_QI], pa[:, COL_KI:ATTN_COLS]], axis=1)
    olru_ref[...] = _dot(hb, wt_s[:, :2 * LRU_WIDTH])
    og_ref[...] = _sigmoid(_dot(hb, wt_s[:, 2 * LRU_WIDTH:])).astype(BF16)


def _proj(x2, mod3, norm_w, w_all, S):
    N, D = x2.shape
    tm = TM_PROJ
    steps_per_batch = S // tm
    const = lambda t: (0, 0)
    return pl.pallas_call(
        _proj_kernel,
        grid=(N // tm,),
        in_specs=[pl.BlockSpec((tm, D), lambda t: (t, 0)),
                  pl.BlockSpec((1, 6, D), lambda t: (t // steps_per_batch, 0, 0)),
                  pl.BlockSpec((1, D), const),
                  pl.BlockSpec(w_all.shape, const)],
        out_specs=[pl.BlockSpec((tm, QA_W), lambda t: (t, 0)),
                   pl.BlockSpec((tm, KV_W), lambda t: (t, 0)),
                   pl.BlockSpec((tm, 2 * LRU_WIDTH), lambda t: (t, 0)),
                   pl.BlockSpec((tm, 2 * D), lambda t: (t, 0))],
        out_shape=[jax.ShapeDtypeStruct((N, QA_W), F32),
                   jax.ShapeDtypeStruct((N, KV_W), F32),
                   jax.ShapeDtypeStruct((N, 2 * LRU_WIDTH), F32),
                   jax.ShapeDtypeStruct((N, 2 * D), BF16)],
        scratch_shapes=[pltpu.VMEM((D, TAIL_COLS), BF16)],
        compiler_params=_params("arbitrary"),
        name="proj",
    )(x2, mod3, norm_w.reshape(1, D), w_all)


ATTN_BUCKETS = 8
ROW_FOLD = 64
KEY_ROWS = 256
COUNT_CHAINS = 2
COARSE_STEP = 1 << 16
FINE_PASSES = 18
NEG_INF_POS = 0x00800000
LOG2E = float(np.log2(np.e))
POS_RADIX = 256
SLOPE_PIECES = 3
V_ONES = 16


def _bf16_pieces(c, n):
    out = []
    rest = float(c)
    for _ in range(n):
        p = float(np.asarray(rest, np.float32).astype(BF16).astype(np.float32))
        out.append(p)
        rest -= p
    return out


def _reduce_rows(op, x):
    r = x.shape[0]
    if r > ROW_FOLD and r % ROW_FOLD == 0:
        x = op(x.reshape(r // ROW_FOLD, ROW_FOLD, x.shape[1]), axis=0)
    return op(x, axis=0, keepdims=True)


def _attn_block(s_eff, topk, i, qa_ref, kvq_ref, qnw_ref, o_ref, kn_s, vt_s, ki_s, key_s, kb_s, bias_s, lg_s, ot_s):
    nq = N_HEADS_A * HEAD_DIM_A
    q_t = qa_ref[:, :nq].T
    qi_t = qa_ref[:, nq:].T
    wq_t = kvq_ref[...].T[WI_OFF:WI_OFF + N_IDX_HEADS, :] * (N_IDX_HEADS ** -0.5 * IDX_DIM ** -0.5)

    q_pos = i * Q_BLOCK + lax.broadcasted_iota(I32, (1, Q_BLOCK), 1)
    limit = (q_pos // CHUNK + 1) * CHUNK
    chunks = [slice(r0, min(r0 + KEY_ROWS, s_eff)) for r0 in range(0, s_eff, KEY_ROWS)]

    qi = [qi_t[h * IDX_DIM:(h + 1) * IDX_DIM, :].astype(BF16) for h in range(N_IDX_HEADS)]
    for rows in chunks:
        ki = ki_s[rows, :]
        score = jnp.zeros((rows.stop - rows.start, Q_BLOCK), F32)
        for h in range(N_IDX_HEADS):
            score = score + jnp.maximum(_dot(ki, qi[h]), 0.0) * wq_t[h:h + 1, :]
        k_pos = rows.start + lax.broadcasted_iota(I32, score.shape, 0)
        masked = jnp.where(k_pos < limit, score, -jnp.inf)
        key_s[rows, :] = masked
        kb_s[rows, :] = masked.astype(BF16)

    def count(mask):
        return _reduce_rows(jnp.sum, jnp.where(mask, 1.0, 0.0))

    def count_keys(test):
        acc = [jnp.zeros((ROW_FOLD, Q_BLOCK), F32) for _ in range(COUNT_CHAINS)]
        for n, r0 in enumerate(range(0, s_eff, ROW_FOLD)):
            hit = jnp.where(test(key_s[r0:r0 + ROW_FOLD, :]), 1.0, 0.0)
            acc[n % COUNT_CHAINS] = acc[n % COUNT_CHAINS] + hit
        return jnp.sum(functools.reduce(lambda x, y: x + y, acc), axis=0, keepdims=True)

    def ordered_to_float(c):
        s = c ^ INT_MIN
        return pltpu.bitcast(jnp.where(s < 0, INT_MIN | (-s), s), F32)

    def count_coarse(test):
        assert s_eff // (ROW_FOLD * COUNT_CHAINS) < 256
        one = jnp.ones((ROW_FOLD, Q_BLOCK), BF16)
        zero = jnp.zeros((ROW_FOLD, Q_BLOCK), BF16)
        acc = [zero for _ in range(COUNT_CHAINS)]
        for n, r0 in enumerate(range(0, s_eff, ROW_FOLD)):
            acc[n % COUNT_CHAINS] = acc[n % COUNT_CHAINS] + jnp.where(test(kb_s[r0:r0 + ROW_FOLD, :]), one, zero)
        return jnp.sum(functools.reduce(lambda x, y: x + y, [a.astype(F32) for a in acc]), axis=0, keepdims=True)

    def bisect_coarse(it, r):
        c = r | jnp.left_shift(jnp.int32(1), 15 - it)
        cand = ordered_to_float(jnp.left_shift(c, 16)).astype(BF16)
        return jnp.where(count_coarse(lambda slab: slab >= cand) >= topk, c, r)

    def bisect_fine(_, lo_hi):
        lo, hi = lo_hi
        mid = lo + lax.shift_right_logical(hi - lo + 1, 1)
        cand = ordered_to_float(mid)
        ok = count_keys(lambda slab: slab >= cand) >= topk
        return jnp.where(ok, mid, lo), jnp.where(ok, hi, mid - 1)

    def select_by_threshold():
        coarse = jnp.left_shift(lax.fori_loop(0, 16, bisect_coarse, jnp.zeros((1, Q_BLOCK), I32)), 16)
        lo = coarse - COARSE_STEP
        lo = jnp.where((lo ^ INT_MIN) < (NEG_INF_POS ^ INT_MIN), NEG_INF_POS, lo)
        lo_hi = lax.fori_loop(0, FINE_PASSES, bisect_fine, (lo, coarse + COARSE_STEP))
        thr = ordered_to_float(lo_hi[0])

        n_ge = count_keys(lambda slab: slab >= thr)
        clean = jnp.max(jnp.where((n_ge == topk) & (thr > -jnp.inf), 0.0, 1.0)) == 0.0

        @pl.when(clean)
        def _():
            for rows in chunks:
                bias_s[rows, :] = jnp.where(key_s[rows, :] >= thr, 0.0, -jnp.inf)

        @pl.when(jnp.logical_not(clean))
        def _():
            key = key_s[:s_eff, :]
            need = topk - count(key > thr)
            rr = lax.broadcasted_iota(I32, (LANES, LANES), 0)
            cc = lax.broadcasted_iota(I32, (LANES, LANES), 1)
            tri = jnp.where(cc < rr, 1.0, 0.0).astype(BF16)
            seen = jnp.zeros((1, Q_BLOCK), F32)
            for c in range(s_eff // LANES):
                kc = key[c * LANES:(c + 1) * LANES, :]
                eq = kc == thr
                eqf = jnp.where(eq, 1.0, 0.0)
                rank = _dot(tri, eqf.astype(BF16)) + seen
                seen = seen + jnp.sum(eqf, axis=0, keepdims=True)
                tie_bias = jnp.where(jnp.where(eq, rank, topk) < need, 0.0, -jnp.inf)
                bias = jnp.where(kc > thr, 0.0, tie_bias)
                k_pos = c * LANES + lax.broadcasted_iota(I32, (LANES, Q_BLOCK), 0)
                bias_s[c * LANES:(c + 1) * LANES, :] = jnp.where(k_pos < limit, bias, -jnp.inf)

    if s_eff > topk:
        select_by_threshold()
    else:
        for rows in chunks:
            k_pos = rows.start + lax.broadcasted_iota(I32, (rows.stop - rows.start, Q_BLOCK), 0)
            bias_s[rows, :] = jnp.where(k_pos < limit, 0.0, -jnp.inf)

    qnw = qnw_ref[...]
    crow = lax.broadcasted_iota(I32, (LANES - HEAD_DIM_A, Q_BLOCK), 0)
    r_k = lax.broadcasted_iota(I32, (Q_BLOCK, Q_BLOCK), 0)
    c_q = lax.broadcasted_iota(I32, (Q_BLOCK, Q_BLOCK), 1)
    after = 2.0 * jnp.maximum(r_k - c_q, 0).astype(F32)
    band = pl.ds(pl.multiple_of(i * Q_BLOCK, Q_BLOCK), Q_BLOCK)
    slope = [sum(_bf16_pieces(LOG2E * 2.0 ** -(h + 1), SLOPE_PIECES)) for h in range(N_HEADS_A)]
    row_max = []
    for h in range(N_HEADS_A):
        pieces = _bf16_pieces(LOG2E * 2.0 ** -(h + 1), SLOPE_PIECES)
        qh = q_t[h * HEAD_DIM_A:(h + 1) * HEAD_DIM_A, :]
        ms = jnp.mean(qh * qh, axis=0, keepdims=True)
        qn = (qh * lax.rsqrt(ms + EPS) * qnw) * (HEAD_DIM_A ** -0.5) * LOG2E
        slopes = jnp.zeros(crow.shape, F32)
        for k, piece in enumerate(pieces):
            slopes = jnp.where(crow == 2 * k, POS_RADIX * piece, jnp.where(crow == 2 * k + 1, piece, slopes))
        qa = jnp.concatenate([qn, slopes], axis=0).astype(BF16)
        folded = []
        for rows in chunks:
            logits = _dot(kn_s[rows, :], qa) + bias_s[rows, :]
            lg_s[h, rows, :] = logits
            folded.append(jnp.max(logits.reshape(-1, ROW_FOLD, Q_BLOCK), axis=0))
        row_max.append(jnp.max(functools.reduce(jnp.maximum, folded), axis=0, keepdims=True))
    for h in range(N_HEADS_A):
        lg_s[h, band, :] = lg_s[h, band, :] - slope[h] * after
    for h in range(N_HEADS_A):
        acc = jnp.zeros((KV_DIM_A + V_ONES, Q_BLOCK), F32)
        for rows in chunks:
            p = jnp.exp2(lg_s[h, rows, :] - row_max[h])
            acc = acc + _dot(vt_s[:, rows], p.astype(BF16))
        ot_s[h * KV_DIM_A:(h + 1) * KV_DIM_A, :] = acc[:KV_DIM_A, :] / acc[KV_DIM_A:KV_DIM_A + 1, :]
    o_ref[...] = ot_s[...].T


def _attn_kernel(qa_ref, kvq_ref, kv_ref, qnw_ref, knw_ref, o_ref, kn_s, vt_s, ki_s, key_s, kb_s, bias_s, lg_s, ot_s):
    i = pl.program_id(1)
    S = kv_ref.shape[0]
    nb = S // Q_BLOCK
    topk = float(min(TOPK_MAX, S // 4))

    @pl.when(i == 0)
    def _():
        kv = kv_ref[:, :LANES]
        lane = lax.broadcasted_iota(I32, kv.shape, 1)
        pos = lax.broadcasted_iota(I32, kv.shape, 0)
        ms = jnp.sum(jnp.where(lane < KV_DIM_A, kv * kv, 0.0), axis=-1, keepdims=True) * (1.0 / KV_DIM_A)
        kn = kv * lax.rsqrt(ms + EPS) * knw_ref[...]
        digit = jnp.where((lane & 1) == 0, pos // POS_RADIX, pos % POS_RADIX).astype(F32)
        is_digit = (lane >= KV_DIM_A) & (lane < KV_DIM_A + 2 * SLOPE_PIECES)
        kn_s[...] = jnp.where(is_digit, digit, kn).astype(BF16)
        vt_s[:KV_DIM_A, :] = kv.T[V_OFF:V_OFF + KV_DIM_A, :].astype(BF16)
        vt_s[KV_DIM_A:, :] = jnp.ones((V_ONES, kv.shape[0]), BF16)
        ki_s[...] = kv_ref[:, KI_OFF:KI_OFF + IDX_DIM].astype(BF16)

    n_buckets = min(ATTN_BUCKETS, nb)
    per = nb // n_buckets
    for j in range(n_buckets):
        pl.when(i // per == j)(functools.partial(
            _attn_block, (j + 1) * per * Q_BLOCK, topk, i, qa_ref, kvq_ref, qnw_ref, o_ref,
            kn_s, vt_s, ki_s, key_s, kb_s, bias_s, lg_s, ot_s))


def _attn(qa, kv, q_norm_w, k_norm_w, B, S):
    N = qa.shape[0]
    nb = S // Q_BLOCK
    return pl.pallas_call(
        _attn_kernel,
        grid=(B, nb),
        in_specs=[pl.BlockSpec((Q_BLOCK, QA_W), lambda b, i: (b * nb + i, 0)),
                  pl.BlockSpec((Q_BLOCK, KV_W), lambda b, i: (b * nb + i, 0)),
                  pl.BlockSpec((S, KV_W), lambda b, i: (b, 0)),
                  pl.BlockSpec((HEAD_DIM_A, 1), lambda b, i: (0, 0)),
                  pl.BlockSpec((1, LANES), lambda b, i: (0, 0))],
        out_specs=pl.BlockSpec((Q_BLOCK, ATTN_OUT), lambda b, i: (b * nb + i, 0)),
        out_shape=jax.ShapeDtypeStruct((N, ATTN_OUT), F32),
        scratch_shapes=[pltpu.VMEM((S, LANES), BF16),
                        pltpu.VMEM((KV_DIM_A + V_ONES, S), BF16),
                        pltpu.VMEM((S, IDX_DIM), BF16),
                        pltpu.VMEM((S, Q_BLOCK), F32),
                        pltpu.VMEM((S, Q_BLOCK), BF16),
                        pltpu.VMEM((S, Q_BLOCK), F32),
                        pltpu.VMEM((N_HEADS_A, S, Q_BLOCK), F32),
                        pltpu.VMEM((ATTN_OUT, Q_BLOCK), F32)],
        compiler_params=_params("arbitrary", "arbitrary"),
        name="attn",
    )(qa, kv, kv, q_norm_w.reshape(HEAD_DIM_A, 1),
      jnp.pad(k_norm_w, (0, LANES - KV_DIM_A)).reshape(1, LANES))


SUBLANES = 8


def _shift_rows(x, d, fill, row):
    if d % SUBLANES == 0:
        return jnp.concatenate([jnp.full((d, x.shape[1]), fill, x.dtype), x[:-d]], axis=0)
    return jnp.where(row >= d, pltpu.roll(x, d, 0), fill)


def _sigmoid(x):
    return 0.5 * (1.0 + jnp.tanh(0.5 * x))


def _doubling_scan(a, b, row):
    n = a.shape[0]
    d = 1
    while d < n:
        b = a * _shift_rows(b, d, 0.0, row) + b
        if 2 * d < n:
            a = a * _shift_rows(a, d, 1.0, row)
        d *= 2
    return b


def _lru_kernel(xb_ref, gb_ref, cw_ref, cb_ref, wr_ref, br_ref, wi_ref, bi_ref, lam_ref, o_ref, a_s, b_s):
    x = xb_ref[...]
    S = x.shape[0]
    row = lax.broadcasted_iota(I32, x.shape, 0)
    xc = cb_ref[...] + _shift_rows(x, CONV_WIDTH - 1, 0.0, row) * cw_ref[0:1, :]
    for j in range(1, CONV_WIDTH):
        d = CONV_WIDTH - 1 - j
        xs = x if d == 0 else _shift_rows(x, d, 0.0, row)
        xc = xc + xs * cw_ref[j:j + 1, :]
    xcb = xc.astype(BF16)
    r = _sigmoid(_dot(xcb, wr_ref[0]) + br_ref[...])
    ig = _sigmoid(_dot(xcb, wi_ref[0]) + bi_ref[...])
    z = -lam_ref[...]
    softplus = jnp.maximum(z, 0.0) + jnp.log1p(jnp.exp(-jnp.abs(z)))
    log_a = -LRU_C * r * softplus
    a = jnp.exp(log_a)
    b = jnp.sqrt(-jnp.tanh(log_a) * (1.0 + a * a)) * (ig * xc)
    sub = row & (SUBLANES - 1)
    d = 1
    while d < SUBLANES:
        keep = sub >= d
        b = a * jnp.where(keep, pltpu.roll(b, d, 0), 0.0) + b
        a = a * jnp.where(keep, pltpu.roll(a, d, 0), 1.0)
        d *= 2
    a_s[...] = a
    b_s[...] = b
    groups = S // SUBLANES
    last = pl.ds(SUBLANES - 1, groups, stride=SUBLANES)
    g_row = lax.broadcasted_iota(I32, (groups, x.shape[1]), 0)
    h_last = _doubling_scan(a_s[last, :], b_s[last, :], g_row)
    carry = _shift_rows(h_last, 1, 0.0, g_row)
    carry = jnp.broadcast_to(carry[:, None, :], (groups, SUBLANES, x.shape[1])).reshape(x.shape)
    h = b_s[...] + a_s[...] * carry
    g = gb_ref[...]
    gelu = 0.5 * g * (1.0 + jnp.tanh(float(np.sqrt(2.0 / np.pi)) * (g + 0.044715 * (g * g * g))))
    o_ref[...] = h * gelu


def _lru(lru, conv_w, conv_b, wr_bd, b_rec, wi_bd, b_in, lam, B, S):
    N = lru.shape[0]
    nt = LRU_WIDTH // LANES
    vec = lambda b, j: (0, j)
    return pl.pallas_call(
        _lru_kernel,
        grid=(B, nt),
        in_specs=[pl.BlockSpec((S, LANES), lambda b, j: (b, j)),
                  pl.BlockSpec((S, LANES), lambda b, j: (b, nt + j)),
                  pl.BlockSpec((CONV_WIDTH, LANES), vec),
                  pl.BlockSpec((1, LANES), vec),
                  pl.BlockSpec((1, LANES, LANES), lambda b, j: (j, 0, 0)),
                  pl.BlockSpec((1, LANES), vec),
                  pl.BlockSpec((1, LANES, LANES), lambda b, j: (j, 0, 0)),
                  pl.BlockSpec((1, LANES), vec),
                  pl.BlockSpec((1, LANES), vec)],
        out_specs=pl.BlockSpec((S, LANES), lambda b, j: (b, j)),
        out_shape=jax.ShapeDtypeStruct((N, LRU_WIDTH), F32),
        scratch_shapes=[pltpu.VMEM((S, LANES), F32), pltpu.VMEM((S, LANES), F32)],
        compiler_params=_params("arbitrary", "arbitrary"),
        name="lru",
    )(lru, lru, conv_w, conv_b.reshape(1, -1), wr_bd, b_rec.reshape(1, -1), wi_bd,
      b_in.reshape(1, -1), lam.reshape(1, -1))


def _block_diag_tiles(w):
    per = LANES // LRU_BLOCK_DIM
    nt = LRU_WIDTH // LANES
    w5 = w.reshape(nt, per, LRU_BLOCK_DIM, 1, LRU_BLOCK_DIM)
    eye = jnp.eye(per, dtype=w.dtype).reshape(1, per, 1, per, 1)
    return (w5 * eye).reshape(nt, LANES, LANES)


def _first_index_of_max(vals, lane):
    m = jnp.max(vals, axis=-1, keepdims=True)
    idx = jnp.min(jnp.where(vals == m, lane, 4 * LANES), axis=-1, keepdims=True)
    return m, idx


def _weight_pieces(out, w, lane, first):
    rest = w
    for k in range(W_PIECES):
        piece = rest.astype(BF16).astype(F32)
        out = jnp.where(lane == first + k, piece, out)
        rest = rest - piece
    return out


def _merge_kernel(attn_ref, lru_ref, g_ref, x_ref, mod_ref, wpa_ref, wpb_ref, wo_ref, nw_ref, wr_ref,
                  br_ref, x1_ref, hsl_ref, route_ref, gc_ref):
    D = x_ref.shape[1]
    tm = x_ref.shape[0]

    y_a = _dot(attn_ref[...].astype(BF16), wpa_ref[...])
    y_b = _dot(lru_ref[...].astype(BF16), wpb_ref[...])
    merged = g_ref[:, :D].astype(F32) * y_a + g_ref[:, D:].astype(F32) * y_b
    x1 = x_ref[...] + mod_ref[0, 2:3, :] * _dot(merged.astype(BF16), wo_ref[...])
    x1_ref[...] = x1
    h2 = _rms_mod(x1, nw_ref[...], mod_ref[0, 3:4, :], mod_ref[0, 4:5, :])
    h2b = h2.astype(BF16)

    h2l = (h2 - h2b.astype(F32)).astype(BF16)
    hi_hi_lo = _dot(h2b, wr_ref[...])
    logits = hi_hi_lo[:, :LANES] + (_dot(h2l, wr_ref[:, :LANES]) + hi_hi_lo[:, LANES:]) + br_ref[...]
    lane = lax.broadcasted_iota(I32, logits.shape, 1)
    gl = jnp.where(lane < N_GROUPS, logits, -jnp.inf)
    gmax, g_sel = _first_index_of_max(gl, lane)
    g_weight = 1.0 / jnp.sum(jnp.exp(gl - gmax), axis=-1, keepdims=True)
    e_lo = N_GROUPS + g_sel * EXPERTS_PER_GROUP
    el = jnp.where((lane >= e_lo) & (lane < e_lo + EXPERTS_PER_GROUP), logits, -jnp.inf)
    v1, i1 = _first_index_of_max(el, lane)
    el2 = jnp.where(lane == i1, -jnp.inf, el)
    v2, i2 = _first_index_of_max(el2, lane)
    e2x = jnp.exp(v2 - v1)
    w1 = g_weight / (1.0 + e2x)
    w2 = g_weight * e2x / (1.0 + e2x)

    hot1 = lane == i1
    hot2 = lane == i2
    hot = jnp.where(hot1, 1.0, jnp.where(hot2, 1.0, 0.0))
    gcnt = jnp.floor((jnp.sum(hot, axis=0, keepdims=True) + (GRAN - 1.0)) * (1.0 / GRAN))
    gc_ref[0] = gcnt
    rr = lax.broadcasted_iota(I32, (LANES, LANES), 0)
    cc = lax.broadcasted_iota(I32, (LANES, LANES), 1)
    upper = jnp.where(rr < cc, 1.0, 0.0).astype(BF16)
    run_start = GRAN * _dot(jnp.broadcast_to(gcnt, (8, LANES)).astype(BF16), upper)[0:1, :]
    rr = lax.broadcasted_iota(I32, (tm, tm), 0)
    cc = lax.broadcasted_iota(I32, (tm, tm), 1)
    lower = jnp.where(cc < rr, 1.0, 0.0).astype(BF16)
    pos = _dot(lower, hot.astype(BF16)) + run_start
    pos1 = jnp.where(hot1, pos, 0.0)
    pos2 = jnp.where(hot2, pos, 0.0)
    lp1 = jnp.sum(pos1, axis=-1, keepdims=True)
    lp2 = jnp.sum(pos2, axis=-1, keepdims=True)
    route_ref[...] = jnp.where(lane == R_LP1, lp1, jnp.where(lane == R_LP2, lp2, 0.0))

    aux = jnp.where(lane == AUX_E1, (i1 - N_GROUPS).astype(F32), 0.0)
    aux = _weight_pieces(_weight_pieces(aux, w1, lane, AUX_W1), w2, lane, AUX_W2)
    lp1r = jnp.sum(pos1.T, axis=0, keepdims=True).astype(I32)
    lp2r = jnp.sum(pos2.T, axis=0, keepdims=True).astype(I32)
    rows = lax.broadcasted_iota(I32, (LOCAL_ROWS, tm), 0)
    sort = jnp.where(rows == lp1r, 1.0, jnp.where(rows == lp2r, 1.0, 0.0)).astype(BF16)
    hsl_ref[...] = _dot(sort, jnp.concatenate([h2b, aux.astype(BF16)], axis=1)).astype(BF16)


def _merge(attn, lru_out, gates, x2, mod3, wpa, wpb, wo, norm_w, wr_hi_lo, b_router, S):
    N, D = x2.shape
    tm = TM_SORT
    n_tiles = N // tm
    steps_per_batch = S // tm
    const = lambda t: (0, 0)
    rows = lambda t: (t, 0)
    return pl.pallas_call(
        _merge_kernel,
        grid=(n_tiles,),
        in_specs=[pl.BlockSpec((tm, ATTN_OUT), rows),
                  pl.BlockSpec((tm, LRU_WIDTH), rows),
                  pl.BlockSpec((tm, 2 * D), rows),
                  pl.BlockSpec((tm, D), rows),
                  pl.BlockSpec((1, 6, D), lambda t: (t // steps_per_batch, 0, 0)),
                  pl.BlockSpec(wpa.shape, const),
                  pl.BlockSpec(wpb.shape, const),
                  pl.BlockSpec(wo.shape, const),
                  pl.BlockSpec((1, D), const),
                  pl.BlockSpec(wr_hi_lo.shape, const),
                  pl.BlockSpec((1, LANES), const)],
        out_specs=[pl.BlockSpec((tm, D), rows),
                   pl.BlockSpec((LOCAL_ROWS, D + AUX_W), rows),
                   pl.BlockSpec((tm, LANES), rows),
                   pl.BlockSpec((1, 1, LANES), lambda t: (t, 0, 0))],
        out_shape=[jax.ShapeDtypeStruct((N, D), F32),
                   jax.ShapeDtypeStruct((n_tiles * LOCAL_ROWS, D + AUX_W), BF16),
                   jax.ShapeDtypeStruct((N, LANES), F32),
                   jax.ShapeDtypeStruct((n_tiles, 1, LANES), F32)],
        compiler_params=_params("arbitrary"),
        name="merge",
    )(attn, lru_out, gates, x2, mod3, wpa, wpb, wo, norm_w.reshape(1, D), wr_hi_lo, b_router)


def _plan_sizes(n_tiles):
    max_gran = n_tiles * LOCAL_GRAN + N_EXPERTS * (GRAN_PER_TILE - 1)
    max_row_tiles = -(-max_gran // GRAN_PER_TILE)
    return max_row_tiles * GRAN_PER_TILE, max_row_tiles


def _plan_kernel(gc_ref, src_ref, texp_ref, nt_ref, dst_ref, local_next):
    n_tiles = gc_ref.shape[0]
    max_gran, max_row_tiles = _plan_sizes(n_tiles)

    def zero(ref, lo, hi):
        def body(k, c):
            ref[k] = 0
            return c
        lax.fori_loop(lo, hi, body, 0)

    zero(local_next, 0, n_tiles)

    def per_expert(e, g):
        first_tile = g // GRAN_PER_TILE

        def per_tile(t, g):
            n = gc_ref[t, N_GROUPS + e]
            base = t * LOCAL_GRAN + local_next[t]

            def per_granule(k, g):
                src_ref[g] = base + k
                dst_ref[base + k] = g
                return g + 1

            g = lax.fori_loop(0, n, per_granule, g)
            local_next[t] = local_next[t] + n
            return g

        g = lax.fori_loop(0, n_tiles, per_tile, g)
        padded = (g + GRAN_PER_TILE - 1) // GRAN_PER_TILE * GRAN_PER_TILE
        zero(src_ref, g, padded)
        g = padded

        def mark(j, c):
            texp_ref[j] = e
            return c

        lax.fori_loop(first_tile, g // GRAN_PER_TILE, mark, 0)
        return g

    g = lax.fori_loop(0, N_EXPERTS, per_expert, 0)
    used = g // GRAN_PER_TILE
    nt_ref[0] = used
    zero(src_ref, g, max_gran)

    def clear_tail(t, c):
        zero(dst_ref, t * LOCAL_GRAN + local_next[t], (t + 1) * LOCAL_GRAN)
        return c

    lax.fori_loop(0, n_tiles, clear_tail, 0)

    def mark_rest(j, c):
        texp_ref[j] = N_EXPERTS - 1
        return c

    lax.fori_loop(used, max_row_tiles, mark_rest, 0)


def _plan(gc):
    n_tiles = gc.shape[0]
    max_gran, max_row_tiles = _plan_sizes(n_tiles)
    smem = pl.BlockSpec(memory_space=pltpu.SMEM)
    return pl.pallas_call(
        _plan_kernel,
        in_specs=[smem],
        out_specs=[smem, smem, smem, smem, smem],
        out_shape=[jax.ShapeDtypeStruct((max_gran,), I32),
                   jax.ShapeDtypeStruct((max_row_tiles,), I32),
                   jax.ShapeDtypeStruct((1,), I32),
                   jax.ShapeDtypeStruct((n_tiles * LOCAL_GRAN,), I32),
                   jax.ShapeDtypeStruct((n_tiles,), I32)],
        name="plan",
    )(gc)


def _granule_copies(idx_ref, first, count, src_hbm, dst_vmem, sem, wait):
    def body(k, c):
        g = idx_ref[first + k]
        cp = pltpu.make_async_copy(src_hbm.at[pl.ds(pl.multiple_of(g * GRAN, GRAN), GRAN)],
                                   dst_vmem.at[pl.ds(pl.multiple_of(k * GRAN, GRAN), GRAN)], sem)
        if wait:
            cp.wait()
        else:
            cp.start()
        return c
    lax.fori_loop(0, count, body, 0)


def _moe_kernel(src_ref, texp_ref, nt_ref, hsl_ref, w1_ref, w3_ref, w2_ref, ys_ref, xbuf, w1b, w3b, w2b, sem):
    i = pl.program_id(0)
    used = nt_ref[0]
    slot = i % MOE_BUFFERS
    D = ys_ref.shape[1]

    def gather(tile, wait):
        slot_ = tile % MOE_BUFFERS
        _granule_copies(src_ref, tile * GRAN_PER_TILE, GRAN_PER_TILE, hsl_ref, xbuf.at[slot_], sem.at[slot_], wait)

    @pl.when(i == 0)
    def _():
        for ahead in range(MOE_BUFFERS - 1):
            pl.when(ahead < used)(functools.partial(gather, ahead, False))

    @pl.when(i + (MOE_BUFFERS - 1) < used)
    def _():
        gather(i + (MOE_BUFFERS - 1), False)

    @pl.when(i < used)
    def _():
        gather(i, True)

        @pl.when((i == 0) | (texp_ref[i] != texp_ref[jnp.maximum(i - 1, 0)]))
        def _():
            w1b[...] = w1_ref[0].astype(BF16)
            w3b[...] = w3_ref[0].astype(BF16)
            w2b[...] = w2_ref[0].astype(BF16)

        xb = xbuf[slot]
        hb = xb[:, :D]
        aux = xb[:, D:].astype(F32)
        first = aux[:, AUX_E1:AUX_E1 + 1] == texp_ref[i].astype(F32)
        w_row = jnp.zeros((TM_MOE, 1), F32)
        for k in range(W_PIECES):
            w_row = w_row + jnp.where(first, aux[:, AUX_W1 + k:AUX_W1 + k + 1], aux[:, AUX_W2 + k:AUX_W2 + k + 1])
        a = _dot(hb, w1b[...])
        act = (a * _sigmoid(a)) * _dot(hb, w3b[...])
        ys_ref[...] = (w_row * _dot(act.astype(BF16), w2b[...])).astype(BF16)

    @pl.when(i >= used)
    def _():
        ys_ref[...] = jnp.zeros_like(ys_ref)


def _moe(src, texp, used, hsl, w1, w3, w2):
    max_row_tiles = texp.shape[0]
    D = w1.shape[1]
    F = w1.shape[2]
    grid_spec = pltpu.PrefetchScalarGridSpec(
        num_scalar_prefetch=3,
        grid=(max_row_tiles,),
        in_specs=[pl.BlockSpec(memory_space=pl.ANY),
                  pl.BlockSpec((1, D, F), lambda i, s, te, nt: (te[i], 0, 0)),
                  pl.BlockSpec((1, D, F), lambda i, s, te, nt: (te[i], 0, 0)),
                  pl.BlockSpec((1, F, D), lambda i, s, te, nt: (te[i], 0, 0))],
        out_specs=pl.BlockSpec((TM_MOE, D), lambda i, s, te, nt: (i, 0)),
        scratch_shapes=[pltpu.VMEM((MOE_BUFFERS, TM_MOE, D + AUX_W), BF16),
                        pltpu.VMEM((D, F), BF16), pltpu.VMEM((D, F), BF16), pltpu.VMEM((F, D), BF16),
                        pltpu.SemaphoreType.DMA((MOE_BUFFERS,))],
    )
    return pl.pallas_call(
        _moe_kernel,
        grid_spec=grid_spec,
        out_shape=jax.ShapeDtypeStruct((max_row_tiles * TM_MOE, D), BF16),
        compiler_params=_params("arbitrary"),
        name="moe",
    )(src, texp, used, hsl, w1, w3, w2)


def _combine_kernel(dst_ref, lcnt_ref, ys_ref, x1_ref, route_ref, mod_ref, o_ref, ybuf, sem):
    t = pl.program_id(0)
    slot = t % 2
    tm = x1_ref.shape[0]

    def gather(tile, slot_, wait):
        _granule_copies(dst_ref, tile * LOCAL_GRAN, lcnt_ref[tile], ys_ref, ybuf.at[slot_], sem.at[slot_], wait)

    @pl.when(t == 0)
    def _():
        ybuf[...] = jnp.zeros_like(ybuf)
        gather(0, 0, False)

    @pl.when(t + 1 < pl.num_programs(0))
    def _():
        gather(t + 1, 1 - slot, False)

    gather(t, slot, True)
    lp1 = route_ref[:, R_LP1:R_LP1 + 1].astype(I32)
    lp2 = route_ref[:, R_LP2:R_LP2 + 1].astype(I32)
    col = lax.broadcasted_iota(I32, (tm, LOCAL_ROWS), 1)
    unsort = jnp.where(col == lp1, 1.0, jnp.where(col == lp2, 1.0, 0.0)).astype(BF16)
    o_ref[...] = x1_ref[...] + mod_ref[0, 5:6, :] * _dot(unsort, ybuf[slot])


def _combine(dst, lcnt, ys, x1, route, mod3, S):
    N, D = x1.shape
    tm = TM_SORT
    steps_per_batch = S // tm
    grid_spec = pltpu.PrefetchScalarGridSpec(
        num_scalar_prefetch=2,
        grid=(N // tm,),
        in_specs=[pl.BlockSpec(memory_space=pl.ANY),
                  pl.BlockSpec((tm, D), lambda t, d, c: (t, 0)),
                  pl.BlockSpec((tm, LANES), lambda t, d, c: (t, 0)),
                  pl.BlockSpec((1, 6, D), lambda t, d, c: (t // steps_per_batch, 0, 0))],
        out_specs=pl.BlockSpec((tm, D), lambda t, d, c: (t, 0)),
        scratch_shapes=[pltpu.VMEM((2, LOCAL_ROWS, D), BF16), pltpu.SemaphoreType.DMA((2,))],
    )
    return pl.pallas_call(
        _combine_kernel,
        grid_spec=grid_spec,
        out_shape=jax.ShapeDtypeStruct((N, D), F32),
        compiler_params=_params("arbitrary"),
        name="combine",
    )(dst, lcnt, ys, x1, route, mod3)


def _layer(x2, mod3, B, S, norm_mix_w, w_in, q_norm_w, k_norm_w, conv_w, conv_b, w_rec_gate, b_rec_gate,
           w_in_gate, b_in_gate, lru_lambda, w_proj_a, w_proj_b, w_out, norm_ffn_w, w_group, b_group,
           w_expert_router, b_expert_router, w1, w3, w2):
    N, D = x2.shape
    w_all = jnp.pad(w_in.astype(BF16), ((0, 0), (0, W_IN_PAD - D_IN)))
    qa, kv, lru, gates = _proj(x2, mod3, norm_mix_w, w_all, S)
    attn = _attn(qa, kv, q_norm_w, k_norm_w, B, S)
    lru_out = _lru(lru, conv_w, conv_b, _block_diag_tiles(w_rec_gate).astype(BF16), b_rec_gate,
                   _block_diag_tiles(w_in_gate).astype(BF16), b_in_gate, lru_lambda, B, S)

    n_r = N_GROUPS + N_EXPERTS
    w_router = jnp.concatenate([w_group, w_expert_router, jnp.zeros((D, LANES - n_r), F32)], axis=1)
    wr_hi = w_router.astype(BF16)
    wr_lo = (w_router - wr_hi.astype(F32)).astype(BF16)
    b_router = jnp.concatenate([b_group, b_expert_router, jnp.zeros((LANES - n_r,), F32)]).reshape(1, LANES)
    x1, hsl, route, gc = _merge(attn, lru_out, gates, x2, mod3, w_proj_a.astype(BF16), w_proj_b.astype(BF16),
                                w_out.astype(BF16), norm_ffn_w, jnp.concatenate([wr_hi, wr_lo], axis=1),
                                b_router, S)

    src, texp, used, dst, lcnt = _plan(gc.reshape(gc.shape[0], LANES).astype(I32))
    ys = _moe(src, texp, used, hsl, w1, w3, w2)
    return _combine(dst, lcnt, ys, x1, route, mod3, S)


def kernel(x, c, ada_w, ada_b, norm_mix_w, w_in, q_norm_w, k_norm_w, conv_w, conv_b, w_rec_gate, b_rec_gate,
           w_in_gate, b_in_gate, lru_lambda, w_proj_a, w_proj_b, w_out, norm_ffn_w, w_group, b_group,
           w_expert_router, b_expert_router, w1, w3, w2):
    B, S, D = x.shape
    x2 = x.reshape(B * S, D)
    for l in range(ada_w.shape[0]):
        mod3 = _ada(c, ada_w[l], ada_b[l]).reshape(B, 6, D)
        x2 = _layer(x2, mod3, B, S, norm_mix_w[l], w_in[l], q_norm_w[l], k_norm_w[l], conv_w[l], conv_b[l],
                    w_rec_gate[l], b_rec_gate[l], w_in_gate[l], b_in_gate[l], lru_lambda[l], w_proj_a[l],
                    w_proj_b[l], w_out[l], norm_ffn_w[l], w_group[l], b_group[l], w_expert_router[l],
                    b_expert_router[l], w1[l], w3[l], w2[l])
    return x2.reshape(B, S, D)
```

```python
import functools

import jax
import jax.numpy as jnp
import numpy as np
from jax import lax
from jax.experimental import pallas as pl
from jax.experimental.pallas import tpu as pltpu

F32 = jnp.float32
BF16 = jnp.bfloat16
I32 = jnp.int32

EPS = 1e-6
CHUNK = 64
Q_BLOCK = 128
N_HEADS_A = 8
HEAD_DIM_A = 64
KV_DIM_A = 64
ATTN_OUT = N_HEADS_A * KV_DIM_A
N_IDX_HEADS = 4
IDX_DIM = 64
TOPK_MAX = 256
LRU_WIDTH = 512
LRU_BLOCKS = 8
LRU_BLOCK_DIM = LRU_WIDTH // LRU_BLOCKS
CONV_WIDTH = 4
LRU_C = 8.0
N_GROUPS = 4
EXPERTS_PER_GROUP = 8
N_EXPERTS = N_GROUPS * EXPERTS_PER_GROUP
D_FF_EXPERT = 256

LANES = 128
INT_MIN = -2 ** 31
VMEM_LIMIT = 56 * 1024 * 1024

D_MODEL = 1024
COL_Q = 0
COL_K = COL_Q + N_HEADS_A * HEAD_DIM_A
COL_V = COL_K + KV_DIM_A
COL_QI = COL_V + KV_DIM_A
COL_KI = COL_QI + N_IDX_HEADS * IDX_DIM
COL_WI = COL_KI + IDX_DIM
COL_LX = COL_WI + N_IDX_HEADS
D_IN = COL_LX + 2 * LRU_WIDTH + 2 * D_MODEL
ATTN_COLS = -(-COL_LX // LANES) * LANES
TAIL_COLS = D_IN - COL_LX
TAIL_SHIFT = COL_LX % LANES
W_IN_PAD = (COL_LX // LANES) * LANES + TAIL_COLS + LANES
assert COL_K % LANES == 0 and COL_QI % LANES == 0 and COL_KI % LANES == 0 and TAIL_COLS % LANES == 0

QA_W = N_HEADS_A * HEAD_DIM_A + N_IDX_HEADS * IDX_DIM
KV_W = (COL_QI - COL_K) + (ATTN_COLS - COL_KI)
K_OFF, V_OFF = 0, KV_DIM_A
KI_OFF = COL_QI - COL_K
WI_OFF = KI_OFF + IDX_DIM

TM_PROJ = 512

TM_SORT = 512
GRAN = 16
TM_MOE = 256
GRAN_PER_TILE = TM_MOE // GRAN
MOE_BUFFERS = 3
LOCAL_ROWS = -(-(2 * TM_SORT + N_EXPERTS * (GRAN - 1)) // LANES) * LANES
LOCAL_GRAN = LOCAL_ROWS // GRAN
AUX_W = LANES
W_PIECES = 3
AUX_W1, AUX_W2 = 0, W_PIECES
AUX_E1 = 2 * W_PIECES

R_LP1, R_LP2 = 0, 1


def _dot(a, b, **kw):
    return jnp.dot(a, b, preferred_element_type=F32, **kw)


def _params(*sem):
    return pltpu.CompilerParams(dimension_semantics=sem, vmem_limit_bytes=VMEM_LIMIT)


def _ada_kernel(c_ref, w_ref, b_ref, o_ref):
    c = c_ref[...]
    cond = c * jax.nn.sigmoid(c)
    o_ref[...] = _dot(cond, w_ref[...], precision=lax.Precision.HIGHEST) + b_ref[...]


def _ada(c, ada_w, ada_b):
    B, D = c.shape
    n_out = ada_w.shape[1]
    tn = 1024
    return pl.pallas_call(
        _ada_kernel,
        grid=(n_out // tn,),
        in_specs=[pl.BlockSpec((B, D), lambda j: (0, 0)),
                  pl.BlockSpec((D, tn), lambda j: (0, j)),
                  pl.BlockSpec((1, tn), lambda j: (0, j))],
        out_specs=pl.BlockSpec((B, tn), lambda j: (0, j)),
        out_shape=jax.ShapeDtypeStruct((B, n_out), F32),
        compiler_params=_params("arbitrary"),
        name="ada",
    )(c, ada_w, ada_b.reshape(1, n_out))


def _rms_mod(x, w, shift, scale):
    ms = jnp.mean(x * x, axis=-1, keepdims=True)
    y = x * lax.rsqrt(ms + EPS) * w
    return y * (1.0 + scale) + shift


def _proj_kernel(x_ref, mod_ref, nw_ref, w_ref, oqa_ref, okv_ref, olru_ref, og_ref, wt_s):
    @pl.when(pl.program_id(0) == 0)
    def _():
        rr = lax.broadcasted_iota(I32, (2 * LANES, LANES), 0)
        cc = lax.broadcasted_iota(I32, (2 * LANES, LANES), 1)
        shift = jnp.where(rr == cc + TAIL_SHIFT, 1.0, 0.0).astype(BF16)
        for j in range(TAIL_COLS // LANES):
            src = (COL_LX // LANES + j) * LANES
            wt_s[:, j * LANES:(j + 1) * LANES] = _dot(w_ref[:, src:src + 2 * LANES], shift).astype(BF16)

    h = _rms_mod(x_ref[...], nw_ref[...], mod_ref[0, 0:1, :], mod_ref[0, 1:2, :])
    hb = h.astype(BF16)
    pa = _dot(hb, w_ref[:, :ATTN_COLS])
    oqa_ref[...] = jnp.concatenate([pa[:, COL_Q:COL_K], pa[:, COL_QI:COL_KI]], axis=1)
    okv_ref[...] = jnp.concatenate([pa[:, COL_K:COL_QI], pa[:, COL_KI:ATTN_COLS]], axis=1)
    olru_ref[...] = _dot(hb, wt_s[:, :2 * LRU_WIDTH])
    og_ref[...] = _sigmoid(_dot(hb, wt_s[:, 2 * LRU_WIDTH:])).astype(BF16)


def _proj(x2, mod3, norm_w, w_all, S):
    N, D = x2.shape
    tm = TM_PROJ
    steps_per_batch = S // tm
    const = lambda t: (0, 0)
    return pl.pallas_call(
        _proj_kernel,
        grid=(N // tm,),
        in_specs=[pl.BlockSpec((tm, D), lambda t: (t, 0)),
                  pl.BlockSpec((1, 6, D), lambda t: (t // steps_per_batch, 0, 0)),
                  pl.BlockSpec((1, D), const),
                  pl.BlockSpec(w_all.shape, const)],
        out_specs=[pl.BlockSpec((tm, QA_W), lambda t: (t, 0)),
                   pl.BlockSpec((tm, KV_W), lambda t: (t, 0)),
                   pl.BlockSpec((tm, 2 * LRU_WIDTH), lambda t: (t, 0)),
                   pl.BlockSpec((tm, 2 * D), lambda t: (t, 0))],
        out_shape=[jax.ShapeDtypeStruct((N, QA_W), F32),
                   jax.ShapeDtypeStruct((N, KV_W), F32),
                   jax.ShapeDtypeStruct((N, 2 * LRU_WIDTH), F32),
                   jax.ShapeDtypeStruct((N, 2 * D), BF16)],
        scratch_shapes=[pltpu.VMEM((D, TAIL_COLS), BF16)],
        compiler_params=_params("arbitrary"),
        name="proj",
    )(x2, mod3, norm_w.reshape(1, D), w_all)


ATTN_BUCKETS = 8
ROW_FOLD = 64
KEY_ROWS = 256
COUNT_CHAINS = 2
LOG2E = float(np.log2(np.e))
POS_RADIX = 256
SLOPE_PIECES = 3
V_ONES = 16


def _bf16_pieces(c, n):
    out = []
    rest = float(c)
    for _ in range(n):
        p = float(np.asarray(rest, np.float32).astype(BF16).astype(np.float32))
        out.append(p)
        rest -= p
    return out


def _reduce_rows(op, x):
    r = x.shape[0]
    if r > ROW_FOLD and r % ROW_FOLD == 0:
        x = op(x.reshape(r // ROW_FOLD, ROW_FOLD, x.shape[1]), axis=0)
    return op(x, axis=0, keepdims=True)


def _attn_block(s_eff, topk, i, qa_ref, kvq_ref, qnw_ref, o_ref, kn_s, vt_s, ki_s, key_s, bias_s, lg_s, ot_s):
    nq = N_HEADS_A * HEAD_DIM_A
    q_t = qa_ref[:, :nq].T
    qi_t = qa_ref[:, nq:].T
    wq_t = kvq_ref[...].T[WI_OFF:WI_OFF + N_IDX_HEADS, :] * (N_IDX_HEADS ** -0.5 * IDX_DIM ** -0.5)

    q_pos = i * Q_BLOCK + lax.broadcasted_iota(I32, (1, Q_BLOCK), 1)
    limit = (q_pos // CHUNK + 1) * CHUNK
    chunks = [slice(r0, min(r0 + KEY_ROWS, s_eff)) for r0 in range(0, s_eff, KEY_ROWS)]

    qi = [qi_t[h * IDX_DIM:(h + 1) * IDX_DIM, :].astype(BF16) for h in range(N_IDX_HEADS)]
    for rows in chunks:
        ki = ki_s[rows, :]
        score = jnp.zeros((rows.stop - rows.start, Q_BLOCK), F32)
        for h in range(N_IDX_HEADS):
            score = score + jnp.maximum(_dot(ki, qi[h]), 0.0) * wq_t[h:h + 1, :]
        k_pos = rows.start + lax.broadcasted_iota(I32, score.shape, 0)
        key_s[rows, :] = jnp.where(k_pos < limit, score, -jnp.inf)

    def count(mask):
        return _reduce_rows(jnp.sum, jnp.where(mask, 1.0, 0.0))

    def count_keys(test):
        acc = [jnp.zeros((ROW_FOLD, Q_BLOCK), F32) for _ in range(COUNT_CHAINS)]
        for n, r0 in enumerate(range(0, s_eff, ROW_FOLD)):
            hit = jnp.where(test(key_s[r0:r0 + ROW_FOLD, :]), 1.0, 0.0)
            acc[n % COUNT_CHAINS] = acc[n % COUNT_CHAINS] + hit
        return jnp.sum(functools.reduce(lambda x, y: x + y, acc), axis=0, keepdims=True)

    def ordered_to_float(c):
        s = c ^ INT_MIN
        return pltpu.bitcast(jnp.where(s < 0, INT_MIN | (-s), s), F32)

    def bisect(it, r):
        c = r | jnp.left_shift(jnp.int32(1), 31 - it)
        cand = ordered_to_float(c)
        return jnp.where(count_keys(lambda slab: slab >= cand) >= topk, c, r)

    def select_by_threshold():
        thr = ordered_to_float(lax.fori_loop(0, 32, bisect, jnp.zeros((1, Q_BLOCK), I32)))

        n_ge = count_keys(lambda slab: slab >= thr)
        clean = jnp.max(jnp.where((n_ge == topk) & (thr > -jnp.inf), 0.0, 1.0)) == 0.0

        @pl.when(clean)
        def _():
            for rows in chunks:
                bias_s[rows, :] = jnp.where(key_s[rows, :] >= thr, 0.0, -jnp.inf)

        @pl.when(jnp.logical_not(clean))
        def _():
            key = key_s[:s_eff, :]
            need = topk - count(key > thr)
            rr = lax.broadcasted_iota(I32, (LANES, LANES), 0)
            cc = lax.broadcasted_iota(I32, (LANES, LANES), 1)
            tri = jnp.where(cc < rr, 1.0, 0.0).astype(BF16)
            seen = jnp.zeros((1, Q_BLOCK), F32)
            for c in range(s_eff // LANES):
                kc = key[c * LANES:(c + 1) * LANES, :]
                eq = kc == thr
                eqf = jnp.where(eq, 1.0, 0.0)
                rank = _dot(tri, eqf.astype(BF16)) + seen
                seen = seen + jnp.sum(eqf, axis=0, keepdims=True)
                tie_bias = jnp.where(jnp.where(eq, rank, topk) < need, 0.0, -jnp.inf)
                bias = jnp.where(kc > thr, 0.0, tie_bias)
                k_pos = c * LANES + lax.broadcasted_iota(I32, (LANES, Q_BLOCK), 0)
                bias_s[c * LANES:(c + 1) * LANES, :] = jnp.where(k_pos < limit, bias, -jnp.inf)

    if s_eff > topk:
        select_by_threshold()
    else:
        for rows in chunks:
            k_pos = rows.start + lax.broadcasted_iota(I32, (rows.stop - rows.start, Q_BLOCK), 0)
            bias_s[rows, :] = jnp.where(k_pos < limit, 0.0, -jnp.inf)

    qnw = qnw_ref[...]
    crow = lax.broadcasted_iota(I32, (LANES - HEAD_DIM_A, Q_BLOCK), 0)
    r_k = lax.broadcasted_iota(I32, (Q_BLOCK, Q_BLOCK), 0)
    c_q = lax.broadcasted_iota(I32, (Q_BLOCK, Q_BLOCK), 1)
    after = 2.0 * jnp.maximum(r_k - c_q, 0).astype(F32)
    band = pl.ds(pl.multiple_of(i * Q_BLOCK, Q_BLOCK), Q_BLOCK)
    slope = [sum(_bf16_pieces(LOG2E * 2.0 ** -(h + 1), SLOPE_PIECES)) for h in range(N_HEADS_A)]
    row_max = []
    for h in range(N_HEADS_A):
        pieces = _bf16_pieces(LOG2E * 2.0 ** -(h + 1), SLOPE_PIECES)
        qh = q_t[h * HEAD_DIM_A:(h + 1) * HEAD_DIM_A, :]
        ms = jnp.mean(qh * qh, axis=0, keepdims=True)
        qn = (qh * lax.rsqrt(ms + EPS) * qnw) * (HEAD_DIM_A ** -0.5) * LOG2E
        slopes = jnp.zeros(crow.shape, F32)
        for k, piece in enumerate(pieces):
            slopes = jnp.where(crow == 2 * k, POS_RADIX * piece, jnp.where(crow == 2 * k + 1, piece, slopes))
        qa = jnp.concatenate([qn, slopes], axis=0).astype(BF16)
        folded = []
        for rows in chunks:
            logits = _dot(kn_s[rows, :], qa) + bias_s[rows, :]
            lg_s[h, rows, :] = logits
            folded.append(jnp.max(logits.reshape(-1, ROW_FOLD, Q_BLOCK), axis=0))
        row_max.append(jnp.max(functools.reduce(jnp.maximum, folded), axis=0, keepdims=True))
    for h in range(N_HEADS_A):
        lg_s[h, band, :] = lg_s[h, band, :] - slope[h] * after
    for h in range(N_HEADS_A):
        acc = jnp.zeros((KV_DIM_A + V_ONES, Q_BLOCK), F32)
        for rows in chunks:
            p = jnp.exp2(lg_s[h, rows, :] - row_max[h])
            acc = acc + _dot(vt_s[:, rows], p.astype(BF16))
        ot_s[h * KV_DIM_A:(h + 1) * KV_DIM_A, :] = acc[:KV_DIM_A, :] / acc[KV_DIM_A:KV_DIM_A + 1, :]
    o_ref[...] = ot_s[...].T


def _attn_kernel(qa_ref, kvq_ref, kv_ref, qnw_ref, knw_ref, o_ref, kn_s, vt_s, ki_s, key_s, bias_s, lg_s, ot_s):
    i = pl.program_id(1)
    S = kv_ref.shape[0]
    nb = S // Q_BLOCK
    topk = float(min(TOPK_MAX, S // 4))

    @pl.when(i == 0)
    def _():
        kv = kv_ref[:, :LANES]
        lane = lax.broadcasted_iota(I32, kv.shape, 1)
        pos = lax.broadcasted_iota(I32, kv.shape, 0)
        ms = jnp.sum(jnp.where(lane < KV_DIM_A, kv * kv, 0.0), axis=-1, keepdims=True) * (1.0 / KV_DIM_A)
        kn = kv * lax.rsqrt(ms + EPS) * knw_ref[...]
        digit = jnp.where((lane & 1) == 0, pos // POS_RADIX, pos % POS_RADIX).astype(F32)
        is_digit = (lane >= KV_DIM_A) & (lane < KV_DIM_A + 2 * SLOPE_PIECES)
        kn_s[...] = jnp.where(is_digit, digit, kn).astype(BF16)
        vt_s[:KV_DIM_A, :] = kv.T[V_OFF:V_OFF + KV_DIM_A, :].astype(BF16)
        vt_s[KV_DIM_A:, :] = jnp.ones((V_ONES, kv.shape[0]), BF16)
        ki_s[...] = kv_ref[:, KI_OFF:KI_OFF + IDX_DIM].astype(BF16)

    n_buckets = min(ATTN_BUCKETS, nb)
    per = nb // n_buckets
    for j in range(n_buckets):
        pl.when(i // per == j)(functools.partial(
            _attn_block, (j + 1) * per * Q_BLOCK, topk, i, qa_ref, kvq_ref, qnw_ref, o_ref,
            kn_s, vt_s, ki_s, key_s, bias_s, lg_s, ot_s))


def _attn(qa, kv, q_norm_w, k_norm_w, B, S):
    N = qa.shape[0]
    nb = S // Q_BLOCK
    return pl.pallas_call(
        _attn_kernel,
        grid=(B, nb),
        in_specs=[pl.BlockSpec((Q_BLOCK, QA_W), lambda b, i: (b * nb + i, 0)),
                  pl.BlockSpec((Q_BLOCK, KV_W), lambda b, i: (b * nb + i, 0)),
                  pl.BlockSpec((S, KV_W), lambda b, i: (b, 0)),
                  pl.BlockSpec((HEAD_DIM_A, 1), lambda b, i: (0, 0)),
                  pl.BlockSpec((1, LANES), lambda b, i: (0, 0))],
        out_specs=pl.BlockSpec((Q_BLOCK, ATTN_OUT), lambda b, i: (b * nb + i, 0)),
        out_shape=jax.ShapeDtypeStruct((N, ATTN_OUT), F32),
        scratch_shapes=[pltpu.VMEM((S, LANES), BF16),
                        pltpu.VMEM((KV_DIM_A + V_ONES, S), BF16),
                        pltpu.VMEM((S, IDX_DIM), BF16),
                        pltpu.VMEM((S, Q_BLOCK), F32),
                        pltpu.VMEM((S, Q_BLOCK), F32),
                        pltpu.VMEM((N_HEADS_A, S, Q_BLOCK), F32),
                        pltpu.VMEM((ATTN_OUT, Q_BLOCK), F32)],
        compiler_params=_params("arbitrary", "arbitrary"),
        name="attn",
    )(qa, kv, kv, q_norm_w.reshape(HEAD_DIM_A, 1),
      jnp.pad(k_norm_w, (0, LANES - KV_DIM_A)).reshape(1, LANES))


SUBLANES = 8


def _shift_rows(x, d, fill, row):
    if d % SUBLANES == 0:
        return jnp.concatenate([jnp.full((d, x.shape[1]), fill, x.dtype), x[:-d]], axis=0)
    return jnp.where(row >= d, pltpu.roll(x, d, 0), fill)


def _sigmoid(x):
    return 0.5 * (1.0 + jnp.tanh(0.5 * x))


def _doubling_scan(a, b, row):
    n = a.shape[0]
    d = 1
    while d < n:
        b = a * _shift_rows(b, d, 0.0, row) + b
        if 2 * d < n:
            a = a * _shift_rows(a, d, 1.0, row)
        d *= 2
    return b


def _lru_kernel(xb_ref, gb_ref, cw_ref, cb_ref, wr_ref, br_ref, wi_ref, bi_ref, lam_ref, o_ref, a_s, b_s):
    x = xb_ref[...]
    S = x.shape[0]
    row = lax.broadcasted_iota(I32, x.shape, 0)
    xc = cb_ref[...] + _shift_rows(x, CONV_WIDTH - 1, 0.0, row) * cw_ref[0:1, :]
    for j in range(1, CONV_WIDTH):
        d = CONV_WIDTH - 1 - j
        xs = x if d == 0 else _shift_rows(x, d, 0.0, row)
        xc = xc + xs * cw_ref[j:j + 1, :]
    xcb = xc.astype(BF16)
    r = _sigmoid(_dot(xcb, wr_ref[0]) + br_ref[...])
    ig = _sigmoid(_dot(xcb, wi_ref[0]) + bi_ref[...])
    z = -lam_ref[...]
    softplus = jnp.maximum(z, 0.0) + jnp.log1p(jnp.exp(-jnp.abs(z)))
    log_a = -LRU_C * r * softplus
    a = jnp.exp(log_a)
    b = jnp.sqrt(-jnp.tanh(log_a) * (1.0 + a * a)) * (ig * xc)
    sub = row & (SUBLANES - 1)
    d = 1
    while d < SUBLANES:
        keep = sub >= d
        b = a * jnp.where(keep, pltpu.roll(b, d, 0), 0.0) + b
        a = a * jnp.where(keep, pltpu.roll(a, d, 0), 1.0)
        d *= 2
    a_s[...] = a
    b_s[...] = b
    groups = S // SUBLANES
    last = pl.ds(SUBLANES - 1, groups, stride=SUBLANES)
    g_row = lax.broadcasted_iota(I32, (groups, x.shape[1]), 0)
    h_last = _doubling_scan(a_s[last, :], b_s[last, :], g_row)
    carry = _shift_rows(h_last, 1, 0.0, g_row)
    carry = jnp.broadcast_to(carry[:, None, :], (groups, SUBLANES, x.shape[1])).reshape(x.shape)
    h = b_s[...] + a_s[...] * carry
    g = gb_ref[...]
    gelu = 0.5 * g * (1.0 + jnp.tanh(float(np.sqrt(2.0 / np.pi)) * (g + 0.044715 * (g * g * g))))
    o_ref[...] = h * gelu


def _lru(lru, conv_w, conv_b, wr_bd, b_rec, wi_bd, b_in, lam, B, S):
    N = lru.shape[0]
    nt = LRU_WIDTH // LANES
    vec = lambda b, j: (0, j)
    return pl.pallas_call(
        _lru_kernel,
        grid=(B, nt),
        in_specs=[pl.BlockSpec((S, LANES), lambda b, j: (b, j)),
                  pl.BlockSpec((S, LANES), lambda b, j: (b, nt + j)),
                  pl.BlockSpec((CONV_WIDTH, LANES), vec),
                  pl.BlockSpec((1, LANES), vec),
                  pl.BlockSpec((1, LANES, LANES), lambda b, j: (j, 0, 0)),
                  pl.BlockSpec((1, LANES), vec),
                  pl.BlockSpec((1, LANES, LANES), lambda b, j: (j, 0, 0)),
                  pl.BlockSpec((1, LANES), vec),
                  pl.BlockSpec((1, LANES), vec)],
        out_specs=pl.BlockSpec((S, LANES), lambda b, j: (b, j)),
        out_shape=jax.ShapeDtypeStruct((N, LRU_WIDTH), F32),
        scratch_shapes=[pltpu.VMEM((S, LANES), F32), pltpu.VMEM((S, LANES), F32)],
        compiler_params=_params("arbitrary", "arbitrary"),
        name="lru",
    )(lru, lru, conv_w, conv_b.reshape(1, -1), wr_bd, b_rec.reshape(1, -1), wi_bd,
      b_in.reshape(1, -1), lam.reshape(1, -1))


def _block_diag_tiles(w):
    per = LANES // LRU_BLOCK_DIM
    nt = LRU_WIDTH // LANES
    w5 = w.reshape(nt, per, LRU_BLOCK_DIM, 1, LRU_BLOCK_DIM)
    eye = jnp.eye(per, dtype=w.dtype).reshape(1, per, 1, per, 1)
    return (w5 * eye).reshape(nt, LANES, LANES)


def _first_index_of_max(vals, lane):
    m = jnp.max(vals, axis=-1, keepdims=True)
    idx = jnp.min(jnp.where(vals == m, lane, 4 * LANES), axis=-1, keepdims=True)
    return m, idx


def _weight_pieces(out, w, lane, first):
    rest = w
    for k in range(W_PIECES):
        piece = rest.astype(BF16).astype(F32)
        out = jnp.where(lane == first + k, piece, out)
        rest = rest - piece
    return out


def _merge_kernel(attn_ref, lru_ref, g_ref, x_ref, mod_ref, wpa_ref, wpb_ref, wo_ref, nw_ref, wr_ref,
                  br_ref, x1_ref, hsl_ref, route_ref, gc_ref):
    D = x_ref.shape[1]
    tm = x_ref.shape[0]

    y_a = _dot(attn_ref[...].astype(BF16), wpa_ref[...])
    y_b = _dot(lru_ref[...].astype(BF16), wpb_ref[...])
    merged = g_ref[:, :D].astype(F32) * y_a + g_ref[:, D:].astype(F32) * y_b
    x1 = x_ref[...] + mod_ref[0, 2:3, :] * _dot(merged.astype(BF16), wo_ref[...])
    x1_ref[...] = x1
    h2 = _rms_mod(x1, nw_ref[...], mod_ref[0, 3:4, :], mod_ref[0, 4:5, :])
    h2b = h2.astype(BF16)

    h2l = (h2 - h2b.astype(F32)).astype(BF16)
    hi_hi_lo = _dot(h2b, wr_ref[...])
    logits = hi_hi_lo[:, :LANES] + (_dot(h2l, wr_ref[:, :LANES]) + hi_hi_lo[:, LANES:]) + br_ref[...]
    lane = lax.broadcasted_iota(I32, logits.shape, 1)
    gl = jnp.where(lane < N_GROUPS, logits, -jnp.inf)
    gmax, g_sel = _first_index_of_max(gl, lane)
    g_weight = 1.0 / jnp.sum(jnp.exp(gl - gmax), axis=-1, keepdims=True)
    e_lo = N_GROUPS + g_sel * EXPERTS_PER_GROUP
    el = jnp.where((lane >= e_lo) & (lane < e_lo + EXPERTS_PER_GROUP), logits, -jnp.inf)
    v1, i1 = _first_index_of_max(el, lane)
    el2 = jnp.where(lane == i1, -jnp.inf, el)
    v2, i2 = _first_index_of_max(el2, lane)
    e2x = jnp.exp(v2 - v1)
    w1 = g_weight / (1.0 + e2x)
    w2 = g_weight * e2x / (1.0 + e2x)

    hot1 = lane == i1
    hot2 = lane == i2
    hot = jnp.where(hot1, 1.0, jnp.where(hot2, 1.0, 0.0))
    gcnt = jnp.floor((jnp.sum(hot, axis=0, keepdims=True) + (GRAN - 1.0)) * (1.0 / GRAN))
    gc_ref[0] = gcnt
    rr = lax.broadcasted_iota(I32, (LANES, LANES), 0)
    cc = lax.broadcasted_iota(I32, (LANES, LANES), 1)
    upper = jnp.where(rr < cc, 1.0, 0.0).astype(BF16)
    run_start = GRAN * _dot(jnp.broadcast_to(gcnt, (8, LANES)).astype(BF16), upper)[0:1, :]
    rr = lax.broadcasted_iota(I32, (tm, tm), 0)
    cc = lax.broadcasted_iota(I32, (tm, tm), 1)
    lower = jnp.where(cc < rr, 1.0, 0.0).astype(BF16)
    pos = _dot(lower, hot.astype(BF16)) + run_start
    pos1 = jnp.where(hot1, pos, 0.0)
    pos2 = jnp.where(hot2, pos, 0.0)
    lp1 = jnp.sum(pos1, axis=-1, keepdims=True)
    lp2 = jnp.sum(pos2, axis=-1, keepdims=True)
    route_ref[...] = jnp.where(lane == R_LP1, lp1, jnp.where(lane == R_LP2, lp2, 0.0))

    aux = jnp.where(lane == AUX_E1, (i1 - N_GROUPS).astype(F32), 0.0)
    aux = _weight_pieces(_weight_pieces(aux, w1, lane, AUX_W1), w2, lane, AUX_W2)
    lp1r = jnp.sum(pos1.T, axis=0, keepdims=True).astype(I32)
    lp2r = jnp.sum(pos2.T, axis=0, keepdims=True).astype(I32)
    rows = lax.broadcasted_iota(I32, (LOCAL_ROWS, tm), 0)
    sort = jnp.where(rows == lp1r, 1.0, jnp.where(rows == lp2r, 1.0, 0.0)).astype(BF16)
    hsl_ref[...] = _dot(sort, jnp.concatenate([h2b, aux.astype(BF16)], axis=1)).astype(BF16)


def _merge(attn, lru_out, gates, x2, mod3, wpa, wpb, wo, norm_w, wr_hi_lo, b_router, S):
    N, D = x2.shape
    tm = TM_SORT
    n_tiles = N // tm
    steps_per_batch = S // tm
    const = lambda t: (0, 0)
    rows = lambda t: (t, 0)
    return pl.pallas_call(
        _merge_kernel,
        grid=(n_tiles,),
        in_specs=[pl.BlockSpec((tm, ATTN_OUT), rows),
                  pl.BlockSpec((tm, LRU_WIDTH), rows),
                  pl.BlockSpec((tm, 2 * D), rows),
                  pl.BlockSpec((tm, D), rows),
                  pl.BlockSpec((1, 6, D), lambda t: (t // steps_per_batch, 0, 0)),
                  pl.BlockSpec(wpa.shape, const),
                  pl.BlockSpec(wpb.shape, const),
                  pl.BlockSpec(wo.shape, const),
                  pl.BlockSpec((1, D), const),
                  pl.BlockSpec(wr_hi_lo.shape, const),
                  pl.BlockSpec((1, LANES), const)],
        out_specs=[pl.BlockSpec((tm, D), rows),
                   pl.BlockSpec((LOCAL_ROWS, D + AUX_W), rows),
                   pl.BlockSpec((tm, LANES), rows),
                   pl.BlockSpec((1, 1, LANES), lambda t: (t, 0, 0))],
        out_shape=[jax.ShapeDtypeStruct((N, D), F32),
                   jax.ShapeDtypeStruct((n_tiles * LOCAL_ROWS, D + AUX_W), BF16),
                   jax.ShapeDtypeStruct((N, LANES), F32),
                   jax.ShapeDtypeStruct((n_tiles, 1, LANES), F32)],
        compiler_params=_params("arbitrary"),
        name="merge",
    )(attn, lru_out, gates, x2, mod3, wpa, wpb, wo, norm_w.reshape(1, D), wr_hi_lo, b_router)


def _plan_sizes(n_tiles):
    max_gran = n_tiles * LOCAL_GRAN + N_EXPERTS * (GRAN_PER_TILE - 1)
    max_row_tiles = -(-max_gran // GRAN_PER_TILE)
    return max_row_tiles * GRAN_PER_TILE, max_row_tiles


def _plan_kernel(gc_ref, src_ref, texp_ref, nt_ref, dst_ref, lcnt_ref):
    gc = gc_ref[...]
    T = gc.shape[0]
    row = lax.broadcasted_iota(I32, (T, LANES), 0)
    local = lax.broadcasted_iota(I32, (T, LANES), 1).astype(F32)
    rr = lax.broadcasted_iota(I32, (LANES, LANES), 0)
    cc = lax.broadcasted_iota(I32, (LANES, LANES), 1)
    upper = jnp.where(rr < cc, 1.0, 0.0).astype(BF16)

    tiles = jnp.floor((jnp.sum(gc, axis=0, keepdims=True) + (GRAN_PER_TILE - 1.0)) * (1.0 / GRAN_PER_TILE))
    tile0 = _dot(jnp.broadcast_to(tiles, (SUBLANES, LANES)).astype(BF16), upper)[0:1, :]
    tile_end = tile0 + tiles
    before = gc
    d = 1
    while d < T:
        before = before + _shift_rows(before, d, 0.0, row)
        d *= 2
    before = before - gc
    run_global = GRAN_PER_TILE * tile0 + before
    run_local = _dot(gc.astype(BF16), upper)
    lcnt_ref[...] = jnp.broadcast_to(jnp.sum(gc, axis=1, keepdims=True), (T, LANES)).astype(I32)

    experts = range(N_GROUPS, N_GROUPS + N_EXPERTS)
    dst = jnp.zeros((T, LANES), F32)
    for e in experts:
        l0 = run_local[:, e:e + 1]
        inside = (local >= l0) & (local < l0 + gc[:, e:e + 1])
        dst = jnp.where(inside, run_global[:, e:e + 1] + (local - l0), dst)
    dst_ref[...] = dst.astype(I32)

    shape = src_ref.shape
    slot = (lax.broadcasted_iota(I32, shape, 0) * LANES + lax.broadcasted_iota(I32, shape, 1)).astype(F32)
    src = jnp.zeros(shape, F32)
    for e in experts:
        off = slot - GRAN_PER_TILE * tile0[0:1, e:e + 1]
        for t in range(T):
            p = before[t:t + 1, e:e + 1]
            hit = (off >= p) & (off < p + gc[t:t + 1, e:e + 1])
            src = jnp.where(hit, (t * LOCAL_GRAN) + run_local[t:t + 1, e:e + 1] + (off - p), src)
    src_ref[...] = src.astype(I32)

    shape = texp_ref.shape
    tile = (lax.broadcasted_iota(I32, shape, 0) * LANES + lax.broadcasted_iota(I32, shape, 1)).astype(F32)
    owner = jnp.zeros(shape, F32)
    for e in experts:
        owner = owner + jnp.where(tile >= tile_end[0:1, e:e + 1], 1.0, 0.0)
    texp_ref[...] = jnp.minimum(owner, N_EXPERTS - 1.0).astype(I32)
    nt_ref[...] = jnp.broadcast_to(tile_end[0:1, N_GROUPS + N_EXPERTS - 1:N_GROUPS + N_EXPERTS],
                                   nt_ref.shape).astype(I32)


def _plan(gc):
    n_tiles = gc.shape[0]
    max_gran, max_row_tiles = _plan_sizes(n_tiles)
    gran_rows = -(-max_gran // LANES)
    tile_rows = -(-max_row_tiles // LANES)
    src, texp, used, dst, lcnt = pl.pallas_call(
        _plan_kernel,
        out_shape=[jax.ShapeDtypeStruct((gran_rows, LANES), I32),
                   jax.ShapeDtypeStruct((tile_rows, LANES), I32),
                   jax.ShapeDtypeStruct((1, LANES), I32),
                   jax.ShapeDtypeStruct((n_tiles, LANES), I32),
                   jax.ShapeDtypeStruct((n_tiles, LANES), I32)],
        name="plan",
    )(gc)
    return (src.reshape(-1), texp.reshape(-1)[:max_row_tiles], used[0, :1],
            dst[:, :LOCAL_GRAN].reshape(-1), lcnt[:, 0])


def _granule_copies(idx_ref, first, count, src_hbm, dst_vmem, sem, wait):
    def body(k, c):
        g = idx_ref[first + k]
        cp = pltpu.make_async_copy(src_hbm.at[pl.ds(pl.multiple_of(g * GRAN, GRAN), GRAN)],
                                   dst_vmem.at[pl.ds(pl.multiple_of(k * GRAN, GRAN), GRAN)], sem)
        if wait:
            cp.wait()
        else:
            cp.start()
        return c
    lax.fori_loop(0, count, body, 0)


def _moe_kernel(src_ref, texp_ref, nt_ref, hsl_ref, w1_ref, w3_ref, w2_ref, ys_ref, xbuf, w1b, w3b, w2b, sem):
    i = pl.program_id(0)
    used = nt_ref[0]
    slot = i % MOE_BUFFERS
    D = ys_ref.shape[1]

    def gather(tile, wait):
        slot_ = tile % MOE_BUFFERS
        _granule_copies(src_ref, tile * GRAN_PER_TILE, GRAN_PER_TILE, hsl_ref, xbuf.at[slot_], sem.at[slot_], wait)

    @pl.when(i == 0)
    def _():
        for ahead in range(MOE_BUFFERS - 1):
            pl.when(ahead < used)(functools.partial(gather, ahead, False))

    @pl.when(i + (MOE_BUFFERS - 1) < used)
    def _():
        gather(i + (MOE_BUFFERS - 1), False)

    @pl.when(i < used)
    def _():
        gather(i, True)

        @pl.when((i == 0) | (texp_ref[i] != texp_ref[jnp.maximum(i - 1, 0)]))
        def _():
            w1b[...] = w1_ref[0].astype(BF16)
            w3b[...] = w3_ref[0].astype(BF16)
            w2b[...] = w2_ref[0].astype(BF16)

        xb = xbuf[slot]
        hb = xb[:, :D]
        aux = xb[:, D:].astype(F32)
        first = aux[:, AUX_E1:AUX_E1 + 1] == texp_ref[i].astype(F32)
        w_row = jnp.zeros((TM_MOE, 1), F32)
        for k in range(W_PIECES):
            w_row = w_row + jnp.where(first, aux[:, AUX_W1 + k:AUX_W1 + k + 1], aux[:, AUX_W2 + k:AUX_W2 + k + 1])
        a = _dot(hb, w1b[...])
        act = (a * _sigmoid(a)) * _dot(hb, w3b[...])
        ys_ref[...] = (w_row * _dot(act.astype(BF16), w2b[...])).astype(BF16)

    @pl.when(i >= used)
    def _():
        ys_ref[...] = jnp.zeros_like(ys_ref)


def _moe(src, texp, used, hsl, w1, w3, w2):
    max_row_tiles = texp.shape[0]
    D = w1.shape[1]
    F = w1.shape[2]
    grid_spec = pltpu.PrefetchScalarGridSpec(
        num_scalar_prefetch=3,
        grid=(max_row_tiles,),
        in_specs=[pl.BlockSpec(memory_space=pl.ANY),
                  pl.BlockSpec((1, D, F), lambda i, s, te, nt: (te[i], 0, 0)),
                  pl.BlockSpec((1, D, F), lambda i, s, te, nt: (te[i], 0, 0)),
                  pl.BlockSpec((1, F, D), lambda i, s, te, nt: (te[i], 0, 0))],
        out_specs=pl.BlockSpec((TM_MOE, D), lambda i, s, te, nt: (i, 0)),
        scratch_shapes=[pltpu.VMEM((MOE_BUFFERS, TM_MOE, D + AUX_W), BF16),
                        pltpu.VMEM((D, F), BF16), pltpu.VMEM((D, F), BF16), pltpu.VMEM((F, D), BF16),
                        pltpu.SemaphoreType.DMA((MOE_BUFFERS,))],
    )
    return pl.pallas_call(
        _moe_kernel,
        grid_spec=grid_spec,
        out_shape=jax.ShapeDtypeStruct((max_row_tiles * TM_MOE, D), BF16),
        compiler_params=_params("arbitrary"),
        name="moe",
    )(src, texp, used, hsl, w1, w3, w2)


def _combine_kernel(dst_ref, lcnt_ref, ys_ref, x1_ref, route_ref, mod_ref, o_ref, ybuf, sem):
    t = pl.program_id(0)
    slot = t % 2
    tm = x1_ref.shape[0]

    def gather(tile, slot_, wait):
        _granule_copies(dst_ref, tile * LOCAL_GRAN, lcnt_ref[tile], ys_ref, ybuf.at[slot_], sem.at[slot_], wait)

    @pl.when(t == 0)
    def _():
        ybuf[...] = jnp.zeros_like(ybuf)
        gather(0, 0, False)

    @pl.when(t + 1 < pl.num_programs(0))
    def _():
        gather(t + 1, 1 - slot, False)

    gather(t, slot, True)
    lp1 = route_ref[:, R_LP1:R_LP1 + 1].astype(I32)
    lp2 = route_ref[:, R_LP2:R_LP2 + 1].astype(I32)
    col = lax.broadcasted_iota(I32, (tm, LOCAL_ROWS), 1)
    unsort = jnp.where(col == lp1, 1.0, jnp.where(col == lp2, 1.0, 0.0)).astype(BF16)
    o_ref[...] = x1_ref[...] + mod_ref[0, 5:6, :] * _dot(unsort, ybuf[slot])


def _combine(dst, lcnt, ys, x1, route, mod3, S):
    N, D = x1.shape
    tm = TM_SORT
    steps_per_batch = S // tm
    grid_spec = pltpu.PrefetchScalarGridSpec(
        num_scalar_prefetch=2,
        grid=(N // tm,),
        in_specs=[pl.BlockSpec(memory_space=pl.ANY),
                  pl.BlockSpec((tm, D), lambda t, d, c: (t, 0)),
                  pl.BlockSpec((tm, LANES), lambda t, d, c: (t, 0)),
                  pl.BlockSpec((1, 6, D), lambda t, d, c: (t // steps_per_batch, 0, 0))],
        out_specs=pl.BlockSpec((tm, D), lambda t, d, c: (t, 0)),
        scratch_shapes=[pltpu.VMEM((2, LOCAL_ROWS, D), BF16), pltpu.SemaphoreType.DMA((2,))],
    )
    return pl.pallas_call(
        _combine_kernel,
        grid_spec=grid_spec,
        out_shape=jax.ShapeDtypeStruct((N, D), F32),
        compiler_params=_params("arbitrary"),
        name="combine",
    )(dst, lcnt, ys, x1, route, mod3)


def _layer(x2, mod3, B, S, norm_mix_w, w_in, q_norm_w, k_norm_w, conv_w, conv_b, w_rec_gate, b_rec_gate,
           w_in_gate, b_in_gate, lru_lambda, w_proj_a, w_proj_b, w_out, norm_ffn_w, w_group, b_group,
           w_expert_router, b_expert_router, w1, w3, w2):
    N, D = x2.shape
    w_all = jnp.pad(w_in.astype(BF16), ((0, 0), (0, W_IN_PAD - D_IN)))
    qa, kv, lru, gates = _proj(x2, mod3, norm_mix_w, w_all, S)
    attn = _attn(qa, kv, q_norm_w, k_norm_w, B, S)
    lru_out = _lru(lru, conv_w, conv_b, _block_diag_tiles(w_rec_gate).astype(BF16), b_rec_gate,
                   _block_diag_tiles(w_in_gate).astype(BF16), b_in_gate, lru_lambda, B, S)

    n_r = N_GROUPS + N_EXPERTS
    w_router = jnp.concatenate([w_group, w_expert_router, jnp.zeros((D, LANES - n_r), F32)], axis=1)
    wr_hi = w_router.astype(BF16)
    wr_lo = (w_router - wr_hi.astype(F32)).astype(BF16)
    b_router = jnp.concatenate([b_group, b_expert_router, jnp.zeros((LANES - n_r,), F32)]).reshape(1, LANES)
    x1, hsl, route, gc = _merge(attn, lru_out, gates, x2, mod3, w_proj_a.astype(BF16), w_proj_b.astype(BF16),
                                w_out.astype(BF16), norm_ffn_w, jnp.concatenate([wr_hi, wr_lo], axis=1),
                                b_router, S)

    src, texp, used, dst, lcnt = _plan(gc.reshape(gc.shape[0], LANES))
    ys = _moe(src, texp, used, hsl, w1, w3, w2)
    return _combine(dst, lcnt, ys, x1, route, mod3, S)


def kernel(x, c, ada_w, ada_b, norm_mix_w, w_in, q_norm_w, k_norm_w, conv_w, conv_b, w_rec_gate, b_rec_gate,
           w_in_gate, b_in_gate, lru_lambda, w_proj_a, w_proj_b, w_out, norm_ffn_w, w_group, b_group,
           w_expert_router, b_expert_router, w1, w3, w2):
    B, S, D = x.shape
    x2 = x.reshape(B * S, D)
    for l in range(ada_w.shape[0]):
        mod3 = _ada(c, ada_w[l], ada_b[l]).reshape(B, 6, D)
        x2 = _layer(x2, mod3, B, S, norm_mix_w[l], w_in[l], q_norm_w[l], k_norm_w[l], conv_w[l], conv_b[l],
                    w_rec_gate[l], b_rec_gate[l], w_in_gate[l], b_in_gate[l], lru_lambda[l], w_proj_a[l],
                    w_proj_b[l], w_out[l], norm_ffn_w[l], w_group[l], b_group[l], w_expert_router[l],
                    b_expert_router[l], w1[l], w3[l], w2[l])
    return x2.reshape(B, S, D)
```

```python
import functools

import jax
import jax.numpy as jnp
import numpy as np
from jax import lax
from jax.experimental import pallas as pl
from jax.experimental.pallas import tpu as pltpu

F32 = jnp.float32
BF16 = jnp.bfloat16
I32 = jnp.int32

EPS = 1e-6
CHUNK = 64
Q_BLOCK = 128
N_HEADS_A = 8
HEAD_DIM_A = 64
KV_DIM_A = 64
ATTN_OUT = N_HEADS_A * KV_DIM_A
N_IDX_HEADS = 4
IDX_DIM = 64
TOPK_MAX = 256
LRU_WIDTH = 512
LRU_BLOCKS = 8
LRU_BLOCK_DIM = LRU_WIDTH // LRU_BLOCKS
CONV_WIDTH = 4
LRU_C = 8.0
N_GROUPS = 4
EXPERTS_PER_GROUP = 8
N_EXPERTS = N_GROUPS * EXPERTS_PER_GROUP
D_FF_EXPERT = 256

LANES = 128
INT_MIN = -2 ** 31
VMEM_LIMIT = 56 * 1024 * 1024

D_MODEL = 1024
COL_Q = 0
COL_K = COL_Q + N_HEADS_A * HEAD_DIM_A
COL_V = COL_K + KV_DIM_A
COL_QI = COL_V + KV_DIM_A
COL_KI = COL_QI + N_IDX_HEADS * IDX_DIM
COL_WI = COL_KI + IDX_DIM
COL_LX = COL_WI + N_IDX_HEADS
D_IN = COL_LX + 2 * LRU_WIDTH + 2 * D_MODEL
ATTN_COLS = -(-COL_LX // LANES) * LANES
TAIL_COLS = D_IN - COL_LX
TAIL_SHIFT = COL_LX % LANES
W_IN_PAD = (COL_LX // LANES) * LANES + TAIL_COLS + LANES
assert COL_K % LANES == 0 and COL_QI % LANES == 0 and COL_KI % LANES == 0 and TAIL_COLS % LANES == 0

QA_W = N_HEADS_A * HEAD_DIM_A + N_IDX_HEADS * IDX_DIM
KV_W = (COL_QI - COL_K) + (ATTN_COLS - COL_KI)
K_OFF, V_OFF = 0, KV_DIM_A
KI_OFF = COL_QI - COL_K
WI_OFF = KI_OFF + IDX_DIM

TM_PROJ = 512

TM_SORT = 512
GRAN = 16
TM_MOE = 256
GRAN_PER_TILE = TM_MOE // GRAN
MOE_BUFFERS = 3
GATHER_UNROLL = 8
LOCAL_ROWS = -(-(2 * TM_SORT + N_EXPERTS * (GRAN - 1)) // LANES) * LANES
LOCAL_GRAN = LOCAL_ROWS // GRAN
AUX_W = LANES
W_PIECES = 3
AUX_W1, AUX_W2 = 0, W_PIECES
AUX_E1 = 2 * W_PIECES

R_LP1, R_LP2 = 0, 1


def _dot(a, b, **kw):
    return jnp.dot(a, b, preferred_element_type=F32, **kw)


def _params(*sem):
    return pltpu.CompilerParams(dimension_semantics=sem, vmem_limit_bytes=VMEM_LIMIT)


def _ada_kernel(c_ref, w_ref, b_ref, o_ref):
    c = c_ref[...]
    cond = c * jax.nn.sigmoid(c)
    o_ref[...] = _dot(cond, w_ref[...], precision=lax.Precision.HIGHEST) + b_ref[...]


def _ada(c, ada_w, ada_b):
    B, D = c.shape
    n_out = ada_w.shape[1]
    tn = 1024
    return pl.pallas_call(
        _ada_kernel,
        grid=(n_out // tn,),
        in_specs=[pl.BlockSpec((B, D), lambda j: (0, 0)),
                  pl.BlockSpec((D, tn), lambda j: (0, j)),
                  pl.BlockSpec((1, tn), lambda j: (0, j))],
        out_specs=pl.BlockSpec((B, tn), lambda j: (0, j)),
        out_shape=jax.ShapeDtypeStruct((B, n_out), F32),
        compiler_params=_params("arbitrary"),
        name="ada",
    )(c, ada_w, ada_b.reshape(1, n_out))


def _rms_mod(x, w, shift, scale):
    ms = jnp.mean(x * x, axis=-1, keepdims=True)
    y = x * lax.rsqrt(ms + EPS) * w
    return y * (1.0 + scale) + shift


def _proj_kernel(x_ref, mod_ref, nw_ref, w_ref, oqa_ref, okv_ref, olru_ref, og_ref, wt_s):
    @pl.when(pl.program_id(0) == 0)
    def _():
        rr = lax.broadcasted_iota(I32, (2 * LANES, LANES), 0)
        cc = lax.broadcasted_iota(I32, (2 * LANES, LANES), 1)
        shift = jnp.where(rr == cc + TAIL_SHIFT, 1.0, 0.0).astype(BF16)
        for j in range(TAIL_COLS // LANES):
            src = (COL_LX // LANES + j) * LANES
            wt_s[:, j * LANES:(j + 1) * LANES] = _dot(w_ref[:, src:src + 2 * LANES], shift).astype(BF16)

    h = _rms_mod(x_ref[...], nw_ref[...], mod_ref[0, 0:1, :], mod_ref[0, 1:2, :])
    hb = h.astype(BF16)
    pa = _dot(hb, w_ref[:, :ATTN_COLS])
    oqa_ref[...] = jnp.concatenate([pa[:, COL_Q:COL_K], pa[:, COL_QI:COL_KI]], axis=1)
    okv_ref[...] = jnp.concatenate([pa[:, COL_K:COL_QI], pa[:, COL_KI:ATTN_COLS]], axis=1)
    olru_ref[...] = _dot(hb, wt_s[:, :2 * LRU_WIDTH])
    og_ref[...] = _sigmoid(_dot(hb, wt_s[:, 2 * LRU_WIDTH:])).astype(BF16)


def _proj(x2, mod3, norm_w, w_all, S):
    N, D = x2.shape
    tm = TM_PROJ
    steps_per_batch = S // tm
    const = lambda t: (0, 0)
    return pl.pallas_call(
        _proj_kernel,
        grid=(N // tm,),
        in_specs=[pl.BlockSpec((tm, D), lambda t: (t, 0)),
                  pl.BlockSpec((1, 6, D), lambda t: (t // steps_per_batch, 0, 0)),
                  pl.BlockSpec((1, D), const),
                  pl.BlockSpec(w_all.shape, const)],
        out_specs=[pl.BlockSpec((tm, QA_W), lambda t: (t, 0)),
                   pl.BlockSpec((tm, KV_W), lambda t: (t, 0)),
                   pl.BlockSpec((tm, 2 * LRU_WIDTH), lambda t: (t, 0)),
                   pl.BlockSpec((tm, 2 * D), lambda t: (t, 0))],
        out_shape=[jax.ShapeDtypeStruct((N, QA_W), F32),
                   jax.ShapeDtypeStruct((N, KV_W), F32),
                   jax.ShapeDtypeStruct((N, 2 * LRU_WIDTH), F32),
                   jax.ShapeDtypeStruct((N, 2 * D), BF16)],
        scratch_shapes=[pltpu.VMEM((D, TAIL_COLS), BF16)],
        compiler_params=_params("arbitrary"),
        name="proj",
    )(x2, mod3, norm_w.reshape(1, D), w_all)


ATTN_BUCKETS = 8
ROW_FOLD = 64
KEY_ROWS = 256
COUNT_CHAINS = 2
LOG2E = float(np.log2(np.e))
POS_RADIX = 256
SLOPE_PIECES = 3
V_ONES = 16


def _bf16_pieces(c, n):
    out = []
    rest = float(c)
    for _ in range(n):
        p = float(np.asarray(rest, np.float32).astype(BF16).astype(np.float32))
        out.append(p)
        rest -= p
    return out


def _reduce_rows(op, x):
    r = x.shape[0]
    if r > ROW_FOLD and r % ROW_FOLD == 0:
        x = op(x.reshape(r // ROW_FOLD, ROW_FOLD, x.shape[1]), axis=0)
    return op(x, axis=0, keepdims=True)


def _attn_block(s_eff, topk, i, qa_ref, kvq_ref, qnw_ref, o_ref, kn_s, vt_s, ki_s, key_s, bias_s, lg_s, ot_s):
    nq = N_HEADS_A * HEAD_DIM_A
    q_t = qa_ref[:, :nq].T
    qi_t = qa_ref[:, nq:].T
    wq_t = kvq_ref[...].T[WI_OFF:WI_OFF + N_IDX_HEADS, :] * (N_IDX_HEADS ** -0.5 * IDX_DIM ** -0.5)

    q_pos = i * Q_BLOCK + lax.broadcasted_iota(I32, (1, Q_BLOCK), 1)
    limit = (q_pos // CHUNK + 1) * CHUNK
    chunks = [slice(r0, min(r0 + KEY_ROWS, s_eff)) for r0 in range(0, s_eff, KEY_ROWS)]

    qi = [qi_t[h * IDX_DIM:(h + 1) * IDX_DIM, :].astype(BF16) for h in range(N_IDX_HEADS)]
    for rows in chunks:
        ki = ki_s[rows, :]
        score = jnp.zeros((rows.stop - rows.start, Q_BLOCK), F32)
        for h in range(N_IDX_HEADS):
            score = score + jnp.maximum(_dot(ki, qi[h]), 0.0) * wq_t[h:h + 1, :]
        k_pos = rows.start + lax.broadcasted_iota(I32, score.shape, 0)
        key_s[rows, :] = jnp.where(k_pos < limit, score, -jnp.inf)

    def count(mask):
        return _reduce_rows(jnp.sum, jnp.where(mask, 1.0, 0.0))

    def count_keys(test):
        acc = [jnp.zeros((ROW_FOLD, Q_BLOCK), F32) for _ in range(COUNT_CHAINS)]
        for n, r0 in enumerate(range(0, s_eff, ROW_FOLD)):
            hit = jnp.where(test(key_s[r0:r0 + ROW_FOLD, :]), 1.0, 0.0)
            acc[n % COUNT_CHAINS] = acc[n % COUNT_CHAINS] + hit
        return jnp.sum(functools.reduce(lambda x, y: x + y, acc), axis=0, keepdims=True)

    def ordered_to_float(c):
        s = c ^ INT_MIN
        return pltpu.bitcast(jnp.where(s < 0, INT_MIN | (-s), s), F32)

    def bisect(it, r):
        c = r | jnp.left_shift(jnp.int32(1), 31 - it)
        cand = ordered_to_float(c)
        return jnp.where(count_keys(lambda slab: slab >= cand) >= topk, c, r)

    def select_by_threshold():
        thr = ordered_to_float(lax.fori_loop(0, 32, bisect, jnp.zeros((1, Q_BLOCK), I32)))

        n_ge = count_keys(lambda slab: slab >= thr)
        clean = jnp.max(jnp.where((n_ge == topk) & (thr > -jnp.inf), 0.0, 1.0)) == 0.0

        @pl.when(clean)
        def _():
            for rows in chunks:
                bias_s[rows, :] = jnp.where(key_s[rows, :] >= thr, 0.0, -jnp.inf)

        @pl.when(jnp.logical_not(clean))
        def _():
            key = key_s[:s_eff, :]
            need = topk - count(key > thr)
            rr = lax.broadcasted_iota(I32, (LANES, LANES), 0)
            cc = lax.broadcasted_iota(I32, (LANES, LANES), 1)
            tri = jnp.where(cc < rr, 1.0, 0.0).astype(BF16)
            seen = jnp.zeros((1, Q_BLOCK), F32)
            for c in range(s_eff // LANES):
                kc = key[c * LANES:(c + 1) * LANES, :]
                eq = kc == thr
                eqf = jnp.where(eq, 1.0, 0.0)
                rank = _dot(tri, eqf.astype(BF16)) + seen
                seen = seen + jnp.sum(eqf, axis=0, keepdims=True)
                tie_bias = jnp.where(jnp.where(eq, rank, topk) < need, 0.0, -jnp.inf)
                bias = jnp.where(kc > thr, 0.0, tie_bias)
                k_pos = c * LANES + lax.broadcasted_iota(I32, (LANES, Q_BLOCK), 0)
                bias_s[c * LANES:(c + 1) * LANES, :] = jnp.where(k_pos < limit, bias, -jnp.inf)

    if s_eff > topk:
        select_by_threshold()
    else:
        for rows in chunks:
            k_pos = rows.start + lax.broadcasted_iota(I32, (rows.stop - rows.start, Q_BLOCK), 0)
            bias_s[rows, :] = jnp.where(k_pos < limit, 0.0, -jnp.inf)

    qnw = qnw_ref[...]
    crow = lax.broadcasted_iota(I32, (LANES - HEAD_DIM_A, Q_BLOCK), 0)
    r_k = lax.broadcasted_iota(I32, (Q_BLOCK, Q_BLOCK), 0)
    c_q = lax.broadcasted_iota(I32, (Q_BLOCK, Q_BLOCK), 1)
    after = 2.0 * jnp.maximum(r_k - c_q, 0).astype(F32)
    band = pl.ds(pl.multiple_of(i * Q_BLOCK, Q_BLOCK), Q_BLOCK)
    slope = [sum(_bf16_pieces(LOG2E * 2.0 ** -(h + 1), SLOPE_PIECES)) for h in range(N_HEADS_A)]
    row_max = []
    for h in range(N_HEADS_A):
        pieces = _bf16_pieces(LOG2E * 2.0 ** -(h + 1), SLOPE_PIECES)
        qh = q_t[h * HEAD_DIM_A:(h + 1) * HEAD_DIM_A, :]
        ms = jnp.mean(qh * qh, axis=0, keepdims=True)
        qn = (qh * lax.rsqrt(ms + EPS) * qnw) * (HEAD_DIM_A ** -0.5) * LOG2E
        slopes = jnp.zeros(crow.shape, F32)
        for k, piece in enumerate(pieces):
            slopes = jnp.where(crow == 2 * k, POS_RADIX * piece, jnp.where(crow == 2 * k + 1, piece, slopes))
        qa = jnp.concatenate([qn, slopes], axis=0).astype(BF16)
        folded = []
        for rows in chunks:
            logits = _dot(kn_s[rows, :], qa) + bias_s[rows, :]
            lg_s[h, rows, :] = logits
            folded.append(jnp.max(logits.reshape(-1, ROW_FOLD, Q_BLOCK), axis=0))
        row_max.append(jnp.max(functools.reduce(jnp.maximum, folded), axis=0, keepdims=True))
    for h in range(N_HEADS_A):
        lg_s[h, band, :] = lg_s[h, band, :] - slope[h] * after
    for h in range(N_HEADS_A):
        acc = jnp.zeros((KV_DIM_A + V_ONES, Q_BLOCK), F32)
        for rows in chunks:
            p = jnp.exp2(lg_s[h, rows, :] - row_max[h])
            acc = acc + _dot(vt_s[:, rows], p.astype(BF16))
        ot_s[h * KV_DIM_A:(h + 1) * KV_DIM_A, :] = acc[:KV_DIM_A, :] / acc[KV_DIM_A:KV_DIM_A + 1, :]
    o_ref[...] = ot_s[...].T


def _attn_kernel(qa_ref, kvq_ref, kv_ref, qnw_ref, knw_ref, o_ref, kn_s, vt_s, ki_s, key_s, bias_s, lg_s, ot_s):
    i = pl.program_id(1)
    S = kv_ref.shape[0]
    nb = S // Q_BLOCK
    topk = float(min(TOPK_MAX, S // 4))

    @pl.when(i == 0)
    def _():
        kv = kv_ref[:, :LANES]
        lane = lax.broadcasted_iota(I32, kv.shape, 1)
        pos = lax.broadcasted_iota(I32, kv.shape, 0)
        ms = jnp.sum(jnp.where(lane < KV_DIM_A, kv * kv, 0.0), axis=-1, keepdims=True) * (1.0 / KV_DIM_A)
        kn = kv * lax.rsqrt(ms + EPS) * knw_ref[...]
        digit = jnp.where((lane & 1) == 0, pos // POS_RADIX, pos % POS_RADIX).astype(F32)
        is_digit = (lane >= KV_DIM_A) & (lane < KV_DIM_A + 2 * SLOPE_PIECES)
        kn_s[...] = jnp.where(is_digit, digit, kn).astype(BF16)
        vt_s[:KV_DIM_A, :] = kv.T[V_OFF:V_OFF + KV_DIM_A, :].astype(BF16)
        vt_s[KV_DIM_A:, :] = jnp.ones((V_ONES, kv.shape[0]), BF16)
        ki_s[...] = kv_ref[:, KI_OFF:KI_OFF + IDX_DIM].astype(BF16)

    n_buckets = min(ATTN_BUCKETS, nb)
    per = nb // n_buckets
    for j in range(n_buckets):
        pl.when(i // per == j)(functools.partial(
            _attn_block, (j + 1) * per * Q_BLOCK, topk, i, qa_ref, kvq_ref, qnw_ref, o_ref,
            kn_s, vt_s, ki_s, key_s, bias_s, lg_s, ot_s))


def _attn(qa, kv, q_norm_w, k_norm_w, B, S):
    N = qa.shape[0]
    nb = S // Q_BLOCK
    return pl.pallas_call(
        _attn_kernel,
        grid=(B, nb),
        in_specs=[pl.BlockSpec((Q_BLOCK, QA_W), lambda b, i: (b * nb + i, 0)),
                  pl.BlockSpec((Q_BLOCK, KV_W), lambda b, i: (b * nb + i, 0)),
                  pl.BlockSpec((S, KV_W), lambda b, i: (b, 0)),
                  pl.BlockSpec((HEAD_DIM_A, 1), lambda b, i: (0, 0)),
                  pl.BlockSpec((1, LANES), lambda b, i: (0, 0))],
        out_specs=pl.BlockSpec((Q_BLOCK, ATTN_OUT), lambda b, i: (b * nb + i, 0)),
        out_shape=jax.ShapeDtypeStruct((N, ATTN_OUT), F32),
        scratch_shapes=[pltpu.VMEM((S, LANES), BF16),
                        pltpu.VMEM((KV_DIM_A + V_ONES, S), BF16),
                        pltpu.VMEM((S, IDX_DIM), BF16),
                        pltpu.VMEM((S, Q_BLOCK), F32),
                        pltpu.VMEM((S, Q_BLOCK), F32),
                        pltpu.VMEM((N_HEADS_A, S, Q_BLOCK), F32),
                        pltpu.VMEM((ATTN_OUT, Q_BLOCK), F32)],
        compiler_params=_params("arbitrary", "arbitrary"),
        name="attn",
    )(qa, kv, kv, q_norm_w.reshape(HEAD_DIM_A, 1),
      jnp.pad(k_norm_w, (0, LANES - KV_DIM_A)).reshape(1, LANES))


SUBLANES = 8


def _shift_rows(x, d, fill, row):
    if d % SUBLANES == 0:
        return jnp.concatenate([jnp.full((d, x.shape[1]), fill, x.dtype), x[:-d]], axis=0)
    return jnp.where(row >= d, pltpu.roll(x, d, 0), fill)


def _sigmoid(x):
    return 0.5 * (1.0 + jnp.tanh(0.5 * x))


def _doubling_scan(a, b, row):
    n = a.shape[0]
    d = 1
    while d < n:
        b = a * _shift_rows(b, d, 0.0, row) + b
        if 2 * d < n:
            a = a * _shift_rows(a, d, 1.0, row)
        d *= 2
    return b


def _lru_kernel(xb_ref, gb_ref, cw_ref, cb_ref, wr_ref, br_ref, wi_ref, bi_ref, lam_ref, o_ref, a_s, b_s):
    x = xb_ref[...]
    S = x.shape[0]
    row = lax.broadcasted_iota(I32, x.shape, 0)
    xc = cb_ref[...] + _shift_rows(x, CONV_WIDTH - 1, 0.0, row) * cw_ref[0:1, :]
    for j in range(1, CONV_WIDTH):
        d = CONV_WIDTH - 1 - j
        xs = x if d == 0 else _shift_rows(x, d, 0.0, row)
        xc = xc + xs * cw_ref[j:j + 1, :]
    xcb = xc.astype(BF16)
    r = _sigmoid(_dot(xcb, wr_ref[0]) + br_ref[...])
    ig = _sigmoid(_dot(xcb, wi_ref[0]) + bi_ref[...])
    z = -lam_ref[...]
    softplus = jnp.maximum(z, 0.0) + jnp.log1p(jnp.exp(-jnp.abs(z)))
    log_a = -LRU_C * r * softplus
    a = jnp.exp(log_a)
    b = jnp.sqrt(-jnp.tanh(log_a) * (1.0 + a * a)) * (ig * xc)
    sub = row & (SUBLANES - 1)
    d = 1
    while d < SUBLANES:
        keep = sub >= d
        b = a * jnp.where(keep, pltpu.roll(b, d, 0), 0.0) + b
        a = a * jnp.where(keep, pltpu.roll(a, d, 0), 1.0)
        d *= 2
    a_s[...] = a
    b_s[...] = b
    groups = S // SUBLANES
    last = pl.ds(SUBLANES - 1, groups, stride=SUBLANES)
    g_row = lax.broadcasted_iota(I32, (groups, x.shape[1]), 0)
    h_last = _doubling_scan(a_s[last, :], b_s[last, :], g_row)
    carry = _shift_rows(h_last, 1, 0.0, g_row)
    carry = jnp.broadcast_to(carry[:, None, :], (groups, SUBLANES, x.shape[1])).reshape(x.shape)
    h = b_s[...] + a_s[...] * carry
    g = gb_ref[...]
    gelu = 0.5 * g * (1.0 + jnp.tanh(float(np.sqrt(2.0 / np.pi)) * (g + 0.044715 * (g * g * g))))
    o_ref[...] = h * gelu


def _lru(lru, conv_w, conv_b, wr_bd, b_rec, wi_bd, b_in, lam, B, S):
    N = lru.shape[0]
    nt = LRU_WIDTH // LANES
    vec = lambda b, j: (0, j)
    return pl.pallas_call(
        _lru_kernel,
        grid=(B, nt),
        in_specs=[pl.BlockSpec((S, LANES), lambda b, j: (b, j)),
                  pl.BlockSpec((S, LANES), lambda b, j: (b, nt + j)),
                  pl.BlockSpec((CONV_WIDTH, LANES), vec),
                  pl.BlockSpec((1, LANES), vec),
                  pl.BlockSpec((1, LANES, LANES), lambda b, j: (j, 0, 0)),
                  pl.BlockSpec((1, LANES), vec),
                  pl.BlockSpec((1, LANES, LANES), lambda b, j: (j, 0, 0)),
                  pl.BlockSpec((1, LANES), vec),
                  pl.BlockSpec((1, LANES), vec)],
        out_specs=pl.BlockSpec((S, LANES), lambda b, j: (b, j)),
        out_shape=jax.ShapeDtypeStruct((N, LRU_WIDTH), F32),
        scratch_shapes=[pltpu.VMEM((S, LANES), F32), pltpu.VMEM((S, LANES), F32)],
        compiler_params=_params("arbitrary", "arbitrary"),
        name="lru",
    )(lru, lru, conv_w, conv_b.reshape(1, -1), wr_bd, b_rec.reshape(1, -1), wi_bd,
      b_in.reshape(1, -1), lam.reshape(1, -1))


def _block_diag_tiles(w):
    per = LANES // LRU_BLOCK_DIM
    nt = LRU_WIDTH // LANES
    w5 = w.reshape(nt, per, LRU_BLOCK_DIM, 1, LRU_BLOCK_DIM)
    eye = jnp.eye(per, dtype=w.dtype).reshape(1, per, 1, per, 1)
    return (w5 * eye).reshape(nt, LANES, LANES)


def _first_index_of_max(vals, lane):
    m = jnp.max(vals, axis=-1, keepdims=True)
    idx = jnp.min(jnp.where(vals == m, lane, 4 * LANES), axis=-1, keepdims=True)
    return m, idx


def _weight_pieces(out, w, lane, first):
    rest = w
    for k in range(W_PIECES):
        piece = rest.astype(BF16).astype(F32)
        out = jnp.where(lane == first + k, piece, out)
        rest = rest - piece
    return out


def _merge_kernel(attn_ref, lru_ref, g_ref, x_ref, mod_ref, wpa_ref, wpb_ref, wo_ref, nw_ref, wr_ref,
                  br_ref, x1_ref, hsl_ref, route_ref, gc_ref):
    D = x_ref.shape[1]
    tm = x_ref.shape[0]

    y_a = _dot(attn_ref[...].astype(BF16), wpa_ref[...])
    y_b = _dot(lru_ref[...].astype(BF16), wpb_ref[...])
    merged = g_ref[:, :D].astype(F32) * y_a + g_ref[:, D:].astype(F32) * y_b
    x1 = x_ref[...] + mod_ref[0, 2:3, :] * _dot(merged.astype(BF16), wo_ref[...])
    x1_ref[...] = x1
    h2 = _rms_mod(x1, nw_ref[...], mod_ref[0, 3:4, :], mod_ref[0, 4:5, :])
    h2b = h2.astype(BF16)

    h2l = (h2 - h2b.astype(F32)).astype(BF16)
    hi_hi_lo = _dot(h2b, wr_ref[...])
    logits = hi_hi_lo[:, :LANES] + (_dot(h2l, wr_ref[:, :LANES]) + hi_hi_lo[:, LANES:]) + br_ref[...]
    lane = lax.broadcasted_iota(I32, logits.shape, 1)
    gl = jnp.where(lane < N_GROUPS, logits, -jnp.inf)
    gmax, g_sel = _first_index_of_max(gl, lane)
    g_weight = 1.0 / jnp.sum(jnp.exp(gl - gmax), axis=-1, keepdims=True)
    e_lo = N_GROUPS + g_sel * EXPERTS_PER_GROUP
    el = jnp.where((lane >= e_lo) & (lane < e_lo + EXPERTS_PER_GROUP), logits, -jnp.inf)
    v1, i1 = _first_index_of_max(el, lane)
    el2 = jnp.where(lane == i1, -jnp.inf, el)
    v2, i2 = _first_index_of_max(el2, lane)
    e2x = jnp.exp(v2 - v1)
    w1 = g_weight / (1.0 + e2x)
    w2 = g_weight * e2x / (1.0 + e2x)

    hot1 = lane == i1
    hot2 = lane == i2
    hot = jnp.where(hot1, 1.0, jnp.where(hot2, 1.0, 0.0))
    gcnt = jnp.floor((jnp.sum(hot, axis=0, keepdims=True) + (GRAN - 1.0)) * (1.0 / GRAN))
    gc_ref[0] = gcnt
    rr = lax.broadcasted_iota(I32, (LANES, LANES), 0)
    cc = lax.broadcasted_iota(I32, (LANES, LANES), 1)
    upper = jnp.where(rr < cc, 1.0, 0.0).astype(BF16)
    run_start = GRAN * _dot(jnp.broadcast_to(gcnt, (8, LANES)).astype(BF16), upper)[0:1, :]
    rr = lax.broadcasted_iota(I32, (tm, tm), 0)
    cc = lax.broadcasted_iota(I32, (tm, tm), 1)
    lower = jnp.where(cc < rr, 1.0, 0.0).astype(BF16)
    pos = _dot(lower, hot.astype(BF16)) + run_start
    pos1 = jnp.where(hot1, pos, 0.0)
    pos2 = jnp.where(hot2, pos, 0.0)
    lp1 = jnp.sum(pos1, axis=-1, keepdims=True)
    lp2 = jnp.sum(pos2, axis=-1, keepdims=True)
    route_ref[...] = jnp.where(lane == R_LP1, lp1, jnp.where(lane == R_LP2, lp2, 0.0))

    aux = jnp.where(lane == AUX_E1, (i1 - N_GROUPS).astype(F32), 0.0)
    aux = _weight_pieces(_weight_pieces(aux, w1, lane, AUX_W1), w2, lane, AUX_W2)
    lp1r = jnp.sum(pos1.T, axis=0, keepdims=True).astype(I32)
    lp2r = jnp.sum(pos2.T, axis=0, keepdims=True).astype(I32)
    rows = lax.broadcasted_iota(I32, (LOCAL_ROWS, tm), 0)
    sort = jnp.where(rows == lp1r, 1.0, jnp.where(rows == lp2r, 1.0, 0.0)).astype(BF16)
    hsl_ref[...] = _dot(sort, jnp.concatenate([h2b, aux.astype(BF16)], axis=1)).astype(BF16)


def _merge(attn, lru_out, gates, x2, mod3, wpa, wpb, wo, norm_w, wr_hi_lo, b_router, S):
    N, D = x2.shape
    tm = TM_SORT
    n_tiles = N // tm
    steps_per_batch = S // tm
    const = lambda t: (0, 0)
    rows = lambda t: (t, 0)
    return pl.pallas_call(
        _merge_kernel,
        grid=(n_tiles,),
        in_specs=[pl.BlockSpec((tm, ATTN_OUT), rows),
                  pl.BlockSpec((tm, LRU_WIDTH), rows),
                  pl.BlockSpec((tm, 2 * D), rows),
                  pl.BlockSpec((tm, D), rows),
                  pl.BlockSpec((1, 6, D), lambda t: (t // steps_per_batch, 0, 0)),
                  pl.BlockSpec(wpa.shape, const),
                  pl.BlockSpec(wpb.shape, const),
                  pl.BlockSpec(wo.shape, const),
                  pl.BlockSpec((1, D), const),
                  pl.BlockSpec(wr_hi_lo.shape, const),
                  pl.BlockSpec((1, LANES), const)],
        out_specs=[pl.BlockSpec((tm, D), rows),
                   pl.BlockSpec((LOCAL_ROWS, D + AUX_W), rows),
                   pl.BlockSpec((tm, LANES), rows),
                   pl.BlockSpec((1, 1, LANES), lambda t: (t, 0, 0))],
        out_shape=[jax.ShapeDtypeStruct((N, D), F32),
                   jax.ShapeDtypeStruct((n_tiles * LOCAL_ROWS, D + AUX_W), BF16),
                   jax.ShapeDtypeStruct((N, LANES), F32),
                   jax.ShapeDtypeStruct((n_tiles, 1, LANES), F32)],
        compiler_params=_params("arbitrary"),
        name="merge",
    )(attn, lru_out, gates, x2, mod3, wpa, wpb, wo, norm_w.reshape(1, D), wr_hi_lo, b_router)


def _plan_sizes(n_tiles):
    max_gran = n_tiles * LOCAL_GRAN + N_EXPERTS * (GRAN_PER_TILE - 1)
    max_row_tiles = -(-max_gran // GRAN_PER_TILE)
    return max_row_tiles * GRAN_PER_TILE, max_row_tiles


def _plan_kernel(gc_ref, src_ref, texp_ref, nt_ref, dst_ref, lcnt_ref):
    gc = gc_ref[...]
    T = gc.shape[0]
    row = lax.broadcasted_iota(I32, (T, LANES), 0)
    local = lax.broadcasted_iota(I32, (T, LANES), 1).astype(F32)
    rr = lax.broadcasted_iota(I32, (LANES, LANES), 0)
    cc = lax.broadcasted_iota(I32, (LANES, LANES), 1)
    upper = jnp.where(rr < cc, 1.0, 0.0).astype(BF16)

    tiles = jnp.floor((jnp.sum(gc, axis=0, keepdims=True) + (GRAN_PER_TILE - 1.0)) * (1.0 / GRAN_PER_TILE))
    tile0 = _dot(jnp.broadcast_to(tiles, (SUBLANES, LANES)).astype(BF16), upper)[0:1, :]
    tile_end = tile0 + tiles
    before = gc
    d = 1
    while d < T:
        before = before + _shift_rows(before, d, 0.0, row)
        d *= 2
    before = before - gc
    run_global = GRAN_PER_TILE * tile0 + before
    run_local = _dot(gc.astype(BF16), upper)
    lcnt_ref[...] = jnp.broadcast_to(jnp.sum(gc, axis=1, keepdims=True), (T, LANES)).astype(I32)

    experts = range(N_GROUPS, N_GROUPS + N_EXPERTS)
    dst = jnp.zeros((T, LANES), F32)
    for e in experts:
        l0 = run_local[:, e:e + 1]
        inside = (local >= l0) & (local < l0 + gc[:, e:e + 1])
        dst = jnp.where(inside, run_global[:, e:e + 1] + (local - l0), dst)
    dst_ref[...] = dst.astype(I32)

    shape = src_ref.shape
    slot = (lax.broadcasted_iota(I32, shape, 0) * LANES + lax.broadcasted_iota(I32, shape, 1)).astype(F32)
    src = jnp.zeros(shape, F32)
    for e in experts:
        off = slot - GRAN_PER_TILE * tile0[0:1, e:e + 1]
        for t in range(T):
            p = before[t:t + 1, e:e + 1]
            hit = (off >= p) & (off < p + gc[t:t + 1, e:e + 1])
            src = jnp.where(hit, (t * LOCAL_GRAN) + run_local[t:t + 1, e:e + 1] + (off - p), src)
    src_ref[...] = src.astype(I32)

    shape = texp_ref.shape
    tile = (lax.broadcasted_iota(I32, shape, 0) * LANES + lax.broadcasted_iota(I32, shape, 1)).astype(F32)
    owner = jnp.zeros(shape, F32)
    for e in experts:
        owner = owner + jnp.where(tile >= tile_end[0:1, e:e + 1], 1.0, 0.0)
    texp_ref[...] = jnp.minimum(owner, N_EXPERTS - 1.0).astype(I32)
    nt_ref[...] = jnp.broadcast_to(tile_end[0:1, N_GROUPS + N_EXPERTS - 1:N_GROUPS + N_EXPERTS],
                                   nt_ref.shape).astype(I32)


def _plan(gc):
    n_tiles = gc.shape[0]
    max_gran, max_row_tiles = _plan_sizes(n_tiles)
    gran_rows = -(-max_gran // LANES)
    tile_rows = -(-max_row_tiles // LANES)
    src, texp, used, dst, lcnt = pl.pallas_call(
        _plan_kernel,
        out_shape=[jax.ShapeDtypeStruct((gran_rows, LANES), I32),
                   jax.ShapeDtypeStruct((tile_rows, LANES), I32),
                   jax.ShapeDtypeStruct((1, LANES), I32),
                   jax.ShapeDtypeStruct((n_tiles, LANES), I32),
                   jax.ShapeDtypeStruct((n_tiles, LANES), I32)],
        name="plan",
    )(gc)
    return (src.reshape(-1), texp.reshape(-1)[:max_row_tiles], used[0, :1],
            dst[:, :LOCAL_GRAN].reshape(-1), lcnt[:, 0])


def _granule_copies(idx_ref, first, count, max_count, src_hbm, dst_vmem, sem, wait):
    def one(k):
        g = idx_ref[first + k]
        cp = pltpu.make_async_copy(src_hbm.at[pl.ds(pl.multiple_of(g * GRAN, GRAN), GRAN)],
                                   dst_vmem.at[k * GRAN:(k + 1) * GRAN], sem)
        if wait:
            cp.wait()
        else:
            cp.start()

    for b0 in range(0, max_count, GATHER_UNROLL):
        def block(b0=b0):
            for k in range(b0, min(b0 + GATHER_UNROLL, max_count)):
                one(k)
        if isinstance(count, int):
            if b0 < count:
                block()
        else:
            pl.when(b0 < count)(block)


def _moe_kernel(src_ref, texp_ref, nt_ref, hsl_ref, w1_ref, w3_ref, w2_ref, ys_ref, xbuf, w1b, w3b, w2b, sem):
    i = pl.program_id(0)
    used = nt_ref[0]
    slot = i % MOE_BUFFERS
    D = ys_ref.shape[1]

    def gather(tile, wait):
        slot_ = tile % MOE_BUFFERS
        _granule_copies(src_ref, tile * GRAN_PER_TILE, GRAN_PER_TILE, GRAN_PER_TILE, hsl_ref, xbuf.at[slot_],
                        sem.at[slot_], wait)

    @pl.when(i == 0)
    def _():
        for ahead in range(MOE_BUFFERS - 1):
            pl.when(ahead < used)(functools.partial(gather, ahead, False))

    @pl.when(i + (MOE_BUFFERS - 1) < used)
    def _():
        gather(i + (MOE_BUFFERS - 1), False)

    @pl.when(i < used)
    def _():
        gather(i, True)

        @pl.when((i == 0) | (texp_ref[i] != texp_ref[jnp.maximum(i - 1, 0)]))
        def _():
            w1b[...] = w1_ref[0].astype(BF16)
            w3b[...] = w3_ref[0].astype(BF16)
            w2b[...] = w2_ref[0].astype(BF16)

        xb = xbuf[slot]
        hb = xb[:, :D]
        aux = xb[:, D:].astype(F32)
        first = aux[:, AUX_E1:AUX_E1 + 1] == texp_ref[i].astype(F32)
        w_row = jnp.zeros((TM_MOE, 1), F32)
        for k in range(W_PIECES):
            w_row = w_row + jnp.where(first, aux[:, AUX_W1 + k:AUX_W1 + k + 1], aux[:, AUX_W2 + k:AUX_W2 + k + 1])
        a = _dot(hb, w1b[...])
        act = (a * _sigmoid(a)) * _dot(hb, w3b[...])
        ys_ref[...] = (w_row * _dot(act.astype(BF16), w2b[...])).astype(BF16)

    @pl.when(i >= used)
    def _():
        ys_ref[...] = jnp.zeros_like(ys_ref)


def _moe(src, texp, used, hsl, w1, w3, w2):
    max_row_tiles = texp.shape[0]
    D = w1.shape[1]
    F = w1.shape[2]
    grid_spec = pltpu.PrefetchScalarGridSpec(
        num_scalar_prefetch=3,
        grid=(max_row_tiles,),
        in_specs=[pl.BlockSpec(memory_space=pl.ANY),
                  pl.BlockSpec((1, D, F), lambda i, s, te, nt: (te[i], 0, 0)),
                  pl.BlockSpec((1, D, F), lambda i, s, te, nt: (te[i], 0, 0)),
                  pl.BlockSpec((1, F, D), lambda i, s, te, nt: (te[i], 0, 0))],
        out_specs=pl.BlockSpec((TM_MOE, D), lambda i, s, te, nt: (i, 0)),
        scratch_shapes=[pltpu.VMEM((MOE_BUFFERS, TM_MOE, D + AUX_W), BF16),
                        pltpu.VMEM((D, F), BF16), pltpu.VMEM((D, F), BF16), pltpu.VMEM((F, D), BF16),
                        pltpu.SemaphoreType.DMA((MOE_BUFFERS,))],
    )
    return pl.pallas_call(
        _moe_kernel,
        grid_spec=grid_spec,
        out_shape=jax.ShapeDtypeStruct((max_row_tiles * TM_MOE, D), BF16),
        compiler_params=_params("arbitrary"),
        name="moe",
    )(src, texp, used, hsl, w1, w3, w2)


def _combine_kernel(dst_ref, lcnt_ref, ys_ref, x1_ref, route_ref, mod_ref, o_ref, ybuf, sem):
    t = pl.program_id(0)
    slot = t % 2
    tm = x1_ref.shape[0]

    def gather(tile, slot_, wait):
        _granule_copies(dst_ref, tile * LOCAL_GRAN, lcnt_ref[tile], LOCAL_GRAN, ys_ref, ybuf.at[slot_],
                        sem.at[slot_], wait)

    @pl.when(t == 0)
    def _():
        ybuf[...] = jnp.zeros_like(ybuf)
        gather(0, 0, False)

    @pl.when(t + 1 < pl.num_programs(0))
    def _():
        gather(t + 1, 1 - slot, False)

    gather(t, slot, True)
    lp1 = route_ref[:, R_LP1:R_LP1 + 1].astype(I32)
    lp2 = route_ref[:, R_LP2:R_LP2 + 1].astype(I32)
    col = lax.broadcasted_iota(I32, (tm, LOCAL_ROWS), 1)
    unsort = jnp.where(col == lp1, 1.0, jnp.where(col == lp2, 1.0, 0.0)).astype(BF16)
    o_ref[...] = x1_ref[...] + mod_ref[0, 5:6, :] * _dot(unsort, ybuf[slot])


def _combine(dst, lcnt, ys, x1, route, mod3, S):
    N, D = x1.shape
    tm = TM_SORT
    steps_per_batch = S // tm
    grid_spec = pltpu.PrefetchScalarGridSpec(
        num_scalar_prefetch=2,
        grid=(N // tm,),
        in_specs=[pl.BlockSpec(memory_space=pl.ANY),
                  pl.BlockSpec((tm, D), lambda t, d, c: (t, 0)),
                  pl.BlockSpec((tm, LANES), lambda t, d, c: (t, 0)),
                  pl.BlockSpec((1, 6, D), lambda t, d, c: (t // steps_per_batch, 0, 0))],
        out_specs=pl.BlockSpec((tm, D), lambda t, d, c: (t, 0)),
        scratch_shapes=[pltpu.VMEM((2, LOCAL_ROWS, D), BF16), pltpu.SemaphoreType.DMA((2,))],
    )
    return pl.pallas_call(
        _combine_kernel,
        grid_spec=grid_spec,
        out_shape=jax.ShapeDtypeStruct((N, D), F32),
        compiler_params=_params("arbitrary"),
        name="combine",
    )(dst, lcnt, ys, x1, route, mod3)


def _layer(x2, mod3, B, S, norm_mix_w, w_in, q_norm_w, k_norm_w, conv_w, conv_b, w_rec_gate, b_rec_gate,
           w_in_gate, b_in_gate, lru_lambda, w_proj_a, w_proj_b, w_out, norm_ffn_w, w_group, b_group,
           w_expert_router, b_expert_router, w1, w3, w2):
    N, D = x2.shape
    w_all = jnp.pad(w_in.astype(BF16), ((0, 0), (0, W_IN_PAD - D_IN)))
    qa, kv, lru, gates = _proj(x2, mod3, norm_mix_w, w_all, S)
    attn = _attn(qa, kv, q_norm_w, k_norm_w, B, S)
    lru_out = _lru(lru, conv_w, conv_b, _block_diag_tiles(w_rec_gate).astype(BF16), b_rec_gate,
                   _block_diag_tiles(w_in_gate).astype(BF16), b_in_gate, lru_lambda, B, S)

    n_r = N_GROUPS + N_EXPERTS
    w_router = jnp.concatenate([w_group, w_expert_router, jnp.zeros((D, LANES - n_r), F32)], axis=1)
    wr_hi = w_router.astype(BF16)
    wr_lo = (w_router - wr_hi.astype(F32)).astype(BF16)
    b_router = jnp.concatenate([b_group, b_expert_router, jnp.zeros((LANES - n_r,), F32)]).reshape(1, LANES)
    x1, hsl, route, gc = _merge(attn, lru_out, gates, x2, mod3, w_proj_a.astype(BF16), w_proj_b.astype(BF16),
                                w_out.astype(BF16), norm_ffn_w, jnp.concatenate([wr_hi, wr_lo], axis=1),
                                b_router, S)

    src, texp, used, dst, lcnt = _plan(gc.reshape(gc.shape[0], LANES))
    ys = _moe(src, texp, used, hsl, w1, w3, w2)
    return _combine(dst, lcnt, ys, x1, route, mod3, S)


def kernel(x, c, ada_w, ada_b, norm_mix_w, w_in, q_norm_w, k_norm_w, conv_w, conv_b, w_rec_gate, b_rec_gate,
           w_in_gate, b_in_gate, lru_lambda, w_proj_a, w_proj_b, w_out, norm_ffn_w, w_group, b_group,
           w_expert_router, b_expert_router, w1, w3, w2):
    B, S, D = x.shape
    x2 = x.reshape(B * S, D)
    for l in range(ada_w.shape[0]):
        mod3 = _ada(c, ada_w[l], ada_b[l]).reshape(B, 6, D)
        x2 = _layer(x2, mod3, B, S, norm_mix_w[l], w_in[l], q_norm_w[l], k_norm_w[l], conv_w[l], conv_b[l],
                    w_rec_gate[l], b_rec_gate[l], w_in_gate[l], b_in_gate[l], lru_lambda[l], w_proj_a[l],
                    w_proj_b[l], w_out[l], norm_ffn_w[l], w_group[l], b_group[l], w_expert_router[l],
                    b_expert_router[l], w1[l], w3[l], w2[l])
    return x2.reshape(B, S, D)
```

```python
import functools

import jax
import jax.numpy as jnp
import numpy as np
from jax import lax
from jax.experimental import pallas as pl
from jax.experimental.pallas import tpu as pltpu

F32 = jnp.float32
BF16 = jnp.bfloat16
I32 = jnp.int32

EPS = 1e-6
CHUNK = 64
Q_BLOCK = 128
N_HEADS_A = 8
HEAD_DIM_A = 64
KV_DIM_A = 64
ATTN_OUT = N_HEADS_A * KV_DIM_A
N_IDX_HEADS = 4
IDX_DIM = 64
TOPK_MAX = 256
LRU_WIDTH = 512
LRU_BLOCKS = 8
LRU_BLOCK_DIM = LRU_WIDTH // LRU_BLOCKS
CONV_WIDTH = 4
LRU_C = 8.0
N_GROUPS = 4
EXPERTS_PER_GROUP = 8
N_EXPERTS = N_GROUPS * EXPERTS_PER_GROUP
D_FF_EXPERT = 256

LANES = 128
INT_MIN = -2 ** 31
VMEM_LIMIT = 56 * 1024 * 1024

D_MODEL = 1024
COL_Q = 0
COL_K = COL_Q + N_HEADS_A * HEAD_DIM_A
COL_V = COL_K + KV_DIM_A
COL_QI = COL_V + KV_DIM_A
COL_KI = COL_QI + N_IDX_HEADS * IDX_DIM
COL_WI = COL_KI + IDX_DIM
COL_LX = COL_WI + N_IDX_HEADS
D_IN = COL_LX + 2 * LRU_WIDTH + 2 * D_MODEL
ATTN_COLS = -(-COL_LX // LANES) * LANES
TAIL_COLS = D_IN - COL_LX
TAIL_SHIFT = COL_LX % LANES
W_IN_PAD = (COL_LX // LANES) * LANES + TAIL_COLS + LANES
assert COL_K % LANES == 0 and COL_QI % LANES == 0 and COL_KI % LANES == 0 and TAIL_COLS % LANES == 0

QA_W = N_HEADS_A * HEAD_DIM_A + N_IDX_HEADS * IDX_DIM
KV_W = (COL_QI - COL_K) + (ATTN_COLS - COL_KI)
K_OFF, V_OFF = 0, KV_DIM_A
KI_OFF = COL_QI - COL_K
WI_OFF = KI_OFF + IDX_DIM

TM_PROJ = 512

TM_SORT = 512
GRAN = 16
TM_MOE = 256
GRAN_PER_TILE = TM_MOE // GRAN
MOE_BUFFERS = 3
GATHER_UNROLL = 8
LOCAL_ROWS = -(-(2 * TM_SORT + N_EXPERTS * (GRAN - 1)) // LANES) * LANES
LOCAL_GRAN = LOCAL_ROWS // GRAN
AUX_W = LANES
W_PIECES = 3
AUX_W1, AUX_W2 = 0, W_PIECES
AUX_E1 = 2 * W_PIECES

R_LP1, R_LP2 = 0, 1


def _dot(a, b, **kw):
    return jnp.dot(a, b, preferred_element_type=F32, **kw)


def _params(*sem):
    return pltpu.CompilerParams(dimension_semantics=sem, vmem_limit_bytes=VMEM_LIMIT)


def _ada_kernel(c_ref, w_ref, b_ref, o_ref):
    c = c_ref[...]
    cond = c * jax.nn.sigmoid(c)
    w = w_ref[...]
    c_hi = cond.astype(BF16)
    c_lo = (cond - c_hi.astype(F32)).astype(BF16)
    w_hi = w.astype(BF16)
    w_lo = (w - w_hi.astype(F32)).astype(BF16)
    o_ref[...] = _dot(c_hi, w_hi) + (_dot(c_lo, w_hi) + _dot(c_hi, w_lo)) + b_ref[...]


def _ada(c, ada_w, ada_b):
    B, D = c.shape
    n_out = ada_w.shape[1]
    tn = 1024
    return pl.pallas_call(
        _ada_kernel,
        grid=(n_out // tn,),
        in_specs=[pl.BlockSpec((B, D), lambda j: (0, 0)),
                  pl.BlockSpec((D, tn), lambda j: (0, j)),
                  pl.BlockSpec((1, tn), lambda j: (0, j))],
        out_specs=pl.BlockSpec((B, tn), lambda j: (0, j)),
        out_shape=jax.ShapeDtypeStruct((B, n_out), F32),
        compiler_params=_params("arbitrary"),
        name="ada",
    )(c, ada_w, ada_b.reshape(1, n_out))


def _rms_mod(x, w, shift, scale):
    ms = jnp.mean(x * x, axis=-1, keepdims=True)
    y = x * lax.rsqrt(ms + EPS) * w
    return y * (1.0 + scale) + shift


def _proj_kernel(x_ref, mod_ref, nw_ref, w_ref, oqa_ref, okv_ref, olru_ref, og_ref, wt_s):
    @pl.when(pl.program_id(0) == 0)
    def _():
        rr = lax.broadcasted_iota(I32, (2 * LANES, LANES), 0)
        cc = lax.broadcasted_iota(I32, (2 * LANES, LANES), 1)
        shift = jnp.where(rr == cc + TAIL_SHIFT, 1.0, 0.0).astype(BF16)
        for j in range(TAIL_COLS // LANES):
            src = (COL_LX // LANES + j) * LANES
            wt_s[:, j * LANES:(j + 1) * LANES] = _dot(w_ref[:, src:src + 2 * LANES], shift).astype(BF16)

    h = _rms_mod(x_ref[...], nw_ref[...], mod_ref[0, 0:1, :], mod_ref[0, 1:2, :])
    hb = h.astype(BF16)
    pa = _dot(hb, w_ref[:, :ATTN_COLS])
    oqa_ref[...] = jnp.concatenate([pa[:, COL_Q:COL_K], pa[:, COL_QI:COL_KI]], axis=1)
    okv_ref[...] = jnp.concatenate([pa[:, COL_K:COL_QI], pa[:, COL_KI:ATTN_COLS]], axis=1)
    olru_ref[...] = _dot(hb, wt_s[:, :2 * LRU_WIDTH])
    og_ref[...] = _sigmoid(_dot(hb, wt_s[:, 2 * LRU_WIDTH:])).astype(BF16)


def _proj(x2, mod3, norm_w, w_all, S):
    N, D = x2.shape
    tm = TM_PROJ
    steps_per_batch = S // tm
    const = lambda t: (0, 0)
    return pl.pallas_call(
        _proj_kernel,
        grid=(N // tm,),
        in_specs=[pl.BlockSpec((tm, D), lambda t: (t, 0)),
                  pl.BlockSpec((1, 6, D), lambda t: (t // steps_per_batch, 0, 0)),
                  pl.BlockSpec((1, D), const),
                  pl.BlockSpec(w_all.shape, const)],
        out_specs=[pl.BlockSpec((tm, QA_W), lambda t: (t, 0)),
                   pl.BlockSpec((tm, KV_W), lambda t: (t, 0)),
                   pl.BlockSpec((tm, 2 * LRU_WIDTH), lambda t: (t, 0)),
                   pl.BlockSpec((tm, 2 * D), lambda t: (t, 0))],
        out_shape=[jax.ShapeDtypeStruct((N, QA_W), F32),
                   jax.ShapeDtypeStruct((N, KV_W), F32),
                   jax.ShapeDtypeStruct((N, 2 * LRU_WIDTH), F32),
                   jax.ShapeDtypeStruct((N, 2 * D), BF16)],
        scratch_shapes=[pltpu.VMEM((D, TAIL_COLS), BF16)],
        compiler_params=_params("arbitrary"),
        name="proj",
    )(x2, mod3, norm_w.reshape(1, D), w_all)


ATTN_BUCKETS = 8
ROW_FOLD = 64
KEY_ROWS = 256
COUNT_CHAINS = 2
LOG2E = float(np.log2(np.e))
POS_RADIX = 256
SLOPE_PIECES = 3
V_ONES = 16


def _bf16_pieces(c, n):
    out = []
    rest = float(c)
    for _ in range(n):
        p = float(np.asarray(rest, np.float32).astype(BF16).astype(np.float32))
        out.append(p)
        rest -= p
    return out


def _reduce_rows(op, x):
    r = x.shape[0]
    if r > ROW_FOLD and r % ROW_FOLD == 0:
        x = op(x.reshape(r // ROW_FOLD, ROW_FOLD, x.shape[1]), axis=0)
    return op(x, axis=0, keepdims=True)


def _attn_block(s_eff, topk, i, qa_ref, kvq_ref, qnw_ref, o_ref, kn_s, vt_s, ki_s, key_s, bias_s, lg_s, ot_s):
    nq = N_HEADS_A * HEAD_DIM_A
    q_t = qa_ref[:, :nq].T
    qi_t = qa_ref[:, nq:].T
    wq_t = kvq_ref[...].T[WI_OFF:WI_OFF + N_IDX_HEADS, :] * (N_IDX_HEADS ** -0.5 * IDX_DIM ** -0.5)

    q_pos = i * Q_BLOCK + lax.broadcasted_iota(I32, (1, Q_BLOCK), 1)
    limit = (q_pos // CHUNK + 1) * CHUNK
    chunks = [slice(r0, min(r0 + KEY_ROWS, s_eff)) for r0 in range(0, s_eff, KEY_ROWS)]

    qi = [qi_t[h * IDX_DIM:(h + 1) * IDX_DIM, :].astype(BF16) for h in range(N_IDX_HEADS)]
    for rows in chunks:
        ki = ki_s[rows, :]
        score = jnp.zeros((rows.stop - rows.start, Q_BLOCK), F32)
        for h in range(N_IDX_HEADS):
            score = score + jnp.maximum(_dot(ki, qi[h]), 0.0) * wq_t[h:h + 1, :]
        k_pos = rows.start + lax.broadcasted_iota(I32, score.shape, 0)
        key_s[rows, :] = jnp.where(k_pos < limit, score, -jnp.inf)

    def count(mask):
        return _reduce_rows(jnp.sum, jnp.where(mask, 1.0, 0.0))

    def count_keys(test):
        acc = [jnp.zeros((ROW_FOLD, Q_BLOCK), F32) for _ in range(COUNT_CHAINS)]
        for n, r0 in enumerate(range(0, s_eff, ROW_FOLD)):
            hit = jnp.where(test(key_s[r0:r0 + ROW_FOLD, :]), 1.0, 0.0)
            acc[n % COUNT_CHAINS] = acc[n % COUNT_CHAINS] + hit
        return jnp.sum(functools.reduce(lambda x, y: x + y, acc), axis=0, keepdims=True)

    def ordered_to_float(c):
        s = c ^ INT_MIN
        return pltpu.bitcast(jnp.where(s < 0, INT_MIN | (-s), s), F32)

    def bisect(it, r):
        c = r | jnp.left_shift(jnp.int32(1), 31 - it)
        cand = ordered_to_float(c)
        return jnp.where(count_keys(lambda slab: slab >= cand) >= topk, c, r)

    def select_by_threshold():
        thr = ordered_to_float(lax.fori_loop(0, 32, bisect, jnp.zeros((1, Q_BLOCK), I32)))

        n_ge = count_keys(lambda slab: slab >= thr)
        clean = jnp.max(jnp.where((n_ge == topk) & (thr > -jnp.inf), 0.0, 1.0)) == 0.0

        @pl.when(clean)
        def _():
            for rows in chunks:
                bias_s[rows, :] = jnp.where(key_s[rows, :] >= thr, 0.0, -jnp.inf)

        @pl.when(jnp.logical_not(clean))
        def _():
            key = key_s[:s_eff, :]
            need = topk - count(key > thr)
            rr = lax.broadcasted_iota(I32, (LANES, LANES), 0)
            cc = lax.broadcasted_iota(I32, (LANES, LANES), 1)
            tri = jnp.where(cc < rr, 1.0, 0.0).astype(BF16)
            seen = jnp.zeros((1, Q_BLOCK), F32)
            for c in range(s_eff // LANES):
                kc = key[c * LANES:(c + 1) * LANES, :]
                eq = kc == thr
                eqf = jnp.where(eq, 1.0, 0.0)
                rank = _dot(tri, eqf.astype(BF16)) + seen
                seen = seen + jnp.sum(eqf, axis=0, keepdims=True)
                tie_bias = jnp.where(jnp.where(eq, rank, topk) < need, 0.0, -jnp.inf)
                bias = jnp.where(kc > thr, 0.0, tie_bias)
                k_pos = c * LANES + lax.broadcasted_iota(I32, (LANES, Q_BLOCK), 0)
                bias_s[c * LANES:(c + 1) * LANES, :] = jnp.where(k_pos < limit, bias, -jnp.inf)

    if s_eff > topk:
        select_by_threshold()
    else:
        for rows in chunks:
            k_pos = rows.start + lax.broadcasted_iota(I32, (rows.stop - rows.start, Q_BLOCK), 0)
            bias_s[rows, :] = jnp.where(k_pos < limit, 0.0, -jnp.inf)

    qnw = qnw_ref[...]
    crow = lax.broadcasted_iota(I32, (LANES - HEAD_DIM_A, Q_BLOCK), 0)
    r_k = lax.broadcasted_iota(I32, (Q_BLOCK, Q_BLOCK), 0)
    c_q = lax.broadcasted_iota(I32, (Q_BLOCK, Q_BLOCK), 1)
    after = 2.0 * jnp.maximum(r_k - c_q, 0).astype(F32)
    band = pl.ds(pl.multiple_of(i * Q_BLOCK, Q_BLOCK), Q_BLOCK)
    slope = [sum(_bf16_pieces(LOG2E * 2.0 ** -(h + 1), SLOPE_PIECES)) for h in range(N_HEADS_A)]
    row_max = []
    for h in range(N_HEADS_A):
        pieces = _bf16_pieces(LOG2E * 2.0 ** -(h + 1), SLOPE_PIECES)
        qh = q_t[h * HEAD_DIM_A:(h + 1) * HEAD_DIM_A, :]
        ms = jnp.mean(qh * qh, axis=0, keepdims=True)
        qn = (qh * lax.rsqrt(ms + EPS) * qnw) * (HEAD_DIM_A ** -0.5) * LOG2E
        slopes = jnp.zeros(crow.shape, F32)
        for k, piece in enumerate(pieces):
            slopes = jnp.where(crow == 2 * k, POS_RADIX * piece, jnp.where(crow == 2 * k + 1, piece, slopes))
        qa = jnp.concatenate([qn, slopes], axis=0).astype(BF16)
        folded = []
        for rows in chunks:
            logits = _dot(kn_s[rows, :], qa) + bias_s[rows, :]
            lg_s[h, rows, :] = logits
            folded.append(jnp.max(logits.reshape(-1, ROW_FOLD, Q_BLOCK), axis=0))
        row_max.append(jnp.max(functools.reduce(jnp.maximum, folded), axis=0, keepdims=True))
    for h in range(N_HEADS_A):
        lg_s[h, band, :] = lg_s[h, band, :] - slope[h] * after
    for h in range(N_HEADS_A):
        acc = jnp.zeros((KV_DIM_A + V_ONES, Q_BLOCK), F32)
        for rows in chunks:
            p = jnp.exp2(lg_s[h, rows, :] - row_max[h])
            acc = acc + _dot(vt_s[:, rows], p.astype(BF16))
        ot_s[h * KV_DIM_A:(h + 1) * KV_DIM_A, :] = acc[:KV_DIM_A, :] / acc[KV_DIM_A:KV_DIM_A + 1, :]
    o_ref[...] = ot_s[...].T


def _attn_kernel(qa_ref, kvq_ref, kv_ref, qnw_ref, knw_ref, o_ref, kn_s, vt_s, ki_s, key_s, bias_s, lg_s, ot_s):
    i = pl.program_id(1)
    S = kv_ref.shape[0]
    nb = S // Q_BLOCK
    topk = float(min(TOPK_MAX, S // 4))

    @pl.when(i == 0)
    def _():
        kv = kv_ref[:, :LANES]
        lane = lax.broadcasted_iota(I32, kv.shape, 1)
        pos = lax.broadcasted_iota(I32, kv.shape, 0)
        ms = jnp.sum(jnp.where(lane < KV_DIM_A, kv * kv, 0.0), axis=-1, keepdims=True) * (1.0 / KV_DIM_A)
        kn = kv * lax.rsqrt(ms + EPS) * knw_ref[...]
        digit = jnp.where((lane & 1) == 0, pos // POS_RADIX, pos % POS_RADIX).astype(F32)
        is_digit = (lane >= KV_DIM_A) & (lane < KV_DIM_A + 2 * SLOPE_PIECES)
        kn_s[...] = jnp.where(is_digit, digit, kn).astype(BF16)
        vt_s[:KV_DIM_A, :] = kv.T[V_OFF:V_OFF + KV_DIM_A, :].astype(BF16)
        vt_s[KV_DIM_A:, :] = jnp.ones((V_ONES, kv.shape[0]), BF16)
        ki_s[...] = kv_ref[:, KI_OFF:KI_OFF + IDX_DIM].astype(BF16)

    n_buckets = min(ATTN_BUCKETS, nb)
    per = nb // n_buckets
    for j in range(n_buckets):
        pl.when(i // per == j)(functools.partial(
            _attn_block, (j + 1) * per * Q_BLOCK, topk, i, qa_ref, kvq_ref, qnw_ref, o_ref,
            kn_s, vt_s, ki_s, key_s, bias_s, lg_s, ot_s))


def _attn(qa, kv, q_norm_w, k_norm_w, B, S):
    N = qa.shape[0]
    nb = S // Q_BLOCK
    return pl.pallas_call(
        _attn_kernel,
        grid=(B, nb),
        in_specs=[pl.BlockSpec((Q_BLOCK, QA_W), lambda b, i: (b * nb + i, 0)),
                  pl.BlockSpec((Q_BLOCK, KV_W), lambda b, i: (b * nb + i, 0)),
                  pl.BlockSpec((S, KV_W), lambda b, i: (b, 0)),
                  pl.BlockSpec((HEAD_DIM_A, 1), lambda b, i: (0, 0)),
                  pl.BlockSpec((1, LANES), lambda b, i: (0, 0))],
        out_specs=pl.BlockSpec((Q_BLOCK, ATTN_OUT), lambda b, i: (b * nb + i, 0)),
        out_shape=jax.ShapeDtypeStruct((N, ATTN_OUT), F32),
        scratch_shapes=[pltpu.VMEM((S, LANES), BF16),
                        pltpu.VMEM((KV_DIM_A + V_ONES, S), BF16),
                        pltpu.VMEM((S, IDX_DIM), BF16),
                        pltpu.VMEM((S, Q_BLOCK), F32),
                        pltpu.VMEM((S, Q_BLOCK), F32),
                        pltpu.VMEM((N_HEADS_A, S, Q_BLOCK), F32),
                        pltpu.VMEM((ATTN_OUT, Q_BLOCK), F32)],
        compiler_params=_params("arbitrary", "arbitrary"),
        name="attn",
    )(qa, kv, kv, q_norm_w.reshape(HEAD_DIM_A, 1),
      jnp.pad(k_norm_w, (0, LANES - KV_DIM_A)).reshape(1, LANES))


SUBLANES = 8


def _shift_rows(x, d, fill, row):
    if d % SUBLANES == 0:
        return jnp.concatenate([jnp.full((d, x.shape[1]), fill, x.dtype), x[:-d]], axis=0)
    return jnp.where(row >= d, pltpu.roll(x, d, 0), fill)


def _sigmoid(x):
    return 0.5 * (1.0 + jnp.tanh(0.5 * x))


def _doubling_scan(a, b, row):
    n = a.shape[0]
    d = 1
    while d < n:
        b = a * _shift_rows(b, d, 0.0, row) + b
        if 2 * d < n:
            a = a * _shift_rows(a, d, 1.0, row)
        d *= 2
    return b


def _lru_kernel(xb_ref, gb_ref, cw_ref, cb_ref, wr_ref, br_ref, wi_ref, bi_ref, lam_ref, o_ref, a_s, b_s):
    x = xb_ref[...]
    S = x.shape[0]
    row = lax.broadcasted_iota(I32, x.shape, 0)
    xc = cb_ref[...] + _shift_rows(x, CONV_WIDTH - 1, 0.0, row) * cw_ref[0:1, :]
    for j in range(1, CONV_WIDTH):
        d = CONV_WIDTH - 1 - j
        xs = x if d == 0 else _shift_rows(x, d, 0.0, row)
        xc = xc + xs * cw_ref[j:j + 1, :]
    xcb = xc.astype(BF16)
    r = _sigmoid(_dot(xcb, wr_ref[0]) + br_ref[...])
    ig = _sigmoid(_dot(xcb, wi_ref[0]) + bi_ref[...])
    z = -lam_ref[...]
    softplus = jnp.maximum(z, 0.0) + jnp.log1p(jnp.exp(-jnp.abs(z)))
    log_a = -LRU_C * r * softplus
    a = jnp.exp(log_a)
    b = jnp.sqrt(-jnp.tanh(log_a) * (1.0 + a * a)) * (ig * xc)
    sub = row & (SUBLANES - 1)
    d = 1
    while d < SUBLANES:
        keep = sub >= d
        b = a * jnp.where(keep, pltpu.roll(b, d, 0), 0.0) + b
        a = a * jnp.where(keep, pltpu.roll(a, d, 0), 1.0)
        d *= 2
    a_s[...] = a
    b_s[...] = b
    groups = S // SUBLANES
    last = pl.ds(SUBLANES - 1, groups, stride=SUBLANES)
    g_row = lax.broadcasted_iota(I32, (groups, x.shape[1]), 0)
    h_last = _doubling_scan(a_s[last, :], b_s[last, :], g_row)
    carry = _shift_rows(h_last, 1, 0.0, g_row)
    carry = jnp.broadcast_to(carry[:, None, :], (groups, SUBLANES, x.shape[1])).reshape(x.shape)
    h = b_s[...] + a_s[...] * carry
    g = gb_ref[...]
    gelu = 0.5 * g * (1.0 + jnp.tanh(float(np.sqrt(2.0 / np.pi)) * (g + 0.044715 * (g * g * g))))
    o_ref[...] = h * gelu


def _lru(lru, conv_w, conv_b, wr_bd, b_rec, wi_bd, b_in, lam, B, S):
    N = lru.shape[0]
    nt = LRU_WIDTH // LANES
    vec = lambda b, j: (0, j)
    return pl.pallas_call(
        _lru_kernel,
        grid=(B, nt),
        in_specs=[pl.BlockSpec((S, LANES), lambda b, j: (b, j)),
                  pl.BlockSpec((S, LANES), lambda b, j: (b, nt + j)),
                  pl.BlockSpec((CONV_WIDTH, LANES), vec),
                  pl.BlockSpec((1, LANES), vec),
                  pl.BlockSpec((1, LANES, LANES), lambda b, j: (j, 0, 0)),
                  pl.BlockSpec((1, LANES), vec),
                  pl.BlockSpec((1, LANES, LANES), lambda b, j: (j, 0, 0)),
                  pl.BlockSpec((1, LANES), vec),
                  pl.BlockSpec((1, LANES), vec)],
        out_specs=pl.BlockSpec((S, LANES), lambda b, j: (b, j)),
        out_shape=jax.ShapeDtypeStruct((N, LRU_WIDTH), F32),
        scratch_shapes=[pltpu.VMEM((S, LANES), F32), pltpu.VMEM((S, LANES), F32)],
        compiler_params=_params("arbitrary", "arbitrary"),
        name="lru",
    )(lru, lru, conv_w, conv_b.reshape(1, -1), wr_bd, b_rec.reshape(1, -1), wi_bd,
      b_in.reshape(1, -1), lam.reshape(1, -1))


def _block_diag_tiles(w):
    per = LANES // LRU_BLOCK_DIM
    nt = LRU_WIDTH // LANES
    w5 = w.reshape(nt, per, LRU_BLOCK_DIM, 1, LRU_BLOCK_DIM)
    eye = jnp.eye(per, dtype=w.dtype).reshape(1, per, 1, per, 1)
    return (w5 * eye).reshape(nt, LANES, LANES)


def _first_index_of_max(vals, lane):
    m = jnp.max(vals, axis=-1, keepdims=True)
    idx = jnp.min(jnp.where(vals == m, lane, 4 * LANES), axis=-1, keepdims=True)
    return m, idx


def _weight_pieces(out, w, lane, first):
    rest = w
    for k in range(W_PIECES):
        piece = rest.astype(BF16).astype(F32)
        out = jnp.where(lane == first + k, piece, out)
        rest = rest - piece
    return out


def _merge_kernel(attn_ref, lru_ref, g_ref, x_ref, mod_ref, wpa_ref, wpb_ref, wo_ref, nw_ref, wr_ref,
                  br_ref, x1_ref, hsl_ref, route_ref, gc_ref):
    D = x_ref.shape[1]
    tm = x_ref.shape[0]

    y_a = _dot(attn_ref[...].astype(BF16), wpa_ref[...])
    y_b = _dot(lru_ref[...].astype(BF16), wpb_ref[...])
    merged = g_ref[:, :D].astype(F32) * y_a + g_ref[:, D:].astype(F32) * y_b
    x1 = x_ref[...] + mod_ref[0, 2:3, :] * _dot(merged.astype(BF16), wo_ref[...])
    x1_ref[...] = x1
    h2 = _rms_mod(x1, nw_ref[...], mod_ref[0, 3:4, :], mod_ref[0, 4:5, :])
    h2b = h2.astype(BF16)

    h2l = (h2 - h2b.astype(F32)).astype(BF16)
    hi_hi_lo = _dot(h2b, wr_ref[...])
    logits = hi_hi_lo[:, :LANES] + (_dot(h2l, wr_ref[:, :LANES]) + hi_hi_lo[:, LANES:]) + br_ref[...]
    lane = lax.broadcasted_iota(I32, logits.shape, 1)
    gl = jnp.where(lane < N_GROUPS, logits, -jnp.inf)
    gmax, g_sel = _first_index_of_max(gl, lane)
    g_weight = 1.0 / jnp.sum(jnp.exp(gl - gmax), axis=-1, keepdims=True)
    e_lo = N_GROUPS + g_sel * EXPERTS_PER_GROUP
    el = jnp.where((lane >= e_lo) & (lane < e_lo + EXPERTS_PER_GROUP), logits, -jnp.inf)
    v1, i1 = _first_index_of_max(el, lane)
    el2 = jnp.where(lane == i1, -jnp.inf, el)
    v2, i2 = _first_index_of_max(el2, lane)
    e2x = jnp.exp(v2 - v1)
    w1 = g_weight / (1.0 + e2x)
    w2 = g_weight * e2x / (1.0 + e2x)

    hot1 = lane == i1
    hot2 = lane == i2
    hot = jnp.where(hot1, 1.0, jnp.where(hot2, 1.0, 0.0))
    gcnt = jnp.floor((jnp.sum(hot, axis=0, keepdims=True) + (GRAN - 1.0)) * (1.0 / GRAN))
    gc_ref[0] = gcnt
    rr = lax.broadcasted_iota(I32, (LANES, LANES), 0)
    cc = lax.broadcasted_iota(I32, (LANES, LANES), 1)
    upper = jnp.where(rr < cc, 1.0, 0.0).astype(BF16)
    run_start = GRAN * _dot(jnp.broadcast_to(gcnt, (8, LANES)).astype(BF16), upper)[0:1, :]
    rr = lax.broadcasted_iota(I32, (tm, tm), 0)
    cc = lax.broadcasted_iota(I32, (tm, tm), 1)
    lower = jnp.where(cc < rr, 1.0, 0.0).astype(BF16)
    pos = _dot(lower, hot.astype(BF16)) + run_start
    pos1 = jnp.where(hot1, pos, 0.0)
    pos2 = jnp.where(hot2, pos, 0.0)
    lp1 = jnp.sum(pos1, axis=-1, keepdims=True)
    lp2 = jnp.sum(pos2, axis=-1, keepdims=True)
    route_ref[...] = jnp.where(lane == R_LP1, lp1, jnp.where(lane == R_LP2, lp2, 0.0))

    aux = jnp.where(lane == AUX_E1, (i1 - N_GROUPS).astype(F32), 0.0)
    aux = _weight_pieces(_weight_pieces(aux, w1, lane, AUX_W1), w2, lane, AUX_W2)
    lp1r = jnp.sum(pos1.T, axis=0, keepdims=True).astype(I32)
    lp2r = jnp.sum(pos2.T, axis=0, keepdims=True).astype(I32)
    rows = lax.broadcasted_iota(I32, (LOCAL_ROWS, tm), 0)
    sort = jnp.where(rows == lp1r, 1.0, jnp.where(rows == lp2r, 1.0, 0.0)).astype(BF16)
    hsl_ref[...] = _dot(sort, jnp.concatenate([h2b, aux.astype(BF16)], axis=1)).astype(BF16)


def _merge(attn, lru_out, gates, x2, mod3, wpa, wpb, wo, norm_w, wr_hi_lo, b_router, S):
    N, D = x2.shape
    tm = TM_SORT
    n_tiles = N // tm
    steps_per_batch = S // tm
    const = lambda t: (0, 0)
    rows = lambda t: (t, 0)
    return pl.pallas_call(
        _merge_kernel,
        grid=(n_tiles,),
        in_specs=[pl.BlockSpec((tm, ATTN_OUT), rows),
                  pl.BlockSpec((tm, LRU_WIDTH), rows),
                  pl.BlockSpec((tm, 2 * D), rows),
                  pl.BlockSpec((tm, D), rows),
                  pl.BlockSpec((1, 6, D), lambda t: (t // steps_per_batch, 0, 0)),
                  pl.BlockSpec(wpa.shape, const),
                  pl.BlockSpec(wpb.shape, const),
                  pl.BlockSpec(wo.shape, const),
                  pl.BlockSpec((1, D), const),
                  pl.BlockSpec(wr_hi_lo.shape, const),
                  pl.BlockSpec((1, LANES), const)],
        out_specs=[pl.BlockSpec((tm, D), rows),
                   pl.BlockSpec((LOCAL_ROWS, D + AUX_W), rows),
                   pl.BlockSpec((tm, LANES), rows),
                   pl.BlockSpec((1, 1, LANES), lambda t: (t, 0, 0))],
        out_shape=[jax.ShapeDtypeStruct((N, D), F32),
                   jax.ShapeDtypeStruct((n_tiles * LOCAL_ROWS, D + AUX_W), BF16),
                   jax.ShapeDtypeStruct((N, LANES), F32),
                   jax.ShapeDtypeStruct((n_tiles, 1, LANES), F32)],
        compiler_params=_params("arbitrary"),
        name="merge",
    )(attn, lru_out, gates, x2, mod3, wpa, wpb, wo, norm_w.reshape(1, D), wr_hi_lo, b_router)


def _plan_sizes(n_tiles):
    max_gran = n_tiles * LOCAL_GRAN + N_EXPERTS * (GRAN_PER_TILE - 1)
    max_row_tiles = -(-max_gran // GRAN_PER_TILE)
    return max_row_tiles * GRAN_PER_TILE, max_row_tiles


def _plan_kernel(gc_ref, src_ref, texp_ref, nt_ref, dst_ref, lcnt_ref):
    gc = gc_ref[...]
    T = gc.shape[0]
    row = lax.broadcasted_iota(I32, (T, LANES), 0)
    local = lax.broadcasted_iota(I32, (T, LANES), 1).astype(F32)
    rr = lax.broadcasted_iota(I32, (LANES, LANES), 0)
    cc = lax.broadcasted_iota(I32, (LANES, LANES), 1)
    upper = jnp.where(rr < cc, 1.0, 0.0).astype(BF16)

    tiles = jnp.floor((jnp.sum(gc, axis=0, keepdims=True) + (GRAN_PER_TILE - 1.0)) * (1.0 / GRAN_PER_TILE))
    tile0 = _dot(jnp.broadcast_to(tiles, (SUBLANES, LANES)).astype(BF16), upper)[0:1, :]
    tile_end = tile0 + tiles
    before = gc
    d = 1
    while d < T:
        before = before + _shift_rows(before, d, 0.0, row)
        d *= 2
    before = before - gc
    run_global = GRAN_PER_TILE * tile0 + before
    run_local = _dot(gc.astype(BF16), upper)
    lcnt_ref[...] = jnp.broadcast_to(jnp.sum(gc, axis=1, keepdims=True), (T, LANES)).astype(I32)

    experts = range(N_GROUPS, N_GROUPS + N_EXPERTS)
    dst = jnp.zeros((T, LANES), F32)
    for e in experts:
        l0 = run_local[:, e:e + 1]
        inside = (local >= l0) & (local < l0 + gc[:, e:e + 1])
        dst = jnp.where(inside, run_global[:, e:e + 1] + (local - l0), dst)
    dst_ref[...] = dst.astype(I32)

    shape = src_ref.shape
    slot = (lax.broadcasted_iota(I32, shape, 0) * LANES + lax.broadcasted_iota(I32, shape, 1)).astype(F32)
    src = jnp.zeros(shape, F32)
    for e in experts:
        off = slot - GRAN_PER_TILE * tile0[0:1, e:e + 1]
        for t in range(T):
            p = before[t:t + 1, e:e + 1]
            hit = (off >= p) & (off < p + gc[t:t + 1, e:e + 1])
            src = jnp.where(hit, (t * LOCAL_GRAN) + run_local[t:t + 1, e:e + 1] + (off - p), src)
    src_ref[...] = src.astype(I32)

    shape = texp_ref.shape
    tile = (lax.broadcasted_iota(I32, shape, 0) * LANES + lax.broadcasted_iota(I32, shape, 1)).astype(F32)
    owner = jnp.zeros(shape, F32)
    for e in experts:
        owner = owner + jnp.where(tile >= tile_end[0:1, e:e + 1], 1.0, 0.0)
    texp_ref[...] = jnp.minimum(owner, N_EXPERTS - 1.0).astype(I32)
    nt_ref[...] = jnp.broadcast_to(tile_end[0:1, N_GROUPS + N_EXPERTS - 1:N_GROUPS + N_EXPERTS],
                                   nt_ref.shape).astype(I32)


def _plan(gc):
    n_tiles = gc.shape[0]
    max_gran, max_row_tiles = _plan_sizes(n_tiles)
    gran_rows = -(-max_gran // LANES)
    tile_rows = -(-max_row_tiles // LANES)
    src, texp, used, dst, lcnt = pl.pallas_call(
        _plan_kernel,
        out_shape=[jax.ShapeDtypeStruct((gran_rows, LANES), I32),
                   jax.ShapeDtypeStruct((tile_rows, LANES), I32),
                   jax.ShapeDtypeStruct((1, LANES), I32),
                   jax.ShapeDtypeStruct((n_tiles, LANES), I32),
                   jax.ShapeDtypeStruct((n_tiles, LANES), I32)],
        name="plan",
    )(gc)
    return (src.reshape(-1), texp.reshape(-1)[:max_row_tiles], used[0, :1],
            dst[:, :LOCAL_GRAN].reshape(-1), lcnt[:, 0])


def _granule_copies(idx_ref, first, count, max_count, src_hbm, dst_vmem, sem, wait):
    def one(k):
        g = idx_ref[first + k]
        cp = pltpu.make_async_copy(src_hbm.at[pl.ds(pl.multiple_of(g * GRAN, GRAN), GRAN)],
                                   dst_vmem.at[k * GRAN:(k + 1) * GRAN], sem)
        if wait:
            cp.wait()
        else:
            cp.start()

    for b0 in range(0, max_count, GATHER_UNROLL):
        def block(b0=b0):
            for k in range(b0, min(b0 + GATHER_UNROLL, max_count)):
                one(k)
        if isinstance(count, int):
            if b0 < count:
                block()
        else:
            pl.when(b0 < count)(block)


def _moe_kernel(src_ref, texp_ref, nt_ref, hsl_ref, w1_ref, w3_ref, w2_ref, ys_ref, xbuf, w1b, w3b, w2b, sem):
    i = pl.program_id(0)
    used = nt_ref[0]
    slot = i % MOE_BUFFERS
    D = ys_ref.shape[1]

    def gather(tile, wait):
        slot_ = tile % MOE_BUFFERS
        _granule_copies(src_ref, tile * GRAN_PER_TILE, GRAN_PER_TILE, GRAN_PER_TILE, hsl_ref, xbuf.at[slot_],
                        sem.at[slot_], wait)

    @pl.when(i == 0)
    def _():
        for ahead in range(MOE_BUFFERS - 1):
            pl.when(ahead < used)(functools.partial(gather, ahead, False))

    def row_tile(prefetch):
        gather(i, True)

        @pl.when((i == 0) | (texp_ref[i] != texp_ref[jnp.maximum(i - 1, 0)]))
        def _():
            w1b[...] = w1_ref[0].astype(BF16)
            w3b[...] = w3_ref[0].astype(BF16)
            w2b[...] = w2_ref[0].astype(BF16)

        if prefetch:
            gather(i + (MOE_BUFFERS - 1), False)
        xb = xbuf[slot]
        hb = xb[:, :D]
        aux = xb[:, D:].astype(F32)
        first = aux[:, AUX_E1:AUX_E1 + 1] == texp_ref[i].astype(F32)
        w_row = jnp.zeros((TM_MOE, 1), F32)
        for k in range(W_PIECES):
            w_row = w_row + jnp.where(first, aux[:, AUX_W1 + k:AUX_W1 + k + 1], aux[:, AUX_W2 + k:AUX_W2 + k + 1])
        a = _dot(hb, w1b[...])
        act = (a * _sigmoid(a)) * _dot(hb, w3b[...])
        ys_ref[...] = (w_row * _dot(act.astype(BF16), w2b[...])).astype(BF16)

    more = i + (MOE_BUFFERS - 1) < used
    pl.when(more)(functools.partial(row_tile, True))
    pl.when((i < used) & jnp.logical_not(more))(functools.partial(row_tile, False))

    @pl.when(i >= used)
    def _():
        ys_ref[...] = jnp.zeros_like(ys_ref)


def _moe(src, texp, used, hsl, w1, w3, w2):
    max_row_tiles = texp.shape[0]
    D = w1.shape[1]
    F = w1.shape[2]
    grid_spec = pltpu.PrefetchScalarGridSpec(
        num_scalar_prefetch=3,
        grid=(max_row_tiles,),
        in_specs=[pl.BlockSpec(memory_space=pl.ANY),
                  pl.BlockSpec((1, D, F), lambda i, s, te, nt: (te[i], 0, 0)),
                  pl.BlockSpec((1, D, F), lambda i, s, te, nt: (te[i], 0, 0)),
                  pl.BlockSpec((1, F, D), lambda i, s, te, nt: (te[i], 0, 0))],
        out_specs=pl.BlockSpec((TM_MOE, D), lambda i, s, te, nt: (i, 0)),
        scratch_shapes=[pltpu.VMEM((MOE_BUFFERS, TM_MOE, D + AUX_W), BF16),
                        pltpu.VMEM((D, F), BF16), pltpu.VMEM((D, F), BF16), pltpu.VMEM((F, D), BF16),
                        pltpu.SemaphoreType.DMA((MOE_BUFFERS,))],
    )
    return pl.pallas_call(
        _moe_kernel,
        grid_spec=grid_spec,
        out_shape=jax.ShapeDtypeStruct((max_row_tiles * TM_MOE, D), BF16),
        compiler_params=_params("arbitrary"),
        name="moe",
    )(src, texp, used, hsl, w1, w3, w2)


def _combine_kernel(dst_ref, lcnt_ref, ys_ref, x1_ref, route_ref, mod_ref, o_ref, ybuf, sem):
    t = pl.program_id(0)
    slot = t % 2
    tm = x1_ref.shape[0]

    def gather(tile, slot_, wait):
        _granule_copies(dst_ref, tile * LOCAL_GRAN, lcnt_ref[tile], LOCAL_GRAN, ys_ref, ybuf.at[slot_],
                        sem.at[slot_], wait)

    @pl.when(t == 0)
    def _():
        ybuf[...] = jnp.zeros_like(ybuf)
        gather(0, 0, False)

    @pl.when(t + 1 < pl.num_programs(0))
    def _():
        gather(t + 1, 1 - slot, False)

    gather(t, slot, True)
    lp1 = route_ref[:, R_LP1:R_LP1 + 1].astype(I32)
    lp2 = route_ref[:, R_LP2:R_LP2 + 1].astype(I32)
    col = lax.broadcasted_iota(I32, (tm, LOCAL_ROWS), 1)
    unsort = jnp.where(col == lp1, 1.0, jnp.where(col == lp2, 1.0, 0.0)).astype(BF16)
    o_ref[...] = x1_ref[...] + mod_ref[0, 5:6, :] * _dot(unsort, ybuf[slot])


def _combine(dst, lcnt, ys, x1, route, mod3, S):
    N, D = x1.shape
    tm = TM_SORT
    steps_per_batch = S // tm
    grid_spec = pltpu.PrefetchScalarGridSpec(
        num_scalar_prefetch=2,
        grid=(N // tm,),
        in_specs=[pl.BlockSpec(memory_space=pl.ANY),
                  pl.BlockSpec((tm, D), lambda t, d, c: (t, 0)),
                  pl.BlockSpec((tm, LANES), lambda t, d, c: (t, 0)),
                  pl.BlockSpec((1, 6, D), lambda t, d, c: (t // steps_per_batch, 0, 0))],
        out_specs=pl.BlockSpec((tm, D), lambda t, d, c: (t, 0)),
        scratch_shapes=[pltpu.VMEM((2, LOCAL_ROWS, D), BF16), pltpu.SemaphoreType.DMA((2,))],
    )
    return pl.pallas_call(
        _combine_kernel,
        grid_spec=grid_spec,
        out_shape=jax.ShapeDtypeStruct((N, D), F32),
        compiler_params=_params("arbitrary"),
        name="combine",
    )(dst, lcnt, ys, x1, route, mod3)


def _layer(x2, mod3, B, S, norm_mix_w, w_in, q_norm_w, k_norm_w, conv_w, conv_b, w_rec_gate, b_rec_gate,
           w_in_gate, b_in_gate, lru_lambda, w_proj_a, w_proj_b, w_out, norm_ffn_w, w_group, b_group,
           w_expert_router, b_expert_router, w1, w3, w2):
    N, D = x2.shape
    w_all = jnp.pad(w_in.astype(BF16), ((0, 0), (0, W_IN_PAD - D_IN)))
    qa, kv, lru, gates = _proj(x2, mod3, norm_mix_w, w_all, S)
    attn = _attn(qa, kv, q_norm_w, k_norm_w, B, S)
    lru_out = _lru(lru, conv_w, conv_b, _block_diag_tiles(w_rec_gate).astype(BF16), b_rec_gate,
                   _block_diag_tiles(w_in_gate).astype(BF16), b_in_gate, lru_lambda, B, S)

    n_r = N_GROUPS + N_EXPERTS
    w_router = jnp.concatenate([w_group, w_expert_router, jnp.zeros((D, LANES - n_r), F32)], axis=1)
    wr_hi = w_router.astype(BF16)
    wr_lo = (w_router - wr_hi.astype(F32)).astype(BF16)
    b_router = jnp.concatenate([b_group, b_expert_router, jnp.zeros((LANES - n_r,), F32)]).reshape(1, LANES)
    x1, hsl, route, gc = _merge(attn, lru_out, gates, x2, mod3, w_proj_a.astype(BF16), w_proj_b.astype(BF16),
                                w_out.astype(BF16), norm_ffn_w, jnp.concatenate([wr_hi, wr_lo], axis=1),
                                b_router, S)

    src, texp, used, dst, lcnt = _plan(gc.reshape(gc.shape[0], LANES))
    ys = _moe(src, texp, used, hsl, w1, w3, w2)
    return _combine(dst, lcnt, ys, x1, route, mod3, S)


def kernel(x, c, ada_w, ada_b, norm_mix_w, w_in, q_norm_w, k_norm_w, conv_w, conv_b, w_rec_gate, b_rec_gate,
           w_in_gate, b_in_gate, lru_lambda, w_proj_a, w_proj_b, w_out, norm_ffn_w, w_group, b_group,
           w_expert_router, b_expert_router, w1, w3, w2):
    B, S, D = x.shape
    x2 = x.reshape(B * S, D)
    for l in range(ada_w.shape[0]):
        mod3 = _ada(c, ada_w[l], ada_b[l]).reshape(B, 6, D)
        x2 = _layer(x2, mod3, B, S, norm_mix_w[l], w_in[l], q_norm_w[l], k_norm_w[l], conv_w[l], conv_b[l],
                    w_rec_gate[l], b_rec_gate[l], w_in_gate[l], b_in_gate[l], lru_lambda[l], w_proj_a[l],
                    w_proj_b[l], w_out[l], norm_ffn_w[l], w_group[l], b_group[l], w_expert_router[l],
                    b_expert_router[l], w1[l], w3[l], w2[l])
    return x2.reshape(B, S, D)
```

```python
import functools

import jax
import jax.numpy as jnp
import numpy as np
from jax import lax
from jax.experimental import pallas as pl
from jax.experimental.pallas import tpu as pltpu

F32 = jnp.float32
BF16 = jnp.bfloat16
I32 = jnp.int32

EPS = 1e-6
CHUNK = 64
Q_BLOCK = 128
N_HEADS_A = 8
HEAD_DIM_A = 64
KV_DIM_A = 64
ATTN_OUT = N_HEADS_A * KV_DIM_A
N_IDX_HEADS = 4
IDX_DIM = 64
TOPK_MAX = 256
LRU_WIDTH = 512
LRU_BLOCKS = 8
LRU_BLOCK_DIM = LRU_WIDTH // LRU_BLOCKS
CONV_WIDTH = 4
LRU_C = 8.0
N_GROUPS = 4
EXPERTS_PER_GROUP = 8
N_EXPERTS = N_GROUPS * EXPERTS_PER_GROUP
D_FF_EXPERT = 256

LANES = 128
INT_MIN = -2 ** 31
VMEM_LIMIT = 56 * 1024 * 1024

D_MODEL = 1024
COL_Q = 0
COL_K = COL_Q + N_HEADS_A * HEAD_DIM_A
COL_V = COL_K + KV_DIM_A
COL_QI = COL_V + KV_DIM_A
COL_KI = COL_QI + N_IDX_HEADS * IDX_DIM
COL_WI = COL_KI + IDX_DIM
COL_LX = COL_WI + N_IDX_HEADS
D_IN = COL_LX + 2 * LRU_WIDTH + 2 * D_MODEL
ATTN_COLS = -(-COL_LX // LANES) * LANES
TAIL_COLS = D_IN - COL_LX
TAIL_SHIFT = COL_LX % LANES
W_IN_PAD = (COL_LX // LANES) * LANES + TAIL_COLS + LANES
assert COL_K % LANES == 0 and COL_QI % LANES == 0 and COL_KI % LANES == 0 and TAIL_COLS % LANES == 0

QA_W = N_HEADS_A * HEAD_DIM_A + N_IDX_HEADS * IDX_DIM
KV_W = (COL_QI - COL_K) + (ATTN_COLS - COL_KI)
K_OFF, V_OFF = 0, KV_DIM_A
KI_OFF = COL_QI - COL_K
WI_OFF = KI_OFF + IDX_DIM

TM_PROJ = 512

TM_SORT = 512
GRAN = 16
TM_MOE = 256
GRAN_PER_TILE = TM_MOE // GRAN
MOE_BUFFERS = 3
GATHER_UNROLL = 8
LOCAL_ROWS = -(-(2 * TM_SORT + N_EXPERTS * (GRAN - 1)) // LANES) * LANES
LOCAL_GRAN = LOCAL_ROWS // GRAN
AUX_W = LANES
W_PIECES = 3
AUX_W1, AUX_W2 = 0, W_PIECES
AUX_E1 = 2 * W_PIECES

R_LP1, R_LP2 = 0, 1


def _dot(a, b, **kw):
    return jnp.dot(a, b, preferred_element_type=F32, **kw)


def _params(*sem):
    return pltpu.CompilerParams(dimension_semantics=sem, vmem_limit_bytes=VMEM_LIMIT)


def _ada_kernel(c_ref, w_ref, b_ref, o_ref):
    c = c_ref[...]
    cond = c * jax.nn.sigmoid(c)
    w = w_ref[...]
    c_hi = cond.astype(BF16)
    c_lo = (cond - c_hi.astype(F32)).astype(BF16)
    w_hi = w.astype(BF16)
    w_lo = (w - w_hi.astype(F32)).astype(BF16)
    o_ref[...] = _dot(c_hi, w_hi) + (_dot(c_lo, w_hi) + _dot(c_hi, w_lo)) + b_ref[...]


def _ada(c, ada_w, ada_b):
    B, D = c.shape
    n_out = ada_w.shape[1]
    tn = 1024
    return pl.pallas_call(
        _ada_kernel,
        grid=(n_out // tn,),
        in_specs=[pl.BlockSpec((B, D), lambda j: (0, 0)),
                  pl.BlockSpec((D, tn), lambda j: (0, j)),
                  pl.BlockSpec((1, tn), lambda j: (0, j))],
        out_specs=pl.BlockSpec((B, tn), lambda j: (0, j)),
        out_shape=jax.ShapeDtypeStruct((B, n_out), F32),
        compiler_params=_params("arbitrary"),
        name="ada",
    )(c, ada_w, ada_b.reshape(1, n_out))


W_PREP_ROWS = 512


def _wprep_kernel(wt_ref, o_ref, stage, sem):
    row = lax.broadcasted_iota(I32, (W_PREP_ROWS, stage.shape[2]), 0)
    for r0 in range(0, W_IN_PAD, W_PREP_ROWS):
        n = min(W_PREP_ROWS, D_IN - r0)
        cp = pltpu.make_async_copy(wt_ref.at[r0:r0 + n], stage.at[0:n], sem)
        cp.start()
        cp.wait()
        blk = stage[:, 0, :]
        if n < W_PREP_ROWS:
            blk = jnp.where(row < n, blk, 0.0)
        o_ref[:, r0:r0 + W_PREP_ROWS] = blk.T.astype(BF16)


def _wprep(w_in):
    D = w_in.shape[1]
    return pl.pallas_call(
        _wprep_kernel,
        in_specs=[pl.BlockSpec(memory_space=pl.ANY)],
        out_shape=jax.ShapeDtypeStruct((D, W_IN_PAD), BF16),
        scratch_shapes=[pltpu.VMEM((W_PREP_ROWS, 1, D), F32), pltpu.SemaphoreType.DMA(())],
        compiler_params=pltpu.CompilerParams(vmem_limit_bytes=VMEM_LIMIT),
        name="wprep",
    )(jnp.transpose(w_in, (2, 0, 1)))


def _rms_mod(x, w, shift, scale):
    ms = jnp.mean(x * x, axis=-1, keepdims=True)
    y = x * lax.rsqrt(ms + EPS) * w
    return y * (1.0 + scale) + shift


def _proj_kernel(x_ref, mod_ref, nw_ref, w_ref, oqa_ref, okv_ref, olru_ref, og_ref, wt_s):
    @pl.when(pl.program_id(0) == 0)
    def _():
        rr = lax.broadcasted_iota(I32, (2 * LANES, LANES), 0)
        cc = lax.broadcasted_iota(I32, (2 * LANES, LANES), 1)
        shift = jnp.where(rr == cc + TAIL_SHIFT, 1.0, 0.0).astype(BF16)
        for j in range(TAIL_COLS // LANES):
            src = (COL_LX // LANES + j) * LANES
            wt_s[:, j * LANES:(j + 1) * LANES] = _dot(w_ref[:, src:src + 2 * LANES], shift).astype(BF16)

    h = _rms_mod(x_ref[...], nw_ref[...], mod_ref[0, 0:1, :], mod_ref[0, 1:2, :])
    hb = h.astype(BF16)
    pa = _dot(hb, w_ref[:, :ATTN_COLS])
    oqa_ref[...] = jnp.concatenate([pa[:, COL_Q:COL_K], pa[:, COL_QI:COL_KI]], axis=1)
    okv_ref[...] = jnp.concatenate([pa[:, COL_K:COL_QI], pa[:, COL_KI:ATTN_COLS]], axis=1)
    olru_ref[...] = _dot(hb, wt_s[:, :2 * LRU_WIDTH])
    og_ref[...] = _sigmoid(_dot(hb, wt_s[:, 2 * LRU_WIDTH:])).astype(BF16)


def _proj(x2, mod3, norm_w, w_all, S):
    N, D = x2.shape
    tm = TM_PROJ
    steps_per_batch = S // tm
    const = lambda t: (0, 0)
    return pl.pallas_call(
        _proj_kernel,
        grid=(N // tm,),
        in_specs=[pl.BlockSpec((tm, D), lambda t: (t, 0)),
                  pl.BlockSpec((1, 6, D), lambda t: (t // steps_per_batch, 0, 0)),
                  pl.BlockSpec((1, D), const),
                  pl.BlockSpec(w_all.shape, const)],
        out_specs=[pl.BlockSpec((tm, QA_W), lambda t: (t, 0)),
                   pl.BlockSpec((tm, KV_W), lambda t: (t, 0)),
                   pl.BlockSpec((tm, 2 * LRU_WIDTH), lambda t: (t, 0)),
                   pl.BlockSpec((tm, 2 * D), lambda t: (t, 0))],
        out_shape=[jax.ShapeDtypeStruct((N, QA_W), F32),
                   jax.ShapeDtypeStruct((N, KV_W), F32),
                   jax.ShapeDtypeStruct((N, 2 * LRU_WIDTH), F32),
                   jax.ShapeDtypeStruct((N, 2 * D), BF16)],
        scratch_shapes=[pltpu.VMEM((D, TAIL_COLS), BF16)],
        compiler_params=_params("arbitrary"),
        name="proj",
    )(x2, mod3, norm_w.reshape(1, D), w_all)


ATTN_BUCKETS = 8
ROW_FOLD = 64
KEY_ROWS = 256
COUNT_CHAINS = 2
LOG2E = float(np.log2(np.e))
POS_RADIX = 256
SLOPE_PIECES = 3
V_ONES = 16


def _bf16_pieces(c, n):
    out = []
    rest = float(c)
    for _ in range(n):
        p = float(np.asarray(rest, np.float32).astype(BF16).astype(np.float32))
        out.append(p)
        rest -= p
    return out


def _reduce_rows(op, x):
    r = x.shape[0]
    if r > ROW_FOLD and r % ROW_FOLD == 0:
        x = op(x.reshape(r // ROW_FOLD, ROW_FOLD, x.shape[1]), axis=0)
    return op(x, axis=0, keepdims=True)


def _attn_block(s_eff, topk, i, qa_ref, kvq_ref, qnw_ref, o_ref, kn_s, vt_s, ki_s, key_s, bias_s, lg_s, ot_s):
    nq = N_HEADS_A * HEAD_DIM_A
    q_t = qa_ref[:, :nq].T
    qi_t = qa_ref[:, nq:].T
    wq_t = kvq_ref[...].T[WI_OFF:WI_OFF + N_IDX_HEADS, :] * (N_IDX_HEADS ** -0.5 * IDX_DIM ** -0.5)

    q_pos = i * Q_BLOCK + lax.broadcasted_iota(I32, (1, Q_BLOCK), 1)
    limit = (q_pos // CHUNK + 1) * CHUNK
    chunks = [slice(r0, min(r0 + KEY_ROWS, s_eff)) for r0 in range(0, s_eff, KEY_ROWS)]

    qi = [qi_t[h * IDX_DIM:(h + 1) * IDX_DIM, :].astype(BF16) for h in range(N_IDX_HEADS)]
    for rows in chunks:
        ki = ki_s[rows, :]
        score = jnp.zeros((rows.stop - rows.start, Q_BLOCK), F32)
        for h in range(N_IDX_HEADS):
            score = score + jnp.maximum(_dot(ki, qi[h]), 0.0) * wq_t[h:h + 1, :]
        k_pos = rows.start + lax.broadcasted_iota(I32, score.shape, 0)
        key_s[rows, :] = jnp.where(k_pos < limit, score, -jnp.inf)

    def count(mask):
        return _reduce_rows(jnp.sum, jnp.where(mask, 1.0, 0.0))

    def count_keys(test):
        acc = [jnp.zeros((ROW_FOLD, Q_BLOCK), F32) for _ in range(COUNT_CHAINS)]
        for n, r0 in enumerate(range(0, s_eff, ROW_FOLD)):
            hit = jnp.where(test(key_s[r0:r0 + ROW_FOLD, :]), 1.0, 0.0)
            acc[n % COUNT_CHAINS] = acc[n % COUNT_CHAINS] + hit
        return jnp.sum(functools.reduce(lambda x, y: x + y, acc), axis=0, keepdims=True)

    def ordered_to_float(c):
        s = c ^ INT_MIN
        return pltpu.bitcast(jnp.where(s < 0, INT_MIN | (-s), s), F32)

    def bisect(it, r):
        c = r | jnp.left_shift(jnp.int32(1), 31 - it)
        cand = ordered_to_float(c)
        return jnp.where(count_keys(lambda slab: slab >= cand) >= topk, c, r)

    def select_by_threshold():
        thr = ordered_to_float(lax.fori_loop(0, 32, bisect, jnp.zeros((1, Q_BLOCK), I32)))

        n_ge = count_keys(lambda slab: slab >= thr)
        clean = jnp.max(jnp.where((n_ge == topk) & (thr > -jnp.inf), 0.0, 1.0)) == 0.0

        @pl.when(clean)
        def _():
            for rows in chunks:
                bias_s[rows, :] = jnp.where(key_s[rows, :] >= thr, 0.0, -jnp.inf)

        @pl.when(jnp.logical_not(clean))
        def _():
            key = key_s[:s_eff, :]
            need = topk - count(key > thr)
            rr = lax.broadcasted_iota(I32, (LANES, LANES), 0)
            cc = lax.broadcasted_iota(I32, (LANES, LANES), 1)
            tri = jnp.where(cc < rr, 1.0, 0.0).astype(BF16)
            seen = jnp.zeros((1, Q_BLOCK), F32)
            for c in range(s_eff // LANES):
                kc = key[c * LANES:(c + 1) * LANES, :]
                eq = kc == thr
                eqf = jnp.where(eq, 1.0, 0.0)
                rank = _dot(tri, eqf.astype(BF16)) + seen
                seen = seen + jnp.sum(eqf, axis=0, keepdims=True)
                tie_bias = jnp.where(jnp.where(eq, rank, topk) < need, 0.0, -jnp.inf)
                bias = jnp.where(kc > thr, 0.0, tie_bias)
                k_pos = c * LANES + lax.broadcasted_iota(I32, (LANES, Q_BLOCK), 0)
                bias_s[c * LANES:(c + 1) * LANES, :] = jnp.where(k_pos < limit, bias, -jnp.inf)

    if s_eff > topk:
        select_by_threshold()
    else:
        for rows in chunks:
            k_pos = rows.start + lax.broadcasted_iota(I32, (rows.stop - rows.start, Q_BLOCK), 0)
            bias_s[rows, :] = jnp.where(k_pos < limit, 0.0, -jnp.inf)

    qnw = qnw_ref[...]
    crow = lax.broadcasted_iota(I32, (LANES - HEAD_DIM_A, Q_BLOCK), 0)
    r_k = lax.broadcasted_iota(I32, (Q_BLOCK, Q_BLOCK), 0)
    c_q = lax.broadcasted_iota(I32, (Q_BLOCK, Q_BLOCK), 1)
    after = 2.0 * jnp.maximum(r_k - c_q, 0).astype(F32)
    band = pl.ds(pl.multiple_of(i * Q_BLOCK, Q_BLOCK), Q_BLOCK)
    slope = [sum(_bf16_pieces(LOG2E * 2.0 ** -(h + 1), SLOPE_PIECES)) for h in range(N_HEADS_A)]
    row_max = []
    for h in range(N_HEADS_A):
        pieces = _bf16_pieces(LOG2E * 2.0 ** -(h + 1), SLOPE_PIECES)
        qh = q_t[h * HEAD_DIM_A:(h + 1) * HEAD_DIM_A, :]
        ms = jnp.mean(qh * qh, axis=0, keepdims=True)
        qn = (qh * lax.rsqrt(ms + EPS) * qnw) * (HEAD_DIM_A ** -0.5) * LOG2E
        slopes = jnp.zeros(crow.shape, F32)
        for k, piece in enumerate(pieces):
            slopes = jnp.where(crow == 2 * k, POS_RADIX * piece, jnp.where(crow == 2 * k + 1, piece, slopes))
        qa = jnp.concatenate([qn, slopes], axis=0).astype(BF16)
        folded = []
        for rows in chunks:
            logits = _dot(kn_s[rows, :], qa) + bias_s[rows, :]
            lg_s[h, rows, :] = logits
            folded.append(jnp.max(logits.reshape(-1, ROW_FOLD, Q_BLOCK), axis=0))
        row_max.append(jnp.max(functools.reduce(jnp.maximum, folded), axis=0, keepdims=True))
    for h in range(N_HEADS_A):
        lg_s[h, band, :] = lg_s[h, band, :] - slope[h] * after
    for h in range(N_HEADS_A):
        acc = jnp.zeros((KV_DIM_A + V_ONES, Q_BLOCK), F32)
        for rows in chunks:
            p = jnp.exp2(lg_s[h, rows, :] - row_max[h])
            acc = acc + _dot(vt_s[:, rows], p.astype(BF16))
        ot_s[h * KV_DIM_A:(h + 1) * KV_DIM_A, :] = acc[:KV_DIM_A, :] / acc[KV_DIM_A:KV_DIM_A + 1, :]
    o_ref[...] = ot_s[...].T


def _attn_kernel(qa_ref, kvq_ref, kv_ref, qnw_ref, knw_ref, o_ref, kn_s, vt_s, ki_s, key_s, bias_s, lg_s, ot_s):
    i = pl.program_id(1)
    S = kv_ref.shape[0]
    nb = S // Q_BLOCK
    topk = float(min(TOPK_MAX, S // 4))

    @pl.when(i == 0)
    def _():
        kv = kv_ref[:, :LANES]
        lane = lax.broadcasted_iota(I32, kv.shape, 1)
        pos = lax.broadcasted_iota(I32, kv.shape, 0)
        ms = jnp.sum(jnp.where(lane < KV_DIM_A, kv * kv, 0.0), axis=-1, keepdims=True) * (1.0 / KV_DIM_A)
        kn = kv * lax.rsqrt(ms + EPS) * knw_ref[...]
        digit = jnp.where((lane & 1) == 0, pos // POS_RADIX, pos % POS_RADIX).astype(F32)
        is_digit = (lane >= KV_DIM_A) & (lane < KV_DIM_A + 2 * SLOPE_PIECES)
        kn_s[...] = jnp.where(is_digit, digit, kn).astype(BF16)
        vt_s[:KV_DIM_A, :] = kv.T[V_OFF:V_OFF + KV_DIM_A, :].astype(BF16)
        vt_s[KV_DIM_A:, :] = jnp.ones((V_ONES, kv.shape[0]), BF16)
        ki_s[...] = kv_ref[:, KI_OFF:KI_OFF + IDX_DIM].astype(BF16)

    n_buckets = min(ATTN_BUCKETS, nb)
    per = nb // n_buckets
    for j in range(n_buckets):
        pl.when(i // per == j)(functools.partial(
            _attn_block, (j + 1) * per * Q_BLOCK, topk, i, qa_ref, kvq_ref, qnw_ref, o_ref,
            kn_s, vt_s, ki_s, key_s, bias_s, lg_s, ot_s))


def _attn(qa, kv, q_norm_w, k_norm_w, B, S):
    N = qa.shape[0]
    nb = S // Q_BLOCK
    return pl.pallas_call(
        _attn_kernel,
        grid=(B, nb),
        in_specs=[pl.BlockSpec((Q_BLOCK, QA_W), lambda b, i: (b * nb + i, 0)),
                  pl.BlockSpec((Q_BLOCK, KV_W), lambda b, i: (b * nb + i, 0)),
                  pl.BlockSpec((S, KV_W), lambda b, i: (b, 0)),
                  pl.BlockSpec((HEAD_DIM_A, 1), lambda b, i: (0, 0)),
                  pl.BlockSpec((1, LANES), lambda b, i: (0, 0))],
        out_specs=pl.BlockSpec((Q_BLOCK, ATTN_OUT), lambda b, i: (b * nb + i, 0)),
        out_shape=jax.ShapeDtypeStruct((N, ATTN_OUT), F32),
        scratch_shapes=[pltpu.VMEM((S, LANES), BF16),
                        pltpu.VMEM((KV_DIM_A + V_ONES, S), BF16),
                        pltpu.VMEM((S, IDX_DIM), BF16),
                        pltpu.VMEM((S, Q_BLOCK), F32),
                        pltpu.VMEM((S, Q_BLOCK), F32),
                        pltpu.VMEM((N_HEADS_A, S, Q_BLOCK), F32),
                        pltpu.VMEM((ATTN_OUT, Q_BLOCK), F32)],
        compiler_params=_params("arbitrary", "arbitrary"),
        name="attn",
    )(qa, kv, kv, q_norm_w.reshape(HEAD_DIM_A, 1),
      jnp.pad(k_norm_w, (0, LANES - KV_DIM_A)).reshape(1, LANES))


SUBLANES = 8


def _shift_rows(x, d, fill, row):
    if d % SUBLANES == 0:
        return jnp.concatenate([jnp.full((d, x.shape[1]), fill, x.dtype), x[:-d]], axis=0)
    return jnp.where(row >= d, pltpu.roll(x, d, 0), fill)


def _sigmoid(x):
    return 0.5 * (1.0 + jnp.tanh(0.5 * x))


def _doubling_scan(a, b, row):
    n = a.shape[0]
    d = 1
    while d < n:
        b = a * _shift_rows(b, d, 0.0, row) + b
        if 2 * d < n:
            a = a * _shift_rows(a, d, 1.0, row)
        d *= 2
    return b


def _lru_kernel(xb_ref, gb_ref, cw_ref, cb_ref, wr_ref, br_ref, wi_ref, bi_ref, lam_ref, o_ref, a_s, b_s):
    x = xb_ref[...]
    S = x.shape[0]
    row = lax.broadcasted_iota(I32, x.shape, 0)
    xc = cb_ref[...] + _shift_rows(x, CONV_WIDTH - 1, 0.0, row) * cw_ref[0:1, :]
    for j in range(1, CONV_WIDTH):
        d = CONV_WIDTH - 1 - j
        xs = x if d == 0 else _shift_rows(x, d, 0.0, row)
        xc = xc + xs * cw_ref[j:j + 1, :]
    xcb = xc.astype(BF16)
    r = _sigmoid(_dot(xcb, wr_ref[0]) + br_ref[...])
    ig = _sigmoid(_dot(xcb, wi_ref[0]) + bi_ref[...])
    z = -lam_ref[...]
    softplus = jnp.maximum(z, 0.0) + jnp.log1p(jnp.exp(-jnp.abs(z)))
    log_a = -LRU_C * r * softplus
    a = jnp.exp(log_a)
    b = jnp.sqrt(-jnp.tanh(log_a) * (1.0 + a * a)) * (ig * xc)
    sub = row & (SUBLANES - 1)
    d = 1
    while d < SUBLANES:
        keep = sub >= d
        b = a * jnp.where(keep, pltpu.roll(b, d, 0), 0.0) + b
        a = a * jnp.where(keep, pltpu.roll(a, d, 0), 1.0)
        d *= 2
    a_s[...] = a
    b_s[...] = b
    groups = S // SUBLANES
    last = pl.ds(SUBLANES - 1, groups, stride=SUBLANES)
    g_row = lax.broadcasted_iota(I32, (groups, x.shape[1]), 0)
    h_last = _doubling_scan(a_s[last, :], b_s[last, :], g_row)
    carry = _shift_rows(h_last, 1, 0.0, g_row)
    carry = jnp.broadcast_to(carry[:, None, :], (groups, SUBLANES, x.shape[1])).reshape(x.shape)
    h = b_s[...] + a_s[...] * carry
    g = gb_ref[...]
    gelu = 0.5 * g * (1.0 + jnp.tanh(float(np.sqrt(2.0 / np.pi)) * (g + 0.044715 * (g * g * g))))
    o_ref[...] = h * gelu


def _lru(lru, conv_w, conv_b, wr_bd, b_rec, wi_bd, b_in, lam, B, S):
    N = lru.shape[0]
    nt = LRU_WIDTH // LANES
    vec = lambda b, j: (0, j)
    return pl.pallas_call(
        _lru_kernel,
        grid=(B, nt),
        in_specs=[pl.BlockSpec((S, LANES), lambda b, j: (b, j)),
                  pl.BlockSpec((S, LANES), lambda b, j: (b, nt + j)),
                  pl.BlockSpec((CONV_WIDTH, LANES), vec),
                  pl.BlockSpec((1, LANES), vec),
                  pl.BlockSpec((1, LANES, LANES), lambda b, j: (j, 0, 0)),
                  pl.BlockSpec((1, LANES), vec),
                  pl.BlockSpec((1, LANES, LANES), lambda b, j: (j, 0, 0)),
                  pl.BlockSpec((1, LANES), vec),
                  pl.BlockSpec((1, LANES), vec)],
        out_specs=pl.BlockSpec((S, LANES), lambda b, j: (b, j)),
        out_shape=jax.ShapeDtypeStruct((N, LRU_WIDTH), F32),
        scratch_shapes=[pltpu.VMEM((S, LANES), F32), pltpu.VMEM((S, LANES), F32)],
        compiler_params=_params("arbitrary", "arbitrary"),
        name="lru",
    )(lru, lru, conv_w, conv_b.reshape(1, -1), wr_bd, b_rec.reshape(1, -1), wi_bd,
      b_in.reshape(1, -1), lam.reshape(1, -1))


def _block_diag_tiles(w):
    per = LANES // LRU_BLOCK_DIM
    nt = LRU_WIDTH // LANES
    w5 = w.reshape(nt, per, LRU_BLOCK_DIM, 1, LRU_BLOCK_DIM)
    eye = jnp.eye(per, dtype=w.dtype).reshape(1, per, 1, per, 1)
    return (w5 * eye).reshape(nt, LANES, LANES)


def _first_index_of_max(vals, lane):
    m = jnp.max(vals, axis=-1, keepdims=True)
    idx = jnp.min(jnp.where(vals == m, lane, 4 * LANES), axis=-1, keepdims=True)
    return m, idx


def _weight_pieces(out, w, lane, first):
    rest = w
    for k in range(W_PIECES):
        piece = rest.astype(BF16).astype(F32)
        out = jnp.where(lane == first + k, piece, out)
        rest = rest - piece
    return out


def _merge_kernel(attn_ref, lru_ref, g_ref, x_ref, mod_ref, wpa_ref, wpb_ref, wo_ref, nw_ref, wr_ref,
                  br_ref, x1_ref, hsl_ref, route_ref, gc_ref):
    D = x_ref.shape[1]
    tm = x_ref.shape[0]

    y_a = _dot(attn_ref[...].astype(BF16), wpa_ref[...])
    y_b = _dot(lru_ref[...].astype(BF16), wpb_ref[...])
    merged = g_ref[:, :D].astype(F32) * y_a + g_ref[:, D:].astype(F32) * y_b
    x1 = x_ref[...] + mod_ref[0, 2:3, :] * _dot(merged.astype(BF16), wo_ref[...])
    x1_ref[...] = x1
    h2 = _rms_mod(x1, nw_ref[...], mod_ref[0, 3:4, :], mod_ref[0, 4:5, :])
    h2b = h2.astype(BF16)

    h2l = (h2 - h2b.astype(F32)).astype(BF16)
    hi_hi_lo = _dot(h2b, wr_ref[...])
    logits = hi_hi_lo[:, :LANES] + (_dot(h2l, wr_ref[:, :LANES]) + hi_hi_lo[:, LANES:]) + br_ref[...]
    lane = lax.broadcasted_iota(I32, logits.shape, 1)
    gl = jnp.where(lane < N_GROUPS, logits, -jnp.inf)
    gmax, g_sel = _first_index_of_max(gl, lane)
    g_weight = 1.0 / jnp.sum(jnp.exp(gl - gmax), axis=-1, keepdims=True)
    e_lo = N_GROUPS + g_sel * EXPERTS_PER_GROUP
    el = jnp.where((lane >= e_lo) & (lane < e_lo + EXPERTS_PER_GROUP), logits, -jnp.inf)
    v1, i1 = _first_index_of_max(el, lane)
    el2 = jnp.where(lane == i1, -jnp.inf, el)
    v2, i2 = _first_index_of_max(el2, lane)
    e2x = jnp.exp(v2 - v1)
    w1 = g_weight / (1.0 + e2x)
    w2 = g_weight * e2x / (1.0 + e2x)

    hot1 = lane == i1
    hot2 = lane == i2
    hot = jnp.where(hot1, 1.0, jnp.where(hot2, 1.0, 0.0))
    gcnt = jnp.floor((jnp.sum(hot, axis=0, keepdims=True) + (GRAN - 1.0)) * (1.0 / GRAN))
    gc_ref[0] = gcnt
    rr = lax.broadcasted_iota(I32, (LANES, LANES), 0)
    cc = lax.broadcasted_iota(I32, (LANES, LANES), 1)
    upper = jnp.where(rr < cc, 1.0, 0.0).astype(BF16)
    run_start = GRAN * _dot(jnp.broadcast_to(gcnt, (8, LANES)).astype(BF16), upper)[0:1, :]
    rr = lax.broadcasted_iota(I32, (tm, tm), 0)
    cc = lax.broadcasted_iota(I32, (tm, tm), 1)
    lower = jnp.where(cc < rr, 1.0, 0.0).astype(BF16)
    pos = _dot(lower, hot.astype(BF16)) + run_start
    pos1 = jnp.where(hot1, pos, 0.0)
    pos2 = jnp.where(hot2, pos, 0.0)
    lp1 = jnp.sum(pos1, axis=-1, keepdims=True)
    lp2 = jnp.sum(pos2, axis=-1, keepdims=True)
    route_ref[...] = jnp.where(lane == R_LP1, lp1, jnp.where(lane == R_LP2, lp2, 0.0))

    aux = jnp.where(lane == AUX_E1, (i1 - N_GROUPS).astype(F32), 0.0)
    aux = _weight_pieces(_weight_pieces(aux, w1, lane, AUX_W1), w2, lane, AUX_W2)
    lp1r = jnp.sum(pos1.T, axis=0, keepdims=True).astype(I32)
    lp2r = jnp.sum(pos2.T, axis=0, keepdims=True).astype(I32)
    rows = lax.broadcasted_iota(I32, (LOCAL_ROWS, tm), 0)
    sort = jnp.where(rows == lp1r, 1.0, jnp.where(rows == lp2r, 1.0, 0.0)).astype(BF16)
    hsl_ref[...] = _dot(sort, jnp.concatenate([h2b, aux.astype(BF16)], axis=1)).astype(BF16)


def _merge(attn, lru_out, gates, x2, mod3, wpa, wpb, wo, norm_w, wr_hi_lo, b_router, S):
    N, D = x2.shape
    tm = TM_SORT
    n_tiles = N // tm
    steps_per_batch = S // tm
    const = lambda t: (0, 0)
    rows = lambda t: (t, 0)
    return pl.pallas_call(
        _merge_kernel,
        grid=(n_tiles,),
        in_specs=[pl.BlockSpec((tm, ATTN_OUT), rows),
                  pl.BlockSpec((tm, LRU_WIDTH), rows),
                  pl.BlockSpec((tm, 2 * D), rows),
                  pl.BlockSpec((tm, D), rows),
                  pl.BlockSpec((1, 6, D), lambda t: (t // steps_per_batch, 0, 0)),
                  pl.BlockSpec(wpa.shape, const),
                  pl.BlockSpec(wpb.shape, const),
                  pl.BlockSpec(wo.shape, const),
                  pl.BlockSpec((1, D), const),
                  pl.BlockSpec(wr_hi_lo.shape, const),
                  pl.BlockSpec((1, LANES), const)],
        out_specs=[pl.BlockSpec((tm, D), rows),
                   pl.BlockSpec((LOCAL_ROWS, D + AUX_W), rows),
                   pl.BlockSpec((tm, LANES), rows),
                   pl.BlockSpec((1, 1, LANES), lambda t: (t, 0, 0))],
        out_shape=[jax.ShapeDtypeStruct((N, D), F32),
                   jax.ShapeDtypeStruct((n_tiles * LOCAL_ROWS, D + AUX_W), BF16),
                   jax.ShapeDtypeStruct((N, LANES), F32),
                   jax.ShapeDtypeStruct((n_tiles, 1, LANES), F32)],
        compiler_params=_params("arbitrary"),
        name="merge",
    )(attn, lru_out, gates, x2, mod3, wpa, wpb, wo, norm_w.reshape(1, D), wr_hi_lo, b_router)


def _plan_sizes(n_tiles):
    max_gran = n_tiles * LOCAL_GRAN + N_EXPERTS * (GRAN_PER_TILE - 1)
    max_row_tiles = -(-max_gran // GRAN_PER_TILE)
    return max_row_tiles * GRAN_PER_TILE, max_row_tiles


def _plan_kernel(gc_ref, src_ref, texp_ref, nt_ref, dst_ref, lcnt_ref):
    gc = gc_ref[...]
    T = gc.shape[0]
    row = lax.broadcasted_iota(I32, (T, LANES), 0)
    local = lax.broadcasted_iota(I32, (T, LANES), 1).astype(F32)
    rr = lax.broadcasted_iota(I32, (LANES, LANES), 0)
    cc = lax.broadcasted_iota(I32, (LANES, LANES), 1)
    upper = jnp.where(rr < cc, 1.0, 0.0).astype(BF16)

    tiles = jnp.floor((jnp.sum(gc, axis=0, keepdims=True) + (GRAN_PER_TILE - 1.0)) * (1.0 / GRAN_PER_TILE))
    tile0 = _dot(jnp.broadcast_to(tiles, (SUBLANES, LANES)).astype(BF16), upper)[0:1, :]
    tile_end = tile0 + tiles
    before = gc
    d = 1
    while d < T:
        before = before + _shift_rows(before, d, 0.0, row)
        d *= 2
    before = before - gc
    run_global = GRAN_PER_TILE * tile0 + before
    run_local = _dot(gc.astype(BF16), upper)
    lcnt_ref[...] = jnp.broadcast_to(jnp.sum(gc, axis=1, keepdims=True), (T, LANES)).astype(I32)

    experts = range(N_GROUPS, N_GROUPS + N_EXPERTS)
    dst = jnp.zeros((T, LANES), F32)
    for e in experts:
        l0 = run_local[:, e:e + 1]
        inside = (local >= l0) & (local < l0 + gc[:, e:e + 1])
        dst = jnp.where(inside, run_global[:, e:e + 1] + (local - l0), dst)
    dst_ref[...] = dst.astype(I32)

    shape = src_ref.shape
    slot = (lax.broadcasted_iota(I32, shape, 0) * LANES + lax.broadcasted_iota(I32, shape, 1)).astype(F32)
    src = jnp.zeros(shape, F32)
    for e in experts:
        off = slot - GRAN_PER_TILE * tile0[0:1, e:e + 1]
        for t in range(T):
            p = before[t:t + 1, e:e + 1]
            hit = (off >= p) & (off < p + gc[t:t + 1, e:e + 1])
            src = jnp.where(hit, (t * LOCAL_GRAN) + run_local[t:t + 1, e:e + 1] + (off - p), src)
    src_ref[...] = src.astype(I32)

    shape = texp_ref.shape
    tile = (lax.broadcasted_iota(I32, shape, 0) * LANES + lax.broadcasted_iota(I32, shape, 1)).astype(F32)
    owner = jnp.zeros(shape, F32)
    for e in experts:
        owner = owner + jnp.where(tile >= tile_end[0:1, e:e + 1], 1.0, 0.0)
    texp_ref[...] = jnp.minimum(owner, N_EXPERTS - 1.0).astype(I32)
    nt_ref[...] = jnp.broadcast_to(tile_end[0:1, N_GROUPS + N_EXPERTS - 1:N_GROUPS + N_EXPERTS],
                                   nt_ref.shape).astype(I32)


def _plan(gc):
    n_tiles = gc.shape[0]
    max_gran, max_row_tiles = _plan_sizes(n_tiles)
    gran_rows = -(-max_gran // LANES)
    tile_rows = -(-max_row_tiles // LANES)
    src, texp, used, dst, lcnt = pl.pallas_call(
        _plan_kernel,
        out_shape=[jax.ShapeDtypeStruct((gran_rows, LANES), I32),
                   jax.ShapeDtypeStruct((tile_rows, LANES), I32),
                   jax.ShapeDtypeStruct((1, LANES), I32),
                   jax.ShapeDtypeStruct((n_tiles, LANES), I32),
                   jax.ShapeDtypeStruct((n_tiles, LANES), I32)],
        name="plan",
    )(gc)
    return (src.reshape(-1), texp.reshape(-1)[:max_row_tiles], used[0, :1],
            dst[:, :LOCAL_GRAN].reshape(-1), lcnt[:, 0])


def _granule_copies(idx_ref, first, count, max_count, src_hbm, dst_vmem, sem, wait):
    def one(k):
        g = idx_ref[first + k]
        cp = pltpu.make_async_copy(src_hbm.at[pl.ds(pl.multiple_of(g * GRAN, GRAN), GRAN)],
                                   dst_vmem.at[k * GRAN:(k + 1) * GRAN], sem)
        if wait:
            cp.wait()
        else:
            cp.start()

    for b0 in range(0, max_count, GATHER_UNROLL):
        def block(b0=b0):
            for k in range(b0, min(b0 + GATHER_UNROLL, max_count)):
                one(k)
        if isinstance(count, int):
            if b0 < count:
                block()
        else:
            pl.when(b0 < count)(block)


def _moe_kernel(src_ref, texp_ref, nt_ref, hsl_ref, w1_ref, w3_ref, w2_ref, ys_ref, xbuf, w1b, w3b, w2b, sem):
    i = pl.program_id(0)
    used = nt_ref[0]
    slot = i % MOE_BUFFERS
    D = ys_ref.shape[1]

    def gather(tile, wait):
        slot_ = tile % MOE_BUFFERS
        _granule_copies(src_ref, tile * GRAN_PER_TILE, GRAN_PER_TILE, GRAN_PER_TILE, hsl_ref, xbuf.at[slot_],
                        sem.at[slot_], wait)

    @pl.when(i == 0)
    def _():
        for ahead in range(MOE_BUFFERS - 1):
            pl.when(ahead < used)(functools.partial(gather, ahead, False))

    def row_tile(prefetch):
        gather(i, True)

        @pl.when((i == 0) | (texp_ref[i] != texp_ref[jnp.maximum(i - 1, 0)]))
        def _():
            w1b[...] = w1_ref[0].astype(BF16)
            w3b[...] = w3_ref[0].astype(BF16)
            w2b[...] = w2_ref[0].astype(BF16)

        if prefetch:
            gather(i + (MOE_BUFFERS - 1), False)
        xb = xbuf[slot]
        hb = xb[:, :D]
        aux = xb[:, D:].astype(F32)
        first = aux[:, AUX_E1:AUX_E1 + 1] == texp_ref[i].astype(F32)
        w_row = jnp.zeros((TM_MOE, 1), F32)
        for k in range(W_PIECES):
            w_row = w_row + jnp.where(first, aux[:, AUX_W1 + k:AUX_W1 + k + 1], aux[:, AUX_W2 + k:AUX_W2 + k + 1])
        a = _dot(hb, w1b[...])
        act = (a * _sigmoid(a)) * _dot(hb, w3b[...])
        ys_ref[...] = (w_row * _dot(act.astype(BF16), w2b[...])).astype(BF16)

    more = i + (MOE_BUFFERS - 1) < used
    pl.when(more)(functools.partial(row_tile, True))
    pl.when((i < used) & jnp.logical_not(more))(functools.partial(row_tile, False))

    @pl.when(i >= used)
    def _():
        ys_ref[...] = jnp.zeros_like(ys_ref)


def _moe(src, texp, used, hsl, w1, w3, w2):
    max_row_tiles = texp.shape[0]
    D = w1.shape[1]
    F = w1.shape[2]
    grid_spec = pltpu.PrefetchScalarGridSpec(
        num_scalar_prefetch=3,
        grid=(max_row_tiles,),
        in_specs=[pl.BlockSpec(memory_space=pl.ANY),
                  pl.BlockSpec((1, D, F), lambda i, s, te, nt: (te[i], 0, 0)),
                  pl.BlockSpec((1, D, F), lambda i, s, te, nt: (te[i], 0, 0)),
                  pl.BlockSpec((1, F, D), lambda i, s, te, nt: (te[i], 0, 0))],
        out_specs=pl.BlockSpec((TM_MOE, D), lambda i, s, te, nt: (i, 0)),
        scratch_shapes=[pltpu.VMEM((MOE_BUFFERS, TM_MOE, D + AUX_W), BF16),
                        pltpu.VMEM((D, F), BF16), pltpu.VMEM((D, F), BF16), pltpu.VMEM((F, D), BF16),
                        pltpu.SemaphoreType.DMA((MOE_BUFFERS,))],
    )
    return pl.pallas_call(
        _moe_kernel,
        grid_spec=grid_spec,
        out_shape=jax.ShapeDtypeStruct((max_row_tiles * TM_MOE, D), BF16),
        compiler_params=_params("arbitrary"),
        name="moe",
    )(src, texp, used, hsl, w1, w3, w2)


def _combine_kernel(dst_ref, lcnt_ref, ys_ref, x1_ref, route_ref, mod_ref, o_ref, ybuf, sem):
    t = pl.program_id(0)
    slot = t % 2
    tm = x1_ref.shape[0]

    def gather(tile, slot_, wait):
        _granule_copies(dst_ref, tile * LOCAL_GRAN, lcnt_ref[tile], LOCAL_GRAN, ys_ref, ybuf.at[slot_],
                        sem.at[slot_], wait)

    @pl.when(t == 0)
    def _():
        ybuf[...] = jnp.zeros_like(ybuf)
        gather(0, 0, False)

    @pl.when(t + 1 < pl.num_programs(0))
    def _():
        gather(t + 1, 1 - slot, False)

    gather(t, slot, True)
    lp1 = route_ref[:, R_LP1:R_LP1 + 1].astype(I32)
    lp2 = route_ref[:, R_LP2:R_LP2 + 1].astype(I32)
    col = lax.broadcasted_iota(I32, (tm, LOCAL_ROWS), 1)
    unsort = jnp.where(col == lp1, 1.0, jnp.where(col == lp2, 1.0, 0.0)).astype(BF16)
    o_ref[...] = x1_ref[...] + mod_ref[0, 5:6, :] * _dot(unsort, ybuf[slot])


def _combine(dst, lcnt, ys, x1, route, mod3, S):
    N, D = x1.shape
    tm = TM_SORT
    steps_per_batch = S // tm
    grid_spec = pltpu.PrefetchScalarGridSpec(
        num_scalar_prefetch=2,
        grid=(N // tm,),
        in_specs=[pl.BlockSpec(memory_space=pl.ANY),
                  pl.BlockSpec((tm, D), lambda t, d, c: (t, 0)),
                  pl.BlockSpec((tm, LANES), lambda t, d, c: (t, 0)),
                  pl.BlockSpec((1, 6, D), lambda t, d, c: (t // steps_per_batch, 0, 0))],
        out_specs=pl.BlockSpec((tm, D), lambda t, d, c: (t, 0)),
        scratch_shapes=[pltpu.VMEM((2, LOCAL_ROWS, D), BF16), pltpu.SemaphoreType.DMA((2,))],
    )
    return pl.pallas_call(
        _combine_kernel,
        grid_spec=grid_spec,
        out_shape=jax.ShapeDtypeStruct((N, D), F32),
        compiler_params=_params("arbitrary"),
        name="combine",
    )(dst, lcnt, ys, x1, route, mod3)


def _layer(x2, mod3, B, S, norm_mix_w, w_in, q_norm_w, k_norm_w, conv_w, conv_b, w_rec_gate, b_rec_gate,
           w_in_gate, b_in_gate, lru_lambda, w_proj_a, w_proj_b, w_out, norm_ffn_w, w_group, b_group,
           w_expert_router, b_expert_router, w1, w3, w2):
    N, D = x2.shape
    qa, kv, lru, gates = _proj(x2, mod3, norm_mix_w, _wprep(w_in), S)
    attn = _attn(qa, kv, q_norm_w, k_norm_w, B, S)
    lru_out = _lru(lru, conv_w, conv_b, _block_diag_tiles(w_rec_gate).astype(BF16), b_rec_gate,
                   _block_diag_tiles(w_in_gate).astype(BF16), b_in_gate, lru_lambda, B, S)

    n_r = N_GROUPS + N_EXPERTS
    w_router = jnp.concatenate([w_group, w_expert_router, jnp.zeros((D, LANES - n_r), F32)], axis=1)
    wr_hi = w_router.astype(BF16)
    wr_lo = (w_router - wr_hi.astype(F32)).astype(BF16)
    b_router = jnp.concatenate([b_group, b_expert_router, jnp.zeros((LANES - n_r,), F32)]).reshape(1, LANES)
    x1, hsl, route, gc = _merge(attn, lru_out, gates, x2, mod3, w_proj_a.astype(BF16), w_proj_b.astype(BF16),
                                w_out.astype(BF16), norm_ffn_w, jnp.concatenate([wr_hi, wr_lo], axis=1),
                                b_router, S)

    src, texp, used, dst, lcnt = _plan(gc.reshape(gc.shape[0], LANES))
    ys = _moe(src, texp, used, hsl, w1, w3, w2)
    return _combine(dst, lcnt, ys, x1, route, mod3, S)


def kernel(x, c, ada_w, ada_b, norm_mix_w, w_in, q_norm_w, k_norm_w, conv_w, conv_b, w_rec_gate, b_rec_gate,
           w_in_gate, b_in_gate, lru_lambda, w_proj_a, w_proj_b, w_out, norm_ffn_w, w_group, b_group,
           w_expert_router, b_expert_router, w1, w3, w2):
    B, S, D = x.shape
    x2 = x.reshape(B * S, D)
    for l in range(ada_w.shape[0]):
        mod3 = _ada(c, ada_w[l], ada_b[l]).reshape(B, 6, D)
        x2 = _layer(x2, mod3, B, S, norm_mix_w[l], w_in[l:l + 1], q_norm_w[l], k_norm_w[l], conv_w[l], conv_b[l],
                    w_rec_gate[l], b_rec_gate[l], w_in_gate[l], b_in_gate[l], lru_lambda[l], w_proj_a[l],
                    w_proj_b[l], w_out[l], norm_ffn_w[l], w_group[l], b_group[l], w_expert_router[l],
                    b_expert_router[l], w1[l], w3[l], w2[l])
    return x2.reshape(B, S, D)
```

```python
import functools

import jax
import jax.numpy as jnp
import numpy as np
from jax import lax
from jax.experimental import pallas as pl
from jax.experimental.pallas import tpu as pltpu

F32 = jnp.float32
BF16 = jnp.bfloat16
I32 = jnp.int32

EPS = 1e-6
CHUNK = 64
Q_BLOCK = 128
N_HEADS_A = 8
HEAD_DIM_A = 64
KV_DIM_A = 64
ATTN_OUT = N_HEADS_A * KV_DIM_A
N_IDX_HEADS = 4
IDX_DIM = 64
TOPK_MAX = 256
LRU_WIDTH = 512
LRU_BLOCKS = 8
LRU_BLOCK_DIM = LRU_WIDTH // LRU_BLOCKS
CONV_WIDTH = 4
LRU_C = 8.0
N_GROUPS = 4
EXPERTS_PER_GROUP = 8
N_EXPERTS = N_GROUPS * EXPERTS_PER_GROUP
D_FF_EXPERT = 256

LANES = 128
INT_MIN = -2 ** 31
VMEM_LIMIT = 56 * 1024 * 1024

D_MODEL = 1024
COL_Q = 0
COL_K = COL_Q + N_HEADS_A * HEAD_DIM_A
COL_V = COL_K + KV_DIM_A
COL_QI = COL_V + KV_DIM_A
COL_KI = COL_QI + N_IDX_HEADS * IDX_DIM
COL_WI = COL_KI + IDX_DIM
COL_LX = COL_WI + N_IDX_HEADS
D_IN = COL_LX + 2 * LRU_WIDTH + 2 * D_MODEL
ATTN_COLS = -(-COL_LX // LANES) * LANES
TAIL_COLS = D_IN - COL_LX
TAIL_SHIFT = COL_LX % LANES
W_IN_PAD = (COL_LX // LANES) * LANES + TAIL_COLS + LANES
assert COL_K % LANES == 0 and COL_QI % LANES == 0 and COL_KI % LANES == 0 and TAIL_COLS % LANES == 0

QA_W = N_HEADS_A * HEAD_DIM_A + N_IDX_HEADS * IDX_DIM
KV_W = (COL_QI - COL_K) + (ATTN_COLS - COL_KI)
K_OFF, V_OFF = 0, KV_DIM_A
KI_OFF = COL_QI - COL_K
WI_OFF = KI_OFF + IDX_DIM

TM_PROJ = 512

TM_SORT = 512
GRAN = 16
TM_MOE = 256
GRAN_PER_TILE = TM_MOE // GRAN
MOE_BUFFERS = 3
GATHER_UNROLL = 8
LOCAL_ROWS = -(-(2 * TM_SORT + N_EXPERTS * (GRAN - 1)) // LANES) * LANES
LOCAL_GRAN = LOCAL_ROWS // GRAN
AUX_W = LANES
W_PIECES = 3
AUX_W1, AUX_W2 = 0, W_PIECES
AUX_E1 = 2 * W_PIECES

R_LP1, R_LP2 = 0, 1


def _dot(a, b, **kw):
    return jnp.dot(a, b, preferred_element_type=F32, **kw)


def _params(*sem):
    return pltpu.CompilerParams(dimension_semantics=sem, vmem_limit_bytes=VMEM_LIMIT)


def _ada_kernel(c_ref, w_ref, b_ref, o_ref):
    c = c_ref[...]
    cond = c * jax.nn.sigmoid(c)
    w = w_ref[...]
    c_hi = cond.astype(BF16)
    c_lo = (cond - c_hi.astype(F32)).astype(BF16)
    w_hi = w.astype(BF16)
    w_lo = (w - w_hi.astype(F32)).astype(BF16)
    o_ref[...] = _dot(c_hi, w_hi) + (_dot(c_lo, w_hi) + _dot(c_hi, w_lo)) + b_ref[...]


def _ada(c, ada_w, ada_b):
    B, D = c.shape
    n_out = ada_w.shape[1]
    tn = 1024
    return pl.pallas_call(
        _ada_kernel,
        grid=(n_out // tn,),
        in_specs=[pl.BlockSpec((B, D), lambda j: (0, 0)),
                  pl.BlockSpec((D, tn), lambda j: (0, j)),
                  pl.BlockSpec((1, tn), lambda j: (0, j))],
        out_specs=pl.BlockSpec((B, tn), lambda j: (0, j)),
        out_shape=jax.ShapeDtypeStruct((B, n_out), F32),
        compiler_params=_params("arbitrary"),
        name="ada",
    )(c, ada_w, ada_b.reshape(1, n_out))


W_PREP_ROWS = 512


def _wprep_kernel(wt_ref, o_ref, stage, sem):
    row = lax.broadcasted_iota(I32, (W_PREP_ROWS, stage.shape[3]), 0)
    starts = list(range(0, W_IN_PAD, W_PREP_ROWS))

    def copy(j):
        r0 = starts[j]
        n = min(W_PREP_ROWS, D_IN - r0)
        return pltpu.make_async_copy(wt_ref.at[r0:r0 + n], stage.at[j % 2, 0:n], sem.at[j % 2]), n

    copy(0)[0].start()
    for j, r0 in enumerate(starts):
        if j + 1 < len(starts):
            copy(j + 1)[0].start()
        cp, n = copy(j)
        cp.wait()
        blk = stage[j % 2, :, 0, :]
        if n < W_PREP_ROWS:
            blk = jnp.where(row < n, blk, 0.0)
        o_ref[:, r0:r0 + W_PREP_ROWS] = blk.T.astype(BF16)


def _wprep(w_in):
    D = w_in.shape[1]
    return pl.pallas_call(
        _wprep_kernel,
        in_specs=[pl.BlockSpec(memory_space=pl.ANY)],
        out_shape=jax.ShapeDtypeStruct((D, W_IN_PAD), BF16),
        scratch_shapes=[pltpu.VMEM((2, W_PREP_ROWS, 1, D), F32), pltpu.SemaphoreType.DMA((2,))],
        compiler_params=pltpu.CompilerParams(vmem_limit_bytes=VMEM_LIMIT),
        name="wprep",
    )(jnp.transpose(w_in, (2, 0, 1)))


def _rms_mod(x, w, shift, scale):
    ms = jnp.mean(x * x, axis=-1, keepdims=True)
    y = x * lax.rsqrt(ms + EPS) * w
    return y * (1.0 + scale) + shift


def _proj_kernel(x_ref, mod_ref, nw_ref, w_ref, oqa_ref, okv_ref, olru_ref, og_ref, wt_s):
    @pl.when(pl.program_id(0) == 0)
    def _():
        rr = lax.broadcasted_iota(I32, (2 * LANES, LANES), 0)
        cc = lax.broadcasted_iota(I32, (2 * LANES, LANES), 1)
        shift = jnp.where(rr == cc + TAIL_SHIFT, 1.0, 0.0).astype(BF16)
        for j in range(TAIL_COLS // LANES):
            src = (COL_LX // LANES + j) * LANES
            wt_s[:, j * LANES:(j + 1) * LANES] = _dot(w_ref[:, src:src + 2 * LANES], shift).astype(BF16)

    h = _rms_mod(x_ref[...], nw_ref[...], mod_ref[0, 0:1, :], mod_ref[0, 1:2, :])
    hb = h.astype(BF16)
    pa = _dot(hb, w_ref[:, :ATTN_COLS])
    oqa_ref[...] = jnp.concatenate([pa[:, COL_Q:COL_K], pa[:, COL_QI:COL_KI]], axis=1)
    okv_ref[...] = jnp.concatenate([pa[:, COL_K:COL_QI], pa[:, COL_KI:ATTN_COLS]], axis=1)
    olru_ref[...] = _dot(hb, wt_s[:, :2 * LRU_WIDTH])
    og_ref[...] = _sigmoid(_dot(hb, wt_s[:, 2 * LRU_WIDTH:])).astype(BF16)


def _proj(x2, mod3, norm_w, w_all, S):
    N, D = x2.shape
    tm = TM_PROJ
    steps_per_batch = S // tm
    const = lambda t: (0, 0)
    return pl.pallas_call(
        _proj_kernel,
        grid=(N // tm,),
        in_specs=[pl.BlockSpec((tm, D), lambda t: (t, 0)),
                  pl.BlockSpec((1, 6, D), lambda t: (t // steps_per_batch, 0, 0)),
                  pl.BlockSpec((1, D), const),
                  pl.BlockSpec(w_all.shape, const)],
        out_specs=[pl.BlockSpec((tm, QA_W), lambda t: (t, 0)),
                   pl.BlockSpec((tm, KV_W), lambda t: (t, 0)),
                   pl.BlockSpec((tm, 2 * LRU_WIDTH), lambda t: (t, 0)),
                   pl.BlockSpec((tm, 2 * D), lambda t: (t, 0))],
        out_shape=[jax.ShapeDtypeStruct((N, QA_W), F32),
                   jax.ShapeDtypeStruct((N, KV_W), F32),
                   jax.ShapeDtypeStruct((N, 2 * LRU_WIDTH), F32),
                   jax.ShapeDtypeStruct((N, 2 * D), BF16)],
        scratch_shapes=[pltpu.VMEM((D, TAIL_COLS), BF16)],
        compiler_params=_params("arbitrary"),
        name="proj",
    )(x2, mod3, norm_w.reshape(1, D), w_all)


ATTN_BUCKETS = 8
ROW_FOLD = 64
KEY_ROWS = 256
COUNT_CHAINS = 2
LOG2E = float(np.log2(np.e))
POS_RADIX = 256
SLOPE_PIECES = 3
V_ONES = 16


def _bf16_pieces(c, n):
    out = []
    rest = float(c)
    for _ in range(n):
        p = float(np.asarray(rest, np.float32).astype(BF16).astype(np.float32))
        out.append(p)
        rest -= p
    return out


def _reduce_rows(op, x):
    r = x.shape[0]
    if r > ROW_FOLD and r % ROW_FOLD == 0:
        x = op(x.reshape(r // ROW_FOLD, ROW_FOLD, x.shape[1]), axis=0)
    return op(x, axis=0, keepdims=True)


def _attn_block(s_eff, topk, i, qa_ref, kvq_ref, qnw_ref, o_ref, kn_s, vt_s, ki_s, key_s, bias_s, lg_s, ot_s):
    nq = N_HEADS_A * HEAD_DIM_A
    q_t = qa_ref[:, :nq].T
    qi_t = qa_ref[:, nq:].T
    wq_t = kvq_ref[...].T[WI_OFF:WI_OFF + N_IDX_HEADS, :] * (N_IDX_HEADS ** -0.5 * IDX_DIM ** -0.5)

    q_pos = i * Q_BLOCK + lax.broadcasted_iota(I32, (1, Q_BLOCK), 1)
    limit = (q_pos // CHUNK + 1) * CHUNK
    chunks = [slice(r0, min(r0 + KEY_ROWS, s_eff)) for r0 in range(0, s_eff, KEY_ROWS)]

    qi = [qi_t[h * IDX_DIM:(h + 1) * IDX_DIM, :].astype(BF16) for h in range(N_IDX_HEADS)]
    for rows in chunks:
        ki = ki_s[rows, :]
        score = jnp.zeros((rows.stop - rows.start, Q_BLOCK), F32)
        for h in range(N_IDX_HEADS):
            score = score + jnp.maximum(_dot(ki, qi[h]), 0.0) * wq_t[h:h + 1, :]
        k_pos = rows.start + lax.broadcasted_iota(I32, score.shape, 0)
        key_s[rows, :] = jnp.where(k_pos < limit, score, -jnp.inf)

    def count(mask):
        return _reduce_rows(jnp.sum, jnp.where(mask, 1.0, 0.0))

    def count_keys(test):
        acc = [jnp.zeros((ROW_FOLD, Q_BLOCK), F32) for _ in range(COUNT_CHAINS)]
        for n, r0 in enumerate(range(0, s_eff, ROW_FOLD)):
            hit = jnp.where(test(key_s[r0:r0 + ROW_FOLD, :]), 1.0, 0.0)
            acc[n % COUNT_CHAINS] = acc[n % COUNT_CHAINS] + hit
        return jnp.sum(functools.reduce(lambda x, y: x + y, acc), axis=0, keepdims=True)

    def ordered_to_float(c):
        s = c ^ INT_MIN
        return pltpu.bitcast(jnp.where(s < 0, INT_MIN | (-s), s), F32)

    def bisect(it, r):
        c = r | jnp.left_shift(jnp.int32(1), 31 - it)
        cand = ordered_to_float(c)
        return jnp.where(count_keys(lambda slab: slab >= cand) >= topk, c, r)

    def select_by_threshold():
        thr = ordered_to_float(lax.fori_loop(0, 32, bisect, jnp.zeros((1, Q_BLOCK), I32)))

        n_ge = count_keys(lambda slab: slab >= thr)
        clean = jnp.max(jnp.where((n_ge == topk) & (thr > -jnp.inf), 0.0, 1.0)) == 0.0

        @pl.when(clean)
        def _():
            for rows in chunks:
                bias_s[rows, :] = jnp.where(key_s[rows, :] >= thr, 0.0, -jnp.inf)

        @pl.when(jnp.logical_not(clean))
        def _():
            key = key_s[:s_eff, :]
            need = topk - count(key > thr)
            rr = lax.broadcasted_iota(I32, (LANES, LANES), 0)
            cc = lax.broadcasted_iota(I32, (LANES, LANES), 1)
            tri = jnp.where(cc < rr, 1.0, 0.0).astype(BF16)
            seen = jnp.zeros((1, Q_BLOCK), F32)
            for c in range(s_eff // LANES):
                kc = key[c * LANES:(c + 1) * LANES, :]
                eq = kc == thr
                eqf = jnp.where(eq, 1.0, 0.0)
                rank = _dot(tri, eqf.astype(BF16)) + seen
                seen = seen + jnp.sum(eqf, axis=0, keepdims=True)
                tie_bias = jnp.where(jnp.where(eq, rank, topk) < need, 0.0, -jnp.inf)
                bias = jnp.where(kc > thr, 0.0, tie_bias)
                k_pos = c * LANES + lax.broadcasted_iota(I32, (LANES, Q_BLOCK), 0)
                bias_s[c * LANES:(c + 1) * LANES, :] = jnp.where(k_pos < limit, bias, -jnp.inf)

    if s_eff > topk:
        select_by_threshold()
    else:
        for rows in chunks:
            k_pos = rows.start + lax.broadcasted_iota(I32, (rows.stop - rows.start, Q_BLOCK), 0)
            bias_s[rows, :] = jnp.where(k_pos < limit, 0.0, -jnp.inf)

    qnw = qnw_ref[...]
    crow = lax.broadcasted_iota(I32, (LANES - HEAD_DIM_A, Q_BLOCK), 0)
    r_k = lax.broadcasted_iota(I32, (Q_BLOCK, Q_BLOCK), 0)
    c_q = lax.broadcasted_iota(I32, (Q_BLOCK, Q_BLOCK), 1)
    after = 2.0 * jnp.maximum(r_k - c_q, 0).astype(F32)
    band = pl.ds(pl.multiple_of(i * Q_BLOCK, Q_BLOCK), Q_BLOCK)
    slope = [sum(_bf16_pieces(LOG2E * 2.0 ** -(h + 1), SLOPE_PIECES)) for h in range(N_HEADS_A)]
    row_max = []
    for h in range(N_HEADS_A):
        pieces = _bf16_pieces(LOG2E * 2.0 ** -(h + 1), SLOPE_PIECES)
        qh = q_t[h * HEAD_DIM_A:(h + 1) * HEAD_DIM_A, :]
        ms = jnp.mean(qh * qh, axis=0, keepdims=True)
        qn = (qh * lax.rsqrt(ms + EPS) * qnw) * (HEAD_DIM_A ** -0.5) * LOG2E
        slopes = jnp.zeros(crow.shape, F32)
        for k, piece in enumerate(pieces):
            slopes = jnp.where(crow == 2 * k, POS_RADIX * piece, jnp.where(crow == 2 * k + 1, piece, slopes))
        qa = jnp.concatenate([qn, slopes], axis=0).astype(BF16)
        folded = []
        for rows in chunks:
            logits = _dot(kn_s[rows, :], qa) + bias_s[rows, :]
            lg_s[h, rows, :] = logits
            folded.append(jnp.max(logits.reshape(-1, ROW_FOLD, Q_BLOCK), axis=0))
        row_max.append(jnp.max(functools.reduce(jnp.maximum, folded), axis=0, keepdims=True))
    for h in range(N_HEADS_A):
        lg_s[h, band, :] = lg_s[h, band, :] - slope[h] * after
    for h in range(N_HEADS_A):
        acc = jnp.zeros((KV_DIM_A + V_ONES, Q_BLOCK), F32)
        for rows in chunks:
            p = jnp.exp2(lg_s[h, rows, :] - row_max[h])
            acc = acc + _dot(vt_s[:, rows], p.astype(BF16))
        ot_s[h * KV_DIM_A:(h + 1) * KV_DIM_A, :] = acc[:KV_DIM_A, :] / acc[KV_DIM_A:KV_DIM_A + 1, :]
    o_ref[...] = ot_s[...].T


def _attn_kernel(qa_ref, kvq_ref, kv_ref, qnw_ref, knw_ref, o_ref, kn_s, vt_s, ki_s, key_s, bias_s, lg_s, ot_s):
    i = pl.program_id(1)
    S = kv_ref.shape[0]
    nb = S // Q_BLOCK
    topk = float(min(TOPK_MAX, S // 4))

    @pl.when(i == 0)
    def _():
        kv = kv_ref[:, :LANES]
        lane = lax.broadcasted_iota(I32, kv.shape, 1)
        pos = lax.broadcasted_iota(I32, kv.shape, 0)
        ms = jnp.sum(jnp.where(lane < KV_DIM_A, kv * kv, 0.0), axis=-1, keepdims=True) * (1.0 / KV_DIM_A)
        kn = kv * lax.rsqrt(ms + EPS) * knw_ref[...]
        digit = jnp.where((lane & 1) == 0, pos // POS_RADIX, pos % POS_RADIX).astype(F32)
        is_digit = (lane >= KV_DIM_A) & (lane < KV_DIM_A + 2 * SLOPE_PIECES)
        kn_s[...] = jnp.where(is_digit, digit, kn).astype(BF16)
        vt_s[:KV_DIM_A, :] = kv.T[V_OFF:V_OFF + KV_DIM_A, :].astype(BF16)
        vt_s[KV_DIM_A:, :] = jnp.ones((V_ONES, kv.shape[0]), BF16)
        ki_s[...] = kv_ref[:, KI_OFF:KI_OFF + IDX_DIM].astype(BF16)

    n_buckets = min(ATTN_BUCKETS, nb)
    per = nb // n_buckets
    for j in range(n_buckets):
        pl.when(i // per == j)(functools.partial(
            _attn_block, (j + 1) * per * Q_BLOCK, topk, i, qa_ref, kvq_ref, qnw_ref, o_ref,
            kn_s, vt_s, ki_s, key_s, bias_s, lg_s, ot_s))


def _attn(qa, kv, q_norm_w, k_norm_w, B, S):
    N = qa.shape[0]
    nb = S // Q_BLOCK
    return pl.pallas_call(
        _attn_kernel,
        grid=(B, nb),
        in_specs=[pl.BlockSpec((Q_BLOCK, QA_W), lambda b, i: (b * nb + i, 0)),
                  pl.BlockSpec((Q_BLOCK, KV_W), lambda b, i: (b * nb + i, 0)),
                  pl.BlockSpec((S, KV_W), lambda b, i: (b, 0)),
                  pl.BlockSpec((HEAD_DIM_A, 1), lambda b, i: (0, 0)),
                  pl.BlockSpec((1, LANES), lambda b, i: (0, 0))],
        out_specs=pl.BlockSpec((Q_BLOCK, ATTN_OUT), lambda b, i: (b * nb + i, 0)),
        out_shape=jax.ShapeDtypeStruct((N, ATTN_OUT), F32),
        scratch_shapes=[pltpu.VMEM((S, LANES), BF16),
                        pltpu.VMEM((KV_DIM_A + V_ONES, S), BF16),
                        pltpu.VMEM((S, IDX_DIM), BF16),
                        pltpu.VMEM((S, Q_BLOCK), F32),
                        pltpu.VMEM((S, Q_BLOCK), F32),
                        pltpu.VMEM((N_HEADS_A, S, Q_BLOCK), F32),
                        pltpu.VMEM((ATTN_OUT, Q_BLOCK), F32)],
        compiler_params=_params("arbitrary", "arbitrary"),
        name="attn",
    )(qa, kv, kv, q_norm_w.reshape(HEAD_DIM_A, 1),
      jnp.pad(k_norm_w, (0, LANES - KV_DIM_A)).reshape(1, LANES))


SUBLANES = 8


def _shift_rows(x, d, fill, row):
    if d % SUBLANES == 0:
        return jnp.concatenate([jnp.full((d, x.shape[1]), fill, x.dtype), x[:-d]], axis=0)
    return jnp.where(row >= d, pltpu.roll(x, d, 0), fill)


def _sigmoid(x):
    return 0.5 * (1.0 + jnp.tanh(0.5 * x))


def _doubling_scan(a, b, row):
    n = a.shape[0]
    d = 1
    while d < n:
        b = a * _shift_rows(b, d, 0.0, row) + b
        if 2 * d < n:
            a = a * _shift_rows(a, d, 1.0, row)
        d *= 2
    return b


def _lru_kernel(xb_ref, gb_ref, cw_ref, cb_ref, wr_ref, br_ref, wi_ref, bi_ref, lam_ref, o_ref, a_s, b_s):
    x = xb_ref[...]
    S = x.shape[0]
    row = lax.broadcasted_iota(I32, x.shape, 0)
    xc = cb_ref[...] + _shift_rows(x, CONV_WIDTH - 1, 0.0, row) * cw_ref[0:1, :]
    for j in range(1, CONV_WIDTH):
        d = CONV_WIDTH - 1 - j
        xs = x if d == 0 else _shift_rows(x, d, 0.0, row)
        xc = xc + xs * cw_ref[j:j + 1, :]
    xcb = xc.astype(BF16)
    r = _sigmoid(_dot(xcb, wr_ref[0]) + br_ref[...])
    ig = _sigmoid(_dot(xcb, wi_ref[0]) + bi_ref[...])
    z = -lam_ref[...]
    softplus = jnp.maximum(z, 0.0) + jnp.log1p(jnp.exp(-jnp.abs(z)))
    log_a = -LRU_C * r * softplus
    a = jnp.exp(log_a)
    b = jnp.sqrt(-jnp.tanh(log_a) * (1.0 + a * a)) * (ig * xc)
    sub = row & (SUBLANES - 1)
    d = 1
    while d < SUBLANES:
        keep = sub >= d
        b = a * jnp.where(keep, pltpu.roll(b, d, 0), 0.0) + b
        a = a * jnp.where(keep, pltpu.roll(a, d, 0), 1.0)
        d *= 2
    a_s[...] = a
    b_s[...] = b
    groups = S // SUBLANES
    last = pl.ds(SUBLANES - 1, groups, stride=SUBLANES)
    g_row = lax.broadcasted_iota(I32, (groups, x.shape[1]), 0)
    h_last = _doubling_scan(a_s[last, :], b_s[last, :], g_row)
    carry = _shift_rows(h_last, 1, 0.0, g_row)
    carry = jnp.broadcast_to(carry[:, None, :], (groups, SUBLANES, x.shape[1])).reshape(x.shape)
    h = b_s[...] + a_s[...] * carry
    g = gb_ref[...]
    gelu = 0.5 * g * (1.0 + jnp.tanh(float(np.sqrt(2.0 / np.pi)) * (g + 0.044715 * (g * g * g))))
    o_ref[...] = h * gelu


def _lru(lru, conv_w, conv_b, wr_bd, b_rec, wi_bd, b_in, lam, B, S):
    N = lru.shape[0]
    nt = LRU_WIDTH // LANES
    vec = lambda b, j: (0, j)
    return pl.pallas_call(
        _lru_kernel,
        grid=(B, nt),
        in_specs=[pl.BlockSpec((S, LANES), lambda b, j: (b, j)),
                  pl.BlockSpec((S, LANES), lambda b, j: (b, nt + j)),
                  pl.BlockSpec((CONV_WIDTH, LANES), vec),
                  pl.BlockSpec((1, LANES), vec),
                  pl.BlockSpec((1, LANES, LANES), lambda b, j: (j, 0, 0)),
                  pl.BlockSpec((1, LANES), vec),
                  pl.BlockSpec((1, LANES, LANES), lambda b, j: (j, 0, 0)),
                  pl.BlockSpec((1, LANES), vec),
                  pl.BlockSpec((1, LANES), vec)],
        out_specs=pl.BlockSpec((S, LANES), lambda b, j: (b, j)),
        out_shape=jax.ShapeDtypeStruct((N, LRU_WIDTH), F32),
        scratch_shapes=[pltpu.VMEM((S, LANES), F32), pltpu.VMEM((S, LANES), F32)],
        compiler_params=_params("arbitrary", "arbitrary"),
        name="lru",
    )(lru, lru, conv_w, conv_b.reshape(1, -1), wr_bd, b_rec.reshape(1, -1), wi_bd,
      b_in.reshape(1, -1), lam.reshape(1, -1))


def _block_diag_tiles(w):
    per = LANES // LRU_BLOCK_DIM
    nt = LRU_WIDTH // LANES
    w5 = w.reshape(nt, per, LRU_BLOCK_DIM, 1, LRU_BLOCK_DIM)
    eye = jnp.eye(per, dtype=w.dtype).reshape(1, per, 1, per, 1)
    return (w5 * eye).reshape(nt, LANES, LANES)


def _first_index_of_max(vals, lane):
    m = jnp.max(vals, axis=-1, keepdims=True)
    idx = jnp.min(jnp.where(vals == m, lane, 4 * LANES), axis=-1, keepdims=True)
    return m, idx


def _weight_pieces(out, w, lane, first):
    rest = w
    for k in range(W_PIECES):
        piece = rest.astype(BF16).astype(F32)
        out = jnp.where(lane == first + k, piece, out)
        rest = rest - piece
    return out


def _merge_kernel(attn_ref, lru_ref, g_ref, x_ref, mod_ref, wpa_ref, wpb_ref, wo_ref, nw_ref, wr_ref,
                  br_ref, x1_ref, hsl_ref, route_ref, gc_ref):
    D = x_ref.shape[1]
    tm = x_ref.shape[0]

    y_a = _dot(attn_ref[...].astype(BF16), wpa_ref[...])
    y_b = _dot(lru_ref[...].astype(BF16), wpb_ref[...])
    merged = g_ref[:, :D].astype(F32) * y_a + g_ref[:, D:].astype(F32) * y_b
    x1 = x_ref[...] + mod_ref[0, 2:3, :] * _dot(merged.astype(BF16), wo_ref[...])
    x1_ref[...] = x1
    h2 = _rms_mod(x1, nw_ref[...], mod_ref[0, 3:4, :], mod_ref[0, 4:5, :])
    h2b = h2.astype(BF16)

    h2l = (h2 - h2b.astype(F32)).astype(BF16)
    hi_hi_lo = _dot(h2b, wr_ref[...])
    logits = hi_hi_lo[:, :LANES] + (_dot(h2l, wr_ref[:, :LANES]) + hi_hi_lo[:, LANES:]) + br_ref[...]
    lane = lax.broadcasted_iota(I32, logits.shape, 1)
    gl = jnp.where(lane < N_GROUPS, logits, -jnp.inf)
    gmax, g_sel = _first_index_of_max(gl, lane)
    g_weight = 1.0 / jnp.sum(jnp.exp(gl - gmax), axis=-1, keepdims=True)
    e_lo = N_GROUPS + g_sel * EXPERTS_PER_GROUP
    el = jnp.where((lane >= e_lo) & (lane < e_lo + EXPERTS_PER_GROUP), logits, -jnp.inf)
    v1, i1 = _first_index_of_max(el, lane)
    el2 = jnp.where(lane == i1, -jnp.inf, el)
    v2, i2 = _first_index_of_max(el2, lane)
    e2x = jnp.exp(v2 - v1)
    w1 = g_weight / (1.0 + e2x)
    w2 = g_weight * e2x / (1.0 + e2x)

    hot1 = lane == i1
    hot2 = lane == i2
    hot = jnp.where(hot1, 1.0, jnp.where(hot2, 1.0, 0.0))
    gcnt = jnp.floor((jnp.sum(hot, axis=0, keepdims=True) + (GRAN - 1.0)) * (1.0 / GRAN))
    gc_ref[0] = gcnt
    rr = lax.broadcasted_iota(I32, (LANES, LANES), 0)
    cc = lax.broadcasted_iota(I32, (LANES, LANES), 1)
    upper = jnp.where(rr < cc, 1.0, 0.0).astype(BF16)
    run_start = GRAN * _dot(jnp.broadcast_to(gcnt, (8, LANES)).astype(BF16), upper)[0:1, :]
    rr = lax.broadcasted_iota(I32, (tm, tm), 0)
    cc = lax.broadcasted_iota(I32, (tm, tm), 1)
    lower = jnp.where(cc < rr, 1.0, 0.0).astype(BF16)
    pos = _dot(lower, hot.astype(BF16)) + run_start
    pos1 = jnp.where(hot1, pos, 0.0)
    pos2 = jnp.where(hot2, pos, 0.0)
    lp1 = jnp.sum(pos1, axis=-1, keepdims=True)
    lp2 = jnp.sum(pos2, axis=-1, keepdims=True)
    route_ref[...] = jnp.where(lane == R_LP1, lp1, jnp.where(lane == R_LP2, lp2, 0.0))

    aux = jnp.where(lane == AUX_E1, (i1 - N_GROUPS).astype(F32), 0.0)
    aux = _weight_pieces(_weight_pieces(aux, w1, lane, AUX_W1), w2, lane, AUX_W2)
    lp1r = jnp.sum(pos1.T, axis=0, keepdims=True).astype(I32)
    lp2r = jnp.sum(pos2.T, axis=0, keepdims=True).astype(I32)
    rows = lax.broadcasted_iota(I32, (LOCAL_ROWS, tm), 0)
    sort = jnp.where(rows == lp1r, 1.0, jnp.where(rows == lp2r, 1.0, 0.0)).astype(BF16)
    hsl_ref[...] = _dot(sort, jnp.concatenate([h2b, aux.astype(BF16)], axis=1)).astype(BF16)


def _merge(attn, lru_out, gates, x2, mod3, wpa, wpb, wo, norm_w, wr_hi_lo, b_router, S):
    N, D = x2.shape
    tm = TM_SORT
    n_tiles = N // tm
    steps_per_batch = S // tm
    const = lambda t: (0, 0)
    rows = lambda t: (t, 0)
    return pl.pallas_call(
        _merge_kernel,
        grid=(n_tiles,),
        in_specs=[pl.BlockSpec((tm, ATTN_OUT), rows),
                  pl.BlockSpec((tm, LRU_WIDTH), rows),
                  pl.BlockSpec((tm, 2 * D), rows),
                  pl.BlockSpec((tm, D), rows),
                  pl.BlockSpec((1, 6, D), lambda t: (t // steps_per_batch, 0, 0)),
                  pl.BlockSpec(wpa.shape, const),
                  pl.BlockSpec(wpb.shape, const),
                  pl.BlockSpec(wo.shape, const),
                  pl.BlockSpec((1, D), const),
                  pl.BlockSpec(wr_hi_lo.shape, const),
                  pl.BlockSpec((1, LANES), const)],
        out_specs=[pl.BlockSpec((tm, D), rows),
                   pl.BlockSpec((LOCAL_ROWS, D + AUX_W), rows),
                   pl.BlockSpec((tm, LANES), rows),
                   pl.BlockSpec((1, 1, LANES), lambda t: (t, 0, 0))],
        out_shape=[jax.ShapeDtypeStruct((N, D), F32),
                   jax.ShapeDtypeStruct((n_tiles * LOCAL_ROWS, D + AUX_W), BF16),
                   jax.ShapeDtypeStruct((N, LANES), F32),
                   jax.ShapeDtypeStruct((n_tiles, 1, LANES), F32)],
        compiler_params=_params("arbitrary"),
        name="merge",
    )(attn, lru_out, gates, x2, mod3, wpa, wpb, wo, norm_w.reshape(1, D), wr_hi_lo, b_router)


def _plan_sizes(n_tiles):
    max_gran = n_tiles * LOCAL_GRAN + N_EXPERTS * (GRAN_PER_TILE - 1)
    max_row_tiles = -(-max_gran // GRAN_PER_TILE)
    return max_row_tiles * GRAN_PER_TILE, max_row_tiles


def _plan_kernel(gc_ref, src_ref, texp_ref, nt_ref, dst_ref, lcnt_ref):
    gc = gc_ref[...]
    T = gc.shape[0]
    row = lax.broadcasted_iota(I32, (T, LANES), 0)
    local = lax.broadcasted_iota(I32, (T, LANES), 1).astype(F32)
    rr = lax.broadcasted_iota(I32, (LANES, LANES), 0)
    cc = lax.broadcasted_iota(I32, (LANES, LANES), 1)
    upper = jnp.where(rr < cc, 1.0, 0.0).astype(BF16)

    tiles = jnp.floor((jnp.sum(gc, axis=0, keepdims=True) + (GRAN_PER_TILE - 1.0)) * (1.0 / GRAN_PER_TILE))
    tile0 = _dot(jnp.broadcast_to(tiles, (SUBLANES, LANES)).astype(BF16), upper)[0:1, :]
    tile_end = tile0 + tiles
    before = gc
    d = 1
    while d < T:
        before = before + _shift_rows(before, d, 0.0, row)
        d *= 2
    before = before - gc
    run_global = GRAN_PER_TILE * tile0 + before
    run_local = _dot(gc.astype(BF16), upper)
    lcnt_ref[...] = jnp.broadcast_to(jnp.sum(gc, axis=1, keepdims=True), (T, LANES)).astype(I32)

    experts = range(N_GROUPS, N_GROUPS + N_EXPERTS)
    dst = jnp.zeros((T, LANES), F32)
    for e in experts:
        l0 = run_local[:, e:e + 1]
        inside = (local >= l0) & (local < l0 + gc[:, e:e + 1])
        dst = jnp.where(inside, run_global[:, e:e + 1] + (local - l0), dst)
    dst_ref[...] = dst.astype(I32)

    shape = src_ref.shape
    slot = (lax.broadcasted_iota(I32, shape, 0) * LANES + lax.broadcasted_iota(I32, shape, 1)).astype(F32)
    src = jnp.zeros(shape, F32)
    for e in experts:
        off = slot - GRAN_PER_TILE * tile0[0:1, e:e + 1]
        for t in range(T):
            p = before[t:t + 1, e:e + 1]
            hit = (off >= p) & (off < p + gc[t:t + 1, e:e + 1])
            src = jnp.where(hit, (t * LOCAL_GRAN) + run_local[t:t + 1, e:e + 1] + (off - p), src)
    src_ref[...] = src.astype(I32)

    shape = texp_ref.shape
    tile = (lax.broadcasted_iota(I32, shape, 0) * LANES + lax.broadcasted_iota(I32, shape, 1)).astype(F32)
    owner = jnp.zeros(shape, F32)
    for e in experts:
        owner = owner + jnp.where(tile >= tile_end[0:1, e:e + 1], 1.0, 0.0)
    texp_ref[...] = jnp.minimum(owner, N_EXPERTS - 1.0).astype(I32)
    nt_ref[...] = jnp.broadcast_to(tile_end[0:1, N_GROUPS + N_EXPERTS - 1:N_GROUPS + N_EXPERTS],
                                   nt_ref.shape).astype(I32)


def _plan(gc):
    n_tiles = gc.shape[0]
    max_gran, max_row_tiles = _plan_sizes(n_tiles)
    gran_rows = -(-max_gran // LANES)
    tile_rows = -(-max_row_tiles // LANES)
    src, texp, used, dst, lcnt = pl.pallas_call(
        _plan_kernel,
        out_shape=[jax.ShapeDtypeStruct((gran_rows, LANES), I32),
                   jax.ShapeDtypeStruct((tile_rows, LANES), I32),
                   jax.ShapeDtypeStruct((1, LANES), I32),
                   jax.ShapeDtypeStruct((n_tiles, LANES), I32),
                   jax.ShapeDtypeStruct((n_tiles, LANES), I32)],
        name="plan",
    )(gc)
    return (src.reshape(-1), texp.reshape(-1)[:max_row_tiles], used[0, :1],
            dst[:, :LOCAL_GRAN].reshape(-1), lcnt[:, 0])


def _granule_copies(idx_ref, first, count, max_count, src_hbm, dst_vmem, sem, wait):
    def one(k):
        g = idx_ref[first + k]
        cp = pltpu.make_async_copy(src_hbm.at[pl.ds(pl.multiple_of(g * GRAN, GRAN), GRAN)],
                                   dst_vmem.at[k * GRAN:(k + 1) * GRAN], sem)
        if wait:
            cp.wait()
        else:
            cp.start()

    for b0 in range(0, max_count, GATHER_UNROLL):
        def block(b0=b0):
            for k in range(b0, min(b0 + GATHER_UNROLL, max_count)):
                one(k)
        if isinstance(count, int):
            if b0 < count:
                block()
        else:
            pl.when(b0 < count)(block)


def _moe_kernel(src_ref, texp_ref, nt_ref, hsl_ref, w1_ref, w3_ref, w2_ref, ys_ref, xbuf, w1b, w3b, w2b, sem):
    i = pl.program_id(0)
    used = nt_ref[0]
    slot = i % MOE_BUFFERS
    D = ys_ref.shape[1]

    def gather(tile, wait):
        slot_ = tile % MOE_BUFFERS
        _granule_copies(src_ref, tile * GRAN_PER_TILE, GRAN_PER_TILE, GRAN_PER_TILE, hsl_ref, xbuf.at[slot_],
                        sem.at[slot_], wait)

    @pl.when(i == 0)
    def _():
        for ahead in range(MOE_BUFFERS - 1):
            pl.when(ahead < used)(functools.partial(gather, ahead, False))

    def row_tile(prefetch):
        gather(i, True)

        @pl.when((i == 0) | (texp_ref[i] != texp_ref[jnp.maximum(i - 1, 0)]))
        def _():
            w1b[...] = w1_ref[0].astype(BF16)
            w3b[...] = w3_ref[0].astype(BF16)
            w2b[...] = w2_ref[0].astype(BF16)

        if prefetch:
            gather(i + (MOE_BUFFERS - 1), False)
        xb = xbuf[slot]
        hb = xb[:, :D]
        aux = xb[:, D:].astype(F32)
        first = aux[:, AUX_E1:AUX_E1 + 1] == texp_ref[i].astype(F32)
        w_row = jnp.zeros((TM_MOE, 1), F32)
        for k in range(W_PIECES):
            w_row = w_row + jnp.where(first, aux[:, AUX_W1 + k:AUX_W1 + k + 1], aux[:, AUX_W2 + k:AUX_W2 + k + 1])
        a = _dot(hb, w1b[...])
        act = (a * _sigmoid(a)) * _dot(hb, w3b[...])
        ys_ref[...] = (w_row * _dot(act.astype(BF16), w2b[...])).astype(BF16)

    more = i + (MOE_BUFFERS - 1) < used
    pl.when(more)(functools.partial(row_tile, True))
    pl.when((i < used) & jnp.logical_not(more))(functools.partial(row_tile, False))

    @pl.when(i >= used)
    def _():
        ys_ref[...] = jnp.zeros_like(ys_ref)


def _moe(src, texp, used, hsl, w1, w3, w2):
    max_row_tiles = texp.shape[0]
    D = w1.shape[1]
    F = w1.shape[2]
    grid_spec = pltpu.PrefetchScalarGridSpec(
        num_scalar_prefetch=3,
        grid=(max_row_tiles,),
        in_specs=[pl.BlockSpec(memory_space=pl.ANY),
                  pl.BlockSpec((1, D, F), lambda i, s, te, nt: (te[i], 0, 0)),
                  pl.BlockSpec((1, D, F), lambda i, s, te, nt: (te[i], 0, 0)),
                  pl.BlockSpec((1, F, D), lambda i, s, te, nt: (te[i], 0, 0))],
        out_specs=pl.BlockSpec((TM_MOE, D), lambda i, s, te, nt: (i, 0)),
        scratch_shapes=[pltpu.VMEM((MOE_BUFFERS, TM_MOE, D + AUX_W), BF16),
                        pltpu.VMEM((D, F), BF16), pltpu.VMEM((D, F), BF16), pltpu.VMEM((F, D), BF16),
                        pltpu.SemaphoreType.DMA((MOE_BUFFERS,))],
    )
    return pl.pallas_call(
        _moe_kernel,
        grid_spec=grid_spec,
        out_shape=jax.ShapeDtypeStruct((max_row_tiles * TM_MOE, D), BF16),
        compiler_params=_params("arbitrary"),
        name="moe",
    )(src, texp, used, hsl, w1, w3, w2)


def _combine_kernel(dst_ref, lcnt_ref, ys_ref, x1_ref, route_ref, mod_ref, o_ref, ybuf, sem):
    t = pl.program_id(0)
    slot = t % 2
    tm = x1_ref.shape[0]

    def gather(tile, slot_, wait):
        _granule_copies(dst_ref, tile * LOCAL_GRAN, lcnt_ref[tile], LOCAL_GRAN, ys_ref, ybuf.at[slot_],
                        sem.at[slot_], wait)

    @pl.when(t == 0)
    def _():
        ybuf[...] = jnp.zeros_like(ybuf)
        gather(0, 0, False)

    @pl.when(t + 1 < pl.num_programs(0))
    def _():
        gather(t + 1, 1 - slot, False)

    gather(t, slot, True)
    lp1 = route_ref[:, R_LP1:R_LP1 + 1].astype(I32)
    lp2 = route_ref[:, R_LP2:R_LP2 + 1].astype(I32)
    col = lax.broadcasted_iota(I32, (tm, LOCAL_ROWS), 1)
    unsort = jnp.where(col == lp1, 1.0, jnp.where(col == lp2, 1.0, 0.0)).astype(BF16)
    o_ref[...] = x1_ref[...] + mod_ref[0, 5:6, :] * _dot(unsort, ybuf[slot])


def _combine(dst, lcnt, ys, x1, route, mod3, S):
    N, D = x1.shape
    tm = TM_SORT
    steps_per_batch = S // tm
    grid_spec = pltpu.PrefetchScalarGridSpec(
        num_scalar_prefetch=2,
        grid=(N // tm,),
        in_specs=[pl.BlockSpec(memory_space=pl.ANY),
                  pl.BlockSpec((tm, D), lambda t, d, c: (t, 0)),
                  pl.BlockSpec((tm, LANES), lambda t, d, c: (t, 0)),
                  pl.BlockSpec((1, 6, D), lambda t, d, c: (t // steps_per_batch, 0, 0))],
        out_specs=pl.BlockSpec((tm, D), lambda t, d, c: (t, 0)),
        scratch_shapes=[pltpu.VMEM((2, LOCAL_ROWS, D), BF16), pltpu.SemaphoreType.DMA((2,))],
    )
    return pl.pallas_call(
        _combine_kernel,
        grid_spec=grid_spec,
        out_shape=jax.ShapeDtypeStruct((N, D), F32),
        compiler_params=_params("arbitrary"),
        name="combine",
    )(dst, lcnt, ys, x1, route, mod3)


def _layer(x2, mod3, B, S, norm_mix_w, w_in, q_norm_w, k_norm_w, conv_w, conv_b, w_rec_gate, b_rec_gate,
           w_in_gate, b_in_gate, lru_lambda, w_proj_a, w_proj_b, w_out, norm_ffn_w, w_group, b_group,
           w_expert_router, b_expert_router, w1, w3, w2):
    N, D = x2.shape
    qa, kv, lru, gates = _proj(x2, mod3, norm_mix_w, _wprep(w_in), S)
    attn = _attn(qa, kv, q_norm_w, k_norm_w, B, S)
    lru_out = _lru(lru, conv_w, conv_b, _block_diag_tiles(w_rec_gate).astype(BF16), b_rec_gate,
                   _block_diag_tiles(w_in_gate).astype(BF16), b_in_gate, lru_lambda, B, S)

    n_r = N_GROUPS + N_EXPERTS
    w_router = jnp.concatenate([w_group, w_expert_router, jnp.zeros((D, LANES - n_r), F32)], axis=1)
    wr_hi = w_router.astype(BF16)
    wr_lo = (w_router - wr_hi.astype(F32)).astype(BF16)
    b_router = jnp.concatenate([b_group, b_expert_router, jnp.zeros((LANES - n_r,), F32)]).reshape(1, LANES)
    x1, hsl, route, gc = _merge(attn, lru_out, gates, x2, mod3, w_proj_a.astype(BF16), w_proj_b.astype(BF16),
                                w_out.astype(BF16), norm_ffn_w, jnp.concatenate([wr_hi, wr_lo], axis=1),
                                b_router, S)

    src, texp, used, dst, lcnt = _plan(gc.reshape(gc.shape[0], LANES))
    ys = _moe(src, texp, used, hsl, w1, w3, w2)
    return _combine(dst, lcnt, ys, x1, route, mod3, S)


def kernel(x, c, ada_w, ada_b, norm_mix_w, w_in, q_norm_w, k_norm_w, conv_w, conv_b, w_rec_gate, b_rec_gate,
           w_in_gate, b_in_gate, lru_lambda, w_proj_a, w_proj_b, w_out, norm_ffn_w, w_group, b_group,
           w_expert_router, b_expert_router, w1, w3, w2):
    B, S, D = x.shape
    x2 = x.reshape(B * S, D)
    for l in range(ada_w.shape[0]):
        mod3 = _ada(c, ada_w[l], ada_b[l]).reshape(B, 6, D)
        x2 = _layer(x2, mod3, B, S, norm_mix_w[l], w_in[l:l + 1], q_norm_w[l], k_norm_w[l], conv_w[l], conv_b[l],
                    w_rec_gate[l], b_rec_gate[l], w_in_gate[l], b_in_gate[l], lru_lambda[l], w_proj_a[l],
                    w_proj_b[l], w_out[l], norm_ffn_w[l], w_group[l], b_group[l], w_expert_router[l],
                    b_expert_router[l], w1[l], w3[l], w2[l])
    return x2.reshape(B, S, D)
```

```python
import functools

import jax
import jax.numpy as jnp
import numpy as np
from jax import lax
from jax.experimental import pallas as pl
from jax.experimental.pallas import tpu as pltpu

F32 = jnp.float32
BF16 = jnp.bfloat16
I32 = jnp.int32

EPS = 1e-6
CHUNK = 64
Q_BLOCK = 128
N_HEADS_A = 8
HEAD_DIM_A = 64
KV_DIM_A = 64
ATTN_OUT = N_HEADS_A * KV_DIM_A
N_IDX_HEADS = 4
IDX_DIM = 64
TOPK_MAX = 256
LRU_WIDTH = 512
LRU_BLOCKS = 8
LRU_BLOCK_DIM = LRU_WIDTH // LRU_BLOCKS
CONV_WIDTH = 4
LRU_C = 8.0
N_GROUPS = 4
EXPERTS_PER_GROUP = 8
N_EXPERTS = N_GROUPS * EXPERTS_PER_GROUP
D_FF_EXPERT = 256

LANES = 128
INT_MIN = -2 ** 31
VMEM_LIMIT = 56 * 1024 * 1024

D_MODEL = 1024
COL_Q = 0
COL_K = COL_Q + N_HEADS_A * HEAD_DIM_A
COL_V = COL_K + KV_DIM_A
COL_QI = COL_V + KV_DIM_A
COL_KI = COL_QI + N_IDX_HEADS * IDX_DIM
COL_WI = COL_KI + IDX_DIM
COL_LX = COL_WI + N_IDX_HEADS
D_IN = COL_LX + 2 * LRU_WIDTH + 2 * D_MODEL
ATTN_COLS = -(-COL_LX // LANES) * LANES
TAIL_COLS = D_IN - COL_LX
TAIL_SHIFT = COL_LX % LANES
W_IN_PAD = (COL_LX // LANES) * LANES + TAIL_COLS + LANES
assert COL_K % LANES == 0 and COL_QI % LANES == 0 and COL_KI % LANES == 0 and TAIL_COLS % LANES == 0

QA_W = N_HEADS_A * HEAD_DIM_A + N_IDX_HEADS * IDX_DIM
KV_W = (COL_QI - COL_K) + (ATTN_COLS - COL_KI)
K_OFF, V_OFF = 0, KV_DIM_A
KI_OFF = COL_QI - COL_K
WI_OFF = KI_OFF + IDX_DIM

TM_PROJ = 512

TM_SORT = 512
GRAN = 16
TM_MOE = 512
GRAN_PER_TILE = TM_MOE // GRAN
MOE_BUFFERS = 3
GATHER_UNROLL = 8
LOCAL_ROWS = -(-(2 * TM_SORT + N_EXPERTS * (GRAN - 1)) // LANES) * LANES
LOCAL_GRAN = LOCAL_ROWS // GRAN
AUX_W = LANES
W_PIECES = 3
AUX_W1, AUX_W2 = 0, W_PIECES
AUX_E1 = 2 * W_PIECES

R_LP1, R_LP2 = 0, 1


def _dot(a, b, **kw):
    return jnp.dot(a, b, preferred_element_type=F32, **kw)


def _params(*sem):
    return pltpu.CompilerParams(dimension_semantics=sem, vmem_limit_bytes=VMEM_LIMIT)


def _ada_kernel(c_ref, w_ref, b_ref, o_ref):
    c = c_ref[...]
    cond = c * jax.nn.sigmoid(c)
    w = w_ref[...]
    c_hi = cond.astype(BF16)
    c_lo = (cond - c_hi.astype(F32)).astype(BF16)
    w_hi = w.astype(BF16)
    w_lo = (w - w_hi.astype(F32)).astype(BF16)
    o_ref[...] = _dot(c_hi, w_hi) + (_dot(c_lo, w_hi) + _dot(c_hi, w_lo)) + b_ref[...]


def _ada(c, ada_w, ada_b):
    B, D = c.shape
    n_out = ada_w.shape[1]
    tn = 1024
    return pl.pallas_call(
        _ada_kernel,
        grid=(n_out // tn,),
        in_specs=[pl.BlockSpec((B, D), lambda j: (0, 0)),
                  pl.BlockSpec((D, tn), lambda j: (0, j)),
                  pl.BlockSpec((1, tn), lambda j: (0, j))],
        out_specs=pl.BlockSpec((B, tn), lambda j: (0, j)),
        out_shape=jax.ShapeDtypeStruct((B, n_out), F32),
        compiler_params=_params("arbitrary"),
        name="ada",
    )(c, ada_w, ada_b.reshape(1, n_out))


W_PREP_ROWS = 512


def _wprep_kernel(wt_ref, o_ref, stage, sem):
    row = lax.broadcasted_iota(I32, (W_PREP_ROWS, stage.shape[3]), 0)
    starts = list(range(0, W_IN_PAD, W_PREP_ROWS))

    def copy(j):
        r0 = starts[j]
        n = min(W_PREP_ROWS, D_IN - r0)
        return pltpu.make_async_copy(wt_ref.at[r0:r0 + n], stage.at[j % 2, 0:n], sem.at[j % 2]), n

    copy(0)[0].start()
    for j, r0 in enumerate(starts):
        if j + 1 < len(starts):
            copy(j + 1)[0].start()
        cp, n = copy(j)
        cp.wait()
        blk = stage[j % 2, :, 0, :]
        if n < W_PREP_ROWS:
            blk = jnp.where(row < n, blk, 0.0)
        o_ref[:, r0:r0 + W_PREP_ROWS] = blk.T.astype(BF16)


def _wprep(w_in):
    D = w_in.shape[1]
    return pl.pallas_call(
        _wprep_kernel,
        in_specs=[pl.BlockSpec(memory_space=pl.ANY)],
        out_shape=jax.ShapeDtypeStruct((D, W_IN_PAD), BF16),
        scratch_shapes=[pltpu.VMEM((2, W_PREP_ROWS, 1, D), F32), pltpu.SemaphoreType.DMA((2,))],
        compiler_params=pltpu.CompilerParams(vmem_limit_bytes=VMEM_LIMIT),
        name="wprep",
    )(jnp.transpose(w_in, (2, 0, 1)))


def _rms_mod(x, w, shift, scale):
    ms = jnp.mean(x * x, axis=-1, keepdims=True)
    y = x * lax.rsqrt(ms + EPS) * w
    return y * (1.0 + scale) + shift


def _proj_kernel(x_ref, mod_ref, nw_ref, w_ref, oqa_ref, okv_ref, olru_ref, og_ref, wt_s):
    @pl.when(pl.program_id(0) == 0)
    def _():
        rr = lax.broadcasted_iota(I32, (2 * LANES, LANES), 0)
        cc = lax.broadcasted_iota(I32, (2 * LANES, LANES), 1)
        shift = jnp.where(rr == cc + TAIL_SHIFT, 1.0, 0.0).astype(BF16)
        for j in range(TAIL_COLS // LANES):
            src = (COL_LX // LANES + j) * LANES
            wt_s[:, j * LANES:(j + 1) * LANES] = _dot(w_ref[:, src:src + 2 * LANES], shift).astype(BF16)

    h = _rms_mod(x_ref[...], nw_ref[...], mod_ref[0, 0:1, :], mod_ref[0, 1:2, :])
    hb = h.astype(BF16)
    pa = _dot(hb, w_ref[:, :ATTN_COLS])
    oqa_ref[...] = jnp.concatenate([pa[:, COL_Q:COL_K], pa[:, COL_QI:COL_KI]], axis=1)
    okv_ref[...] = jnp.concatenate([pa[:, COL_K:COL_QI], pa[:, COL_KI:ATTN_COLS]], axis=1)
    olru_ref[...] = _dot(hb, wt_s[:, :2 * LRU_WIDTH])
    og_ref[...] = _sigmoid(_dot(hb, wt_s[:, 2 * LRU_WIDTH:])).astype(BF16)


def _proj(x2, mod3, norm_w, w_all, S):
    N, D = x2.shape
    tm = TM_PROJ
    steps_per_batch = S // tm
    const = lambda t: (0, 0)
    return pl.pallas_call(
        _proj_kernel,
        grid=(N // tm,),
        in_specs=[pl.BlockSpec((tm, D), lambda t: (t, 0)),
                  pl.BlockSpec((1, 6, D), lambda t: (t // steps_per_batch, 0, 0)),
                  pl.BlockSpec((1, D), const),
                  pl.BlockSpec(w_all.shape, const)],
        out_specs=[pl.BlockSpec((tm, QA_W), lambda t: (t, 0)),
                   pl.BlockSpec((tm, KV_W), lambda t: (t, 0)),
                   pl.BlockSpec((tm, 2 * LRU_WIDTH), lambda t: (t, 0)),
                   pl.BlockSpec((tm, 2 * D), lambda t: (t, 0))],
        out_shape=[jax.ShapeDtypeStruct((N, QA_W), F32),
                   jax.ShapeDtypeStruct((N, KV_W), F32),
                   jax.ShapeDtypeStruct((N, 2 * LRU_WIDTH), F32),
                   jax.ShapeDtypeStruct((N, 2 * D), BF16)],
        scratch_shapes=[pltpu.VMEM((D, TAIL_COLS), BF16)],
        compiler_params=_params("arbitrary"),
        name="proj",
    )(x2, mod3, norm_w.reshape(1, D), w_all)


ATTN_BUCKETS = 8
ROW_FOLD = 64
KEY_ROWS = 256
COUNT_CHAINS = 2
LOG2E = float(np.log2(np.e))
POS_RADIX = 256
SLOPE_PIECES = 3
V_ONES = 16


def _bf16_pieces(c, n):
    out = []
    rest = float(c)
    for _ in range(n):
        p = float(np.asarray(rest, np.float32).astype(BF16).astype(np.float32))
        out.append(p)
        rest -= p
    return out


def _reduce_rows(op, x):
    r = x.shape[0]
    if r > ROW_FOLD and r % ROW_FOLD == 0:
        x = op(x.reshape(r // ROW_FOLD, ROW_FOLD, x.shape[1]), axis=0)
    return op(x, axis=0, keepdims=True)


def _attn_block(s_eff, topk, i, qa_ref, kvq_ref, qnw_ref, o_ref, kn_s, vt_s, ki_s, key_s, bias_s, lg_s, ot_s):
    nq = N_HEADS_A * HEAD_DIM_A
    q_t = qa_ref[:, :nq].T
    qi_t = qa_ref[:, nq:].T
    wq_t = kvq_ref[...].T[WI_OFF:WI_OFF + N_IDX_HEADS, :] * (N_IDX_HEADS ** -0.5 * IDX_DIM ** -0.5)

    q_pos = i * Q_BLOCK + lax.broadcasted_iota(I32, (1, Q_BLOCK), 1)
    limit = (q_pos // CHUNK + 1) * CHUNK
    chunks = [slice(r0, min(r0 + KEY_ROWS, s_eff)) for r0 in range(0, s_eff, KEY_ROWS)]

    qi = [qi_t[h * IDX_DIM:(h + 1) * IDX_DIM, :].astype(BF16) for h in range(N_IDX_HEADS)]
    for rows in chunks:
        ki = ki_s[rows, :]
        score = jnp.zeros((rows.stop - rows.start, Q_BLOCK), F32)
        for h in range(N_IDX_HEADS):
            score = score + jnp.maximum(_dot(ki, qi[h]), 0.0) * wq_t[h:h + 1, :]
        k_pos = rows.start + lax.broadcasted_iota(I32, score.shape, 0)
        key_s[rows, :] = jnp.where(k_pos < limit, score, -jnp.inf)

    def count(mask):
        return _reduce_rows(jnp.sum, jnp.where(mask, 1.0, 0.0))

    def count_keys(test):
        acc = [jnp.zeros((ROW_FOLD, Q_BLOCK), F32) for _ in range(COUNT_CHAINS)]
        for n, r0 in enumerate(range(0, s_eff, ROW_FOLD)):
            hit = jnp.where(test(key_s[r0:r0 + ROW_FOLD, :]), 1.0, 0.0)
            acc[n % COUNT_CHAINS] = acc[n % COUNT_CHAINS] + hit
        return jnp.sum(functools.reduce(lambda x, y: x + y, acc), axis=0, keepdims=True)

    def ordered_to_float(c):
        s = c ^ INT_MIN
        return pltpu.bitcast(jnp.where(s < 0, INT_MIN | (-s), s), F32)

    def bisect(it, r):
        c = r | jnp.left_shift(jnp.int32(1), 31 - it)
        cand = ordered_to_float(c)
        return jnp.where(count_keys(lambda slab: slab >= cand) >= topk, c, r)

    def select_by_threshold():
        thr = ordered_to_float(lax.fori_loop(0, 32, bisect, jnp.zeros((1, Q_BLOCK), I32)))

        n_ge = count_keys(lambda slab: slab >= thr)
        clean = jnp.max(jnp.where((n_ge == topk) & (thr > -jnp.inf), 0.0, 1.0)) == 0.0

        @pl.when(clean)
        def _():
            for rows in chunks:
                bias_s[rows, :] = jnp.where(key_s[rows, :] >= thr, 0.0, -jnp.inf)

        @pl.when(jnp.logical_not(clean))
        def _():
            key = key_s[:s_eff, :]
            need = topk - count(key > thr)
            rr = lax.broadcasted_iota(I32, (LANES, LANES), 0)
            cc = lax.broadcasted_iota(I32, (LANES, LANES), 1)
            tri = jnp.where(cc < rr, 1.0, 0.0).astype(BF16)
            seen = jnp.zeros((1, Q_BLOCK), F32)
            for c in range(s_eff // LANES):
                kc = key[c * LANES:(c + 1) * LANES, :]
                eq = kc == thr
                eqf = jnp.where(eq, 1.0, 0.0)
                rank = _dot(tri, eqf.astype(BF16)) + seen
                seen = seen + jnp.sum(eqf, axis=0, keepdims=True)
                tie_bias = jnp.where(jnp.where(eq, rank, topk) < need, 0.0, -jnp.inf)
                bias = jnp.where(kc > thr, 0.0, tie_bias)
                k_pos = c * LANES + lax.broadcasted_iota(I32, (LANES, Q_BLOCK), 0)
                bias_s[c * LANES:(c + 1) * LANES, :] = jnp.where(k_pos < limit, bias, -jnp.inf)

    if s_eff > topk:
        select_by_threshold()
    else:
        for rows in chunks:
            k_pos = rows.start + lax.broadcasted_iota(I32, (rows.stop - rows.start, Q_BLOCK), 0)
            bias_s[rows, :] = jnp.where(k_pos < limit, 0.0, -jnp.inf)

    qnw = qnw_ref[...]
    crow = lax.broadcasted_iota(I32, (LANES - HEAD_DIM_A, Q_BLOCK), 0)
    r_k = lax.broadcasted_iota(I32, (Q_BLOCK, Q_BLOCK), 0)
    c_q = lax.broadcasted_iota(I32, (Q_BLOCK, Q_BLOCK), 1)
    after = 2.0 * jnp.maximum(r_k - c_q, 0).astype(F32)
    band = pl.ds(pl.multiple_of(i * Q_BLOCK, Q_BLOCK), Q_BLOCK)
    slope = [sum(_bf16_pieces(LOG2E * 2.0 ** -(h + 1), SLOPE_PIECES)) for h in range(N_HEADS_A)]
    row_max = []
    for h in range(N_HEADS_A):
        pieces = _bf16_pieces(LOG2E * 2.0 ** -(h + 1), SLOPE_PIECES)
        qh = q_t[h * HEAD_DIM_A:(h + 1) * HEAD_DIM_A, :]
        ms = jnp.mean(qh * qh, axis=0, keepdims=True)
        qn = (qh * lax.rsqrt(ms + EPS) * qnw) * (HEAD_DIM_A ** -0.5) * LOG2E
        slopes = jnp.zeros(crow.shape, F32)
        for k, piece in enumerate(pieces):
            slopes = jnp.where(crow == 2 * k, POS_RADIX * piece, jnp.where(crow == 2 * k + 1, piece, slopes))
        qa = jnp.concatenate([qn, slopes], axis=0).astype(BF16)
        folded = []
        for rows in chunks:
            logits = _dot(kn_s[rows, :], qa) + bias_s[rows, :]
            lg_s[h, rows, :] = logits
            folded.append(jnp.max(logits.reshape(-1, ROW_FOLD, Q_BLOCK), axis=0))
        row_max.append(jnp.max(functools.reduce(jnp.maximum, folded), axis=0, keepdims=True))
    for h in range(N_HEADS_A):
        lg_s[h, band, :] = lg_s[h, band, :] - slope[h] * after
    for h in range(N_HEADS_A):
        acc = jnp.zeros((KV_DIM_A + V_ONES, Q_BLOCK), F32)
        for rows in chunks:
            p = jnp.exp2(lg_s[h, rows, :] - row_max[h])
            acc = acc + _dot(vt_s[:, rows], p.astype(BF16))
        ot_s[h * KV_DIM_A:(h + 1) * KV_DIM_A, :] = acc[:KV_DIM_A, :] / acc[KV_DIM_A:KV_DIM_A + 1, :]
    o_ref[...] = ot_s[...].T


def _attn_kernel(qa_ref, kvq_ref, kv_ref, qnw_ref, knw_ref, o_ref, kn_s, vt_s, ki_s, key_s, bias_s, lg_s, ot_s):
    i = pl.program_id(1)
    S = kv_ref.shape[0]
    nb = S // Q_BLOCK
    topk = float(min(TOPK_MAX, S // 4))

    @pl.when(i == 0)
    def _():
        kv = kv_ref[:, :LANES]
        lane = lax.broadcasted_iota(I32, kv.shape, 1)
        pos = lax.broadcasted_iota(I32, kv.shape, 0)
        ms = jnp.sum(jnp.where(lane < KV_DIM_A, kv * kv, 0.0), axis=-1, keepdims=True) * (1.0 / KV_DIM_A)
        kn = kv * lax.rsqrt(ms + EPS) * knw_ref[...]
        digit = jnp.where((lane & 1) == 0, pos // POS_RADIX, pos % POS_RADIX).astype(F32)
        is_digit = (lane >= KV_DIM_A) & (lane < KV_DIM_A + 2 * SLOPE_PIECES)
        kn_s[...] = jnp.where(is_digit, digit, kn).astype(BF16)
        vt_s[:KV_DIM_A, :] = kv.T[V_OFF:V_OFF + KV_DIM_A, :].astype(BF16)
        vt_s[KV_DIM_A:, :] = jnp.ones((V_ONES, kv.shape[0]), BF16)
        ki_s[...] = kv_ref[:, KI_OFF:KI_OFF + IDX_DIM].astype(BF16)

    n_buckets = min(ATTN_BUCKETS, nb)
    per = nb // n_buckets
    for j in range(n_buckets):
        pl.when(i // per == j)(functools.partial(
            _attn_block, (j + 1) * per * Q_BLOCK, topk, i, qa_ref, kvq_ref, qnw_ref, o_ref,
            kn_s, vt_s, ki_s, key_s, bias_s, lg_s, ot_s))


def _attn(qa, kv, q_norm_w, k_norm_w, B, S):
    N = qa.shape[0]
    nb = S // Q_BLOCK
    return pl.pallas_call(
        _attn_kernel,
        grid=(B, nb),
        in_specs=[pl.BlockSpec((Q_BLOCK, QA_W), lambda b, i: (b * nb + i, 0)),
                  pl.BlockSpec((Q_BLOCK, KV_W), lambda b, i: (b * nb + i, 0)),
                  pl.BlockSpec((S, KV_W), lambda b, i: (b, 0)),
                  pl.BlockSpec((HEAD_DIM_A, 1), lambda b, i: (0, 0)),
                  pl.BlockSpec((1, LANES), lambda b, i: (0, 0))],
        out_specs=pl.BlockSpec((Q_BLOCK, ATTN_OUT), lambda b, i: (b * nb + i, 0)),
        out_shape=jax.ShapeDtypeStruct((N, ATTN_OUT), F32),
        scratch_shapes=[pltpu.VMEM((S, LANES), BF16),
                        pltpu.VMEM((KV_DIM_A + V_ONES, S), BF16),
                        pltpu.VMEM((S, IDX_DIM), BF16),
                        pltpu.VMEM((S, Q_BLOCK), F32),
                        pltpu.VMEM((S, Q_BLOCK), F32),
                        pltpu.VMEM((N_HEADS_A, S, Q_BLOCK), F32),
                        pltpu.VMEM((ATTN_OUT, Q_BLOCK), F32)],
        compiler_params=_params("arbitrary", "arbitrary"),
        name="attn",
    )(qa, kv, kv, q_norm_w.reshape(HEAD_DIM_A, 1),
      jnp.pad(k_norm_w, (0, LANES - KV_DIM_A)).reshape(1, LANES))


SUBLANES = 8


def _shift_rows(x, d, fill, row):
    if d % SUBLANES == 0:
        return jnp.concatenate([jnp.full((d, x.shape[1]), fill, x.dtype), x[:-d]], axis=0)
    return jnp.where(row >= d, pltpu.roll(x, d, 0), fill)


def _sigmoid(x):
    return 0.5 * (1.0 + jnp.tanh(0.5 * x))


def _doubling_scan(a, b, row):
    n = a.shape[0]
    d = 1
    while d < n:
        b = a * _shift_rows(b, d, 0.0, row) + b
        if 2 * d < n:
            a = a * _shift_rows(a, d, 1.0, row)
        d *= 2
    return b


def _lru_kernel(xb_ref, gb_ref, cw_ref, cb_ref, wr_ref, br_ref, wi_ref, bi_ref, lam_ref, o_ref, a_s, b_s):
    x = xb_ref[...]
    S = x.shape[0]
    row = lax.broadcasted_iota(I32, x.shape, 0)
    xc = cb_ref[...] + _shift_rows(x, CONV_WIDTH - 1, 0.0, row) * cw_ref[0:1, :]
    for j in range(1, CONV_WIDTH):
        d = CONV_WIDTH - 1 - j
        xs = x if d == 0 else _shift_rows(x, d, 0.0, row)
        xc = xc + xs * cw_ref[j:j + 1, :]
    xcb = xc.astype(BF16)
    r = _sigmoid(_dot(xcb, wr_ref[0]) + br_ref[...])
    ig = _sigmoid(_dot(xcb, wi_ref[0]) + bi_ref[...])
    z = -lam_ref[...]
    softplus = jnp.maximum(z, 0.0) + jnp.log1p(jnp.exp(-jnp.abs(z)))
    log_a = -LRU_C * r * softplus
    a = jnp.exp(log_a)
    b = jnp.sqrt(-jnp.tanh(log_a) * (1.0 + a * a)) * (ig * xc)
    sub = row & (SUBLANES - 1)
    d = 1
    while d < SUBLANES:
        keep = sub >= d
        b = a * jnp.where(keep, pltpu.roll(b, d, 0), 0.0) + b
        a = a * jnp.where(keep, pltpu.roll(a, d, 0), 1.0)
        d *= 2
    a_s[...] = a
    b_s[...] = b
    groups = S // SUBLANES
    last = pl.ds(SUBLANES - 1, groups, stride=SUBLANES)
    g_row = lax.broadcasted_iota(I32, (groups, x.shape[1]), 0)
    h_last = _doubling_scan(a_s[last, :], b_s[last, :], g_row)
    carry = _shift_rows(h_last, 1, 0.0, g_row)
    carry = jnp.broadcast_to(carry[:, None, :], (groups, SUBLANES, x.shape[1])).reshape(x.shape)
    h = b_s[...] + a_s[...] * carry
    g = gb_ref[...]
    gelu = 0.5 * g * (1.0 + jnp.tanh(float(np.sqrt(2.0 / np.pi)) * (g + 0.044715 * (g * g * g))))
    o_ref[...] = h * gelu


def _lru(lru, conv_w, conv_b, wr_bd, b_rec, wi_bd, b_in, lam, B, S):
    N = lru.shape[0]
    nt = LRU_WIDTH // LANES
    vec = lambda b, j: (0, j)
    return pl.pallas_call(
        _lru_kernel,
        grid=(B, nt),
        in_specs=[pl.BlockSpec((S, LANES), lambda b, j: (b, j)),
                  pl.BlockSpec((S, LANES), lambda b, j: (b, nt + j)),
                  pl.BlockSpec((CONV_WIDTH, LANES), vec),
                  pl.BlockSpec((1, LANES), vec),
                  pl.BlockSpec((1, LANES, LANES), lambda b, j: (j, 0, 0)),
                  pl.BlockSpec((1, LANES), vec),
                  pl.BlockSpec((1, LANES, LANES), lambda b, j: (j, 0, 0)),
                  pl.BlockSpec((1, LANES), vec),
                  pl.BlockSpec((1, LANES), vec)],
        out_specs=pl.BlockSpec((S, LANES), lambda b, j: (b, j)),
        out_shape=jax.ShapeDtypeStruct((N, LRU_WIDTH), F32),
        scratch_shapes=[pltpu.VMEM((S, LANES), F32), pltpu.VMEM((S, LANES), F32)],
        compiler_params=_params("arbitrary", "arbitrary"),
        name="lru",
    )(lru, lru, conv_w, conv_b.reshape(1, -1), wr_bd, b_rec.reshape(1, -1), wi_bd,
      b_in.reshape(1, -1), lam.reshape(1, -1))


def _block_diag_tiles(w):
    per = LANES // LRU_BLOCK_DIM
    nt = LRU_WIDTH // LANES
    w5 = w.reshape(nt, per, LRU_BLOCK_DIM, 1, LRU_BLOCK_DIM)
    eye = jnp.eye(per, dtype=w.dtype).reshape(1, per, 1, per, 1)
    return (w5 * eye).reshape(nt, LANES, LANES)


def _first_index_of_max(vals, lane):
    m = jnp.max(vals, axis=-1, keepdims=True)
    idx = jnp.min(jnp.where(vals == m, lane, 4 * LANES), axis=-1, keepdims=True)
    return m, idx


def _weight_pieces(out, w, lane, first):
    rest = w
    for k in range(W_PIECES):
        piece = rest.astype(BF16).astype(F32)
        out = jnp.where(lane == first + k, piece, out)
        rest = rest - piece
    return out


def _merge_kernel(attn_ref, lru_ref, g_ref, x_ref, mod_ref, wpa_ref, wpb_ref, wo_ref, nw_ref, wr_ref,
                  br_ref, x1_ref, hsl_ref, route_ref, gc_ref):
    D = x_ref.shape[1]
    tm = x_ref.shape[0]

    y_a = _dot(attn_ref[...].astype(BF16), wpa_ref[...])
    y_b = _dot(lru_ref[...].astype(BF16), wpb_ref[...])
    merged = g_ref[:, :D].astype(F32) * y_a + g_ref[:, D:].astype(F32) * y_b
    x1 = x_ref[...] + mod_ref[0, 2:3, :] * _dot(merged.astype(BF16), wo_ref[...])
    x1_ref[...] = x1
    h2 = _rms_mod(x1, nw_ref[...], mod_ref[0, 3:4, :], mod_ref[0, 4:5, :])
    h2b = h2.astype(BF16)

    h2l = (h2 - h2b.astype(F32)).astype(BF16)
    hi_hi_lo = _dot(h2b, wr_ref[...])
    logits = hi_hi_lo[:, :LANES] + (_dot(h2l, wr_ref[:, :LANES]) + hi_hi_lo[:, LANES:]) + br_ref[...]
    lane = lax.broadcasted_iota(I32, logits.shape, 1)
    gl = jnp.where(lane < N_GROUPS, logits, -jnp.inf)
    gmax, g_sel = _first_index_of_max(gl, lane)
    g_weight = 1.0 / jnp.sum(jnp.exp(gl - gmax), axis=-1, keepdims=True)
    e_lo = N_GROUPS + g_sel * EXPERTS_PER_GROUP
    el = jnp.where((lane >= e_lo) & (lane < e_lo + EXPERTS_PER_GROUP), logits, -jnp.inf)
    v1, i1 = _first_index_of_max(el, lane)
    el2 = jnp.where(lane == i1, -jnp.inf, el)
    v2, i2 = _first_index_of_max(el2, lane)
    e2x = jnp.exp(v2 - v1)
    w1 = g_weight / (1.0 + e2x)
    w2 = g_weight * e2x / (1.0 + e2x)

    hot1 = lane == i1
    hot2 = lane == i2
    hot = jnp.where(hot1, 1.0, jnp.where(hot2, 1.0, 0.0))
    gcnt = jnp.floor((jnp.sum(hot, axis=0, keepdims=True) + (GRAN - 1.0)) * (1.0 / GRAN))
    gc_ref[0] = gcnt
    rr = lax.broadcasted_iota(I32, (LANES, LANES), 0)
    cc = lax.broadcasted_iota(I32, (LANES, LANES), 1)
    upper = jnp.where(rr < cc, 1.0, 0.0).astype(BF16)
    run_start = GRAN * _dot(jnp.broadcast_to(gcnt, (8, LANES)).astype(BF16), upper)[0:1, :]
    rr = lax.broadcasted_iota(I32, (tm, tm), 0)
    cc = lax.broadcasted_iota(I32, (tm, tm), 1)
    lower = jnp.where(cc < rr, 1.0, 0.0).astype(BF16)
    pos = _dot(lower, hot.astype(BF16)) + run_start
    pos1 = jnp.where(hot1, pos, 0.0)
    pos2 = jnp.where(hot2, pos, 0.0)
    lp1 = jnp.sum(pos1, axis=-1, keepdims=True)
    lp2 = jnp.sum(pos2, axis=-1, keepdims=True)
    route_ref[...] = jnp.where(lane == R_LP1, lp1, jnp.where(lane == R_LP2, lp2, 0.0))

    aux = jnp.where(lane == AUX_E1, (i1 - N_GROUPS).astype(F32), 0.0)
    aux = _weight_pieces(_weight_pieces(aux, w1, lane, AUX_W1), w2, lane, AUX_W2)
    lp1r = jnp.sum(pos1.T, axis=0, keepdims=True).astype(I32)
    lp2r = jnp.sum(pos2.T, axis=0, keepdims=True).astype(I32)
    rows = lax.broadcasted_iota(I32, (LOCAL_ROWS, tm), 0)
    sort = jnp.where(rows == lp1r, 1.0, jnp.where(rows == lp2r, 1.0, 0.0)).astype(BF16)
    hsl_ref[...] = _dot(sort, jnp.concatenate([h2b, aux.astype(BF16)], axis=1)).astype(BF16)


def _merge(attn, lru_out, gates, x2, mod3, wpa, wpb, wo, norm_w, wr_hi_lo, b_router, S):
    N, D = x2.shape
    tm = TM_SORT
    n_tiles = N // tm
    steps_per_batch = S // tm
    const = lambda t: (0, 0)
    rows = lambda t: (t, 0)
    return pl.pallas_call(
        _merge_kernel,
        grid=(n_tiles,),
        in_specs=[pl.BlockSpec((tm, ATTN_OUT), rows),
                  pl.BlockSpec((tm, LRU_WIDTH), rows),
                  pl.BlockSpec((tm, 2 * D), rows),
                  pl.BlockSpec((tm, D), rows),
                  pl.BlockSpec((1, 6, D), lambda t: (t // steps_per_batch, 0, 0)),
                  pl.BlockSpec(wpa.shape, const),
                  pl.BlockSpec(wpb.shape, const),
                  pl.BlockSpec(wo.shape, const),
                  pl.BlockSpec((1, D), const),
                  pl.BlockSpec(wr_hi_lo.shape, const),
                  pl.BlockSpec((1, LANES), const)],
        out_specs=[pl.BlockSpec((tm, D), rows),
                   pl.BlockSpec((LOCAL_ROWS, D + AUX_W), rows),
                   pl.BlockSpec((tm, LANES), rows),
                   pl.BlockSpec((1, 1, LANES), lambda t: (t, 0, 0))],
        out_shape=[jax.ShapeDtypeStruct((N, D), F32),
                   jax.ShapeDtypeStruct((n_tiles * LOCAL_ROWS, D + AUX_W), BF16),
                   jax.ShapeDtypeStruct((N, LANES), F32),
                   jax.ShapeDtypeStruct((n_tiles, 1, LANES), F32)],
        compiler_params=_params("arbitrary"),
        name="merge",
    )(attn, lru_out, gates, x2, mod3, wpa, wpb, wo, norm_w.reshape(1, D), wr_hi_lo, b_router)


def _plan_sizes(n_tiles):
    max_gran = n_tiles * LOCAL_GRAN + N_EXPERTS * (GRAN_PER_TILE - 1)
    max_row_tiles = -(-max_gran // GRAN_PER_TILE)
    return max_row_tiles * GRAN_PER_TILE, max_row_tiles


def _plan_kernel(gc_ref, src_ref, texp_ref, nt_ref, dst_ref, lcnt_ref):
    gc = gc_ref[...]
    T = gc.shape[0]
    row = lax.broadcasted_iota(I32, (T, LANES), 0)
    local = lax.broadcasted_iota(I32, (T, LANES), 1).astype(F32)
    rr = lax.broadcasted_iota(I32, (LANES, LANES), 0)
    cc = lax.broadcasted_iota(I32, (LANES, LANES), 1)
    upper = jnp.where(rr < cc, 1.0, 0.0).astype(BF16)

    tiles = jnp.floor((jnp.sum(gc, axis=0, keepdims=True) + (GRAN_PER_TILE - 1.0)) * (1.0 / GRAN_PER_TILE))
    tile0 = _dot(jnp.broadcast_to(tiles, (SUBLANES, LANES)).astype(BF16), upper)[0:1, :]
    tile_end = tile0 + tiles
    before = gc
    d = 1
    while d < T:
        before = before + _shift_rows(before, d, 0.0, row)
        d *= 2
    before = before - gc
    run_global = GRAN_PER_TILE * tile0 + before
    run_local = _dot(gc.astype(BF16), upper)
    lcnt_ref[...] = jnp.broadcast_to(jnp.sum(gc, axis=1, keepdims=True), (T, LANES)).astype(I32)

    experts = range(N_GROUPS, N_GROUPS + N_EXPERTS)
    dst = jnp.zeros((T, LANES), F32)
    for e in experts:
        l0 = run_local[:, e:e + 1]
        inside = (local >= l0) & (local < l0 + gc[:, e:e + 1])
        dst = jnp.where(inside, run_global[:, e:e + 1] + (local - l0), dst)
    dst_ref[...] = dst.astype(I32)

    shape = src_ref.shape
    slot = (lax.broadcasted_iota(I32, shape, 0) * LANES + lax.broadcasted_iota(I32, shape, 1)).astype(F32)
    src = jnp.zeros(shape, F32)
    for e in experts:
        off = slot - GRAN_PER_TILE * tile0[0:1, e:e + 1]
        for t in range(T):
            p = before[t:t + 1, e:e + 1]
            hit = (off >= p) & (off < p + gc[t:t + 1, e:e + 1])
            src = jnp.where(hit, (t * LOCAL_GRAN) + run_local[t:t + 1, e:e + 1] + (off - p), src)
    src_ref[...] = src.astype(I32)

    shape = texp_ref.shape
    tile = (lax.broadcasted_iota(I32, shape, 0) * LANES + lax.broadcasted_iota(I32, shape, 1)).astype(F32)
    owner = jnp.zeros(shape, F32)
    for e in experts:
        owner = owner + jnp.where(tile >= tile_end[0:1, e:e + 1], 1.0, 0.0)
    texp_ref[...] = jnp.minimum(owner, N_EXPERTS - 1.0).astype(I32)
    nt_ref[...] = jnp.broadcast_to(tile_end[0:1, N_GROUPS + N_EXPERTS - 1:N_GROUPS + N_EXPERTS],
                                   nt_ref.shape).astype(I32)


def _plan(gc):
    n_tiles = gc.shape[0]
    max_gran, max_row_tiles = _plan_sizes(n_tiles)
    gran_rows = -(-max_gran // LANES)
    tile_rows = -(-max_row_tiles // LANES)
    src, texp, used, dst, lcnt = pl.pallas_call(
        _plan_kernel,
        out_shape=[jax.ShapeDtypeStruct((gran_rows, LANES), I32),
                   jax.ShapeDtypeStruct((tile_rows, LANES), I32),
                   jax.ShapeDtypeStruct((1, LANES), I32),
                   jax.ShapeDtypeStruct((n_tiles, LANES), I32),
                   jax.ShapeDtypeStruct((n_tiles, LANES), I32)],
        name="plan",
    )(gc)
    return (src.reshape(-1), texp.reshape(-1)[:max_row_tiles], used[0, :1],
            dst[:, :LOCAL_GRAN].reshape(-1), lcnt[:, 0])


def _granule_copies(idx_ref, first, count, max_count, src_hbm, dst_vmem, sem, wait):
    def one(k):
        g = idx_ref[first + k]
        cp = pltpu.make_async_copy(src_hbm.at[pl.ds(pl.multiple_of(g * GRAN, GRAN), GRAN)],
                                   dst_vmem.at[k * GRAN:(k + 1) * GRAN], sem)
        if wait:
            cp.wait()
        else:
            cp.start()

    for b0 in range(0, max_count, GATHER_UNROLL):
        def block(b0=b0):
            for k in range(b0, min(b0 + GATHER_UNROLL, max_count)):
                one(k)
        if isinstance(count, int):
            if b0 < count:
                block()
        else:
            pl.when(b0 < count)(block)


def _moe_kernel(src_ref, texp_ref, nt_ref, hsl_ref, w1_ref, w3_ref, w2_ref, ys_ref, xbuf, w1b, w3b, w2b, sem):
    i = pl.program_id(0)
    used = nt_ref[0]
    slot = i % MOE_BUFFERS
    D = ys_ref.shape[1]

    def gather(tile, wait):
        slot_ = tile % MOE_BUFFERS
        _granule_copies(src_ref, tile * GRAN_PER_TILE, GRAN_PER_TILE, GRAN_PER_TILE, hsl_ref, xbuf.at[slot_],
                        sem.at[slot_], wait)

    @pl.when(i == 0)
    def _():
        for ahead in range(MOE_BUFFERS - 1):
            pl.when(ahead < used)(functools.partial(gather, ahead, False))

    def row_tile(prefetch):
        gather(i, True)

        @pl.when((i == 0) | (texp_ref[i] != texp_ref[jnp.maximum(i - 1, 0)]))
        def _():
            w1b[...] = w1_ref[0].astype(BF16)
            w3b[...] = w3_ref[0].astype(BF16)
            w2b[...] = w2_ref[0].astype(BF16)

        if prefetch:
            gather(i + (MOE_BUFFERS - 1), False)
        xb = xbuf[slot]
        hb = xb[:, :D]
        aux = xb[:, D:].astype(F32)
        first = aux[:, AUX_E1:AUX_E1 + 1] == texp_ref[i].astype(F32)
        w_row = jnp.zeros((TM_MOE, 1), F32)
        for k in range(W_PIECES):
            w_row = w_row + jnp.where(first, aux[:, AUX_W1 + k:AUX_W1 + k + 1], aux[:, AUX_W2 + k:AUX_W2 + k + 1])
        a = _dot(hb, w1b[...])
        act = (a * _sigmoid(a)) * _dot(hb, w3b[...])
        ys_ref[...] = (w_row * _dot(act.astype(BF16), w2b[...])).astype(BF16)

    more = i + (MOE_BUFFERS - 1) < used
    pl.when(more)(functools.partial(row_tile, True))
    pl.when((i < used) & jnp.logical_not(more))(functools.partial(row_tile, False))

    @pl.when(i >= used)
    def _():
        ys_ref[...] = jnp.zeros_like(ys_ref)


def _moe(src, texp, used, hsl, w1, w3, w2):
    max_row_tiles = texp.shape[0]
    D = w1.shape[1]
    F = w1.shape[2]
    grid_spec = pltpu.PrefetchScalarGridSpec(
        num_scalar_prefetch=3,
        grid=(max_row_tiles,),
        in_specs=[pl.BlockSpec(memory_space=pl.ANY),
                  pl.BlockSpec((1, D, F), lambda i, s, te, nt: (te[i], 0, 0)),
                  pl.BlockSpec((1, D, F), lambda i, s, te, nt: (te[i], 0, 0)),
                  pl.BlockSpec((1, F, D), lambda i, s, te, nt: (te[i], 0, 0))],
        out_specs=pl.BlockSpec((TM_MOE, D), lambda i, s, te, nt: (i, 0)),
        scratch_shapes=[pltpu.VMEM((MOE_BUFFERS, TM_MOE, D + AUX_W), BF16),
                        pltpu.VMEM((D, F), BF16), pltpu.VMEM((D, F), BF16), pltpu.VMEM((F, D), BF16),
                        pltpu.SemaphoreType.DMA((MOE_BUFFERS,))],
    )
    return pl.pallas_call(
        _moe_kernel,
        grid_spec=grid_spec,
        out_shape=jax.ShapeDtypeStruct((max_row_tiles * TM_MOE, D), BF16),
        compiler_params=_params("arbitrary"),
        name="moe",
    )(src, texp, used, hsl, w1, w3, w2)


def _combine_kernel(dst_ref, lcnt_ref, ys_ref, x1_ref, route_ref, mod_ref, o_ref, ybuf, sem):
    t = pl.program_id(0)
    slot = t % 2
    tm = x1_ref.shape[0]

    def gather(tile, slot_, wait):
        _granule_copies(dst_ref, tile * LOCAL_GRAN, lcnt_ref[tile], LOCAL_GRAN, ys_ref, ybuf.at[slot_],
                        sem.at[slot_], wait)

    @pl.when(t == 0)
    def _():
        ybuf[...] = jnp.zeros_like(ybuf)
        gather(0, 0, False)

    @pl.when(t + 1 < pl.num_programs(0))
    def _():
        gather(t + 1, 1 - slot, False)

    gather(t, slot, True)
    lp1 = route_ref[:, R_LP1:R_LP1 + 1].astype(I32)
    lp2 = route_ref[:, R_LP2:R_LP2 + 1].astype(I32)
    col = lax.broadcasted_iota(I32, (tm, LOCAL_ROWS), 1)
    unsort = jnp.where(col == lp1, 1.0, jnp.where(col == lp2, 1.0, 0.0)).astype(BF16)
    o_ref[...] = x1_ref[...] + mod_ref[0, 5:6, :] * _dot(unsort, ybuf[slot])


def _combine(dst, lcnt, ys, x1, route, mod3, S):
    N, D = x1.shape
    tm = TM_SORT
    steps_per_batch = S // tm
    grid_spec = pltpu.PrefetchScalarGridSpec(
        num_scalar_prefetch=2,
        grid=(N // tm,),
        in_specs=[pl.BlockSpec(memory_space=pl.ANY),
                  pl.BlockSpec((tm, D), lambda t, d, c: (t, 0)),
                  pl.BlockSpec((tm, LANES), lambda t, d, c: (t, 0)),
                  pl.BlockSpec((1, 6, D), lambda t, d, c: (t // steps_per_batch, 0, 0))],
        out_specs=pl.BlockSpec((tm, D), lambda t, d, c: (t, 0)),
        scratch_shapes=[pltpu.VMEM((2, LOCAL_ROWS, D), BF16), pltpu.SemaphoreType.DMA((2,))],
    )
    return pl.pallas_call(
        _combine_kernel,
        grid_spec=grid_spec,
        out_shape=jax.ShapeDtypeStruct((N, D), F32),
        compiler_params=_params("arbitrary"),
        name="combine",
    )(dst, lcnt, ys, x1, route, mod3)


def _layer(x2, mod3, B, S, norm_mix_w, w_in, q_norm_w, k_norm_w, conv_w, conv_b, w_rec_gate, b_rec_gate,
           w_in_gate, b_in_gate, lru_lambda, w_proj_a, w_proj_b, w_out, norm_ffn_w, w_group, b_group,
           w_expert_router, b_expert_router, w1, w3, w2):
    N, D = x2.shape
    qa, kv, lru, gates = _proj(x2, mod3, norm_mix_w, _wprep(w_in), S)
    attn = _attn(qa, kv, q_norm_w, k_norm_w, B, S)
    lru_out = _lru(lru, conv_w, conv_b, _block_diag_tiles(w_rec_gate).astype(BF16), b_rec_gate,
                   _block_diag_tiles(w_in_gate).astype(BF16), b_in_gate, lru_lambda, B, S)

    n_r = N_GROUPS + N_EXPERTS
    w_router = jnp.concatenate([w_group, w_expert_router, jnp.zeros((D, LANES - n_r), F32)], axis=1)
    wr_hi = w_router.astype(BF16)
    wr_lo = (w_router - wr_hi.astype(F32)).astype(BF16)
    b_router = jnp.concatenate([b_group, b_expert_router, jnp.zeros((LANES - n_r,), F32)]).reshape(1, LANES)
    x1, hsl, route, gc = _merge(attn, lru_out, gates, x2, mod3, w_proj_a.astype(BF16), w_proj_b.astype(BF16),
                                w_out.astype(BF16), norm_ffn_w, jnp.concatenate([wr_hi, wr_lo], axis=1),
                                b_router, S)

    src, texp, used, dst, lcnt = _plan(gc.reshape(gc.shape[0], LANES))
    ys = _moe(src, texp, used, hsl, w1, w3, w2)
    return _combine(dst, lcnt, ys, x1, route, mod3, S)


def kernel(x, c, ada_w, ada_b, norm_mix_w, w_in, q_norm_w, k_norm_w, conv_w, conv_b, w_rec_gate, b_rec_gate,
           w_in_gate, b_in_gate, lru_lambda, w_proj_a, w_proj_b, w_out, norm_ffn_w, w_group, b_group,
           w_expert_router, b_expert_router, w1, w3, w2):
    B, S, D = x.shape
    x2 = x.reshape(B * S, D)
    for l in range(ada_w.shape[0]):
        mod3 = _ada(c, ada_w[l], ada_b[l]).reshape(B, 6, D)
        x2 = _layer(x2, mod3, B, S, norm_mix_w[l], w_in[l:l + 1], q_norm_w[l], k_norm_w[l], conv_w[l], conv_b[l],
                    w_rec_gate[l], b_rec_gate[l], w_in_gate[l], b_in_gate[l], lru_lambda[l], w_proj_a[l],
                    w_proj_b[l], w_out[l], norm_ffn_w[l], w_group[l], b_group[l], w_expert_router[l],
                    b_expert_router[l], w1[l], w3[l], w2[l])
    return x2.reshape(B, S, D)
```

```python
import functools

import jax
import jax.numpy as jnp
import numpy as np
from jax import lax
from jax.experimental import pallas as pl
from jax.experimental.pallas import tpu as pltpu

F32 = jnp.float32
BF16 = jnp.bfloat16
I32 = jnp.int32

EPS = 1e-6
CHUNK = 64
Q_BLOCK = 128
N_HEADS_A = 8
HEAD_DIM_A = 64
KV_DIM_A = 64
ATTN_OUT = N_HEADS_A * KV_DIM_A
N_IDX_HEADS = 4
IDX_DIM = 64
TOPK_MAX = 256
LRU_WIDTH = 512
LRU_BLOCKS = 8
LRU_BLOCK_DIM = LRU_WIDTH // LRU_BLOCKS
CONV_WIDTH = 4
LRU_C = 8.0
N_GROUPS = 4
EXPERTS_PER_GROUP = 8
N_EXPERTS = N_GROUPS * EXPERTS_PER_GROUP
D_FF_EXPERT = 256

LANES = 128
INT_MIN = -2 ** 31
VMEM_LIMIT = 56 * 1024 * 1024

D_MODEL = 1024
COL_Q = 0
COL_K = COL_Q + N_HEADS_A * HEAD_DIM_A
COL_V = COL_K + KV_DIM_A
COL_QI = COL_V + KV_DIM_A
COL_KI = COL_QI + N_IDX_HEADS * IDX_DIM
COL_WI = COL_KI + IDX_DIM
COL_LX = COL_WI + N_IDX_HEADS
D_IN = COL_LX + 2 * LRU_WIDTH + 2 * D_MODEL
ATTN_COLS = -(-COL_LX // LANES) * LANES
TAIL_COLS = D_IN - COL_LX
TAIL_SHIFT = COL_LX % LANES
W_IN_PAD = (COL_LX // LANES) * LANES + TAIL_COLS + LANES
assert COL_K % LANES == 0 and COL_QI % LANES == 0 and COL_KI % LANES == 0 and TAIL_COLS % LANES == 0

QA_W = N_HEADS_A * HEAD_DIM_A + N_IDX_HEADS * IDX_DIM
KV_W = (COL_QI - COL_K) + (ATTN_COLS - COL_KI)
K_OFF, V_OFF = 0, KV_DIM_A
KI_OFF = COL_QI - COL_K
WI_OFF = KI_OFF + IDX_DIM

TM_PROJ = 512

TM_SORT = 512
GRAN = 16
TM_MOE = 512
GRAN_PER_TILE = TM_MOE // GRAN
MOE_BUFFERS = 3
GATHER_UNROLL = 8
LOCAL_ROWS = -(-(2 * TM_SORT + N_EXPERTS * (GRAN - 1)) // LANES) * LANES
LOCAL_GRAN = LOCAL_ROWS // GRAN
AUX_W = LANES
W_PIECES = 3
AUX_W1, AUX_W2 = 0, W_PIECES
AUX_E1 = 2 * W_PIECES

R_LP1, R_LP2 = 0, 1


def _dot(a, b, **kw):
    return jnp.dot(a, b, preferred_element_type=F32, **kw)


def _params(*sem):
    return pltpu.CompilerParams(dimension_semantics=sem, vmem_limit_bytes=VMEM_LIMIT)


def _ada_kernel(c_ref, w_ref, b_ref, o_ref):
    c = c_ref[...]
    cond = c * jax.nn.sigmoid(c)
    w = w_ref[...]
    c_hi = cond.astype(BF16)
    c_lo = (cond - c_hi.astype(F32)).astype(BF16)
    w_hi = w.astype(BF16)
    w_lo = (w - w_hi.astype(F32)).astype(BF16)
    o_ref[...] = _dot(c_hi, w_hi) + (_dot(c_lo, w_hi) + _dot(c_hi, w_lo)) + b_ref[...]


def _ada(c, ada_w, ada_b):
    B, D = c.shape
    n_out = ada_w.shape[1]
    tn = 1024
    return pl.pallas_call(
        _ada_kernel,
        grid=(n_out // tn,),
        in_specs=[pl.BlockSpec((B, D), lambda j: (0, 0)),
                  pl.BlockSpec((D, tn), lambda j: (0, j)),
                  pl.BlockSpec((1, tn), lambda j: (0, j))],
        out_specs=pl.BlockSpec((B, tn), lambda j: (0, j)),
        out_shape=jax.ShapeDtypeStruct((B, n_out), F32),
        compiler_params=_params("arbitrary"),
        name="ada",
    )(c, ada_w, ada_b.reshape(1, n_out))


W_PREP_ROWS = 512


def _wprep_kernel(wt_ref, o_ref, stage, sem):
    row = lax.broadcasted_iota(I32, (W_PREP_ROWS, stage.shape[3]), 0)
    starts = list(range(0, W_IN_PAD, W_PREP_ROWS))

    def copy(j):
        r0 = starts[j]
        n = min(W_PREP_ROWS, D_IN - r0)
        return pltpu.make_async_copy(wt_ref.at[r0:r0 + n], stage.at[j % 2, 0:n], sem.at[j % 2]), n

    copy(0)[0].start()
    for j, r0 in enumerate(starts):
        if j + 1 < len(starts):
            copy(j + 1)[0].start()
        cp, n = copy(j)
        cp.wait()
        blk = stage[j % 2, :, 0, :]
        if n < W_PREP_ROWS:
            blk = jnp.where(row < n, blk, 0.0)
        o_ref[:, r0:r0 + W_PREP_ROWS] = blk.T.astype(BF16)


def _wprep(w_in):
    D = w_in.shape[1]
    return pl.pallas_call(
        _wprep_kernel,
        in_specs=[pl.BlockSpec(memory_space=pl.ANY)],
        out_shape=jax.ShapeDtypeStruct((D, W_IN_PAD), BF16),
        scratch_shapes=[pltpu.VMEM((2, W_PREP_ROWS, 1, D), F32), pltpu.SemaphoreType.DMA((2,))],
        compiler_params=pltpu.CompilerParams(vmem_limit_bytes=VMEM_LIMIT),
        name="wprep",
    )(jnp.transpose(w_in, (2, 0, 1)))


def _rms_mod(x, w, shift, scale):
    ms = jnp.mean(x * x, axis=-1, keepdims=True)
    y = x * lax.rsqrt(ms + EPS) * w
    return y * (1.0 + scale) + shift


def _proj_kernel(x_ref, mod_ref, nw_ref, w_ref, oqa_ref, okv_ref, olru_ref, og_ref, wt_s):
    @pl.when(pl.program_id(0) == 0)
    def _():
        rr = lax.broadcasted_iota(I32, (2 * LANES, LANES), 0)
        cc = lax.broadcasted_iota(I32, (2 * LANES, LANES), 1)
        shift = jnp.where(rr == cc + TAIL_SHIFT, 1.0, 0.0).astype(BF16)
        for j in range(TAIL_COLS // LANES):
            src = (COL_LX // LANES + j) * LANES
            wt_s[:, j * LANES:(j + 1) * LANES] = _dot(w_ref[:, src:src + 2 * LANES], shift).astype(BF16)

    h = _rms_mod(x_ref[...], nw_ref[...], mod_ref[0, 0:1, :], mod_ref[0, 1:2, :])
    hb = h.astype(BF16)
    pa = _dot(hb, w_ref[:, :ATTN_COLS])
    oqa_ref[...] = jnp.concatenate([pa[:, COL_Q:COL_K], pa[:, COL_QI:COL_KI]], axis=1)
    okv_ref[...] = jnp.concatenate([pa[:, COL_K:COL_QI], pa[:, COL_KI:ATTN_COLS]], axis=1)
    olru_ref[...] = _dot(hb, wt_s[:, :2 * LRU_WIDTH])
    og_ref[...] = _sigmoid(_dot(hb, wt_s[:, 2 * LRU_WIDTH:])).astype(BF16)


def _proj(x2, mod3, norm_w, w_all, S):
    N, D = x2.shape
    tm = TM_PROJ
    steps_per_batch = S // tm
    const = lambda t: (0, 0)
    return pl.pallas_call(
        _proj_kernel,
        grid=(N // tm,),
        in_specs=[pl.BlockSpec((tm, D), lambda t: (t, 0)),
                  pl.BlockSpec((1, 6, D), lambda t: (t // steps_per_batch, 0, 0)),
                  pl.BlockSpec((1, D), const),
                  pl.BlockSpec(w_all.shape, const)],
        out_specs=[pl.BlockSpec((tm, QA_W), lambda t: (t, 0)),
                   pl.BlockSpec((tm, KV_W), lambda t: (t, 0)),
                   pl.BlockSpec((tm, 2 * LRU_WIDTH), lambda t: (t, 0)),
                   pl.BlockSpec((tm, 2 * D), lambda t: (t, 0))],
        out_shape=[jax.ShapeDtypeStruct((N, QA_W), F32),
                   jax.ShapeDtypeStruct((N, KV_W), F32),
                   jax.ShapeDtypeStruct((N, 2 * LRU_WIDTH), F32),
                   jax.ShapeDtypeStruct((N, 2 * D), BF16)],
        scratch_shapes=[pltpu.VMEM((D, TAIL_COLS), BF16)],
        compiler_params=_params("arbitrary"),
        name="proj",
    )(x2, mod3, norm_w.reshape(1, D), w_all)


ATTN_BUCKETS = 8
ROW_FOLD = 64
KEY_ROWS = 256
COUNT_CHAINS = 2
LOG2E = float(np.log2(np.e))
POS_RADIX = 256
SLOPE_PIECES = 3
V_ONES = 16


def _bf16_pieces(c, n):
    out = []
    rest = float(c)
    for _ in range(n):
        p = float(np.asarray(rest, np.float32).astype(BF16).astype(np.float32))
        out.append(p)
        rest -= p
    return out


def _reduce_rows(op, x):
    r = x.shape[0]
    if r > ROW_FOLD and r % ROW_FOLD == 0:
        x = op(x.reshape(r // ROW_FOLD, ROW_FOLD, x.shape[1]), axis=0)
    return op(x, axis=0, keepdims=True)


def _attn_block(s_eff, topk, i, qa_ref, kvq_ref, qnw_ref, o_ref, kn_s, vt_s, ki_s, key_s, bias_s, lg_s, ot_s):
    nq = N_HEADS_A * HEAD_DIM_A
    q_t = qa_ref[:, :nq].T
    qi_t = qa_ref[:, nq:].T
    wq_t = kvq_ref[...].T[WI_OFF:WI_OFF + N_IDX_HEADS, :] * (N_IDX_HEADS ** -0.5 * IDX_DIM ** -0.5)

    q_pos = i * Q_BLOCK + lax.broadcasted_iota(I32, (1, Q_BLOCK), 1)
    limit = (q_pos // CHUNK + 1) * CHUNK
    chunks = [slice(r0, min(r0 + KEY_ROWS, s_eff)) for r0 in range(0, s_eff, KEY_ROWS)]

    qi = [qi_t[h * IDX_DIM:(h + 1) * IDX_DIM, :].astype(BF16) for h in range(N_IDX_HEADS)]
    for rows in chunks:
        ki = ki_s[rows, :]
        score = jnp.zeros((rows.stop - rows.start, Q_BLOCK), F32)
        for h in range(N_IDX_HEADS):
            score = score + jnp.maximum(_dot(ki, qi[h]), 0.0) * wq_t[h:h + 1, :]
        k_pos = rows.start + lax.broadcasted_iota(I32, score.shape, 0)
        key_s[rows, :] = jnp.where(k_pos < limit, score, -jnp.inf)

    def count(mask):
        return _reduce_rows(jnp.sum, jnp.where(mask, 1.0, 0.0))

    def count_keys(test):
        acc = [jnp.zeros((ROW_FOLD, Q_BLOCK), F32) for _ in range(COUNT_CHAINS)]
        for n, r0 in enumerate(range(0, s_eff, ROW_FOLD)):
            hit = jnp.where(test(key_s[r0:r0 + ROW_FOLD, :]), 1.0, 0.0)
            acc[n % COUNT_CHAINS] = acc[n % COUNT_CHAINS] + hit
        return jnp.sum(functools.reduce(lambda x, y: x + y, acc), axis=0, keepdims=True)

    def ordered_to_float(c):
        s = c ^ INT_MIN
        return pltpu.bitcast(jnp.where(s < 0, INT_MIN | (-s), s), F32)

    def bisect(it, r):
        c = r | jnp.left_shift(jnp.int32(1), 31 - it)
        cand = ordered_to_float(c)
        return jnp.where(count_keys(lambda slab: slab >= cand) >= topk, c, r)

    def select_by_threshold():
        thr = ordered_to_float(lax.fori_loop(0, 32, bisect, jnp.zeros((1, Q_BLOCK), I32)))

        n_ge = count_keys(lambda slab: slab >= thr)
        clean = jnp.max(jnp.where((n_ge == topk) & (thr > -jnp.inf), 0.0, 1.0)) == 0.0

        @pl.when(clean)
        def _():
            for rows in chunks:
                bias_s[rows, :] = jnp.where(key_s[rows, :] >= thr, 0.0, -jnp.inf)

        @pl.when(jnp.logical_not(clean))
        def _():
            key = key_s[:s_eff, :]
            need = topk - count(key > thr)
            rr = lax.broadcasted_iota(I32, (LANES, LANES), 0)
            cc = lax.broadcasted_iota(I32, (LANES, LANES), 1)
            tri = jnp.where(cc < rr, 1.0, 0.0).astype(BF16)
            seen = jnp.zeros((1, Q_BLOCK), F32)
            for c in range(s_eff // LANES):
                kc = key[c * LANES:(c + 1) * LANES, :]
                eq = kc == thr
                eqf = jnp.where(eq, 1.0, 0.0)
                rank = _dot(tri, eqf.astype(BF16)) + seen
                seen = seen + jnp.sum(eqf, axis=0, keepdims=True)
                tie_bias = jnp.where(jnp.where(eq, rank, topk) < need, 0.0, -jnp.inf)
                bias = jnp.where(kc > thr, 0.0, tie_bias)
                k_pos = c * LANES + lax.broadcasted_iota(I32, (LANES, Q_BLOCK), 0)
                bias_s[c * LANES:(c + 1) * LANES, :] = jnp.where(k_pos < limit, bias, -jnp.inf)

    if s_eff > topk:
        select_by_threshold()
    else:
        for rows in chunks:
            k_pos = rows.start + lax.broadcasted_iota(I32, (rows.stop - rows.start, Q_BLOCK), 0)
            bias_s[rows, :] = jnp.where(k_pos < limit, 0.0, -jnp.inf)

    qnw = qnw_ref[...]
    crow = lax.broadcasted_iota(I32, (LANES - HEAD_DIM_A, Q_BLOCK), 0)
    r_k = lax.broadcasted_iota(I32, (Q_BLOCK, Q_BLOCK), 0)
    c_q = lax.broadcasted_iota(I32, (Q_BLOCK, Q_BLOCK), 1)
    after = 2.0 * jnp.maximum(r_k - c_q, 0).astype(F32)
    band = pl.ds(pl.multiple_of(i * Q_BLOCK, Q_BLOCK), Q_BLOCK)
    slope = [sum(_bf16_pieces(LOG2E * 2.0 ** -(h + 1), SLOPE_PIECES)) for h in range(N_HEADS_A)]
    row_max = []
    for h in range(N_HEADS_A):
        pieces = _bf16_pieces(LOG2E * 2.0 ** -(h + 1), SLOPE_PIECES)
        qh = q_t[h * HEAD_DIM_A:(h + 1) * HEAD_DIM_A, :]
        ms = jnp.mean(qh * qh, axis=0, keepdims=True)
        qn = (qh * lax.rsqrt(ms + EPS) * qnw) * (HEAD_DIM_A ** -0.5) * LOG2E
        slopes = jnp.zeros(crow.shape, F32)
        for k, piece in enumerate(pieces):
            slopes = jnp.where(crow == 2 * k, POS_RADIX * piece, jnp.where(crow == 2 * k + 1, piece, slopes))
        qa = jnp.concatenate([qn, slopes], axis=0).astype(BF16)
        folded = []
        for rows in chunks:
            logits = _dot(kn_s[rows, :], qa) + bias_s[rows, :]
            lg_s[h, rows, :] = logits
            folded.append(jnp.max(logits.reshape(-1, ROW_FOLD, Q_BLOCK), axis=0))
        row_max.append(jnp.max(functools.reduce(jnp.maximum, folded), axis=0, keepdims=True))
    for h in range(N_HEADS_A):
        lg_s[h, band, :] = lg_s[h, band, :] - slope[h] * after
    for h in range(N_HEADS_A):
        acc = jnp.zeros((KV_DIM_A + V_ONES, Q_BLOCK), F32)
        for rows in chunks:
            p = jnp.exp2(lg_s[h, rows, :] - row_max[h])
            acc = acc + _dot(vt_s[:, rows], p.astype(BF16))
        ot_s[h * KV_DIM_A:(h + 1) * KV_DIM_A, :] = acc[:KV_DIM_A, :] / acc[KV_DIM_A:KV_DIM_A + 1, :]
    o_ref[...] = ot_s[...].T


def _attn_kernel(qa_ref, kvq_ref, kv_ref, qnw_ref, knw_ref, o_ref, kn_s, vt_s, ki_s, key_s, bias_s, lg_s, ot_s):
    i = pl.program_id(1)
    S = kv_ref.shape[0]
    nb = S // Q_BLOCK
    topk = float(min(TOPK_MAX, S // 4))

    @pl.when(i == 0)
    def _():
        kv = kv_ref[:, :LANES]
        lane = lax.broadcasted_iota(I32, kv.shape, 1)
        pos = lax.broadcasted_iota(I32, kv.shape, 0)
        ms = jnp.sum(jnp.where(lane < KV_DIM_A, kv * kv, 0.0), axis=-1, keepdims=True) * (1.0 / KV_DIM_A)
        kn = kv * lax.rsqrt(ms + EPS) * knw_ref[...]
        digit = jnp.where((lane & 1) == 0, pos // POS_RADIX, pos % POS_RADIX).astype(F32)
        is_digit = (lane >= KV_DIM_A) & (lane < KV_DIM_A + 2 * SLOPE_PIECES)
        kn_s[...] = jnp.where(is_digit, digit, kn).astype(BF16)
        vt_s[:KV_DIM_A, :] = kv.T[V_OFF:V_OFF + KV_DIM_A, :].astype(BF16)
        vt_s[KV_DIM_A:, :] = jnp.ones((V_ONES, kv.shape[0]), BF16)
        ki_s[...] = kv_ref[:, KI_OFF:KI_OFF + IDX_DIM].astype(BF16)

    n_buckets = min(ATTN_BUCKETS, nb)
    per = nb // n_buckets
    for j in range(n_buckets):
        pl.when(i // per == j)(functools.partial(
            _attn_block, (j + 1) * per * Q_BLOCK, topk, i, qa_ref, kvq_ref, qnw_ref, o_ref,
            kn_s, vt_s, ki_s, key_s, bias_s, lg_s, ot_s))


def _attn(qa, kv, q_norm_w, k_norm_w, B, S):
    N = qa.shape[0]
    nb = S // Q_BLOCK
    return pl.pallas_call(
        _attn_kernel,
        grid=(B, nb),
        in_specs=[pl.BlockSpec((Q_BLOCK, QA_W), lambda b, i: (b * nb + i, 0)),
                  pl.BlockSpec((Q_BLOCK, KV_W), lambda b, i: (b * nb + i, 0)),
                  pl.BlockSpec((S, KV_W), lambda b, i: (b, 0)),
                  pl.BlockSpec((HEAD_DIM_A, 1), lambda b, i: (0, 0)),
                  pl.BlockSpec((1, LANES), lambda b, i: (0, 0))],
        out_specs=pl.BlockSpec((Q_BLOCK, ATTN_OUT), lambda b, i: (b * nb + i, 0)),
        out_shape=jax.ShapeDtypeStruct((N, ATTN_OUT), F32),
        scratch_shapes=[pltpu.VMEM((S, LANES), BF16),
                        pltpu.VMEM((KV_DIM_A + V_ONES, S), BF16),
                        pltpu.VMEM((S, IDX_DIM), BF16),
                        pltpu.VMEM((S, Q_BLOCK), F32),
                        pltpu.VMEM((S, Q_BLOCK), F32),
                        pltpu.VMEM((N_HEADS_A, S, Q_BLOCK), F32),
                        pltpu.VMEM((ATTN_OUT, Q_BLOCK), F32)],
        compiler_params=_params("arbitrary", "arbitrary"),
        name="attn",
    )(qa, kv, kv, q_norm_w.reshape(HEAD_DIM_A, 1),
      jnp.pad(k_norm_w, (0, LANES - KV_DIM_A)).reshape(1, LANES))


SUBLANES = 8


def _shift_rows(x, d, fill, row):
    if d % SUBLANES == 0:
        return jnp.concatenate([jnp.full((d, x.shape[1]), fill, x.dtype), x[:-d]], axis=0)
    return jnp.where(row >= d, pltpu.roll(x, d, 0), fill)


def _sigmoid(x):
    return 0.5 * (1.0 + jnp.tanh(0.5 * x))


def _doubling_scan(a, b, row):
    n = a.shape[0]
    d = 1
    while d < n:
        b = a * _shift_rows(b, d, 0.0, row) + b
        if 2 * d < n:
            a = a * _shift_rows(a, d, 1.0, row)
        d *= 2
    return b


def _lru_kernel(xb_ref, gb_ref, cw_ref, cb_ref, wr_ref, br_ref, wi_ref, bi_ref, lam_ref, o_ref, a_s, b_s):
    x = xb_ref[...]
    S = x.shape[0]
    row = lax.broadcasted_iota(I32, x.shape, 0)
    xc = cb_ref[...] + _shift_rows(x, CONV_WIDTH - 1, 0.0, row) * cw_ref[0:1, :]
    for j in range(1, CONV_WIDTH):
        d = CONV_WIDTH - 1 - j
        xs = x if d == 0 else _shift_rows(x, d, 0.0, row)
        xc = xc + xs * cw_ref[j:j + 1, :]
    xcb = xc.astype(BF16)
    r = _sigmoid(_dot(xcb, wr_ref[0]) + br_ref[...])
    ig = _sigmoid(_dot(xcb, wi_ref[0]) + bi_ref[...])
    z = -lam_ref[...]
    softplus = jnp.maximum(z, 0.0) + jnp.log1p(jnp.exp(-jnp.abs(z)))
    log_a = -LRU_C * r * softplus
    a = jnp.exp(log_a)
    b = jnp.sqrt(-jnp.tanh(log_a) * (1.0 + a * a)) * (ig * xc)
    sub = row & (SUBLANES - 1)
    d = 1
    while d < SUBLANES:
        keep = sub >= d
        b = a * jnp.where(keep, pltpu.roll(b, d, 0), 0.0) + b
        a = a * jnp.where(keep, pltpu.roll(a, d, 0), 1.0)
        d *= 2
    a_s[...] = a
    b_s[...] = b
    groups = S // SUBLANES
    last = pl.ds(SUBLANES - 1, groups, stride=SUBLANES)
    g_row = lax.broadcasted_iota(I32, (groups, x.shape[1]), 0)
    h_last = _doubling_scan(a_s[last, :], b_s[last, :], g_row)
    carry = _shift_rows(h_last, 1, 0.0, g_row)
    carry = jnp.broadcast_to(carry[:, None, :], (groups, SUBLANES, x.shape[1])).reshape(x.shape)
    h = b_s[...] + a_s[...] * carry
    g = gb_ref[...]
    gelu = 0.5 * g * (1.0 + jnp.tanh(float(np.sqrt(2.0 / np.pi)) * (g + 0.044715 * (g * g * g))))
    o_ref[...] = h * gelu


def _lru(lru, conv_w, conv_b, wr_bd, b_rec, wi_bd, b_in, lam, B, S):
    N = lru.shape[0]
    nt = LRU_WIDTH // LANES
    vec = lambda b, j: (0, j)
    return pl.pallas_call(
        _lru_kernel,
        grid=(B, nt),
        in_specs=[pl.BlockSpec((S, LANES), lambda b, j: (b, j)),
                  pl.BlockSpec((S, LANES), lambda b, j: (b, nt + j)),
                  pl.BlockSpec((CONV_WIDTH, LANES), vec),
                  pl.BlockSpec((1, LANES), vec),
                  pl.BlockSpec((1, LANES, LANES), lambda b, j: (j, 0, 0)),
                  pl.BlockSpec((1, LANES), vec),
                  pl.BlockSpec((1, LANES, LANES), lambda b, j: (j, 0, 0)),
                  pl.BlockSpec((1, LANES), vec),
                  pl.BlockSpec((1, LANES), vec)],
        out_specs=pl.BlockSpec((S, LANES), lambda b, j: (b, j)),
        out_shape=jax.ShapeDtypeStruct((N, LRU_WIDTH), F32),
        scratch_shapes=[pltpu.VMEM((S, LANES), F32), pltpu.VMEM((S, LANES), F32)],
        compiler_params=_params("arbitrary", "arbitrary"),
        name="lru",
    )(lru, lru, conv_w, conv_b.reshape(1, -1), wr_bd, b_rec.reshape(1, -1), wi_bd,
      b_in.reshape(1, -1), lam.reshape(1, -1))


def _block_diag_tiles(w):
    per = LANES // LRU_BLOCK_DIM
    nt = LRU_WIDTH // LANES
    w5 = w.reshape(nt, per, LRU_BLOCK_DIM, 1, LRU_BLOCK_DIM)
    eye = jnp.eye(per, dtype=w.dtype).reshape(1, per, 1, per, 1)
    return (w5 * eye).reshape(nt, LANES, LANES)


def _first_index_of_max(vals, lane):
    m = jnp.max(vals, axis=-1, keepdims=True)
    idx = jnp.min(jnp.where(vals == m, lane, 4 * LANES), axis=-1, keepdims=True)
    return m, idx


def _weight_pieces(out, w, lane, first):
    rest = w
    for k in range(W_PIECES):
        piece = rest.astype(BF16).astype(F32)
        out = jnp.where(lane == first + k, piece, out)
        rest = rest - piece
    return out


def _merge_kernel(attn_ref, lru_ref, g_ref, x_ref, mod_ref, wpa_ref, wpb_ref, wo_ref, nw_ref, wr_ref,
                  br_ref, x1_ref, hsl_ref, route_ref, gc_ref):
    D = x_ref.shape[1]
    tm = x_ref.shape[0]

    y_a = _dot(attn_ref[...].astype(BF16), wpa_ref[...])
    y_b = _dot(lru_ref[...].astype(BF16), wpb_ref[...])
    merged = g_ref[:, :D].astype(F32) * y_a + g_ref[:, D:].astype(F32) * y_b
    x1 = x_ref[...] + mod_ref[0, 2:3, :] * _dot(merged.astype(BF16), wo_ref[...])
    x1_ref[...] = x1
    h2 = _rms_mod(x1, nw_ref[...], mod_ref[0, 3:4, :], mod_ref[0, 4:5, :])
    h2b = h2.astype(BF16)

    h2l = (h2 - h2b.astype(F32)).astype(BF16)
    hi_hi_lo = _dot(h2b, wr_ref[...])
    logits = hi_hi_lo[:, :LANES] + (_dot(h2l, wr_ref[:, :LANES]) + hi_hi_lo[:, LANES:]) + br_ref[...]
    lane = lax.broadcasted_iota(I32, logits.shape, 1)
    gl = jnp.where(lane < N_GROUPS, logits, -jnp.inf)
    gmax, g_sel = _first_index_of_max(gl, lane)
    g_weight = 1.0 / jnp.sum(jnp.exp(gl - gmax), axis=-1, keepdims=True)
    e_lo = N_GROUPS + g_sel * EXPERTS_PER_GROUP
    el = jnp.where((lane >= e_lo) & (lane < e_lo + EXPERTS_PER_GROUP), logits, -jnp.inf)
    v1, i1 = _first_index_of_max(el, lane)
    el2 = jnp.where(lane == i1, -jnp.inf, el)
    v2, i2 = _first_index_of_max(el2, lane)
    e2x = jnp.exp(v2 - v1)
    w1 = g_weight / (1.0 + e2x)
    w2 = g_weight * e2x / (1.0 + e2x)

    hot1 = lane == i1
    hot2 = lane == i2
    hot = jnp.where(hot1, 1.0, jnp.where(hot2, 1.0, 0.0))
    gcnt = jnp.floor((jnp.sum(hot, axis=0, keepdims=True) + (GRAN - 1.0)) * (1.0 / GRAN))
    gc_ref[0] = gcnt
    rr = lax.broadcasted_iota(I32, (LANES, LANES), 0)
    cc = lax.broadcasted_iota(I32, (LANES, LANES), 1)
    upper = jnp.where(rr < cc, 1.0, 0.0).astype(BF16)
    run_start = GRAN * _dot(jnp.broadcast_to(gcnt, (8, LANES)).astype(BF16), upper)[0:1, :]
    rr = lax.broadcasted_iota(I32, (tm, tm), 0)
    cc = lax.broadcasted_iota(I32, (tm, tm), 1)
    lower = jnp.where(cc < rr, 1.0, 0.0).astype(BF16)
    pos = _dot(lower, hot.astype(BF16)) + run_start
    pos1 = jnp.where(hot1, pos, 0.0)
    pos2 = jnp.where(hot2, pos, 0.0)
    lp1 = jnp.sum(pos1, axis=-1, keepdims=True)
    lp2 = jnp.sum(pos2, axis=-1, keepdims=True)
    route_ref[...] = jnp.where(lane == R_LP1, lp1, jnp.where(lane == R_LP2, lp2, 0.0))

    aux = jnp.where(lane == AUX_E1, (i1 - N_GROUPS).astype(F32), 0.0)
    aux = _weight_pieces(_weight_pieces(aux, w1, lane, AUX_W1), w2, lane, AUX_W2)
    lp1r = jnp.sum(pos1.T, axis=0, keepdims=True).astype(I32)
    lp2r = jnp.sum(pos2.T, axis=0, keepdims=True).astype(I32)
    rows = lax.broadcasted_iota(I32, (LOCAL_ROWS, tm), 0)
    sort = jnp.where(rows == lp1r, 1.0, jnp.where(rows == lp2r, 1.0, 0.0)).astype(BF16)
    hsl_ref[...] = _dot(sort, jnp.concatenate([h2b, aux.astype(BF16)], axis=1)).astype(BF16)


def _merge(attn, lru_out, gates, x2, mod3, wpa, wpb, wo, norm_w, wr_hi_lo, b_router, S):
    N, D = x2.shape
    tm = TM_SORT
    n_tiles = N // tm
    steps_per_batch = S // tm
    const = lambda t: (0, 0)
    rows = lambda t: (t, 0)
    return pl.pallas_call(
        _merge_kernel,
        grid=(n_tiles,),
        in_specs=[pl.BlockSpec((tm, ATTN_OUT), rows),
                  pl.BlockSpec((tm, LRU_WIDTH), rows),
                  pl.BlockSpec((tm, 2 * D), rows),
                  pl.BlockSpec((tm, D), rows),
                  pl.BlockSpec((1, 6, D), lambda t: (t // steps_per_batch, 0, 0)),
                  pl.BlockSpec(wpa.shape, const),
                  pl.BlockSpec(wpb.shape, const),
                  pl.BlockSpec(wo.shape, const),
                  pl.BlockSpec((1, D), const),
                  pl.BlockSpec(wr_hi_lo.shape, const),
                  pl.BlockSpec((1, LANES), const)],
        out_specs=[pl.BlockSpec((tm, D), rows),
                   pl.BlockSpec((LOCAL_ROWS, D + AUX_W), rows),
                   pl.BlockSpec((tm, LANES), rows),
                   pl.BlockSpec((1, 1, LANES), lambda t: (t, 0, 0))],
        out_shape=[jax.ShapeDtypeStruct((N, D), F32),
                   jax.ShapeDtypeStruct((n_tiles * LOCAL_ROWS, D + AUX_W), BF16),
                   jax.ShapeDtypeStruct((N, LANES), F32),
                   jax.ShapeDtypeStruct((n_tiles, 1, LANES), F32)],
        compiler_params=_params("arbitrary"),
        name="merge",
    )(attn, lru_out, gates, x2, mod3, wpa, wpb, wo, norm_w.reshape(1, D), wr_hi_lo, b_router)


def _plan_sizes(n_tiles):
    max_gran = n_tiles * LOCAL_GRAN + N_EXPERTS * (GRAN_PER_TILE - 1)
    max_row_tiles = -(-max_gran // GRAN_PER_TILE)
    return max_row_tiles * GRAN_PER_TILE, max_row_tiles


def _plan_kernel(gc_ref, src_ref, texp_ref, nt_ref, dst_ref, lcnt_ref):
    gc = gc_ref[...]
    T = gc.shape[0]
    row = lax.broadcasted_iota(I32, (T, LANES), 0)
    local = lax.broadcasted_iota(I32, (T, LANES), 1).astype(F32)
    rr = lax.broadcasted_iota(I32, (LANES, LANES), 0)
    cc = lax.broadcasted_iota(I32, (LANES, LANES), 1)
    upper = jnp.where(rr < cc, 1.0, 0.0).astype(BF16)

    tiles = jnp.floor((jnp.sum(gc, axis=0, keepdims=True) + (GRAN_PER_TILE - 1.0)) * (1.0 / GRAN_PER_TILE))
    tile0 = _dot(jnp.broadcast_to(tiles, (SUBLANES, LANES)).astype(BF16), upper)[0:1, :]
    tile_end = tile0 + tiles
    before = gc
    d = 1
    while d < T:
        before = before + _shift_rows(before, d, 0.0, row)
        d *= 2
    before = before - gc
    run_global = GRAN_PER_TILE * tile0 + before
    run_local = _dot(gc.astype(BF16), upper)
    lcnt_ref[...] = jnp.broadcast_to(jnp.sum(gc, axis=1, keepdims=True), (T, LANES)).astype(I32)

    experts = range(N_GROUPS, N_GROUPS + N_EXPERTS)
    dst = jnp.zeros((T, LANES), F32)
    for e in experts:
        l0 = run_local[:, e:e + 1]
        inside = (local >= l0) & (local < l0 + gc[:, e:e + 1])
        dst = jnp.where(inside, run_global[:, e:e + 1] + (local - l0), dst)
    dst_ref[...] = dst.astype(I32)

    shape = src_ref.shape
    slot = (lax.broadcasted_iota(I32, shape, 0) * LANES + lax.broadcasted_iota(I32, shape, 1)).astype(F32)
    src = jnp.zeros(shape, F32)
    for e in experts:
        off = slot - GRAN_PER_TILE * tile0[0:1, e:e + 1]
        for t in range(T):
            p = before[t:t + 1, e:e + 1]
            hit = (off >= p) & (off < p + gc[t:t + 1, e:e + 1])
            src = jnp.where(hit, (t * LOCAL_GRAN) + run_local[t:t + 1, e:e + 1] + (off - p), src)
    src_ref[...] = src.astype(I32)

    shape = texp_ref.shape
    tile = (lax.broadcasted_iota(I32, shape, 0) * LANES + lax.broadcasted_iota(I32, shape, 1)).astype(F32)
    owner = jnp.zeros(shape, F32)
    for e in experts:
        owner = owner + jnp.where(tile >= tile_end[0:1, e:e + 1], 1.0, 0.0)
    texp_ref[...] = jnp.minimum(owner, N_EXPERTS - 1.0).astype(I32)
    nt_ref[...] = jnp.broadcast_to(tile_end[0:1, N_GROUPS + N_EXPERTS - 1:N_GROUPS + N_EXPERTS],
                                   nt_ref.shape).astype(I32)


def _plan(gc):
    n_tiles = gc.shape[0]
    max_gran, max_row_tiles = _plan_sizes(n_tiles)
    gran_rows = -(-max_gran // LANES)
    tile_rows = -(-max_row_tiles // LANES)
    src, texp, used, dst, lcnt = pl.pallas_call(
        _plan_kernel,
        out_shape=[jax.ShapeDtypeStruct((gran_rows, LANES), I32),
                   jax.ShapeDtypeStruct((tile_rows, LANES), I32),
                   jax.ShapeDtypeStruct((1, LANES), I32),
                   jax.ShapeDtypeStruct((n_tiles, LANES), I32),
                   jax.ShapeDtypeStruct((n_tiles, LANES), I32)],
        name="plan",
    )(gc)
    return (src.reshape(-1), texp.reshape(-1)[:max_row_tiles], used[0, :1],
            dst[:, :LOCAL_GRAN].reshape(-1), lcnt[:, 0])


def _granule_copies(idx_ref, first, count, max_count, src_hbm, dst_vmem, sem, wait):
    def one(k):
        g = idx_ref[first + k]
        cp = pltpu.make_async_copy(src_hbm.at[pl.ds(pl.multiple_of(g * GRAN, GRAN), GRAN)],
                                   dst_vmem.at[k * GRAN:(k + 1) * GRAN], sem)
        if wait:
            cp.wait()
        else:
            cp.start(priority=k % 2)

    for b0 in range(0, max_count, GATHER_UNROLL):
        def block(b0=b0):
            for k in range(b0, min(b0 + GATHER_UNROLL, max_count)):
                one(k)
        if isinstance(count, int):
            if b0 < count:
                block()
        else:
            pl.when(b0 < count)(block)


def _moe_kernel(src_ref, texp_ref, nt_ref, hsl_ref, w1_ref, w3_ref, w2_ref, ys_ref, xbuf, w1b, w3b, w2b, sem):
    i = pl.program_id(0)
    used = nt_ref[0]
    slot = i % MOE_BUFFERS
    D = ys_ref.shape[1]

    def gather(tile, wait):
        slot_ = tile % MOE_BUFFERS
        _granule_copies(src_ref, tile * GRAN_PER_TILE, GRAN_PER_TILE, GRAN_PER_TILE, hsl_ref, xbuf.at[slot_],
                        sem.at[slot_], wait)

    @pl.when(i == 0)
    def _():
        for ahead in range(MOE_BUFFERS - 1):
            pl.when(ahead < used)(functools.partial(gather, ahead, False))

    def row_tile(prefetch):
        gather(i, True)

        @pl.when((i == 0) | (texp_ref[i] != texp_ref[jnp.maximum(i - 1, 0)]))
        def _():
            w1b[...] = w1_ref[0].astype(BF16)
            w3b[...] = w3_ref[0].astype(BF16)
            w2b[...] = w2_ref[0].astype(BF16)

        if prefetch:
            gather(i + (MOE_BUFFERS - 1), False)
        xb = xbuf[slot]
        hb = xb[:, :D]
        aux = xb[:, D:].astype(F32)
        first = aux[:, AUX_E1:AUX_E1 + 1] == texp_ref[i].astype(F32)
        w_row = jnp.zeros((TM_MOE, 1), F32)
        for k in range(W_PIECES):
            w_row = w_row + jnp.where(first, aux[:, AUX_W1 + k:AUX_W1 + k + 1], aux[:, AUX_W2 + k:AUX_W2 + k + 1])
        a = _dot(hb, w1b[...])
        act = (a * _sigmoid(a)) * _dot(hb, w3b[...])
        ys_ref[...] = (w_row * _dot(act.astype(BF16), w2b[...])).astype(BF16)

    more = i + (MOE_BUFFERS - 1) < used
    pl.when(more)(functools.partial(row_tile, True))
    pl.when((i < used) & jnp.logical_not(more))(functools.partial(row_tile, False))

    @pl.when(i >= used)
    def _():
        ys_ref[...] = jnp.zeros_like(ys_ref)


def _moe(src, texp, used, hsl, w1, w3, w2):
    max_row_tiles = texp.shape[0]
    D = w1.shape[1]
    F = w1.shape[2]
    grid_spec = pltpu.PrefetchScalarGridSpec(
        num_scalar_prefetch=3,
        grid=(max_row_tiles,),
        in_specs=[pl.BlockSpec(memory_space=pl.ANY),
                  pl.BlockSpec((1, D, F), lambda i, s, te, nt: (te[i], 0, 0)),
                  pl.BlockSpec((1, D, F), lambda i, s, te, nt: (te[i], 0, 0)),
                  pl.BlockSpec((1, F, D), lambda i, s, te, nt: (te[i], 0, 0))],
        out_specs=pl.BlockSpec((TM_MOE, D), lambda i, s, te, nt: (i, 0)),
        scratch_shapes=[pltpu.VMEM((MOE_BUFFERS, TM_MOE, D + AUX_W), BF16),
                        pltpu.VMEM((D, F), BF16), pltpu.VMEM((D, F), BF16), pltpu.VMEM((F, D), BF16),
                        pltpu.SemaphoreType.DMA((MOE_BUFFERS,))],
    )
    return pl.pallas_call(
        _moe_kernel,
        grid_spec=grid_spec,
        out_shape=jax.ShapeDtypeStruct((max_row_tiles * TM_MOE, D), BF16),
        compiler_params=_params("arbitrary"),
        name="moe",
    )(src, texp, used, hsl, w1, w3, w2)


def _combine_kernel(dst_ref, lcnt_ref, ys_ref, x1_ref, route_ref, mod_ref, o_ref, ybuf, sem):
    t = pl.program_id(0)
    slot = t % 2
    tm = x1_ref.shape[0]

    def gather(tile, slot_, wait):
        _granule_copies(dst_ref, tile * LOCAL_GRAN, lcnt_ref[tile], LOCAL_GRAN, ys_ref, ybuf.at[slot_],
                        sem.at[slot_], wait)

    @pl.when(t == 0)
    def _():
        ybuf[...] = jnp.zeros_like(ybuf)
        gather(0, 0, False)

    @pl.when(t + 1 < pl.num_programs(0))
    def _():
        gather(t + 1, 1 - slot, False)

    gather(t, slot, True)
    lp1 = route_ref[:, R_LP1:R_LP1 + 1].astype(I32)
    lp2 = route_ref[:, R_LP2:R_LP2 + 1].astype(I32)
    col = lax.broadcasted_iota(I32, (tm, LOCAL_ROWS), 1)
    unsort = jnp.where(col == lp1, 1.0, jnp.where(col == lp2, 1.0, 0.0)).astype(BF16)
    o_ref[...] = x1_ref[...] + mod_ref[0, 5:6, :] * _dot(unsort, ybuf[slot])


def _combine(dst, lcnt, ys, x1, route, mod3, S):
    N, D = x1.shape
    tm = TM_SORT
    steps_per_batch = S // tm
    grid_spec = pltpu.PrefetchScalarGridSpec(
        num_scalar_prefetch=2,
        grid=(N // tm,),
        in_specs=[pl.BlockSpec(memory_space=pl.ANY),
                  pl.BlockSpec((tm, D), lambda t, d, c: (t, 0)),
                  pl.BlockSpec((tm, LANES), lambda t, d, c: (t, 0)),
                  pl.BlockSpec((1, 6, D), lambda t, d, c: (t // steps_per_batch, 0, 0))],
        out_specs=pl.BlockSpec((tm, D), lambda t, d, c: (t, 0)),
        scratch_shapes=[pltpu.VMEM((2, LOCAL_ROWS, D), BF16), pltpu.SemaphoreType.DMA((2,))],
    )
    return pl.pallas_call(
        _combine_kernel,
        grid_spec=grid_spec,
        out_shape=jax.ShapeDtypeStruct((N, D), F32),
        compiler_params=_params("arbitrary"),
        name="combine",
    )(dst, lcnt, ys, x1, route, mod3)


def _layer(x2, mod3, B, S, norm_mix_w, w_in, q_norm_w, k_norm_w, conv_w, conv_b, w_rec_gate, b_rec_gate,
           w_in_gate, b_in_gate, lru_lambda, w_proj_a, w_proj_b, w_out, norm_ffn_w, w_group, b_group,
           w_expert_router, b_expert_router, w1, w3, w2):
    N, D = x2.shape
    qa, kv, lru, gates = _proj(x2, mod3, norm_mix_w, _wprep(w_in), S)
    attn = _attn(qa, kv, q_norm_w, k_norm_w, B, S)
    lru_out = _lru(lru, conv_w, conv_b, _block_diag_tiles(w_rec_gate).astype(BF16), b_rec_gate,
                   _block_diag_tiles(w_in_gate).astype(BF16), b_in_gate, lru_lambda, B, S)

    n_r = N_GROUPS + N_EXPERTS
    w_router = jnp.concatenate([w_group, w_expert_router, jnp.zeros((D, LANES - n_r), F32)], axis=1)
    wr_hi = w_router.astype(BF16)
    wr_lo = (w_router - wr_hi.astype(F32)).astype(BF16)
    b_router = jnp.concatenate([b_group, b_expert_router, jnp.zeros((LANES - n_r,), F32)]).reshape(1, LANES)
    x1, hsl, route, gc = _merge(attn, lru_out, gates, x2, mod3, w_proj_a.astype(BF16), w_proj_b.astype(BF16),
                                w_out.astype(BF16), norm_ffn_w, jnp.concatenate([wr_hi, wr_lo], axis=1),
                                b_router, S)

    src, texp, used, dst, lcnt = _plan(gc.reshape(gc.shape[0], LANES))
    ys = _moe(src, texp, used, hsl, w1, w3, w2)
    return _combine(dst, lcnt, ys, x1, route, mod3, S)


def kernel(x, c, ada_w, ada_b, norm_mix_w, w_in, q_norm_w, k_norm_w, conv_w, conv_b, w_rec_gate, b_rec_gate,
           w_in_gate, b_in_gate, lru_lambda, w_proj_a, w_proj_b, w_out, norm_ffn_w, w_group, b_group,
           w_expert_router, b_expert_router, w1, w3, w2):
    B, S, D = x.shape
    x2 = x.reshape(B * S, D)
    for l in range(ada_w.shape[0]):
        mod3 = _ada(c, ada_w[l], ada_b[l]).reshape(B, 6, D)
        x2 = _layer(x2, mod3, B, S, norm_mix_w[l], w_in[l:l + 1], q_norm_w[l], k_norm_w[l], conv_w[l], conv_b[l],
                    w_rec_gate[l], b_rec_gate[l], w_in_gate[l], b_in_gate[l], lru_lambda[l], w_proj_a[l],
                    w_proj_b[l], w_out[l], norm_ffn_w[l], w_group[l], b_group[l], w_expert_router[l],
                    b_expert_router[l], w1[l], w3[l], w2[l])
    return x2.reshape(B, S, D)
```
